```python
import jax, jax.numpy as jnp
from jax import lax
import numpy as np

D_MODEL = 2048
BATCH = 2
SEQ = 4096
DEPTH = 1

CHUNK = 64
QBLOCK = 128
FOX_HEADS = 16
FOX_HEAD_DIM = 64
FOX_WIDTH = FOX_HEADS * FOX_HEAD_DIM
MLA_HEADS = 8
MLA_NOPE_DIM = 128
MLA_ROPE_DIM = 64
MLA_V_DIM = 128
MLA_Q_LORA = 512
MLA_KV_LORA = 256
MLA_QK_DIM = MLA_NOPE_DIM + MLA_ROPE_DIM
MLA_WIDTH = MLA_HEADS * MLA_V_DIM
ROPE_THETA = 10000.0
D_FF = 4 * D_MODEL
LN_EPS = 1e-5
RMS_EPS = 1e-6
N_ADA = 6
IN_SPLITS = (3 * FOX_WIDTH, FOX_HEADS, MLA_Q_LORA, MLA_KV_LORA, MLA_ROPE_DIM, D_MODEL, D_MODEL)
D_IN = sum(IN_SPLITS)

kernel_name = 'hybrid_fox_mla_sqrelu_block'


def layer_norm(x, g, b):
    xf = x.astype(jnp.float32)
    mu = jnp.mean(xf, axis=-1, keepdims=True)
    var = jnp.mean(jnp.square(xf - mu), axis=-1, keepdims=True)
    y = (xf - mu) * lax.rsqrt(var + LN_EPS)
    return (y * g + b).astype(x.dtype)


def rms_norm(x, g):
    xf = x.astype(jnp.float32)
    y = xf * lax.rsqrt(jnp.mean(jnp.square(xf), axis=-1, keepdims=True) + RMS_EPS)
    return (y * g).astype(x.dtype)


def rope_tables(seq_len):
    pos = jnp.arange(seq_len, dtype=jnp.float32)
    inv_freq = ROPE_THETA ** (-jnp.arange(0, MLA_ROPE_DIM, 2, dtype=jnp.float32) / MLA_ROPE_DIM)
    ang = pos[:, None] * inv_freq[None, :]
    return jnp.cos(ang), jnp.sin(ang)


def apply_rope(x, cos, sin):
    half = x.shape[-1] // 2
    x1, x2 = x[..., :half], x[..., half:]
    cos = cos.astype(x.dtype)
    sin = sin.astype(x.dtype)
    return jnp.concatenate([x1 * cos - x2 * sin, x1 * sin + x2 * cos], axis=-1)


def to_blocks(t):
    b, s = t.shape[0], t.shape[1]
    return t.reshape(b, s // QBLOCK, QBLOCK, *t.shape[2:]).swapaxes(0, 1)


def from_blocks(t):
    t = t.swapaxes(0, 1)
    return t.reshape(t.shape[0], t.shape[1] * t.shape[2], *t.shape[3:])


def fox_attention(q, k, v, log_f):
    b, s, h, dh = q.shape
    nb = s // QBLOCK
    scale = dh ** -0.5
    cum = jnp.cumsum(log_f, axis=1)
    cum_k = cum.transpose(0, 2, 1)
    k_pos = jnp.arange(s)

    def one_block(args):
        i, q_i, c_i = args
        logits = jnp.einsum('bqhd,bkhd->bhqk', q_i, k, preferred_element_type=jnp.float32) * scale
        logits = logits + (c_i.transpose(0, 2, 1)[..., :, None] - cum_k[..., None, :])
        q_pos = i * QBLOCK + jnp.arange(QBLOCK)
        mask = k_pos[None, :] <= q_pos[:, None]
        logits = jnp.where(mask, logits, -jnp.inf)
        p = jax.nn.softmax(logits, axis=-1).astype(v.dtype)
        return jnp.einsum('bhqk,bkhd->bqhd', p, v)

    out = lax.map(one_block, (jnp.arange(nb), to_blocks(q), to_blocks(cum)))
    return from_blocks(out)


def mla_attention(q_nope, q_rope, k_nope, k_rope, v):
    b, s, h, _ = q_nope.shape
    nb = s // QBLOCK
    scale = MLA_QK_DIM ** -0.5
    k_chunk = jnp.arange(s) // CHUNK

    def one_block(args):
        i, qn, qr = args
        logits = (jnp.einsum('bqhd,bkhd->bhqk', qn, k_nope, preferred_element_type=jnp.float32)
                  + jnp.einsum('bqhr,bkr->bhqk', qr, k_rope, preferred_element_type=jnp.float32)) * scale
        q_chunk = (i * QBLOCK + jnp.arange(QBLOCK)) // CHUNK
        mask = k_chunk[None, :] <= q_chunk[:, None]
        logits = jnp.where(mask, logits, -jnp.inf)
        p = jax.nn.softmax(logits, axis=-1).astype(v.dtype)
        return jnp.einsum('bhqk,bkhd->bqhd', p, v)

    out = lax.map(one_block, (jnp.arange(nb), to_blocks(q_nope), to_blocks(q_rope)))
    return from_blocks(out)


def token_mixers(u, w_in, b_forget, g_q_norm, w_q_up, g_kv_norm, w_kv_up, w_branch_fox, w_branch_mla, w_out):
    b, s, _ = u.shape
    proj = u @ w_in
    cuts = [int(v) for v in np.cumsum(IN_SPLITS)[:-1]]
    qkv, f_logit, c_q, c_kv, k_rope, g_fox, g_mla = jnp.split(proj, cuts, axis=-1)

    qkv = qkv.reshape(b, s, 3, FOX_HEADS, FOX_HEAD_DIM)
    log_f = jax.nn.log_sigmoid(f_logit.astype(jnp.float32) + b_forget.astype(jnp.float32))
    y_fox = fox_attention(qkv[:, :, 0], qkv[:, :, 1], qkv[:, :, 2], log_f).reshape(b, s, FOX_WIDTH)

    q = (rms_norm(c_q, g_q_norm) @ w_q_up).reshape(b, s, MLA_HEADS, MLA_QK_DIM)
    kv = (rms_norm(c_kv, g_kv_norm) @ w_kv_up).reshape(b, s, MLA_HEADS, MLA_NOPE_DIM + MLA_V_DIM)
    cos, sin = rope_tables(s)
    q_nope = q[..., :MLA_NOPE_DIM]
    q_rope = apply_rope(q[..., MLA_NOPE_DIM:], cos[:, None, :], sin[:, None, :])
    k_rope = apply_rope(k_rope, cos, sin)
    k_nope = kv[..., :MLA_NOPE_DIM]
    v = kv[..., MLA_NOPE_DIM:]
    y_mla = mla_attention(q_nope, q_rope, k_nope, k_rope, v).reshape(b, s, MLA_WIDTH)

    merged = jax.nn.sigmoid(g_fox) * (y_fox @ w_branch_fox) + jax.nn.sigmoid(g_mla) * (y_mla @ w_branch_mla)
    return merged @ w_out


def setup_inputs(seed: int = 0) -> dict:
    key = jax.random.key(seed)
    ks = jax.random.split(key, 24)
    beta = (8.0 * DEPTH) ** -0.25
    f32 = jnp.float32

    def nrm(k, shape, fan_in, mult=1.0):
        return jax.random.normal(k, shape, f32) * (fan_in ** -0.5) * mult

    def gain(k, shape):
        return 1.0 + 0.02 * jax.random.normal(k, shape, f32)

    def bias(k, shape):
        return 0.02 * jax.random.normal(k, shape, f32)

    return {
        'x': jax.random.normal(ks[0], (BATCH, SEQ, D_MODEL), f32),
        'c': jax.random.normal(ks[1], (BATCH, D_MODEL), f32),
        'w_ada': nrm(ks[2], (DEPTH, D_MODEL, N_ADA * D_MODEL), D_MODEL),
        'b_ada': bias(ks[3], (DEPTH, N_ADA * D_MODEL)),
        'w_in': nrm(ks[4], (DEPTH, D_MODEL, D_IN), D_MODEL),
        'b_forget': jax.random.uniform(ks[5], (DEPTH, FOX_HEADS), f32, minval=1.0, maxval=5.0),
        'g_q_norm': gain(ks[6], (DEPTH, MLA_Q_LORA)),
        'w_q_up': nrm(ks[7], (DEPTH, MLA_Q_LORA, MLA_HEADS * MLA_QK_DIM), MLA_Q_LORA),
        'g_kv_norm': gain(ks[8], (DEPTH, MLA_KV_LORA)),
        'w_kv_up': nrm(ks[9], (DEPTH, MLA_KV_LORA, MLA_HEADS * (MLA_NOPE_DIM + MLA_V_DIM)), MLA_KV_LORA),
        'w_branch_fox': nrm(ks[10], (DEPTH, FOX_WIDTH, D_MODEL), FOX_WIDTH),
        'w_branch_mla': nrm(ks[11], (DEPTH, MLA_WIDTH, D_MODEL), MLA_WIDTH),
        'w_out': nrm(ks[12], (DEPTH, D_MODEL, D_MODEL), D_MODEL, beta),
        'ln1_g': gain(ks[13], (DEPTH, D_MODEL)),
        'ln1_b': bias(ks[14], (DEPTH, D_MODEL)),
        'w_mlp_up': nrm(ks[15], (DEPTH, D_MODEL, D_FF), D_MODEL),
        'w_mlp_down': nrm(ks[16], (DEPTH, D_FF, D_MODEL), D_FF, beta),
        'ln2_g': gain(ks[17], (DEPTH, D_MODEL)),
        'ln2_b': bias(ks[18], (DEPTH, D_MODEL)),
    }


def reference(x, c, w_ada, b_ada, w_in, b_forget, g_q_norm, w_q_up, g_kv_norm, w_kv_up,
              w_branch_fox, w_branch_mla, w_out, ln1_g, ln1_b, w_mlp_up, w_mlp_down, ln2_g, ln2_b):
    alpha = (2.0 * DEPTH) ** 0.25
    for l in range(DEPTH):
        mod = jax.nn.silu(c) @ w_ada[l] + b_ada[l]
        shift1, scale1, gate1, shift2, scale2, gate2 = jnp.split(mod[:, None, :], N_ADA, axis=-1)

        u = x * (1.0 + scale1) + shift1
        mix = token_mixers(u, w_in[l], b_forget[l], g_q_norm[l], w_q_up[l], g_kv_norm[l], w_kv_up[l],
                           w_branch_fox[l], w_branch_mla[l], w_out[l])
        x = layer_norm(alpha * x + gate1 * mix, ln1_g[l], ln1_b[l])

        u2 = x * (1.0 + scale2) + shift2
        h = jnp.square(jax.nn.relu(u2 @ w_mlp_up[l])) @ w_mlp_down[l]
        x = layer_norm(alpha * x + gate2 * h, ln2_g[l], ln2_b[l])
    return x
```

```python
import functools
import math

import jax
import jax.numpy as jnp
from jax import lax
from jax.experimental import pallas as pl
from jax.experimental.pallas import tpu as pltpu

F32 = jnp.float32
BF16 = jnp.bfloat16

FOX_HEADS = 16
FOX_HEAD_DIM = 64
FOX_WIDTH = FOX_HEADS * FOX_HEAD_DIM
MLA_HEADS = 8
MLA_NOPE_DIM = 128
MLA_ROPE_DIM = 64
MLA_V_DIM = 128
MLA_Q_LORA = 512
MLA_KV_LORA = 256
MLA_QK_DIM = MLA_NOPE_DIM + MLA_ROPE_DIM
CHUNK = 64
ROPE_THETA = 10000.0
LN_EPS = 1e-5
RMS_EPS = 1e-6
N_ADA = 6
DEPTH = 1
ALPHA = (2.0 * DEPTH) ** 0.25

LANES = 128
LOG2E = math.log2(math.e)
FOX_QSCALE = FOX_HEAD_DIM ** -0.5 * LOG2E
MLA_QSCALE = MLA_QK_DIM ** -0.5 * LOG2E
NEG_BIG = -1e30
VMEM_LIMIT = 56 * 1024 * 1024

SM_F = (0, 128)
SM_CQ = (128, 640)
SM_CKV = (640, 896)
SM_KR = (896, 1024)
SM_KRS = (1024, 1152)
SM_WIDTH = 1152


def _cparams(semantics):
    return pltpu.CompilerParams(dimension_semantics=semantics, vmem_limit_bytes=VMEM_LIMIT)


def _layer_norm(y, g, b):
    mu = jnp.mean(y, axis=-1, keepdims=True)
    yc = y - mu
    var = jnp.mean(yc * yc, axis=-1, keepdims=True)
    return yc * lax.rsqrt(var + LN_EPS) * g + b


def _rms_norm(c, g):
    return c * lax.rsqrt(jnp.mean(c * c, axis=-1, keepdims=True) + RMS_EPS) * g


def _ada_kernel(ct_ref, w_ref, b_ref, o_ref, *, n_batch, k_chunk):
    d = w_ref.shape[0]
    ct = ct_ref[...]
    s = ct * jax.nn.sigmoid(ct)
    accs = [b_ref[...] for _ in range(n_batch)]
    for kc in range(d // k_chunk):
        rows = slice(kc * k_chunk, (kc + 1) * k_chunk)
        w = w_ref[rows, :]
        for b in range(n_batch):
            accs[b] = accs[b] + jnp.sum(w * s[rows, b:b + 1], axis=0, keepdims=True)
    for b in range(n_batch):
        o_ref[b:b + 1, :] = accs[b]


def _ada(c, w_ada, b_ada):
    n_batch, d = c.shape
    n = w_ada.shape[1]
    tn = 1024
    return pl.pallas_call(
        functools.partial(_ada_kernel, n_batch=n_batch, k_chunk=256),
        grid=(n // tn,),
        in_specs=[
            pl.BlockSpec((d, n_batch), lambda j: (0, 0)),
            pl.BlockSpec((d, tn), lambda j: (0, j)),
            pl.BlockSpec((1, tn), lambda j: (0, j)),
        ],
        out_specs=pl.BlockSpec((n_batch, tn), lambda j: (0, j)),
        out_shape=jax.ShapeDtypeStruct((n_batch, n), F32),
        compiler_params=_cparams(("arbitrary",)),
        name="ada",
    )(c.T, w_ada, b_ada.reshape(1, n))


def _inproj_kernel(x_ref, mod_ref, w_ref, o_ref, u_ref, *, q_tile):
    j = pl.program_id(1)

    @pl.when(j == 0)
    def _():
        shift = mod_ref[0:1, :]
        scale = mod_ref[1:2, :]
        u_ref[...] = (x_ref[...] * (1.0 + scale) + shift).astype(BF16)

    acc = jnp.dot(u_ref[...], w_ref[...], preferred_element_type=F32)

    @pl.when(j == q_tile)
    def _():
        o_ref[...] = (acc * FOX_QSCALE).astype(o_ref.dtype)

    @pl.when(j != q_tile)
    def _():
        o_ref[...] = acc.astype(o_ref.dtype)


def _inproj(x2d, mod3, w_main, seq):
    m, d = x2d.shape
    n = w_main.shape[1]
    tm, tn = 1024, 1024
    tiles_per_batch = seq // tm
    return pl.pallas_call(
        functools.partial(_inproj_kernel, q_tile=4),
        grid=(m // tm, n // tn),
        in_specs=[
            pl.BlockSpec((tm, d), lambda i, j: (i, 0)),
            pl.BlockSpec((None, N_ADA, d), lambda i, j: (i // tiles_per_batch, 0, 0)),
            pl.BlockSpec((d, tn), lambda i, j: (0, j)),
        ],
        out_specs=pl.BlockSpec((tm, tn), lambda i, j: (i, j)),
        out_shape=jax.ShapeDtypeStruct((m, n), BF16),
        scratch_shapes=[pltpu.VMEM((tm, d), BF16)],
        compiler_params=_cparams(("parallel", "arbitrary")),
        name="inproj",
    )(x2d, mod3, w_main)


def _log_sigmoid(x):
    return jnp.minimum(x, 0.0) - jnp.log1p(jnp.exp(-jnp.abs(x)))


def _prep_kernel(x_ref, mod_ref, wsm_ref, bf_ref, gq_ref, gkv_ref, wq_ref, wkv_ref, cos_ref, sin_ref,
                 qn_ref, qr_ref, kv_ref, kr_ref, cum_ref, cumt_ref, carry_ref, *, tiles_per_batch):
    i = pl.program_id(0)
    tm = x_ref.shape[0]
    shift = mod_ref[0:1, :]
    scale = mod_ref[1:2, :]
    u = (x_ref[...] * (1.0 + scale) + shift).astype(BF16)
    ps = jnp.dot(u, wsm_ref[...], preferred_element_type=F32)
    cos = cos_ref[...]
    sin = sin_ref[...]

    logf = _log_sigmoid(ps[:, SM_F[0]:SM_F[1]] + bf_ref[...]) * LOG2E
    hi = logf.astype(BF16)
    r1 = logf - hi.astype(F32)
    mid = r1.astype(BF16)
    lo = (r1 - mid.astype(F32)).astype(BF16)
    row = lax.broadcasted_iota(jnp.int32, (tm, tm), 0)
    col = lax.broadcasted_iota(jnp.int32, (tm, tm), 1)
    tri = (row >= col).astype(BF16)
    cs = (jnp.dot(tri, hi, preferred_element_type=F32)
          + jnp.dot(tri, mid, preferred_element_type=F32)
          + jnp.dot(tri, lo, preferred_element_type=F32))

    @pl.when(i % tiles_per_batch == 0)
    def _():
        carry_ref[...] = jnp.zeros_like(carry_ref)

    cum = cs + carry_ref[...]
    carry_ref[...] = cum[tm - 1:tm, :]
    cum_ref[...] = cum
    cumt_ref[...] = cum.T[0:FOX_HEADS, :]

    cqn = _rms_norm(ps[:, SM_CQ[0]:SM_CQ[1]], gq_ref[...]).astype(BF16)
    qa = jnp.dot(cqn, wq_ref[...], preferred_element_type=F32)
    hw = MLA_HEADS * LANES
    qn_ref[...] = (qa[:, 0:hw] * MLA_QSCALE).astype(BF16)
    for h in range(MLA_HEADS):
        a = qa[:, hw + h * LANES: hw + (h + 1) * LANES]
        b = qa[:, 2 * hw + h * LANES: 2 * hw + (h + 1) * LANES]
        qr_ref[:, h * LANES:(h + 1) * LANES] = ((a * cos + b * sin) * MLA_QSCALE).astype(BF16)

    ckvn = _rms_norm(ps[:, SM_CKV[0]:SM_CKV[1]], gkv_ref[...]).astype(BF16)
    kv_ref[...] = jnp.dot(ckvn, wkv_ref[...], preferred_element_type=F32).astype(BF16)
    kr_ref[...] = (ps[:, SM_KR[0]:SM_KR[1]] * cos + ps[:, SM_KRS[0]:SM_KRS[1]] * sin).astype(BF16)


def _prep(x2d, mod3, w_small, bf_pad, g_q, g_kv, w_q_all, w_kv, cos_t, sin_t, seq):
    m, d = x2d.shape
    n_batch = m // seq
    tm = 512
    tiles_per_batch = seq // tm
    hw = MLA_HEADS * LANES
    const = lambda i: (0, 0)
    return pl.pallas_call(
        functools.partial(_prep_kernel, tiles_per_batch=tiles_per_batch),
        grid=(m // tm,),
        in_specs=[
            pl.BlockSpec((tm, d), lambda i: (i, 0)),
            pl.BlockSpec((None, N_ADA, d), lambda i: (i // tiles_per_batch, 0, 0)),
            pl.BlockSpec(w_small.shape, const),
            pl.BlockSpec(bf_pad.shape, const),
            pl.BlockSpec(g_q.shape, const),
            pl.BlockSpec(g_kv.shape, const),
            pl.BlockSpec(w_q_all.shape, const),
            pl.BlockSpec(w_kv.shape, const),
            pl.BlockSpec((tm, LANES), lambda i: (i % tiles_per_batch, 0)),
            pl.BlockSpec((tm, LANES), lambda i: (i % tiles_per_batch, 0)),
        ],
        out_specs=[
            pl.BlockSpec((tm, hw), lambda i: (i, 0)),
            pl.BlockSpec((tm, hw), lambda i: (i, 0)),
            pl.BlockSpec((tm, w_kv.shape[1]), lambda i: (i, 0)),
            pl.BlockSpec((tm, LANES), lambda i: (i, 0)),
            pl.BlockSpec((tm, LANES), lambda i: (i, 0)),
            pl.BlockSpec((None, FOX_HEADS, tm), lambda i: (i // tiles_per_batch, 0, i % tiles_per_batch)),
        ],
        out_shape=[
            jax.ShapeDtypeStruct((m, hw), BF16),
            jax.ShapeDtypeStruct((m, hw), BF16),
            jax.ShapeDtypeStruct((m, w_kv.shape[1]), BF16),
            jax.ShapeDtypeStruct((m, LANES), BF16),
            jax.ShapeDtypeStruct((m, LANES), F32),
            jax.ShapeDtypeStruct((n_batch, FOX_HEADS, seq), F32),
        ],
        scratch_shapes=[pltpu.VMEM((1, LANES), F32)],
        compiler_params=_cparams(("arbitrary",)),
        name="prep",
    )(x2d, mod3, w_small, bf_pad, g_q, g_kv, w_q_all, w_kv, cos_t, sin_t)


def _fox_kernel(q_ref, k_ref, v_ref, cum_ref, cumt_ref, o_ref, m_ref, l_ref, acc_ref, *, tq, tk):
    hp = pl.program_id(1)
    qi = pl.program_id(2)
    lane = lax.broadcasted_iota(jnp.int32, (tq, LANES), 1)
    lo_half = lane < FOX_HEAD_DIM
    q = q_ref[...]
    zero = jnp.zeros_like(q)
    qm = (jnp.where(lo_half, q, zero), jnp.where(lo_half, zero, q))
    cum_t = cum_ref[...]
    ci = tuple(jnp.sum(jnp.where(lane == 2 * hp + hh, cum_t, 0.0), axis=1, keepdims=True) for hh in (0, 1))
    m_ref[...] = jnp.full(m_ref.shape, NEG_BIG, F32)
    l_ref[...] = jnp.zeros(l_ref.shape, F32)
    acc_ref[...] = jnp.zeros(acc_ref.shape, F32)

    def tile(j, masked):
        k0 = pl.multiple_of(j * tk, tk)
        kb = k_ref[pl.ds(k0, tk), :]
        vb = v_ref[pl.ds(k0, tk), :]
        alphas, pvs = [], []
        for hh in (0, 1):
            s = lax.dot_general(qm[hh], kb, (((1,), (1,)), ((), ())), preferred_element_type=F32)
            ck = cumt_ref[hh:hh + 1, pl.ds(k0, tk)]
            t = s + (ci[hh] - ck)
            if masked:
                r = lax.broadcasted_iota(jnp.int32, (tq, tk), 0)
                c = lax.broadcasted_iota(jnp.int32, (tq, tk), 1)
                t = jnp.where(c <= r, t, NEG_BIG)
            m_old = m_ref[hh]
            m_new = jnp.maximum(m_old, jnp.max(t, axis=1, keepdims=True))
            alpha = jnp.exp2(m_old - m_new)
            p = jnp.exp2(t - m_new)
            l_ref[hh] = alpha * l_ref[hh] + jnp.sum(p, axis=1, keepdims=True)
            m_ref[hh] = m_new
            alphas.append(alpha)
            pvs.append(jnp.dot(p.astype(BF16), vb, preferred_element_type=F32))
        acc_ref[...] = (acc_ref[...] * jnp.where(lo_half, alphas[0], alphas[1])
                        + jnp.where(lo_half, pvs[0], pvs[1]))

    def body(j, carry):
        tile(j, False)
        return carry

    lax.fori_loop(0, qi, body, 0)
    tile(qi, True)
    inv = jnp.where(lo_half, 1.0 / l_ref[0], 1.0 / l_ref[1])
    o_ref[...] = (acc_ref[...] * inv).astype(o_ref.dtype)


def _fox_attention(proj, cum, cumt4, seq, q_col0):
    m = proj.shape[0]
    n_batch = m // seq
    tq = tk = 512
    nq = seq // tq
    n_pairs = FOX_HEADS // 2
    qb = q_col0 // LANES
    return pl.pallas_call(
        functools.partial(_fox_kernel, tq=tq, tk=tk),
        grid=(n_batch, n_pairs, nq),
        in_specs=[
            pl.BlockSpec((tq, LANES), lambda b, hp, qi: (b * nq + qi, qb + hp)),
            pl.BlockSpec((seq, LANES), lambda b, hp, qi: (b, qb + n_pairs + hp)),
            pl.BlockSpec((seq, LANES), lambda b, hp, qi: (b, qb + 2 * n_pairs + hp)),
            pl.BlockSpec((tq, LANES), lambda b, hp, qi: (b * nq + qi, 0)),
            pl.BlockSpec((None, None, 2, seq), lambda b, hp, qi: (b, hp, 0, 0)),
        ],
        out_specs=pl.BlockSpec((tq, LANES), lambda b, hp, qi: (b * nq + qi, hp)),
        out_shape=jax.ShapeDtypeStruct((m, FOX_WIDTH), BF16),
        scratch_shapes=[
            pltpu.VMEM((2, tq, 1), F32),
            pltpu.VMEM((2, tq, 1), F32),
            pltpu.VMEM((tq, LANES), F32),
        ],
        compiler_params=_cparams(("parallel", "parallel", "arbitrary")),
        name="fox_attn",
    )(proj, proj, proj, cum, cumt4)


def _mla_kernel(qn_ref, qr_ref, kn_ref, v_ref, kr_ref, o_ref, m_ref, l_ref, acc_ref, *, tq, tk):
    qi = pl.program_id(2)
    q = jnp.concatenate([qn_ref[...], qr_ref[...]], axis=1)
    m_ref[...] = jnp.full(m_ref.shape, NEG_BIG, F32)
    l_ref[...] = jnp.zeros(l_ref.shape, F32)
    acc_ref[...] = jnp.zeros(acc_ref.shape, F32)

    def tile(j, masked):
        k0 = pl.multiple_of(j * tk, tk)
        kb = jnp.concatenate([kn_ref[pl.ds(k0, tk), :], kr_ref[pl.ds(k0, tk), :]], axis=1)
        vb = v_ref[pl.ds(k0, tk), :]
        t = lax.dot_general(q, kb, (((1,), (1,)), ((), ())), preferred_element_type=F32)
        if masked:
            r = lax.broadcasted_iota(jnp.int32, (tq, tk), 0)
            c = lax.broadcasted_iota(jnp.int32, (tq, tk), 1)
            t = jnp.where((c // CHUNK) <= (r // CHUNK), t, NEG_BIG)
        m_old = m_ref[...]
        m_new = jnp.maximum(m_old, jnp.max(t, axis=1, keepdims=True))
        alpha = jnp.exp2(m_old - m_new)
        p = jnp.exp2(t - m_new)
        l_ref[...] = alpha * l_ref[...] + jnp.sum(p, axis=1, keepdims=True)
        m_ref[...] = m_new
        acc_ref[...] = acc_ref[...] * alpha + jnp.dot(p.astype(BF16), vb, preferred_element_type=F32)

    def body(j, carry):
        tile(j, False)
        return carry

    lax.fori_loop(0, qi, body, 0)
    tile(qi, True)
    o_ref[...] = (acc_ref[...] * (1.0 / l_ref[...])).astype(o_ref.dtype)


def _mla_attention(qn, qr, kv, kr, seq):
    m = qn.shape[0]
    n_batch = m // seq
    tq = tk = 512
    nq = seq // tq
    return pl.pallas_call(
        functools.partial(_mla_kernel, tq=tq, tk=tk),
        grid=(n_batch, MLA_HEADS, nq),
        in_specs=[
            pl.BlockSpec((tq, LANES), lambda b, h, qi: (b * nq + qi, h)),
            pl.BlockSpec((tq, LANES), lambda b, h, qi: (b * nq + qi, h)),
            pl.BlockSpec((seq, LANES), lambda b, h, qi: (b, 2 * h)),
            pl.BlockSpec((seq, LANES), lambda b, h, qi: (b, 2 * h + 1)),
            pl.BlockSpec((seq, LANES), lambda b, h, qi: (b, 0)),
        ],
        out_specs=pl.BlockSpec((tq, LANES), lambda b, h, qi: (b * nq + qi, h)),
        out_shape=jax.ShapeDtypeStruct((m, MLA_HEADS * MLA_V_DIM), BF16),
        scratch_shapes=[
            pltpu.VMEM((tq, 1), F32),
            pltpu.VMEM((tq, 1), F32),
            pltpu.VMEM((tq, LANES), F32),
        ],
        compiler_params=_cparams(("parallel", "parallel", "arbitrary")),
        name="mla_attn",
    )(qn, qr, kv, kv, kr)


def _mix_kernel(yf_ref, ym_ref, gf_ref, gm_ref, x_ref, mod_ref, wbf_ref, wbm_ref, wo_ref, g_ref, b_ref,
                o_ref, mg_ref, *, n_chunk):
    d = o_ref.shape[1]
    yf = yf_ref[...]
    ym = ym_ref[...]
    for c in range(d // n_chunk):
        cols = slice(c * n_chunk, (c + 1) * n_chunk)
        a = jnp.dot(yf, wbf_ref[:, cols], preferred_element_type=F32)
        b = jnp.dot(ym, wbm_ref[:, cols], preferred_element_type=F32)
        merged = (jax.nn.sigmoid(gf_ref[:, cols].astype(F32)) * a
                  + jax.nn.sigmoid(gm_ref[:, cols].astype(F32)) * b)
        mg_ref[:, cols] = merged.astype(BF16)
    mix = jnp.dot(mg_ref[...], wo_ref[...], preferred_element_type=F32)
    gate = mod_ref[2:3, :]
    o_ref[...] = _layer_norm(ALPHA * x_ref[...] + gate * mix, g_ref[...], b_ref[...])


def _mix(y_fox, y_mla, proj, x2d, mod3, w_bf, w_bm, w_o, ln_g, ln_b, seq):
    m, d = x2d.shape
    tm = 512
    tiles_per_batch = seq // tm
    const = lambda i: (0, 0)
    resident = dict(pipeline_mode=pl.Buffered(1))
    return pl.pallas_call(
        functools.partial(_mix_kernel, n_chunk=512),
        grid=(m // tm,),
        in_specs=[
            pl.BlockSpec((tm, y_fox.shape[1]), lambda i: (i, 0)),
            pl.BlockSpec((tm, y_mla.shape[1]), lambda i: (i, 0)),
            pl.BlockSpec((tm, d), lambda i: (i, 0)),
            pl.BlockSpec((tm, d), lambda i: (i, 1)),
            pl.BlockSpec((tm, d), lambda i: (i, 0)),
            pl.BlockSpec((None, N_ADA, d), lambda i: (i // tiles_per_batch, 0, 0)),
            pl.BlockSpec(w_bf.shape, const, **resident),
            pl.BlockSpec(w_bm.shape, const, **resident),
            pl.BlockSpec(w_o.shape, const, **resident),
            pl.BlockSpec((1, d), const),
            pl.BlockSpec((1, d), const),
        ],
        out_specs=pl.BlockSpec((tm, d), lambda i: (i, 0)),
        out_shape=jax.ShapeDtypeStruct((m, d), F32),
        scratch_shapes=[pltpu.VMEM((tm, d), BF16)],
        compiler_params=_cparams(("parallel",)),
        name="mix",
    )(y_fox, y_mla, proj, proj, x2d, mod3, w_bf, w_bm, w_o, ln_g, ln_b)


def _mlp_kernel(x_ref, mod_ref, wu_ref, wd_ref, g_ref, b_ref, o_ref, u_ref):
    f = pl.program_id(1)

    @pl.when(f == 0)
    def _():
        shift = mod_ref[3:4, :]
        scale = mod_ref[4:5, :]
        u_ref[...] = (x_ref[...] * (1.0 + scale) + shift).astype(BF16)
        o_ref[...] = jnp.zeros_like(o_ref)

    z = jnp.dot(u_ref[...], wu_ref[...], preferred_element_type=F32)
    h = jnp.square(jnp.maximum(z, 0.0)).astype(BF16)
    o_ref[...] += jnp.dot(h, wd_ref[...], preferred_element_type=F32)

    @pl.when(f == pl.num_programs(1) - 1)
    def _():
        gate = mod_ref[5:6, :]
        o_ref[...] = _layer_norm(ALPHA * x_ref[...] + gate * o_ref[...], g_ref[...], b_ref[...])


def _mlp(x1, mod3, w_up, w_down, ln_g, ln_b, seq):
    m, d = x1.shape
    d_ff = w_up.shape[1]
    tm, tf = 1024, 512
    tiles_per_batch = seq // tm
    return pl.pallas_call(
        _mlp_kernel,
        grid=(m // tm, d_ff // tf),
        in_specs=[
            pl.BlockSpec((tm, d), lambda i, f: (i, 0)),
            pl.BlockSpec((None, N_ADA, d), lambda i, f: (i // tiles_per_batch, 0, 0)),
            pl.BlockSpec((d, tf), lambda i, f: (0, f)),
            pl.BlockSpec((tf, d), lambda i, f: (f, 0)),
            pl.BlockSpec((1, d), lambda i, f: (0, 0)),
            pl.BlockSpec((1, d), lambda i, f: (0, 0)),
        ],
        out_specs=pl.BlockSpec((tm, d), lambda i, f: (i, 0)),
        out_shape=jax.ShapeDtypeStruct((m, d), F32),
        scratch_shapes=[pltpu.VMEM((tm, d), BF16)],
        compiler_params=_cparams(("parallel", "arbitrary")),
        name="mlp",
    )(x1, mod3, w_up, w_down, ln_g, ln_b)


def _pad_cols(w, width):
    return jnp.pad(w, ((0, 0), (0, width - w.shape[1])))


def _swap_halves(w):
    half = w.shape[-1] // 2
    return jnp.concatenate([w[..., half:], w[..., :half]], axis=-1)


def _layout_in_weights(w_in, d):
    c0 = 3 * FOX_WIDTH
    c1 = c0 + FOX_HEADS
    c2 = c1 + MLA_Q_LORA
    c3 = c2 + MLA_KV_LORA
    c4 = c3 + MLA_ROPE_DIM
    c5 = c4 + d
    w_main = jnp.concatenate([w_in[:, c4:c5], w_in[:, c5:c5 + d], w_in[:, :c0]], axis=1).astype(BF16)
    w_kr = w_in[:, c3:c4]
    w_small = jnp.concatenate([
        _pad_cols(w_in[:, c0:c1], LANES), w_in[:, c1:c2], w_in[:, c2:c3],
        _pad_cols(w_kr, LANES), _pad_cols(_swap_halves(w_kr), LANES)], axis=1).astype(BF16)
    return w_main, w_small


def _layout_q_up(w_q_up):
    r = w_q_up.shape[0]
    w3 = w_q_up.reshape(r, MLA_HEADS, MLA_QK_DIM)
    w_nope = w3[:, :, :MLA_NOPE_DIM].reshape(r, MLA_HEADS * MLA_NOPE_DIM)
    w_rope = w3[:, :, MLA_NOPE_DIM:]
    pad = ((0, 0), (0, 0), (0, LANES - MLA_ROPE_DIM))
    w_r = jnp.pad(w_rope, pad).reshape(r, MLA_HEADS * LANES)
    w_rs = jnp.pad(_swap_halves(w_rope), pad).reshape(r, MLA_HEADS * LANES)
    return jnp.concatenate([w_nope, w_r, w_rs], axis=1).astype(BF16)


def _rope_tables(seq):
    pos = jnp.arange(seq, dtype=F32)
    inv_freq = ROPE_THETA ** (-jnp.arange(0, MLA_ROPE_DIM, 2, dtype=F32) / MLA_ROPE_DIM)
    ang = pos[:, None] * inv_freq[None, :]
    cos, sin = jnp.cos(ang), jnp.sin(ang)
    cos_t = _pad_cols(jnp.concatenate([cos, cos], axis=1), LANES)
    sin_t = _pad_cols(jnp.concatenate([-sin, sin], axis=1), LANES)
    return cos_t, sin_t


def kernel(x, c, w_ada, b_ada, w_in, b_forget, g_q_norm, w_q_up, g_kv_norm, w_kv_up, w_branch_fox,
           w_branch_mla, w_out, ln1_g, ln1_b, w_mlp_up, w_mlp_down, ln2_g, ln2_b):
    n_batch, seq, d = x.shape
    assert w_ada.shape[0] == DEPTH
    m = n_batch * seq
    x2d = x.reshape(m, d)

    mod3 = _ada(c, w_ada[0], b_ada[0]).reshape(n_batch, N_ADA, d)

    w_main, w_small = _layout_in_weights(w_in[0], d)
    proj = _inproj(x2d, mod3, w_main, seq)

    cos_t, sin_t = _rope_tables(seq)
    bf_pad = _pad_cols(b_forget[0].reshape(1, FOX_HEADS), LANES)
    qn, qr, kv, kr, cum, cumt = _prep(
        x2d, mod3, w_small, bf_pad, g_q_norm[0].reshape(1, -1), g_kv_norm[0].reshape(1, -1),
        _layout_q_up(w_q_up[0]), w_kv_up[0].astype(BF16), cos_t, sin_t, seq)

    y_fox = _fox_attention(proj, cum, cumt.reshape(n_batch, FOX_HEADS // 2, 2, seq), seq, q_col0=2 * d)
    y_mla = _mla_attention(qn, qr, kv, kr, seq)

    x1 = _mix(y_fox, y_mla, proj, x2d, mod3, w_branch_fox[0].astype(BF16), w_branch_mla[0].astype(BF16),
              w_out[0].astype(BF16), ln1_g[0].reshape(1, d), ln1_b[0].reshape(1, d), seq)
    x2 = _mlp(x1, mod3, w_mlp_up[0].astype(BF16), w_mlp_down[0].astype(BF16),
              ln2_g[0].reshape(1, d), ln2_b[0].reshape(1, d), seq)
    return x2.reshape(n_batch, seq, d)
```

```python
import functools
import math

import jax
import jax.numpy as jnp
from jax import lax
from jax.experimental import pallas as pl
from jax.experimental.pallas import tpu as pltpu

F32 = jnp.float32
BF16 = jnp.bfloat16

FOX_HEADS = 16
FOX_HEAD_DIM = 64
FOX_WIDTH = FOX_HEADS * FOX_HEAD_DIM
MLA_HEADS = 8
MLA_NOPE_DIM = 128
MLA_ROPE_DIM = 64
MLA_V_DIM = 128
MLA_Q_LORA = 512
MLA_KV_LORA = 256
MLA_QK_DIM = MLA_NOPE_DIM + MLA_ROPE_DIM
CHUNK = 64
ROPE_THETA = 10000.0
LN_EPS = 1e-5
RMS_EPS = 1e-6
N_ADA = 6
DEPTH = 1
ALPHA = (2.0 * DEPTH) ** 0.25

LANES = 128
LOG2E = math.log2(math.e)
FOX_QSCALE = FOX_HEAD_DIM ** -0.5 * LOG2E
MLA_QSCALE = MLA_QK_DIM ** -0.5 * LOG2E
NEG_BIG = -1e30
VMEM_LIMIT = 56 * 1024 * 1024

SM_F = (0, 128)
SM_CQ = (128, 640)
SM_CKV = (640, 896)
SM_KR = (896, 1024)
SM_KRS = (1024, 1152)
SM_WIDTH = 1152
SUB = 128
HALF = 256


def _cparams(semantics):
    return pltpu.CompilerParams(dimension_semantics=semantics, vmem_limit_bytes=VMEM_LIMIT)


def _layer_norm(y, g, b):
    mu = jnp.mean(y, axis=-1, keepdims=True)
    yc = y - mu
    var = jnp.mean(yc * yc, axis=-1, keepdims=True)
    return yc * lax.rsqrt(var + LN_EPS) * g + b


def _rms_norm(c, g):
    return c * lax.rsqrt(jnp.mean(c * c, axis=-1, keepdims=True) + RMS_EPS) * g


def _ada_kernel(ct_ref, w_ref, b_ref, o_ref, *, n_batch, k_chunk):
    d = w_ref.shape[0]
    ct = ct_ref[...]
    s = ct * jax.nn.sigmoid(ct)
    accs = [b_ref[...] for _ in range(n_batch)]
    for kc in range(d // k_chunk):
        rows = slice(kc * k_chunk, (kc + 1) * k_chunk)
        w = w_ref[rows, :]
        for b in range(n_batch):
            accs[b] = accs[b] + jnp.sum(w * s[rows, b:b + 1], axis=0, keepdims=True)
    for b in range(n_batch):
        o_ref[b:b + 1, :] = accs[b]


def _ada(c, w_ada, b_ada):
    n_batch, d = c.shape
    n = w_ada.shape[1]
    tn = 1024
    return pl.pallas_call(
        functools.partial(_ada_kernel, n_batch=n_batch, k_chunk=256),
        grid=(n // tn,),
        in_specs=[
            pl.BlockSpec((d, n_batch), lambda j: (0, 0)),
            pl.BlockSpec((d, tn), lambda j: (0, j)),
            pl.BlockSpec((1, tn), lambda j: (0, j)),
        ],
        out_specs=pl.BlockSpec((n_batch, tn), lambda j: (0, j)),
        out_shape=jax.ShapeDtypeStruct((n_batch, n), F32),
        compiler_params=_cparams(("arbitrary",)),
        name="ada",
    )(c.T, w_ada, b_ada.reshape(1, n))


def _inproj_kernel(x_ref, mod_ref, w_ref, o_ref, u_ref, *, q_tile):
    j = pl.program_id(1)

    @pl.when(j == 0)
    def _():
        shift = mod_ref[0:1, :]
        scale = mod_ref[1:2, :]
        u_ref[...] = (x_ref[...] * (1.0 + scale) + shift).astype(BF16)

    acc = jnp.dot(u_ref[...], w_ref[...], preferred_element_type=F32)

    @pl.when(j == q_tile)
    def _():
        o_ref[...] = (acc * FOX_QSCALE).astype(o_ref.dtype)

    @pl.when(j != q_tile)
    def _():
        o_ref[...] = acc.astype(o_ref.dtype)


def _inproj(x2d, mod3, w_main, seq):
    m, d = x2d.shape
    n = w_main.shape[1]
    tm, tn = 1024, 1024
    tiles_per_batch = seq // tm
    return pl.pallas_call(
        functools.partial(_inproj_kernel, q_tile=4),
        grid=(m // tm, n // tn),
        in_specs=[
            pl.BlockSpec((tm, d), lambda i, j: (i, 0)),
            pl.BlockSpec((None, N_ADA, d), lambda i, j: (i // tiles_per_batch, 0, 0)),
            pl.BlockSpec((d, tn), lambda i, j: (0, j)),
        ],
        out_specs=pl.BlockSpec((tm, tn), lambda i, j: (i, j)),
        out_shape=jax.ShapeDtypeStruct((m, n), BF16),
        scratch_shapes=[pltpu.VMEM((tm, d), BF16)],
        compiler_params=_cparams(("parallel", "arbitrary")),
        name="inproj",
    )(x2d, mod3, w_main)


def _log_sigmoid(x):
    return jnp.minimum(x, 0.0) - jnp.log1p(jnp.exp(-jnp.abs(x)))


def _prep_kernel(x_ref, mod_ref, wsm_ref, bf_ref, gq_ref, gkv_ref, wq_ref, wkv_ref, cos_ref, sin_ref,
                 qn_ref, qr_ref, kv_ref, kr_ref, cum_ref, cumt_ref, carry_ref, *, tiles_per_batch):
    i = pl.program_id(0)
    tm = x_ref.shape[0]
    shift = mod_ref[0:1, :]
    scale = mod_ref[1:2, :]
    u = (x_ref[...] * (1.0 + scale) + shift).astype(BF16)
    ps = jnp.dot(u, wsm_ref[...], preferred_element_type=F32)
    cos = cos_ref[...]
    sin = sin_ref[...]

    logf = _log_sigmoid(ps[:, SM_F[0]:SM_F[1]] + bf_ref[...]) * LOG2E
    hi = logf.astype(BF16)
    r1 = logf - hi.astype(F32)
    mid = r1.astype(BF16)
    lo = (r1 - mid.astype(F32)).astype(BF16)
    row = lax.broadcasted_iota(jnp.int32, (tm, tm), 0)
    col = lax.broadcasted_iota(jnp.int32, (tm, tm), 1)
    tri = (row >= col).astype(BF16)
    cs = (jnp.dot(tri, hi, preferred_element_type=F32)
          + jnp.dot(tri, mid, preferred_element_type=F32)
          + jnp.dot(tri, lo, preferred_element_type=F32))

    @pl.when(i % tiles_per_batch == 0)
    def _():
        carry_ref[...] = jnp.zeros_like(carry_ref)

    cum = cs + carry_ref[...]
    carry_ref[...] = cum[tm - 1:tm, :]
    cum_ref[...] = cum
    cumt_ref[...] = cum.T[0:FOX_HEADS, :]

    cqn = _rms_norm(ps[:, SM_CQ[0]:SM_CQ[1]], gq_ref[...]).astype(BF16)
    qa = jnp.dot(cqn, wq_ref[...], preferred_element_type=F32)
    hw = MLA_HEADS * LANES
    qn_ref[...] = (qa[:, 0:hw] * MLA_QSCALE).astype(BF16)
    for h in range(MLA_HEADS):
        a = qa[:, hw + h * LANES: hw + (h + 1) * LANES]
        b = qa[:, 2 * hw + h * LANES: 2 * hw + (h + 1) * LANES]
        qr_ref[:, h * LANES:(h + 1) * LANES] = ((a * cos + b * sin) * MLA_QSCALE).astype(BF16)

    ckvn = _rms_norm(ps[:, SM_CKV[0]:SM_CKV[1]], gkv_ref[...]).astype(BF16)
    kv_ref[...] = jnp.dot(ckvn, wkv_ref[...], preferred_element_type=F32).astype(BF16)
    kr_ref[...] = (ps[:, SM_KR[0]:SM_KR[1]] * cos + ps[:, SM_KRS[0]:SM_KRS[1]] * sin).astype(BF16)


def _prep(x2d, mod3, w_small, bf_pad, g_q, g_kv, w_q_all, w_kv, cos_t, sin_t, seq):
    m, d = x2d.shape
    n_batch = m // seq
    tm = 512
    tiles_per_batch = seq // tm
    hw = MLA_HEADS * LANES
    const = lambda i: (0, 0)
    return pl.pallas_call(
        functools.partial(_prep_kernel, tiles_per_batch=tiles_per_batch),
        grid=(m // tm,),
        in_specs=[
            pl.BlockSpec((tm, d), lambda i: (i, 0)),
            pl.BlockSpec((None, N_ADA, d), lambda i: (i // tiles_per_batch, 0, 0)),
            pl.BlockSpec(w_small.shape, const),
            pl.BlockSpec(bf_pad.shape, const),
            pl.BlockSpec(g_q.shape, const),
            pl.BlockSpec(g_kv.shape, const),
            pl.BlockSpec(w_q_all.shape, const),
            pl.BlockSpec(w_kv.shape, const),
            pl.BlockSpec((tm, LANES), lambda i: (i % tiles_per_batch, 0)),
            pl.BlockSpec((tm, LANES), lambda i: (i % tiles_per_batch, 0)),
        ],
        out_specs=[
            pl.BlockSpec((tm, hw), lambda i: (i, 0)),
            pl.BlockSpec((tm, hw), lambda i: (i, 0)),
            pl.BlockSpec((tm, w_kv.shape[1]), lambda i: (i, 0)),
            pl.BlockSpec((tm, LANES), lambda i: (i, 0)),
            pl.BlockSpec((tm, LANES), lambda i: (i, 0)),
            pl.BlockSpec((None, FOX_HEADS, tm), lambda i: (i // tiles_per_batch, 0, i % tiles_per_batch)),
        ],
        out_shape=[
            jax.ShapeDtypeStruct((m, hw), BF16),
            jax.ShapeDtypeStruct((m, hw), BF16),
            jax.ShapeDtypeStruct((m, w_kv.shape[1]), BF16),
            jax.ShapeDtypeStruct((m, LANES), BF16),
            jax.ShapeDtypeStruct((m, LANES), F32),
            jax.ShapeDtypeStruct((n_batch, FOX_HEADS, seq), F32),
        ],
        scratch_shapes=[pltpu.VMEM((1, LANES), F32)],
        compiler_params=_cparams(("arbitrary",)),
        name="prep",
    )(x2d, mod3, w_small, bf_pad, g_q, g_kv, w_q_all, w_kv, cos_t, sin_t)


def _split3(x):
    hi = x.astype(BF16).astype(F32)
    r = x - hi
    mid = r.astype(BF16).astype(F32)
    lo = (r - mid).astype(BF16).astype(F32)
    return hi, mid, lo


def _online_softmax_step(t, m, acc, vt_aug):
    m_new = jnp.maximum(m, jnp.max(t, axis=0, keepdims=True))
    alpha = jnp.exp2(m - m_new)
    p = jnp.exp2(t - m_new).astype(BF16)
    pv = jnp.dot(vt_aug, p, preferred_element_type=F32)
    return m_new, alpha * acc + pv[0:acc.shape[0], :]


def _causal_mask(t, key0, query0, granule):
    r = lax.broadcasted_iota(jnp.int32, t.shape, 0) + key0
    c = lax.broadcasted_iota(jnp.int32, t.shape, 1) + query0
    return jnp.where(r <= (c | (granule - 1)), t, NEG_BIG)


def _flash_key_major(qi, tile, chains, keys, rhs, vt_aug, d_v, granule):
    heads = sorted({hh for _, hh in chains})
    acc_rows = d_v + 8

    def qk(kbs, n):
        half, hh = chains[n]
        return jnp.dot(kbs[hh], rhs(hh)[:, half * HALF:(half + 1) * HALF], preferred_element_type=F32)

    def full_tile(j, state):
        k0 = pl.multiple_of(j * tile, tile)
        kbs = {hh: keys(hh, k0, tile) for hh in heads}
        ss = [qk(kbs, n) for n in range(len(chains))]
        st = list(state)
        for sb in range(tile // SUB):
            ks = pl.multiple_of(k0 + sb * SUB, SUB)
            for n, (half, hh) in enumerate(chains):
                st[n] = _online_softmax_step(ss[n][sb * SUB:(sb + 1) * SUB, :], *st[n], vt_aug(hh, ks))
        return tuple(st)

    init = tuple((jnp.full((1, HALF), NEG_BIG, F32), jnp.zeros((acc_rows, HALF), F32)) for _ in chains)
    state = lax.fori_loop(0, qi, full_tile, init)

    q0 = pl.multiple_of(qi * tile, tile)
    ss = []
    for n, (half, hh) in enumerate(chains):
        n_keys = (half + 1) * HALF
        ss.append(qk({hh: keys(hh, q0, n_keys)}, n))
    st = list(state)
    for sb in range(tile // SUB):
        ks = pl.multiple_of(q0 + sb * SUB, SUB)
        for n, (half, hh) in enumerate(chains):
            if sb * SUB >= (half + 1) * HALF:
                continue
            t = ss[n][sb * SUB:(sb + 1) * SUB, :]
            if (sb + 1) * SUB > half * HALF:
                t = _causal_mask(t, sb * SUB, half * HALF, granule)
            st[n] = _online_softmax_step(t, *st[n], vt_aug(hh, ks))
    return [acc[0:d_v, :] * (1.0 / acc[d_v:d_v + 1, :]) for (_, acc) in st]


def _fox_kernel(q_ref, k_ref, v_ref, cum_ref, cumt_ref, o_ref, vta_ref, kaug_ref, *, tq):
    hp = pl.program_id(1)
    qi = pl.program_id(2)
    seq = k_ref.shape[0]
    dh = FOX_HEAD_DIM
    n_half = tq // HALF
    aug = 6

    @pl.when(qi == 0)
    def _():
        vt = v_ref[...].T
        tail_rows = lax.broadcasted_iota(jnp.int32, (16, seq), 0)
        tail = jnp.where(tail_rows < 8, 1.0, 0.0).astype(BF16)
        for hh in (0, 1):
            vta_ref[hh, 0:dh, :] = vt[hh * dh:(hh + 1) * dh, :]
            vta_ref[hh, dh:dh + 16, :] = tail
        lane = lax.broadcasted_iota(jnp.int32, (seq, LANES), 1)
        cum = cum_ref[...]
        kaug = jnp.zeros((seq, LANES), F32)
        for hh in (0, 1):
            col = jnp.sum(jnp.where(lane == 2 * hp + hh, cum, 0.0), axis=1, keepdims=True)
            neg = jnp.broadcast_to(-col, (seq, LANES))
            for idx, piece in enumerate(_split3(neg)):
                kaug = jnp.where(lane == aug * hh + idx, piece, kaug)
            kaug = jnp.where((lane >= aug * hh + 3) & (lane < aug * hh + 6), 1.0, kaug)
        kaug_ref[...] = kaug.astype(BF16)

    q0 = pl.multiple_of(qi * tq, tq)
    qt = q_ref[...].T
    row = lax.broadcasted_iota(jnp.int32, (LANES, tq), 0)
    rhs = []
    for hh in (0, 1):
        qm = jnp.where((row >= hh * dh) & (row < (hh + 1) * dh), qt, jnp.zeros_like(qt))
        ci = cumt_ref[hh:hh + 1, pl.ds(q0, tq)]
        qa = jnp.where((row >= aug * hh) & (row < aug * hh + 3), 1.0, 0.0)
        for idx, piece in enumerate(_split3(ci)):
            qa = jnp.where(row == aug * hh + 3 + idx, jnp.broadcast_to(piece, (LANES, tq)), qa)
        rhs.append(jnp.concatenate([qm, qa.astype(BF16)], axis=0))
    chains = [(half, hh) for half in range(n_half) for hh in (0, 1)]

    def keys(hh, k0, n):
        return jnp.concatenate([k_ref[pl.ds(k0, n), :], kaug_ref[pl.ds(k0, n), :]], axis=1)

    outs = _flash_key_major(qi, tq, chains, keys, lambda hh: rhs[hh],
                            lambda hh, ks: vta_ref[hh, :, pl.ds(ks, SUB)], dh, granule=1)
    o_t = jnp.concatenate(
        [jnp.concatenate([outs[half * 2 + hh] for half in range(n_half)], axis=1) for hh in (0, 1)], axis=0)
    o_ref[...] = o_t.T.astype(o_ref.dtype)


def _fox_attention(proj, cum, cumt4, seq, q_col0):
    m = proj.shape[0]
    n_batch = m // seq
    tq = 512
    nq = seq // tq
    n_pairs = FOX_HEADS // 2
    qb = q_col0 // LANES
    return pl.pallas_call(
        functools.partial(_fox_kernel, tq=tq),
        grid=(n_batch, n_pairs, nq),
        in_specs=[
            pl.BlockSpec((tq, LANES), lambda b, hp, qi: (b * nq + qi, qb + hp)),
            pl.BlockSpec((seq, LANES), lambda b, hp, qi: (b, qb + n_pairs + hp)),
            pl.BlockSpec((seq, LANES), lambda b, hp, qi: (b, qb + 2 * n_pairs + hp)),
            pl.BlockSpec((seq, LANES), lambda b, hp, qi: (b, 0)),
            pl.BlockSpec((None, None, 2, seq), lambda b, hp, qi: (b, hp, 0, 0)),
        ],
        out_specs=pl.BlockSpec((tq, LANES), lambda b, hp, qi: (b * nq + qi, hp)),
        out_shape=jax.ShapeDtypeStruct((m, FOX_WIDTH), BF16),
        scratch_shapes=[
            pltpu.VMEM((2, FOX_HEAD_DIM + 16, seq), BF16),
            pltpu.VMEM((seq, LANES), BF16),
        ],
        compiler_params=_cparams(("parallel", "parallel", "arbitrary")),
        name="fox_attn",
    )(proj, proj, proj, cum, cumt4)


def _mla_kernel(qn_ref, qr_ref, kv_ref, kr_ref, o_ref, vta_ref, *, tq):
    qi = pl.program_id(2)
    seq = kv_ref.shape[0]
    dv = MLA_V_DIM
    n_half = tq // HALF

    @pl.when(qi == 0)
    def _():
        tail_rows = lax.broadcasted_iota(jnp.int32, (16, seq), 0)
        tail = jnp.where(tail_rows < 8, 1.0, 0.0).astype(BF16)
        for hh in (0, 1):
            vta_ref[hh, 0:dv, :] = kv_ref[:, (2 * hh + 1) * LANES:(2 * hh + 2) * LANES].T
            vta_ref[hh, dv:dv + 16, :] = tail

    rhs = [jnp.concatenate([qn_ref[:, hh * LANES:(hh + 1) * LANES],
                            qr_ref[:, hh * LANES:(hh + 1) * LANES]], axis=1).T for hh in (0, 1)]
    chains = [(half, hh) for half in range(n_half) for hh in (0, 1)]

    def keys(hh, k0, n):
        return jnp.concatenate([kv_ref[pl.ds(k0, n), 2 * hh * LANES:(2 * hh + 1) * LANES],
                                kr_ref[pl.ds(k0, n), :]], axis=1)

    outs = _flash_key_major(qi, tq, chains, keys, lambda hh: rhs[hh],
                            lambda hh, ks: vta_ref[hh, :, pl.ds(ks, SUB)], dv, granule=CHUNK)
    o_t = jnp.concatenate(
        [jnp.concatenate([outs[half * 2 + hh] for half in range(n_half)], axis=1) for hh in (0, 1)], axis=0)
    o_ref[...] = o_t.T.astype(o_ref.dtype)


def _mla_attention(qn, qr, kv, kr, seq):
    m = qn.shape[0]
    n_batch = m // seq
    tq = 512
    nq = seq // tq
    pair = 2 * LANES
    return pl.pallas_call(
        functools.partial(_mla_kernel, tq=tq),
        grid=(n_batch, MLA_HEADS // 2, nq),
        in_specs=[
            pl.BlockSpec((tq, pair), lambda b, hp, qi: (b * nq + qi, hp)),
            pl.BlockSpec((tq, pair), lambda b, hp, qi: (b * nq + qi, hp)),
            pl.BlockSpec((seq, 2 * pair), lambda b, hp, qi: (b, hp)),
            pl.BlockSpec((seq, LANES), lambda b, hp, qi: (b, 0)),
        ],
        out_specs=pl.BlockSpec((tq, pair), lambda b, hp, qi: (b * nq + qi, hp)),
        out_shape=jax.ShapeDtypeStruct((m, MLA_HEADS * MLA_V_DIM), BF16),
        scratch_shapes=[pltpu.VMEM((2, MLA_V_DIM + 16, seq), BF16)],
        compiler_params=_cparams(("parallel", "parallel", "arbitrary")),
        name="mla_attn",
    )(qn, qr, kv, kr)


def _mix_kernel(yf_ref, ym_ref, gf_ref, gm_ref, x_ref, mod_ref, wbf_ref, wbm_ref, wo_ref, g_ref, b_ref,
                o_ref, mg_ref, *, n_chunk):
    d = o_ref.shape[1]
    yf = yf_ref[...]
    ym = ym_ref[...]
    for c in range(d // n_chunk):
        cols = slice(c * n_chunk, (c + 1) * n_chunk)
        a = jnp.dot(yf, wbf_ref[:, cols], preferred_element_type=F32)
        b = jnp.dot(ym, wbm_ref[:, cols], preferred_element_type=F32)
        merged = (jax.nn.sigmoid(gf_ref[:, cols].astype(F32)) * a
                  + jax.nn.sigmoid(gm_ref[:, cols].astype(F32)) * b)
        mg_ref[:, cols] = merged.astype(BF16)
    mix = jnp.dot(mg_ref[...], wo_ref[...], preferred_element_type=F32)
    gate = mod_ref[2:3, :]
    o_ref[...] = _layer_norm(ALPHA * x_ref[...] + gate * mix, g_ref[...], b_ref[...])


def _mix(y_fox, y_mla, proj, x2d, mod3, w_bf, w_bm, w_o, ln_g, ln_b, seq):
    m, d = x2d.shape
    tm = 512
    tiles_per_batch = seq // tm
    const = lambda i: (0, 0)
    resident = dict(pipeline_mode=pl.Buffered(1))
    return pl.pallas_call(
        functools.partial(_mix_kernel, n_chunk=512),
        grid=(m // tm,),
        in_specs=[
            pl.BlockSpec((tm, y_fox.shape[1]), lambda i: (i, 0)),
            pl.BlockSpec((tm, y_mla.shape[1]), lambda i: (i, 0)),
            pl.BlockSpec((tm, d), lambda i: (i, 0)),
            pl.BlockSpec((tm, d), lambda i: (i, 1)),
            pl.BlockSpec((tm, d), lambda i: (i, 0)),
            pl.BlockSpec((None, N_ADA, d), lambda i: (i // tiles_per_batch, 0, 0)),
            pl.BlockSpec(w_bf.shape, const, **resident),
            pl.BlockSpec(w_bm.shape, const, **resident),
            pl.BlockSpec(w_o.shape, const, **resident),
            pl.BlockSpec((1, d), const),
            pl.BlockSpec((1, d), const),
        ],
        out_specs=pl.BlockSpec((tm, d), lambda i: (i, 0)),
        out_shape=jax.ShapeDtypeStruct((m, d), F32),
        scratch_shapes=[pltpu.VMEM((tm, d), BF16)],
        compiler_params=_cparams(("parallel",)),
        name="mix",
    )(y_fox, y_mla, proj, proj, x2d, mod3, w_bf, w_bm, w_o, ln_g, ln_b)


def _mlp_kernel(x_ref, mod_ref, wu_ref, wd_ref, g_ref, b_ref, o_ref, u_ref):
    f = pl.program_id(1)

    @pl.when(f == 0)
    def _():
        shift = mod_ref[3:4, :]
        scale = mod_ref[4:5, :]
        u_ref[...] = (x_ref[...] * (1.0 + scale) + shift).astype(BF16)
        o_ref[...] = jnp.zeros_like(o_ref)

    z = jnp.dot(u_ref[...], wu_ref[...], preferred_element_type=F32)
    h = jnp.square(jnp.maximum(z, 0.0)).astype(BF16)
    o_ref[...] += jnp.dot(h, wd_ref[...], preferred_element_type=F32)

    @pl.when(f == pl.num_programs(1) - 1)
    def _():
        gate = mod_ref[5:6, :]
        o_ref[...] = _layer_norm(ALPHA * x_ref[...] + gate * o_ref[...], g_ref[...], b_ref[...])


def _mlp(x1, mod3, w_up, w_down, ln_g, ln_b, seq):
    m, d = x1.shape
    d_ff = w_up.shape[1]
    tm, tf = 1024, 512
    tiles_per_batch = seq // tm
    return pl.pallas_call(
        _mlp_kernel,
        grid=(m // tm, d_ff // tf),
        in_specs=[
            pl.BlockSpec((tm, d), lambda i, f: (i, 0)),
            pl.BlockSpec((None, N_ADA, d), lambda i, f: (i // tiles_per_batch, 0, 0)),
            pl.BlockSpec((d, tf), lambda i, f: (0, f)),
            pl.BlockSpec((tf, d), lambda i, f: (f, 0)),
            pl.BlockSpec((1, d), lambda i, f: (0, 0)),
            pl.BlockSpec((1, d), lambda i, f: (0, 0)),
        ],
        out_specs=pl.BlockSpec((tm, d), lambda i, f: (i, 0)),
        out_shape=jax.ShapeDtypeStruct((m, d), F32),
        scratch_shapes=[pltpu.VMEM((tm, d), BF16)],
        compiler_params=_cparams(("parallel", "arbitrary")),
        name="mlp",
    )(x1, mod3, w_up, w_down, ln_g, ln_b)


def _pad_cols(w, width):
    return jnp.pad(w, ((0, 0), (0, width - w.shape[1])))


def _swap_halves(w):
    half = w.shape[-1] // 2
    return jnp.concatenate([w[..., half:], w[..., :half]], axis=-1)


def _layout_in_weights(w_in, d):
    c0 = 3 * FOX_WIDTH
    c1 = c0 + FOX_HEADS
    c2 = c1 + MLA_Q_LORA
    c3 = c2 + MLA_KV_LORA
    c4 = c3 + MLA_ROPE_DIM
    c5 = c4 + d
    w_main = jnp.concatenate([w_in[:, c4:c5], w_in[:, c5:c5 + d], w_in[:, :c0]], axis=1).astype(BF16)
    w_kr = w_in[:, c3:c4]
    w_small = jnp.concatenate([
        _pad_cols(w_in[:, c0:c1], LANES), w_in[:, c1:c2], w_in[:, c2:c3],
        _pad_cols(w_kr, LANES), _pad_cols(_swap_halves(w_kr), LANES)], axis=1).astype(BF16)
    return w_main, w_small


def _layout_q_up(w_q_up):
    r = w_q_up.shape[0]
    w3 = w_q_up.reshape(r, MLA_HEADS, MLA_QK_DIM)
    w_nope = w3[:, :, :MLA_NOPE_DIM].reshape(r, MLA_HEADS * MLA_NOPE_DIM)
    w_rope = w3[:, :, MLA_NOPE_DIM:]
    pad = ((0, 0), (0, 0), (0, LANES - MLA_ROPE_DIM))
    w_r = jnp.pad(w_rope, pad).reshape(r, MLA_HEADS * LANES)
    w_rs = jnp.pad(_swap_halves(w_rope), pad).reshape(r, MLA_HEADS * LANES)
    return jnp.concatenate([w_nope, w_r, w_rs], axis=1).astype(BF16)


def _rope_tables(seq):
    pos = jnp.arange(seq, dtype=F32)
    inv_freq = ROPE_THETA ** (-jnp.arange(0, MLA_ROPE_DIM, 2, dtype=F32) / MLA_ROPE_DIM)
    ang = pos[:, None] * inv_freq[None, :]
    cos, sin = jnp.cos(ang), jnp.sin(ang)
    cos_t = _pad_cols(jnp.concatenate([cos, cos], axis=1), LANES)
    sin_t = _pad_cols(jnp.concatenate([-sin, sin], axis=1), LANES)
    return cos_t, sin_t


def kernel(x, c, w_ada, b_ada, w_in, b_forget, g_q_norm, w_q_up, g_kv_norm, w_kv_up, w_branch_fox,
           w_branch_mla, w_out, ln1_g, ln1_b, w_mlp_up, w_mlp_down, ln2_g, ln2_b):
    n_batch, seq, d = x.shape
    assert w_ada.shape[0] == DEPTH
    m = n_batch * seq
    x2d = x.reshape(m, d)

    mod3 = _ada(c, w_ada[0], b_ada[0]).reshape(n_batch, N_ADA, d)

    w_main, w_small = _layout_in_weights(w_in[0], d)
    proj = _inproj(x2d, mod3, w_main, seq)

    cos_t, sin_t = _rope_tables(seq)
    bf_pad = _pad_cols(b_forget[0].reshape(1, FOX_HEADS), LANES)
    qn, qr, kv, kr, cum, cumt = _prep(
        x2d, mod3, w_small, bf_pad, g_q_norm[0].reshape(1, -1), g_kv_norm[0].reshape(1, -1),
        _layout_q_up(w_q_up[0]), w_kv_up[0].astype(BF16), cos_t, sin_t, seq)

    y_fox = _fox_attention(proj, cum, cumt.reshape(n_batch, FOX_HEADS // 2, 2, seq), seq, q_col0=2 * d)
    y_mla = _mla_attention(qn, qr, kv, kr, seq)

    x1 = _mix(y_fox, y_mla, proj, x2d, mod3, w_branch_fox[0].astype(BF16), w_branch_mla[0].astype(BF16),
              w_out[0].astype(BF16), ln1_g[0].reshape(1, d), ln1_b[0].reshape(1, d), seq)
    x2 = _mlp(x1, mod3, w_mlp_up[0].astype(BF16), w_mlp_down[0].astype(BF16),
              ln2_g[0].reshape(1, d), ln2_b[0].reshape(1, d), seq)
    return x2.reshape(n_batch, seq, d)
```

```python
import functools
import math

import jax
import jax.numpy as jnp
from jax import lax
from jax.experimental import pallas as pl
from jax.experimental.pallas import tpu as pltpu

F32 = jnp.float32
BF16 = jnp.bfloat16

FOX_HEADS = 16
FOX_HEAD_DIM = 64
FOX_WIDTH = FOX_HEADS * FOX_HEAD_DIM
MLA_HEADS = 8
MLA_NOPE_DIM = 128
MLA_ROPE_DIM = 64
MLA_V_DIM = 128
MLA_Q_LORA = 512
MLA_KV_LORA = 256
MLA_QK_DIM = MLA_NOPE_DIM + MLA_ROPE_DIM
CHUNK = 64
ROPE_THETA = 10000.0
LN_EPS = 1e-5
RMS_EPS = 1e-6
N_ADA = 6
DEPTH = 1
ALPHA = (2.0 * DEPTH) ** 0.25

LANES = 128
LOG2E = math.log2(math.e)
FOX_QSCALE = FOX_HEAD_DIM ** -0.5 * LOG2E
MLA_QSCALE = MLA_QK_DIM ** -0.5 * LOG2E
NEG_BIG = -1e30
VMEM_LIMIT = 56 * 1024 * 1024

SM_F = (0, 128)
SM_CQ = (128, 640)
SM_CKV = (640, 896)
SM_KR = (896, 1024)
SM_KRS = (1024, 1152)
SM_WIDTH = 1152
SUB = 128
HALF = 256


def _cparams(semantics):
    return pltpu.CompilerParams(dimension_semantics=semantics, vmem_limit_bytes=VMEM_LIMIT)


def _layer_norm(y, g, b):
    mu = jnp.mean(y, axis=-1, keepdims=True)
    yc = y - mu
    var = jnp.mean(yc * yc, axis=-1, keepdims=True)
    return yc * lax.rsqrt(var + LN_EPS) * g + b


def _rms_norm(c, g):
    return c * lax.rsqrt(jnp.mean(c * c, axis=-1, keepdims=True) + RMS_EPS) * g


def _ada_kernel(ct_ref, w_ref, b_ref, o_ref, *, n_batch, k_chunk):
    d = w_ref.shape[0]
    ct = ct_ref[...]
    s = ct * jax.nn.sigmoid(ct)
    accs = [b_ref[...] for _ in range(n_batch)]
    for kc in range(d // k_chunk):
        rows = slice(kc * k_chunk, (kc + 1) * k_chunk)
        w = w_ref[rows, :]
        for b in range(n_batch):
            accs[b] = accs[b] + jnp.sum(w * s[rows, b:b + 1], axis=0, keepdims=True)
    for b in range(n_batch):
        o_ref[b:b + 1, :] = accs[b]


def _ada(c, w_ada, b_ada):
    n_batch, d = c.shape
    n = w_ada.shape[1]
    tn = 1024
    return pl.pallas_call(
        functools.partial(_ada_kernel, n_batch=n_batch, k_chunk=256),
        grid=(n // tn,),
        in_specs=[
            pl.BlockSpec((d, n_batch), lambda j: (0, 0)),
            pl.BlockSpec((d, tn), lambda j: (0, j)),
            pl.BlockSpec((1, tn), lambda j: (0, j)),
        ],
        out_specs=pl.BlockSpec((n_batch, tn), lambda j: (0, j)),
        out_shape=jax.ShapeDtypeStruct((n_batch, n), F32),
        compiler_params=_cparams(("arbitrary",)),
        name="ada",
    )(c.T, w_ada, b_ada.reshape(1, n))


def _inproj_kernel(x_ref, mod_ref, w_ref, o_ref, u_ref, *, q_tile):
    j = pl.program_id(1)

    @pl.when(j == 0)
    def _():
        shift = mod_ref[0:1, :]
        scale = mod_ref[1:2, :]
        u_ref[...] = (x_ref[...] * (1.0 + scale) + shift).astype(BF16)

    acc = jnp.dot(u_ref[...], w_ref[...], preferred_element_type=F32)

    @pl.when(j == q_tile)
    def _():
        o_ref[...] = (acc * FOX_QSCALE).astype(o_ref.dtype)

    @pl.when(j != q_tile)
    def _():
        o_ref[...] = acc.astype(o_ref.dtype)


def _inproj(x2d, mod3, w_main, seq):
    m, d = x2d.shape
    n = w_main.shape[1]
    tm, tn = 1024, 1024
    tiles_per_batch = seq // tm
    return pl.pallas_call(
        functools.partial(_inproj_kernel, q_tile=4),
        grid=(m // tm, n // tn),
        in_specs=[
            pl.BlockSpec((tm, d), lambda i, j: (i, 0)),
            pl.BlockSpec((None, N_ADA, d), lambda i, j: (i // tiles_per_batch, 0, 0)),
            pl.BlockSpec((d, tn), lambda i, j: (0, j)),
        ],
        out_specs=pl.BlockSpec((tm, tn), lambda i, j: (i, j)),
        out_shape=jax.ShapeDtypeStruct((m, n), BF16),
        scratch_shapes=[pltpu.VMEM((tm, d), BF16)],
        compiler_params=_cparams(("parallel", "arbitrary")),
        name="inproj",
    )(x2d, mod3, w_main)


def _log_sigmoid(x):
    return jnp.minimum(x, 0.0) - jnp.log1p(jnp.exp(-jnp.abs(x)))


def _prep_kernel(x_ref, mod_ref, wsm_ref, bf_ref, gq_ref, gkv_ref, wq_ref, wkv_ref, cos_ref, sin_ref,
                 qn_ref, qr_ref, kv_ref, kr_ref, cum_ref, cumt_ref, carry_ref, *, tiles_per_batch):
    i = pl.program_id(0)
    tm = x_ref.shape[0]
    shift = mod_ref[0:1, :]
    scale = mod_ref[1:2, :]
    u = (x_ref[...] * (1.0 + scale) + shift).astype(BF16)
    ps = jnp.dot(u, wsm_ref[...], preferred_element_type=F32)
    cos = cos_ref[...]
    sin = sin_ref[...]

    logf = _log_sigmoid(ps[:, SM_F[0]:SM_F[1]] + bf_ref[...]) * LOG2E
    hi = logf.astype(BF16)
    r1 = logf - hi.astype(F32)
    mid = r1.astype(BF16)
    lo = (r1 - mid.astype(F32)).astype(BF16)
    row = lax.broadcasted_iota(jnp.int32, (tm, tm), 0)
    col = lax.broadcasted_iota(jnp.int32, (tm, tm), 1)
    tri = (row >= col).astype(BF16)
    cs = (jnp.dot(tri, hi, preferred_element_type=F32)
          + jnp.dot(tri, mid, preferred_element_type=F32)
          + jnp.dot(tri, lo, preferred_element_type=F32))

    @pl.when(i % tiles_per_batch == 0)
    def _():
        carry_ref[...] = jnp.zeros_like(carry_ref)

    cum = cs + carry_ref[...]
    carry_ref[...] = cum[tm - 1:tm, :]
    cum_ref[...] = cum
    cumt_ref[...] = cum.T[0:FOX_HEADS, :]

    cqn = _rms_norm(ps[:, SM_CQ[0]:SM_CQ[1]], gq_ref[...]).astype(BF16)
    qa = jnp.dot(cqn, wq_ref[...], preferred_element_type=F32)
    hw = MLA_HEADS * LANES
    qn_ref[...] = (qa[:, 0:hw] * MLA_QSCALE).astype(BF16)
    for h in range(MLA_HEADS):
        a = qa[:, hw + h * LANES: hw + (h + 1) * LANES]
        b = qa[:, 2 * hw + h * LANES: 2 * hw + (h + 1) * LANES]
        qr_ref[:, h * LANES:(h + 1) * LANES] = ((a * cos + b * sin) * MLA_QSCALE).astype(BF16)

    ckvn = _rms_norm(ps[:, SM_CKV[0]:SM_CKV[1]], gkv_ref[...]).astype(BF16)
    kv_ref[...] = jnp.dot(ckvn, wkv_ref[...], preferred_element_type=F32).astype(BF16)
    kr_ref[...] = (ps[:, SM_KR[0]:SM_KR[1]] * cos + ps[:, SM_KRS[0]:SM_KRS[1]] * sin).astype(BF16)


def _prep(x2d, mod3, w_small, bf_pad, g_q, g_kv, w_q_all, w_kv, cos_t, sin_t, seq):
    m, d = x2d.shape
    n_batch = m // seq
    tm = 512
    tiles_per_batch = seq // tm
    hw = MLA_HEADS * LANES
    const = lambda i: (0, 0)
    return pl.pallas_call(
        functools.partial(_prep_kernel, tiles_per_batch=tiles_per_batch),
        grid=(m // tm,),
        in_specs=[
            pl.BlockSpec((tm, d), lambda i: (i, 0)),
            pl.BlockSpec((None, N_ADA, d), lambda i: (i // tiles_per_batch, 0, 0)),
            pl.BlockSpec(w_small.shape, const),
            pl.BlockSpec(bf_pad.shape, const),
            pl.BlockSpec(g_q.shape, const),
            pl.BlockSpec(g_kv.shape, const),
            pl.BlockSpec(w_q_all.shape, const),
            pl.BlockSpec(w_kv.shape, const),
            pl.BlockSpec((tm, LANES), lambda i: (i % tiles_per_batch, 0)),
            pl.BlockSpec((tm, LANES), lambda i: (i % tiles_per_batch, 0)),
        ],
        out_specs=[
            pl.BlockSpec((tm, hw), lambda i: (i, 0)),
            pl.BlockSpec((tm, hw), lambda i: (i, 0)),
            pl.BlockSpec((tm, w_kv.shape[1]), lambda i: (i, 0)),
            pl.BlockSpec((tm, LANES), lambda i: (i, 0)),
            pl.BlockSpec((tm, LANES), lambda i: (i, 0)),
            pl.BlockSpec((None, FOX_HEADS, tm), lambda i: (i // tiles_per_batch, 0, i % tiles_per_batch)),
        ],
        out_shape=[
            jax.ShapeDtypeStruct((m, hw), BF16),
            jax.ShapeDtypeStruct((m, hw), BF16),
            jax.ShapeDtypeStruct((m, w_kv.shape[1]), BF16),
            jax.ShapeDtypeStruct((m, LANES), BF16),
            jax.ShapeDtypeStruct((m, LANES), F32),
            jax.ShapeDtypeStruct((n_batch, FOX_HEADS, seq), F32),
        ],
        scratch_shapes=[pltpu.VMEM((1, LANES), F32)],
        compiler_params=_cparams(("arbitrary",)),
        name="prep",
    )(x2d, mod3, w_small, bf_pad, g_q, g_kv, w_q_all, w_kv, cos_t, sin_t)


def _split3(x):
    hi = x.astype(BF16).astype(F32)
    r = x - hi
    mid = r.astype(BF16).astype(F32)
    lo = (r - mid).astype(BF16).astype(F32)
    return hi, mid, lo


def _online_softmax_step(t, m, acc, vt_aug):
    m_new = jnp.maximum(m, jnp.max(t, axis=0, keepdims=True))
    alpha = jnp.exp2(m - m_new)
    p = jnp.exp2(t - m_new).astype(BF16)
    pv = jnp.dot(vt_aug, p, preferred_element_type=F32)
    return m_new, alpha * acc + pv[0:acc.shape[0], :]


def _causal_mask(t, key0, query0, granule):
    r = lax.broadcasted_iota(jnp.int32, t.shape, 0) + key0
    c = lax.broadcasted_iota(jnp.int32, t.shape, 1) + query0
    return jnp.where(r <= (c | (granule - 1)), t, NEG_BIG)


def _flash_key_major(qi, tile, chains, keys, rhs, vt_aug, d_v, granule):
    acc_rows = d_v + 8
    n_rounds = tile // SUB
    n_pairs = len(chains) // 2
    issue_rounds = {g * n_rounds // n_pairs: (2 * g, 2 * g + 1) for g in range(n_pairs)}

    def qk_of(k0, is_diag):
        loaded = {}

        def product(n):
            half, hh = chains[n]
            n_keys = (half + 1) * HALF if is_diag else tile
            if (hh, n_keys) not in loaded:
                loaded[hh, n_keys] = keys(hh, k0, n_keys)
            return jnp.dot(loaded[hh, n_keys], rhs(hh)[:, half * HALF:(half + 1) * HALF],
                           preferred_element_type=F32)
        return product

    def visit(state, tiles):
        st = list(state)
        first = qk_of(*tiles[0])
        ss = [first(n) for n in range(len(chains))]
        for ti, (k0, is_diag) in enumerate(tiles):
            nxt = qk_of(*tiles[ti + 1]) if ti + 1 < len(tiles) else None
            ss_next = [None] * len(chains)
            for sb in range(n_rounds):
                if nxt is not None and sb in issue_rounds:
                    for n in issue_rounds[sb]:
                        ss_next[n] = nxt(n)
                ks = pl.multiple_of(k0 + sb * SUB, SUB)
                for n, (half, hh) in enumerate(chains):
                    if is_diag and sb * SUB >= (half + 1) * HALF:
                        continue
                    t = ss[n][sb * SUB:(sb + 1) * SUB, :]
                    if is_diag and (sb + 1) * SUB > half * HALF:
                        t = _causal_mask(t, sb * SUB, half * HALF, granule)
                    st[n] = _online_softmax_step(t, *st[n], vt_aug(hh, ks))
            ss = ss_next
        return tuple(st)

    def two_full(j, state):
        k0 = pl.multiple_of(2 * j * tile, tile)
        return visit(state, [(k0, False), (pl.multiple_of(k0 + tile, tile), False)])

    init = tuple((jnp.full((1, HALF), NEG_BIG, F32), jnp.zeros((acc_rows, HALF), F32)) for _ in chains)
    state = lax.fori_loop(0, qi // 2, two_full, init)
    q0 = pl.multiple_of(qi * tile, tile)
    state = lax.cond(
        qi % 2 == 1,
        lambda st: visit(st, [(pl.multiple_of(q0 - tile, tile), False), (q0, True)]),
        lambda st: visit(st, [(q0, True)]),
        state)
    return [acc[0:d_v, :] * (1.0 / acc[d_v:d_v + 1, :]) for (_, acc) in state]


def _fox_kernel(q_ref, k_ref, v_ref, cum_ref, cumt_ref, o_ref, vta_ref, kaug_ref, *, tq):
    hp = pl.program_id(1)
    qi = pl.program_id(2)
    seq = k_ref.shape[0]
    dh = FOX_HEAD_DIM
    n_half = tq // HALF
    aug = 6

    @pl.when(qi == 0)
    def _():
        vt = v_ref[...].T
        tail_rows = lax.broadcasted_iota(jnp.int32, (16, seq), 0)
        tail = jnp.where(tail_rows < 8, 1.0, 0.0).astype(BF16)
        for hh in (0, 1):
            vta_ref[hh, 0:dh, :] = vt[hh * dh:(hh + 1) * dh, :]
            vta_ref[hh, dh:dh + 16, :] = tail
        lane = lax.broadcasted_iota(jnp.int32, (seq, LANES), 1)
        cum = cum_ref[...]
        kaug = jnp.zeros((seq, LANES), F32)
        for hh in (0, 1):
            col = jnp.sum(jnp.where(lane == 2 * hp + hh, cum, 0.0), axis=1, keepdims=True)
            neg = jnp.broadcast_to(-col, (seq, LANES))
            for idx, piece in enumerate(_split3(neg)):
                kaug = jnp.where(lane == aug * hh + idx, piece, kaug)
            kaug = jnp.where((lane >= aug * hh + 3) & (lane < aug * hh + 6), 1.0, kaug)
        kaug_ref[...] = kaug.astype(BF16)

    q0 = pl.multiple_of(qi * tq, tq)
    qt = q_ref[...].T
    row = lax.broadcasted_iota(jnp.int32, (LANES, tq), 0)
    rhs = []
    for hh in (0, 1):
        qm = jnp.where((row >= hh * dh) & (row < (hh + 1) * dh), qt, jnp.zeros_like(qt))
        ci = cumt_ref[hh:hh + 1, pl.ds(q0, tq)]
        qa = jnp.where((row >= aug * hh) & (row < aug * hh + 3), 1.0, 0.0)
        for idx, piece in enumerate(_split3(ci)):
            qa = jnp.where(row == aug * hh + 3 + idx, jnp.broadcast_to(piece, (LANES, tq)), qa)
        rhs.append(jnp.concatenate([qm, qa.astype(BF16)], axis=0))
    chains = [(half, hh) for half in range(n_half) for hh in (0, 1)]

    def keys(hh, k0, n):
        return jnp.concatenate([k_ref[pl.ds(k0, n), :], kaug_ref[pl.ds(k0, n), :]], axis=1)

    outs = _flash_key_major(qi, tq, chains, keys, lambda hh: rhs[hh],
                            lambda hh, ks: vta_ref[hh, :, pl.ds(ks, SUB)], dh, granule=1)
    o_t = jnp.concatenate(
        [jnp.concatenate([outs[half * 2 + hh] for half in range(n_half)], axis=1) for hh in (0, 1)], axis=0)
    o_ref[...] = o_t.T.astype(o_ref.dtype)


def _fox_attention(proj, cum, cumt4, seq, q_col0):
    m = proj.shape[0]
    n_batch = m // seq
    tq = 512
    nq = seq // tq
    n_pairs = FOX_HEADS // 2
    qb = q_col0 // LANES
    return pl.pallas_call(
        functools.partial(_fox_kernel, tq=tq),
        grid=(n_batch, n_pairs, nq),
        in_specs=[
            pl.BlockSpec((tq, LANES), lambda b, hp, qi: (b * nq + qi, qb + hp)),
            pl.BlockSpec((seq, LANES), lambda b, hp, qi: (b, qb + n_pairs + hp)),
            pl.BlockSpec((seq, LANES), lambda b, hp, qi: (b, qb + 2 * n_pairs + hp)),
            pl.BlockSpec((seq, LANES), lambda b, hp, qi: (b, 0)),
            pl.BlockSpec((None, None, 2, seq), lambda b, hp, qi: (b, hp, 0, 0)),
        ],
        out_specs=pl.BlockSpec((tq, LANES), lambda b, hp, qi: (b * nq + qi, hp)),
        out_shape=jax.ShapeDtypeStruct((m, FOX_WIDTH), BF16),
        scratch_shapes=[
            pltpu.VMEM((2, FOX_HEAD_DIM + 16, seq), BF16),
            pltpu.VMEM((seq, LANES), BF16),
        ],
        compiler_params=_cparams(("parallel", "parallel", "arbitrary")),
        name="fox_attn",
    )(proj, proj, proj, cum, cumt4)


def _mla_kernel(qn_ref, qr_ref, kv_ref, kr_ref, o_ref, vta_ref, *, tq):
    qi = pl.program_id(2)
    seq = kv_ref.shape[0]
    dv = MLA_V_DIM
    n_half = tq // HALF

    @pl.when(qi == 0)
    def _():
        tail_rows = lax.broadcasted_iota(jnp.int32, (16, seq), 0)
        tail = jnp.where(tail_rows < 8, 1.0, 0.0).astype(BF16)
        for hh in (0, 1):
            vta_ref[hh, 0:dv, :] = kv_ref[:, (2 * hh + 1) * LANES:(2 * hh + 2) * LANES].T
            vta_ref[hh, dv:dv + 16, :] = tail

    rhs = [jnp.concatenate([qn_ref[:, hh * LANES:(hh + 1) * LANES],
                            qr_ref[:, hh * LANES:(hh + 1) * LANES]], axis=1).T for hh in (0, 1)]
    chains = [(half, hh) for half in range(n_half) for hh in (0, 1)]

    def keys(hh, k0, n):
        return jnp.concatenate([kv_ref[pl.ds(k0, n), 2 * hh * LANES:(2 * hh + 1) * LANES],
                                kr_ref[pl.ds(k0, n), :]], axis=1)

    outs = _flash_key_major(qi, tq, chains, keys, lambda hh: rhs[hh],
                            lambda hh, ks: vta_ref[hh, :, pl.ds(ks, SUB)], dv, granule=CHUNK)
    o_t = jnp.concatenate(
        [jnp.concatenate([outs[half * 2 + hh] for half in range(n_half)], axis=1) for hh in (0, 1)], axis=0)
    o_ref[...] = o_t.T.astype(o_ref.dtype)


def _mla_attention(qn, qr, kv, kr, seq):
    m = qn.shape[0]
    n_batch = m // seq
    tq = 512
    nq = seq // tq
    pair = 2 * LANES
    return pl.pallas_call(
        functools.partial(_mla_kernel, tq=tq),
        grid=(n_batch, MLA_HEADS // 2, nq),
        in_specs=[
            pl.BlockSpec((tq, pair), lambda b, hp, qi: (b * nq + qi, hp)),
            pl.BlockSpec((tq, pair), lambda b, hp, qi: (b * nq + qi, hp)),
            pl.BlockSpec((seq, 2 * pair), lambda b, hp, qi: (b, hp)),
            pl.BlockSpec((seq, LANES), lambda b, hp, qi: (b, 0)),
        ],
        out_specs=pl.BlockSpec((tq, pair), lambda b, hp, qi: (b * nq + qi, hp)),
        out_shape=jax.ShapeDtypeStruct((m, MLA_HEADS * MLA_V_DIM), BF16),
        scratch_shapes=[pltpu.VMEM((2, MLA_V_DIM + 16, seq), BF16)],
        compiler_params=_cparams(("parallel", "parallel", "arbitrary")),
        name="mla_attn",
    )(qn, qr, kv, kr)


def _mix_kernel(yf_ref, ym_ref, gf_ref, gm_ref, x_ref, mod_ref, wbf_ref, wbm_ref, wo_ref, g_ref, b_ref,
                o_ref, mg_ref, *, n_chunk):
    d = o_ref.shape[1]
    yf = yf_ref[...]
    ym = ym_ref[...]
    for c in range(d // n_chunk):
        cols = slice(c * n_chunk, (c + 1) * n_chunk)
        a = jnp.dot(yf, wbf_ref[:, cols], preferred_element_type=F32)
        b = jnp.dot(ym, wbm_ref[:, cols], preferred_element_type=F32)
        merged = (jax.nn.sigmoid(gf_ref[:, cols].astype(F32)) * a
                  + jax.nn.sigmoid(gm_ref[:, cols].astype(F32)) * b)
        mg_ref[:, cols] = merged.astype(BF16)
    mix = jnp.dot(mg_ref[...], wo_ref[...], preferred_element_type=F32)
    gate = mod_ref[2:3, :]
    o_ref[...] = _layer_norm(ALPHA * x_ref[...] + gate * mix, g_ref[...], b_ref[...])


def _mix(y_fox, y_mla, proj, x2d, mod3, w_bf, w_bm, w_o, ln_g, ln_b, seq):
    m, d = x2d.shape
    tm = 512
    tiles_per_batch = seq // tm
    const = lambda i: (0, 0)
    resident = dict(pipeline_mode=pl.Buffered(1))
    return pl.pallas_call(
        functools.partial(_mix_kernel, n_chunk=512),
        grid=(m // tm,),
        in_specs=[
            pl.BlockSpec((tm, y_fox.shape[1]), lambda i: (i, 0)),
            pl.BlockSpec((tm, y_mla.shape[1]), lambda i: (i, 0)),
            pl.BlockSpec((tm, d), lambda i: (i, 0)),
            pl.BlockSpec((tm, d), lambda i: (i, 1)),
            pl.BlockSpec((tm, d), lambda i: (i, 0)),
            pl.BlockSpec((None, N_ADA, d), lambda i: (i // tiles_per_batch, 0, 0)),
            pl.BlockSpec(w_bf.shape, const, **resident),
            pl.BlockSpec(w_bm.shape, const, **resident),
            pl.BlockSpec(w_o.shape, const, **resident),
            pl.BlockSpec((1, d), const),
            pl.BlockSpec((1, d), const),
        ],
        out_specs=pl.BlockSpec((tm, d), lambda i: (i, 0)),
        out_shape=jax.ShapeDtypeStruct((m, d), F32),
        scratch_shapes=[pltpu.VMEM((tm, d), BF16)],
        compiler_params=_cparams(("parallel",)),
        name="mix",
    )(y_fox, y_mla, proj, proj, x2d, mod3, w_bf, w_bm, w_o, ln_g, ln_b)


def _mlp_kernel(x_ref, mod_ref, wu_ref, wd_ref, g_ref, b_ref, o_ref, u_ref):
    f = pl.program_id(1)

    @pl.when(f == 0)
    def _():
        shift = mod_ref[3:4, :]
        scale = mod_ref[4:5, :]
        u_ref[...] = (x_ref[...] * (1.0 + scale) + shift).astype(BF16)
        o_ref[...] = jnp.zeros_like(o_ref)

    z = jnp.dot(u_ref[...], wu_ref[...], preferred_element_type=F32)
    h = jnp.square(jnp.maximum(z, 0.0)).astype(BF16)
    o_ref[...] += jnp.dot(h, wd_ref[...], preferred_element_type=F32)

    @pl.when(f == pl.num_programs(1) - 1)
    def _():
        gate = mod_ref[5:6, :]
        o_ref[...] = _layer_norm(ALPHA * x_ref[...] + gate * o_ref[...], g_ref[...], b_ref[...])


def _mlp(x1, mod3, w_up, w_down, ln_g, ln_b, seq):
    m, d = x1.shape
    d_ff = w_up.shape[1]
    tm, tf = 1024, 512
    tiles_per_batch = seq // tm
    return pl.pallas_call(
        _mlp_kernel,
        grid=(m // tm, d_ff // tf),
        in_specs=[
            pl.BlockSpec((tm, d), lambda i, f: (i, 0)),
            pl.BlockSpec((None, N_ADA, d), lambda i, f: (i // tiles_per_batch, 0, 0)),
            pl.BlockSpec((d, tf), lambda i, f: (0, f)),
            pl.BlockSpec((tf, d), lambda i, f: (f, 0)),
            pl.BlockSpec((1, d), lambda i, f: (0, 0)),
            pl.BlockSpec((1, d), lambda i, f: (0, 0)),
        ],
        out_specs=pl.BlockSpec((tm, d), lambda i, f: (i, 0)),
        out_shape=jax.ShapeDtypeStruct((m, d), F32),
        scratch_shapes=[pltpu.VMEM((tm, d), BF16)],
        compiler_params=_cparams(("parallel", "arbitrary")),
        name="mlp",
    )(x1, mod3, w_up, w_down, ln_g, ln_b)


def _pad_cols(w, width):
    return jnp.pad(w, ((0, 0), (0, width - w.shape[1])))


def _swap_halves(w):
    half = w.shape[-1] // 2
    return jnp.concatenate([w[..., half:], w[..., :half]], axis=-1)


def _layout_in_weights(w_in, d):
    c0 = 3 * FOX_WIDTH
    c1 = c0 + FOX_HEADS
    c2 = c1 + MLA_Q_LORA
    c3 = c2 + MLA_KV_LORA
    c4 = c3 + MLA_ROPE_DIM
    c5 = c4 + d
    w_main = jnp.concatenate([w_in[:, c4:c5], w_in[:, c5:c5 + d], w_in[:, :c0]], axis=1).astype(BF16)
    w_kr = w_in[:, c3:c4]
    w_small = jnp.concatenate([
        _pad_cols(w_in[:, c0:c1], LANES), w_in[:, c1:c2], w_in[:, c2:c3],
        _pad_cols(w_kr, LANES), _pad_cols(_swap_halves(w_kr), LANES)], axis=1).astype(BF16)
    return w_main, w_small


def _layout_q_up(w_q_up):
    r = w_q_up.shape[0]
    w3 = w_q_up.reshape(r, MLA_HEADS, MLA_QK_DIM)
    w_nope = w3[:, :, :MLA_NOPE_DIM].reshape(r, MLA_HEADS * MLA_NOPE_DIM)
    w_rope = w3[:, :, MLA_NOPE_DIM:]
    pad = ((0, 0), (0, 0), (0, LANES - MLA_ROPE_DIM))
    w_r = jnp.pad(w_rope, pad).reshape(r, MLA_HEADS * LANES)
    w_rs = jnp.pad(_swap_halves(w_rope), pad).reshape(r, MLA_HEADS * LANES)
    return jnp.concatenate([w_nope, w_r, w_rs], axis=1).astype(BF16)


def _rope_tables(seq):
    pos = jnp.arange(seq, dtype=F32)
    inv_freq = ROPE_THETA ** (-jnp.arange(0, MLA_ROPE_DIM, 2, dtype=F32) / MLA_ROPE_DIM)
    ang = pos[:, None] * inv_freq[None, :]
    cos, sin = jnp.cos(ang), jnp.sin(ang)
    cos_t = _pad_cols(jnp.concatenate([cos, cos], axis=1), LANES)
    sin_t = _pad_cols(jnp.concatenate([-sin, sin], axis=1), LANES)
    return cos_t, sin_t


def kernel(x, c, w_ada, b_ada, w_in, b_forget, g_q_norm, w_q_up, g_kv_norm, w_kv_up, w_branch_fox,
           w_branch_mla, w_out, ln1_g, ln1_b, w_mlp_up, w_mlp_down, ln2_g, ln2_b):
    n_batch, seq, d = x.shape
    assert w_ada.shape[0] == DEPTH
    m = n_batch * seq
    x2d = x.reshape(m, d)

    mod3 = _ada(c, w_ada[0], b_ada[0]).reshape(n_batch, N_ADA, d)

    w_main, w_small = _layout_in_weights(w_in[0], d)
    proj = _inproj(x2d, mod3, w_main, seq)

    cos_t, sin_t = _rope_tables(seq)
    bf_pad = _pad_cols(b_forget[0].reshape(1, FOX_HEADS), LANES)
    qn, qr, kv, kr, cum, cumt = _prep(
        x2d, mod3, w_small, bf_pad, g_q_norm[0].reshape(1, -1), g_kv_norm[0].reshape(1, -1),
        _layout_q_up(w_q_up[0]), w_kv_up[0].astype(BF16), cos_t, sin_t, seq)

    y_fox = _fox_attention(proj, cum, cumt.reshape(n_batch, FOX_HEADS // 2, 2, seq), seq, q_col0=2 * d)
    y_mla = _mla_attention(qn, qr, kv, kr, seq)

    x1 = _mix(y_fox, y_mla, proj, x2d, mod3, w_branch_fox[0].astype(BF16), w_branch_mla[0].astype(BF16),
              w_out[0].astype(BF16), ln1_g[0].reshape(1, d), ln1_b[0].reshape(1, d), seq)
    x2 = _mlp(x1, mod3, w_mlp_up[0].astype(BF16), w_mlp_down[0].astype(BF16),
              ln2_g[0].reshape(1, d), ln2_b[0].reshape(1, d), seq)
    return x2.reshape(n_batch, seq, d)
```

```python
import functools
import math

import jax
import jax.numpy as jnp
from jax import lax
from jax.experimental import pallas as pl
from jax.experimental.pallas import tpu as pltpu

F32 = jnp.float32
BF16 = jnp.bfloat16

FOX_HEADS = 16
FOX_HEAD_DIM = 64
FOX_WIDTH = FOX_HEADS * FOX_HEAD_DIM
MLA_HEADS = 8
MLA_NOPE_DIM = 128
MLA_ROPE_DIM = 64
MLA_V_DIM = 128
MLA_Q_LORA = 512
MLA_KV_LORA = 256
MLA_QK_DIM = MLA_NOPE_DIM + MLA_ROPE_DIM
CHUNK = 64
ROPE_THETA = 10000.0
LN_EPS = 1e-5
RMS_EPS = 1e-6
N_ADA = 6
DEPTH = 1
ALPHA = (2.0 * DEPTH) ** 0.25

LANES = 128
LOG2E = math.log2(math.e)
FOX_QSCALE = FOX_HEAD_DIM ** -0.5 * LOG2E
MLA_QSCALE = MLA_QK_DIM ** -0.5 * LOG2E
NEG_BIG = -1e30
VMEM_LIMIT = 56 * 1024 * 1024

SM_F = (0, 128)
SM_CQ = (128, 640)
SM_CKV = (640, 896)
SM_KR = (896, 1024)
SM_KRS = (1024, 1152)
SM_WIDTH = 1152
SUB = 128
HALF = 256


def _cparams(semantics):
    return pltpu.CompilerParams(dimension_semantics=semantics, vmem_limit_bytes=VMEM_LIMIT)


def _layer_norm(y, g, b):
    mu = jnp.mean(y, axis=-1, keepdims=True)
    yc = y - mu
    var = jnp.mean(yc * yc, axis=-1, keepdims=True)
    return yc * lax.rsqrt(var + LN_EPS) * g + b


def _rms_norm(c, g):
    return c * lax.rsqrt(jnp.mean(c * c, axis=-1, keepdims=True) + RMS_EPS) * g


def _ada_kernel(ct_ref, w_ref, b_ref, o_ref, *, n_batch, k_chunk):
    d = w_ref.shape[0]
    ct = ct_ref[...]
    s = ct * jax.nn.sigmoid(ct)
    accs = [b_ref[...] for _ in range(n_batch)]
    for kc in range(d // k_chunk):
        rows = slice(kc * k_chunk, (kc + 1) * k_chunk)
        w = w_ref[rows, :]
        for b in range(n_batch):
            accs[b] = accs[b] + jnp.sum(w * s[rows, b:b + 1], axis=0, keepdims=True)
    for b in range(n_batch):
        o_ref[b:b + 1, :] = accs[b]


def _ada(c, w_ada, b_ada):
    n_batch, d = c.shape
    n = w_ada.shape[1]
    tn = 1024
    return pl.pallas_call(
        functools.partial(_ada_kernel, n_batch=n_batch, k_chunk=256),
        grid=(n // tn,),
        in_specs=[
            pl.BlockSpec((d, n_batch), lambda j: (0, 0)),
            pl.BlockSpec((d, tn), lambda j: (0, j)),
            pl.BlockSpec((1, tn), lambda j: (0, j)),
        ],
        out_specs=pl.BlockSpec((n_batch, tn), lambda j: (0, j)),
        out_shape=jax.ShapeDtypeStruct((n_batch, n), F32),
        compiler_params=_cparams(("arbitrary",)),
        name="ada",
    )(c.T, w_ada, b_ada.reshape(1, n))


def _inproj_kernel(x_ref, mod_ref, w_ref, o_ref, u_ref, *, q_tile):
    j = pl.program_id(1)

    @pl.when(j == 0)
    def _():
        shift = mod_ref[0:1, :]
        scale = mod_ref[1:2, :]
        u_ref[...] = (x_ref[...] * (1.0 + scale) + shift).astype(BF16)

    acc = jnp.dot(u_ref[...], w_ref[...], preferred_element_type=F32)

    @pl.when(j == q_tile)
    def _():
        o_ref[...] = (acc * FOX_QSCALE).astype(o_ref.dtype)

    @pl.when(j != q_tile)
    def _():
        o_ref[...] = acc.astype(o_ref.dtype)


def _inproj(x2d, mod3, w_main, seq):
    m, d = x2d.shape
    n = w_main.shape[1]
    tm, tn = 1024, 1024
    tiles_per_batch = seq // tm
    return pl.pallas_call(
        functools.partial(_inproj_kernel, q_tile=4),
        grid=(m // tm, n // tn),
        in_specs=[
            pl.BlockSpec((tm, d), lambda i, j: (i, 0)),
            pl.BlockSpec((None, N_ADA, d), lambda i, j: (i // tiles_per_batch, 0, 0)),
            pl.BlockSpec((d, tn), lambda i, j: (0, j)),
        ],
        out_specs=pl.BlockSpec((tm, tn), lambda i, j: (i, j)),
        out_shape=jax.ShapeDtypeStruct((m, n), BF16),
        scratch_shapes=[pltpu.VMEM((tm, d), BF16)],
        compiler_params=_cparams(("parallel", "arbitrary")),
        name="inproj",
    )(x2d, mod3, w_main)


def _log_sigmoid(x):
    return jnp.minimum(x, 0.0) - jnp.log1p(jnp.exp(-jnp.abs(x)))


def _prep_kernel(x_ref, mod_ref, wsm_ref, bf_ref, gq_ref, gkv_ref, wq_ref, wkv_ref, cos_ref, sin_ref,
                 qn_ref, qr_ref, kv_ref, kr_ref, cum_ref, cumt_ref, carry_ref, *, tiles_per_batch):
    i = pl.program_id(0)
    tm = x_ref.shape[0]
    shift = mod_ref[0:1, :]
    scale = mod_ref[1:2, :]
    u = (x_ref[...] * (1.0 + scale) + shift).astype(BF16)
    ps = jnp.dot(u, wsm_ref[...], preferred_element_type=F32)
    cos = cos_ref[...]
    sin = sin_ref[...]

    logf = _log_sigmoid(ps[:, SM_F[0]:SM_F[1]] + bf_ref[...]) * LOG2E
    hi = logf.astype(BF16)
    r1 = logf - hi.astype(F32)
    mid = r1.astype(BF16)
    lo = (r1 - mid.astype(F32)).astype(BF16)
    row = lax.broadcasted_iota(jnp.int32, (tm, tm), 0)
    col = lax.broadcasted_iota(jnp.int32, (tm, tm), 1)
    tri = (row >= col).astype(BF16)
    cs = (jnp.dot(tri, hi, preferred_element_type=F32)
          + jnp.dot(tri, mid, preferred_element_type=F32)
          + jnp.dot(tri, lo, preferred_element_type=F32))

    @pl.when(i % tiles_per_batch == 0)
    def _():
        carry_ref[...] = jnp.zeros_like(carry_ref)

    cum = cs + carry_ref[...]
    carry_ref[...] = cum[tm - 1:tm, :]
    cum_ref[...] = cum
    cumt_ref[...] = cum.T[0:FOX_HEADS, :]

    cqn = _rms_norm(ps[:, SM_CQ[0]:SM_CQ[1]], gq_ref[...]).astype(BF16)
    qa = jnp.dot(cqn, wq_ref[...], preferred_element_type=F32)
    hw = MLA_HEADS * LANES
    qn_ref[...] = (qa[:, 0:hw] * MLA_QSCALE).astype(BF16)
    for h in range(MLA_HEADS):
        a = qa[:, hw + h * LANES: hw + (h + 1) * LANES]
        b = qa[:, 2 * hw + h * LANES: 2 * hw + (h + 1) * LANES]
        qr_ref[:, h * LANES:(h + 1) * LANES] = ((a * cos + b * sin) * MLA_QSCALE).astype(BF16)

    ckvn = _rms_norm(ps[:, SM_CKV[0]:SM_CKV[1]], gkv_ref[...]).astype(BF16)
    kv_ref[...] = jnp.dot(ckvn, wkv_ref[...], preferred_element_type=F32).astype(BF16)
    kr_ref[...] = (ps[:, SM_KR[0]:SM_KR[1]] * cos + ps[:, SM_KRS[0]:SM_KRS[1]] * sin).astype(BF16)


def _prep(x2d, mod3, w_small, bf_pad, g_q, g_kv, w_q_all, w_kv, cos_t, sin_t, seq):
    m, d = x2d.shape
    n_batch = m // seq
    tm = 512
    tiles_per_batch = seq // tm
    hw = MLA_HEADS * LANES
    const = lambda i: (0, 0)
    return pl.pallas_call(
        functools.partial(_prep_kernel, tiles_per_batch=tiles_per_batch),
        grid=(m // tm,),
        in_specs=[
            pl.BlockSpec((tm, d), lambda i: (i, 0)),
            pl.BlockSpec((None, N_ADA, d), lambda i: (i // tiles_per_batch, 0, 0)),
            pl.BlockSpec(w_small.shape, const),
            pl.BlockSpec(bf_pad.shape, const),
            pl.BlockSpec(g_q.shape, const),
            pl.BlockSpec(g_kv.shape, const),
            pl.BlockSpec(w_q_all.shape, const),
            pl.BlockSpec(w_kv.shape, const),
            pl.BlockSpec((tm, LANES), lambda i: (i % tiles_per_batch, 0)),
            pl.BlockSpec((tm, LANES), lambda i: (i % tiles_per_batch, 0)),
        ],
        out_specs=[
            pl.BlockSpec((tm, hw), lambda i: (i, 0)),
            pl.BlockSpec((tm, hw), lambda i: (i, 0)),
            pl.BlockSpec((tm, w_kv.shape[1]), lambda i: (i, 0)),
            pl.BlockSpec((tm, LANES), lambda i: (i, 0)),
            pl.BlockSpec((tm, LANES), lambda i: (i, 0)),
            pl.BlockSpec((None, FOX_HEADS, tm), lambda i: (i // tiles_per_batch, 0, i % tiles_per_batch)),
        ],
        out_shape=[
            jax.ShapeDtypeStruct((m, hw), BF16),
            jax.ShapeDtypeStruct((m, hw), BF16),
            jax.ShapeDtypeStruct((m, w_kv.shape[1]), BF16),
            jax.ShapeDtypeStruct((m, LANES), BF16),
            jax.ShapeDtypeStruct((m, LANES), F32),
            jax.ShapeDtypeStruct((n_batch, FOX_HEADS, seq), F32),
        ],
        scratch_shapes=[pltpu.VMEM((1, LANES), F32)],
        compiler_params=_cparams(("arbitrary",)),
        name="prep",
    )(x2d, mod3, w_small, bf_pad, g_q, g_kv, w_q_all, w_kv, cos_t, sin_t)


def _split3(x):
    hi = x.astype(BF16).astype(F32)
    r = x - hi
    mid = r.astype(BF16).astype(F32)
    lo = (r - mid).astype(BF16).astype(F32)
    return hi, mid, lo


def _online_softmax_step(t, m, acc, vt_aug):
    m_new = jnp.maximum(m, jnp.max(t, axis=0, keepdims=True))
    alpha = jnp.exp2(m - m_new)
    p = jnp.exp2(t - m_new).astype(BF16)
    pv = jnp.dot(vt_aug, p, preferred_element_type=F32)
    return m_new, alpha * acc + pv[0:acc.shape[0], :]


def _causal_mask(t, key0, query0, granule):
    r = lax.broadcasted_iota(jnp.int32, t.shape, 0) + key0
    c = lax.broadcasted_iota(jnp.int32, t.shape, 1) + query0
    return jnp.where(r <= (c | (granule - 1)), t, NEG_BIG)


def _flash_key_major(qi, tile, chains, keys, rhs, vt_aug, d_v, granule):
    acc_rows = d_v + 8
    n_rounds = tile // SUB
    n_pairs = len(chains) // 2
    issue_rounds = {g * n_rounds // n_pairs: (2 * g, 2 * g + 1) for g in range(n_pairs)}

    def qk_of(k0, is_diag):
        loaded = {}

        def product(n):
            half, hh = chains[n]
            n_keys = (half + 1) * HALF if is_diag else tile
            if (hh, n_keys) not in loaded:
                loaded[hh, n_keys] = keys(hh, k0, n_keys)
            return jnp.dot(loaded[hh, n_keys], rhs(hh)[:, half * HALF:(half + 1) * HALF],
                           preferred_element_type=F32)
        return product

    def visit(state, tiles):
        st = list(state)
        first = qk_of(*tiles[0])
        ss = [first(n) for n in range(len(chains))]
        for ti, (k0, is_diag) in enumerate(tiles):
            nxt = qk_of(*tiles[ti + 1]) if ti + 1 < len(tiles) else None
            ss_next = [None] * len(chains)
            for sb in range(n_rounds):
                if nxt is not None and sb in issue_rounds:
                    for n in issue_rounds[sb]:
                        ss_next[n] = nxt(n)
                ks = pl.multiple_of(k0 + sb * SUB, SUB)
                for n, (half, hh) in enumerate(chains):
                    if is_diag and sb * SUB >= (half + 1) * HALF:
                        continue
                    t = ss[n][sb * SUB:(sb + 1) * SUB, :]
                    if is_diag and (sb + 1) * SUB > half * HALF:
                        t = _causal_mask(t, sb * SUB, half * HALF, granule)
                    st[n] = _online_softmax_step(t, *st[n], vt_aug(hh, ks))
            ss = ss_next
        return tuple(st)

    def two_full(j, state):
        k0 = pl.multiple_of(2 * j * tile, tile)
        return visit(state, [(k0, False), (pl.multiple_of(k0 + tile, tile), False)])

    init = tuple((jnp.full((1, HALF), NEG_BIG, F32), jnp.zeros((acc_rows, HALF), F32)) for _ in chains)
    state = lax.fori_loop(0, qi // 2, two_full, init)
    q0 = pl.multiple_of(qi * tile, tile)
    state = lax.cond(
        qi % 2 == 1,
        lambda st: visit(st, [(pl.multiple_of(q0 - tile, tile), False), (q0, True)]),
        lambda st: visit(st, [(q0, True)]),
        state)
    return [acc[0:d_v, :] * (1.0 / acc[d_v:d_v + 1, :]) for (_, acc) in state]


def _cast_specs(weights, grid):
    n_steps = math.prod(grid)
    specs, shapes = [], []
    for w in weights:
        rows, cols = w.shape
        slab = rows // n_steps
        assert slab * n_steps == rows and slab % 16 == 0, (w.shape, n_steps)
        specs.append(pl.BlockSpec((slab, cols), lambda a, b, c: ((a * grid[1] + b) * grid[2] + c, 0)))
        shapes.append(jax.ShapeDtypeStruct(w.shape, BF16))
    return specs, shapes


def _cast_slabs(refs):
    n = len(refs) // 2
    for src, dst in zip(refs[:n], refs[n:]):
        dst[...] = src[...].astype(dst.dtype)


def _fox_kernel(q_ref, k_ref, v_ref, cum_ref, cumt_ref, *rest, tq, n_cast):
    cast_in, (o_ref, *cast_out), (vta_ref, kaug_ref) = rest[:n_cast], rest[n_cast:2 * n_cast + 1], rest[2 * n_cast + 1:]
    hp = pl.program_id(1)
    qi = pl.program_id(2)
    seq = k_ref.shape[0]
    dh = FOX_HEAD_DIM
    n_half = tq // HALF
    aug = 6

    @pl.when(qi == 0)
    def _():
        vt = v_ref[...].T
        tail_rows = lax.broadcasted_iota(jnp.int32, (16, seq), 0)
        tail = jnp.where(tail_rows < 8, 1.0, 0.0).astype(BF16)
        for hh in (0, 1):
            vta_ref[hh, 0:dh, :] = vt[hh * dh:(hh + 1) * dh, :]
            vta_ref[hh, dh:dh + 16, :] = tail
        lane = lax.broadcasted_iota(jnp.int32, (seq, LANES), 1)
        cum = cum_ref[...]
        kaug = jnp.zeros((seq, LANES), F32)
        for hh in (0, 1):
            col = jnp.sum(jnp.where(lane == 2 * hp + hh, cum, 0.0), axis=1, keepdims=True)
            neg = jnp.broadcast_to(-col, (seq, LANES))
            for idx, piece in enumerate(_split3(neg)):
                kaug = jnp.where(lane == aug * hh + idx, piece, kaug)
            kaug = jnp.where((lane >= aug * hh + 3) & (lane < aug * hh + 6), 1.0, kaug)
        kaug_ref[...] = kaug.astype(BF16)

    q0 = pl.multiple_of(qi * tq, tq)
    qt = q_ref[...].T
    row = lax.broadcasted_iota(jnp.int32, (LANES, tq), 0)
    rhs = []
    for hh in (0, 1):
        qm = jnp.where((row >= hh * dh) & (row < (hh + 1) * dh), qt, jnp.zeros_like(qt))
        ci = cumt_ref[hh:hh + 1, pl.ds(q0, tq)]
        qa = jnp.where((row >= aug * hh) & (row < aug * hh + 3), 1.0, 0.0)
        for idx, piece in enumerate(_split3(ci)):
            qa = jnp.where(row == aug * hh + 3 + idx, jnp.broadcast_to(piece, (LANES, tq)), qa)
        rhs.append(jnp.concatenate([qm, qa.astype(BF16)], axis=0))
    chains = [(half, hh) for half in range(n_half) for hh in (0, 1)]

    def keys(hh, k0, n):
        return jnp.concatenate([k_ref[pl.ds(k0, n), :], kaug_ref[pl.ds(k0, n), :]], axis=1)

    outs = _flash_key_major(qi, tq, chains, keys, lambda hh: rhs[hh],
                            lambda hh, ks: vta_ref[hh, :, pl.ds(ks, SUB)], dh, granule=1)
    o_t = jnp.concatenate(
        [jnp.concatenate([outs[half * 2 + hh] for half in range(n_half)], axis=1) for hh in (0, 1)], axis=0)
    o_ref[...] = o_t.T.astype(o_ref.dtype)
    _cast_slabs(cast_in + tuple(cast_out))


def _fox_attention(proj, cum, cumt4, seq, q_col0, cast_weights):
    m = proj.shape[0]
    n_batch = m // seq
    tq = 512
    nq = seq // tq
    n_pairs = FOX_HEADS // 2
    qb = q_col0 // LANES
    grid = (n_batch, n_pairs, nq)
    cast_specs, cast_shapes = _cast_specs(cast_weights, grid)
    return pl.pallas_call(
        functools.partial(_fox_kernel, tq=tq, n_cast=len(cast_weights)),
        grid=grid,
        in_specs=[
            pl.BlockSpec((tq, LANES), lambda b, hp, qi: (b * nq + qi, qb + hp)),
            pl.BlockSpec((seq, LANES), lambda b, hp, qi: (b, qb + n_pairs + hp)),
            pl.BlockSpec((seq, LANES), lambda b, hp, qi: (b, qb + 2 * n_pairs + hp)),
            pl.BlockSpec((seq, LANES), lambda b, hp, qi: (b, 0)),
            pl.BlockSpec((None, None, 2, seq), lambda b, hp, qi: (b, hp, 0, 0)),
        ] + cast_specs,
        out_specs=[pl.BlockSpec((tq, LANES), lambda b, hp, qi: (b * nq + qi, hp))] + cast_specs,
        out_shape=[jax.ShapeDtypeStruct((m, FOX_WIDTH), BF16)] + cast_shapes,
        scratch_shapes=[
            pltpu.VMEM((2, FOX_HEAD_DIM + 16, seq), BF16),
            pltpu.VMEM((seq, LANES), BF16),
        ],
        compiler_params=_cparams(("parallel", "parallel", "arbitrary")),
        name="fox_attn",
    )(proj, proj, proj, cum, cumt4, *cast_weights)


def _mla_kernel(qn_ref, qr_ref, kv_ref, kr_ref, *rest, tq, n_cast):
    cast_in, (o_ref, *cast_out), (vta_ref,) = rest[:n_cast], rest[n_cast:2 * n_cast + 1], rest[2 * n_cast + 1:]
    qi = pl.program_id(2)
    seq = kv_ref.shape[0]
    dv = MLA_V_DIM
    n_half = tq // HALF

    @pl.when(qi == 0)
    def _():
        tail_rows = lax.broadcasted_iota(jnp.int32, (16, seq), 0)
        tail = jnp.where(tail_rows < 8, 1.0, 0.0).astype(BF16)
        for hh in (0, 1):
            vta_ref[hh, 0:dv, :] = kv_ref[:, (2 * hh + 1) * LANES:(2 * hh + 2) * LANES].T
            vta_ref[hh, dv:dv + 16, :] = tail

    rhs = [jnp.concatenate([qn_ref[:, hh * LANES:(hh + 1) * LANES],
                            qr_ref[:, hh * LANES:(hh + 1) * LANES]], axis=1).T for hh in (0, 1)]
    chains = [(half, hh) for half in range(n_half) for hh in (0, 1)]

    def keys(hh, k0, n):
        return jnp.concatenate([kv_ref[pl.ds(k0, n), 2 * hh * LANES:(2 * hh + 1) * LANES],
                                kr_ref[pl.ds(k0, n), :]], axis=1)

    outs = _flash_key_major(qi, tq, chains, keys, lambda hh: rhs[hh],
                            lambda hh, ks: vta_ref[hh, :, pl.ds(ks, SUB)], dv, granule=CHUNK)
    o_t = jnp.concatenate(
        [jnp.concatenate([outs[half * 2 + hh] for half in range(n_half)], axis=1) for hh in (0, 1)], axis=0)
    o_ref[...] = o_t.T.astype(o_ref.dtype)
    _cast_slabs(cast_in + tuple(cast_out))


def _mla_attention(qn, qr, kv, kr, seq, cast_weights):
    m = qn.shape[0]
    n_batch = m // seq
    tq = 512
    nq = seq // tq
    pair = 2 * LANES
    grid = (n_batch, MLA_HEADS // 2, nq)
    cast_specs, cast_shapes = _cast_specs(cast_weights, grid)
    return pl.pallas_call(
        functools.partial(_mla_kernel, tq=tq, n_cast=len(cast_weights)),
        grid=grid,
        in_specs=[
            pl.BlockSpec((tq, pair), lambda b, hp, qi: (b * nq + qi, hp)),
            pl.BlockSpec((tq, pair), lambda b, hp, qi: (b * nq + qi, hp)),
            pl.BlockSpec((seq, 2 * pair), lambda b, hp, qi: (b, hp)),
            pl.BlockSpec((seq, LANES), lambda b, hp, qi: (b, 0)),
        ] + cast_specs,
        out_specs=[pl.BlockSpec((tq, pair), lambda b, hp, qi: (b * nq + qi, hp))] + cast_specs,
        out_shape=[jax.ShapeDtypeStruct((m, MLA_HEADS * MLA_V_DIM), BF16)] + cast_shapes,
        scratch_shapes=[pltpu.VMEM((2, MLA_V_DIM + 16, seq), BF16)],
        compiler_params=_cparams(("parallel", "parallel", "arbitrary")),
        name="mla_attn",
    )(qn, qr, kv, kr, *cast_weights)


def _mix_kernel(yf_ref, ym_ref, gf_ref, gm_ref, x_ref, mod_ref, wbf_ref, wbm_ref, wo_ref, g_ref, b_ref,
                o_ref, mg_ref, *, n_chunk):
    d = o_ref.shape[1]
    yf = yf_ref[...]
    ym = ym_ref[...]
    for c in range(d // n_chunk):
        cols = slice(c * n_chunk, (c + 1) * n_chunk)
        a = jnp.dot(yf, wbf_ref[:, cols], preferred_element_type=F32)
        b = jnp.dot(ym, wbm_ref[:, cols], preferred_element_type=F32)
        merged = (jax.nn.sigmoid(gf_ref[:, cols].astype(F32)) * a
                  + jax.nn.sigmoid(gm_ref[:, cols].astype(F32)) * b)
        mg_ref[:, cols] = merged.astype(BF16)
    mix = jnp.dot(mg_ref[...], wo_ref[...], preferred_element_type=F32)
    gate = mod_ref[2:3, :]
    o_ref[...] = _layer_norm(ALPHA * x_ref[...] + gate * mix, g_ref[...], b_ref[...])


def _mix(y_fox, y_mla, proj, x2d, mod3, w_bf, w_bm, w_o, ln_g, ln_b, seq):
    m, d = x2d.shape
    tm = 512
    tiles_per_batch = seq // tm
    const = lambda i: (0, 0)
    resident = dict(pipeline_mode=pl.Buffered(1))
    return pl.pallas_call(
        functools.partial(_mix_kernel, n_chunk=512),
        grid=(m // tm,),
        in_specs=[
            pl.BlockSpec((tm, y_fox.shape[1]), lambda i: (i, 0)),
            pl.BlockSpec((tm, y_mla.shape[1]), lambda i: (i, 0)),
            pl.BlockSpec((tm, d), lambda i: (i, 0)),
            pl.BlockSpec((tm, d), lambda i: (i, 1)),
            pl.BlockSpec((tm, d), lambda i: (i, 0)),
            pl.BlockSpec((None, N_ADA, d), lambda i: (i // tiles_per_batch, 0, 0)),
            pl.BlockSpec(w_bf.shape, const, **resident),
            pl.BlockSpec(w_bm.shape, const, **resident),
            pl.BlockSpec(w_o.shape, const, **resident),
            pl.BlockSpec((1, d), const),
            pl.BlockSpec((1, d), const),
        ],
        out_specs=pl.BlockSpec((tm, d), lambda i: (i, 0)),
        out_shape=jax.ShapeDtypeStruct((m, d), F32),
        scratch_shapes=[pltpu.VMEM((tm, d), BF16)],
        compiler_params=_cparams(("parallel",)),
        name="mix",
    )(y_fox, y_mla, proj, proj, x2d, mod3, w_bf, w_bm, w_o, ln_g, ln_b)


def _mlp_kernel(x_ref, mod_ref, wu_ref, wd_ref, g_ref, b_ref, o_ref, u_ref):
    f = pl.program_id(1)

    @pl.when(f == 0)
    def _():
        shift = mod_ref[3:4, :]
        scale = mod_ref[4:5, :]
        u_ref[...] = (x_ref[...] * (1.0 + scale) + shift).astype(BF16)
        o_ref[...] = jnp.zeros_like(o_ref)

    z = jnp.dot(u_ref[...], wu_ref[...], preferred_element_type=F32)
    h = jnp.square(jnp.maximum(z, 0.0)).astype(BF16)
    o_ref[...] += jnp.dot(h, wd_ref[...], preferred_element_type=F32)

    @pl.when(f == pl.num_programs(1) - 1)
    def _():
        gate = mod_ref[5:6, :]
        o_ref[...] = _layer_norm(ALPHA * x_ref[...] + gate * o_ref[...], g_ref[...], b_ref[...])


def _mlp(x1, mod3, w_up, w_down, ln_g, ln_b, seq):
    m, d = x1.shape
    d_ff = w_up.shape[1]
    tm, tf = 1024, 512
    tiles_per_batch = seq // tm
    return pl.pallas_call(
        _mlp_kernel,
        grid=(m // tm, d_ff // tf),
        in_specs=[
            pl.BlockSpec((tm, d), lambda i, f: (i, 0)),
            pl.BlockSpec((None, N_ADA, d), lambda i, f: (i // tiles_per_batch, 0, 0)),
            pl.BlockSpec((d, tf), lambda i, f: (0, f)),
            pl.BlockSpec((tf, d), lambda i, f: (f, 0)),
            pl.BlockSpec((1, d), lambda i, f: (0, 0)),
            pl.BlockSpec((1, d), lambda i, f: (0, 0)),
        ],
        out_specs=pl.BlockSpec((tm, d), lambda i, f: (i, 0)),
        out_shape=jax.ShapeDtypeStruct((m, d), F32),
        scratch_shapes=[pltpu.VMEM((tm, d), BF16)],
        compiler_params=_cparams(("parallel", "arbitrary")),
        name="mlp",
    )(x1, mod3, w_up, w_down, ln_g, ln_b)


def _pad_cols(w, width):
    return jnp.pad(w, ((0, 0), (0, width - w.shape[1])))


def _swap_halves(w):
    half = w.shape[-1] // 2
    return jnp.concatenate([w[..., half:], w[..., :half]], axis=-1)


def _layout_in_weights(w_in, d):
    c0 = 3 * FOX_WIDTH
    c1 = c0 + FOX_HEADS
    c2 = c1 + MLA_Q_LORA
    c3 = c2 + MLA_KV_LORA
    c4 = c3 + MLA_ROPE_DIM
    c5 = c4 + d
    w_main = jnp.concatenate([w_in[:, c4:c5], w_in[:, c5:c5 + d], w_in[:, :c0]], axis=1).astype(BF16)
    w_kr = w_in[:, c3:c4]
    w_small = jnp.concatenate([
        _pad_cols(w_in[:, c0:c1], LANES), w_in[:, c1:c2], w_in[:, c2:c3],
        _pad_cols(w_kr, LANES), _pad_cols(_swap_halves(w_kr), LANES)], axis=1).astype(BF16)
    return w_main, w_small


def _layout_q_up(w_q_up):
    r = w_q_up.shape[0]
    w3 = w_q_up.reshape(r, MLA_HEADS, MLA_QK_DIM)
    w_nope = w3[:, :, :MLA_NOPE_DIM].reshape(r, MLA_HEADS * MLA_NOPE_DIM)
    w_rope = w3[:, :, MLA_NOPE_DIM:]
    pad = ((0, 0), (0, 0), (0, LANES - MLA_ROPE_DIM))
    w_r = jnp.pad(w_rope, pad).reshape(r, MLA_HEADS * LANES)
    w_rs = jnp.pad(_swap_halves(w_rope), pad).reshape(r, MLA_HEADS * LANES)
    return jnp.concatenate([w_nope, w_r, w_rs], axis=1).astype(BF16)


def _rope_tables(seq):
    pos = jnp.arange(seq, dtype=F32)
    inv_freq = ROPE_THETA ** (-jnp.arange(0, MLA_ROPE_DIM, 2, dtype=F32) / MLA_ROPE_DIM)
    ang = pos[:, None] * inv_freq[None, :]
    cos, sin = jnp.cos(ang), jnp.sin(ang)
    cos_t = _pad_cols(jnp.concatenate([cos, cos], axis=1), LANES)
    sin_t = _pad_cols(jnp.concatenate([-sin, sin], axis=1), LANES)
    return cos_t, sin_t


def kernel(x, c, w_ada, b_ada, w_in, b_forget, g_q_norm, w_q_up, g_kv_norm, w_kv_up, w_branch_fox,
           w_branch_mla, w_out, ln1_g, ln1_b, w_mlp_up, w_mlp_down, ln2_g, ln2_b):
    n_batch, seq, d = x.shape
    assert w_ada.shape[0] == DEPTH
    m = n_batch * seq
    x2d = x.reshape(m, d)

    mod3 = _ada(c, w_ada[0], b_ada[0]).reshape(n_batch, N_ADA, d)

    w_main, w_small = _layout_in_weights(w_in[0], d)
    proj = _inproj(x2d, mod3, w_main, seq)

    cos_t, sin_t = _rope_tables(seq)
    bf_pad = _pad_cols(b_forget[0].reshape(1, FOX_HEADS), LANES)
    qn, qr, kv, kr, cum, cumt = _prep(
        x2d, mod3, w_small, bf_pad, g_q_norm[0].reshape(1, -1), g_kv_norm[0].reshape(1, -1),
        _layout_q_up(w_q_up[0]), w_kv_up[0].astype(BF16), cos_t, sin_t, seq)

    y_fox, w_up_b, w_down_b = _fox_attention(
        proj, cum, cumt.reshape(n_batch, FOX_HEADS // 2, 2, seq), seq, 2 * d, (w_mlp_up[0], w_mlp_down[0]))
    y_mla, w_bf_b, w_bm_b, w_o_b = _mla_attention(
        qn, qr, kv, kr, seq, (w_branch_fox[0], w_branch_mla[0], w_out[0]))

    x1 = _mix(y_fox, y_mla, proj, x2d, mod3, w_bf_b, w_bm_b, w_o_b,
              ln1_g[0].reshape(1, d), ln1_b[0].reshape(1, d), seq)
    x2 = _mlp(x1, mod3, w_up_b, w_down_b, ln2_g[0].reshape(1, d), ln2_b[0].reshape(1, d), seq)
    return x2.reshape(n_batch, seq, d)
```

```python
import functools
import math

import jax
import jax.numpy as jnp
import numpy as np
from jax import lax
from jax.experimental import pallas as pl
from jax.experimental.pallas import tpu as pltpu

F32 = jnp.float32
BF16 = jnp.bfloat16

FOX_HEADS = 16
FOX_HEAD_DIM = 64
FOX_WIDTH = FOX_HEADS * FOX_HEAD_DIM
MLA_HEADS = 8
MLA_NOPE_DIM = 128
MLA_ROPE_DIM = 64
MLA_V_DIM = 128
MLA_Q_LORA = 512
MLA_KV_LORA = 256
MLA_QK_DIM = MLA_NOPE_DIM + MLA_ROPE_DIM
CHUNK = 64
ROPE_THETA = 10000.0
LN_EPS = 1e-5
RMS_EPS = 1e-6
N_ADA = 6
DEPTH = 1
ALPHA = (2.0 * DEPTH) ** 0.25

LANES = 128
LOG2E = math.log2(math.e)
FOX_QSCALE = FOX_HEAD_DIM ** -0.5 * LOG2E
MLA_QSCALE = MLA_QK_DIM ** -0.5 * LOG2E
NEG_BIG = -1e30
VMEM_LIMIT = 56 * 1024 * 1024

SUB = 128
HALF = 256


def _cparams(semantics):
    return pltpu.CompilerParams(dimension_semantics=semantics, vmem_limit_bytes=VMEM_LIMIT)


def _layer_norm(y, g, b):
    mu = jnp.mean(y, axis=-1, keepdims=True)
    yc = y - mu
    var = jnp.mean(yc * yc, axis=-1, keepdims=True)
    return yc * lax.rsqrt(var + LN_EPS) * g + b


def _rms_norm(c, g):
    return c * lax.rsqrt(jnp.mean(c * c, axis=-1, keepdims=True) + RMS_EPS) * g


def _ada_kernel(ct_ref, w_ref, b_ref, o_ref, *, n_batch, k_chunk):
    d = w_ref.shape[0]
    ct = ct_ref[...]
    s = ct * jax.nn.sigmoid(ct)
    accs = [b_ref[...] for _ in range(n_batch)]
    for kc in range(d // k_chunk):
        rows = slice(kc * k_chunk, (kc + 1) * k_chunk)
        w = w_ref[rows, :]
        for b in range(n_batch):
            accs[b] = accs[b] + jnp.sum(w * s[rows, b:b + 1], axis=0, keepdims=True)
    for b in range(n_batch):
        o_ref[b:b + 1, :] = accs[b]


def _ada(c, w_ada, b_ada):
    n_batch, d = c.shape
    n = w_ada.shape[1]
    tn = 1024
    return pl.pallas_call(
        functools.partial(_ada_kernel, n_batch=n_batch, k_chunk=256),
        grid=(n // tn,),
        in_specs=[
            pl.BlockSpec((d, n_batch), lambda j: (0, 0)),
            pl.BlockSpec((d, tn), lambda j: (0, j)),
            pl.BlockSpec((1, tn), lambda j: (0, j)),
        ],
        out_specs=pl.BlockSpec((n_batch, tn), lambda j: (0, j)),
        out_shape=jax.ShapeDtypeStruct((n_batch, n), F32),
        compiler_params=_cparams(("arbitrary",)),
        name="ada",
    )(c.T, w_ada, b_ada.reshape(1, n))


IN_TN = 1024
IN_SMALL_TILE = 3 * FOX_WIDTH // IN_TN


def _inproj_kernel(x_ref, mod_ref, w_ref, o_ref, lo_ref, u_ref, *, n_cols):
    j = pl.program_id(1)
    last = pl.num_programs(1) - 1

    @pl.when(j == 0)
    def _():
        shift = mod_ref[0:1, :]
        scale = mod_ref[1:2, :]
        u_ref[...] = (x_ref[...] * (1.0 + scale) + shift).astype(BF16)

    def product(w):
        return jnp.dot(u_ref[...], w.astype(BF16), preferred_element_type=F32)

    @pl.when(j == 0)
    def _():
        o_ref[...] = (product(w_ref[...]) * FOX_QSCALE).astype(o_ref.dtype)

    @pl.when(j == IN_SMALL_TILE)
    def _():
        acc = product(w_ref[...])
        hi = acc.astype(o_ref.dtype)
        o_ref[...] = hi
        lo_ref[...] = (acc - hi.astype(F32)).astype(lo_ref.dtype)

    @pl.when(j == last)
    def _():
        col = lax.broadcasted_iota(jnp.int32, w_ref.shape, 1)
        w = jnp.where(col < n_cols - last * IN_TN, w_ref[...], 0.0)
        o_ref[...] = product(w).astype(o_ref.dtype)

    @pl.when((j != 0) & (j != IN_SMALL_TILE) & (j != last))
    def _():
        o_ref[...] = product(w_ref[...]).astype(o_ref.dtype)


def _inproj(x2d, mod3, w_in, seq):
    m, d = x2d.shape
    n_cols = w_in.shape[1]
    tm = 1024
    n_tiles = pl.cdiv(n_cols, IN_TN)
    tiles_per_batch = seq // tm
    return pl.pallas_call(
        functools.partial(_inproj_kernel, n_cols=n_cols),
        grid=(m // tm, n_tiles),
        in_specs=[
            pl.BlockSpec((tm, d), lambda i, j: (i, 0)),
            pl.BlockSpec((None, N_ADA, d), lambda i, j: (i // tiles_per_batch, 0, 0)),
            pl.BlockSpec((d, IN_TN), lambda i, j: (0, j)),
        ],
        out_specs=[pl.BlockSpec((tm, IN_TN), lambda i, j: (i, j)),
                   pl.BlockSpec((tm, IN_TN), lambda i, j: (i, 0))],
        out_shape=[jax.ShapeDtypeStruct((m, n_tiles * IN_TN), BF16),
                   jax.ShapeDtypeStruct((m, IN_TN), BF16)],
        scratch_shapes=[pltpu.VMEM((tm, d), BF16)],
        compiler_params=_cparams(("parallel", "arbitrary")),
        name="inproj",
    )(x2d, mod3, w_in)


def _log_sigmoid(x):
    return jnp.minimum(x, 0.0) - jnp.log1p(jnp.exp(-jnp.abs(x)))


def _prep_kernel(hi_ref, lo_ref, bf_ref, gq_ref, gkv_ref, wq_ref, wkv_ref, cos_ref, sin_ref,
                 qn_ref, qr_ref, kv_ref, kr_ref, cum_ref, cumt_ref, carry_ref, *, tiles_per_batch):
    i = pl.program_id(0)
    tm = hi_ref.shape[0]
    ps = hi_ref[...].astype(F32) + lo_ref[...].astype(F32)
    c0 = FOX_HEADS
    c1 = c0 + MLA_Q_LORA
    c2 = c1 + MLA_KV_LORA
    c3 = c2 + MLA_ROPE_DIM
    half = MLA_ROPE_DIM // 2
    lane_pad = jnp.zeros((tm, LANES - MLA_ROPE_DIM), F32)
    k_rope = jnp.concatenate([ps[:, c2:c3], lane_pad], axis=1)
    k_rope_partner = jnp.concatenate([ps[:, c2 + half:c3], ps[:, c2:c2 + half], lane_pad], axis=1)
    cos = cos_ref[...]
    sin = sin_ref[...]

    logf = _log_sigmoid(ps[:, 0:LANES] + bf_ref[...]) * LOG2E
    hi = logf.astype(BF16)
    r1 = logf - hi.astype(F32)
    mid = r1.astype(BF16)
    lo = (r1 - mid.astype(F32)).astype(BF16)
    row = lax.broadcasted_iota(jnp.int32, (tm, tm), 0)
    col = lax.broadcasted_iota(jnp.int32, (tm, tm), 1)
    tri = (row >= col).astype(BF16)
    cs = (jnp.dot(tri, hi, preferred_element_type=F32)
          + jnp.dot(tri, mid, preferred_element_type=F32)
          + jnp.dot(tri, lo, preferred_element_type=F32))

    @pl.when(i % tiles_per_batch == 0)
    def _():
        carry_ref[...] = jnp.zeros_like(carry_ref)

    cum = cs + carry_ref[...]
    carry_ref[...] = cum[tm - 1:tm, :]
    cum_ref[...] = cum
    cumt_ref[...] = cum.T[0:FOX_HEADS, :]

    cqn = _rms_norm(ps[:, c0:c1], gq_ref[...]).astype(BF16)
    qa = jnp.dot(cqn, wq_ref[...], preferred_element_type=F32)
    hw = MLA_HEADS * LANES
    qn_ref[...] = (qa[:, 0:hw] * MLA_QSCALE).astype(BF16)
    for h in range(MLA_HEADS):
        a = qa[:, hw + h * LANES: hw + (h + 1) * LANES]
        b = qa[:, 2 * hw + h * LANES: 2 * hw + (h + 1) * LANES]
        qr_ref[:, h * LANES:(h + 1) * LANES] = ((a * cos + b * sin) * MLA_QSCALE).astype(BF16)

    ckvn = _rms_norm(ps[:, c1:c2], gkv_ref[...]).astype(BF16)
    kv_ref[...] = jnp.dot(ckvn, wkv_ref[...], preferred_element_type=F32).astype(BF16)
    kr_ref[...] = (k_rope * cos + k_rope_partner * sin).astype(BF16)


def _prep(proj, proj_lo, bf_pad, g_q, g_kv, w_q_all, w_kv, cos_t, sin_t, seq):
    m = proj.shape[0]
    n_batch = m // seq
    tm = 512
    tiles_per_batch = seq // tm
    hw = MLA_HEADS * LANES
    const = lambda i: (0, 0)
    return pl.pallas_call(
        functools.partial(_prep_kernel, tiles_per_batch=tiles_per_batch),
        grid=(m // tm,),
        in_specs=[
            pl.BlockSpec((tm, IN_TN), lambda i: (i, IN_SMALL_TILE)),
            pl.BlockSpec((tm, IN_TN), lambda i: (i, 0)),
            pl.BlockSpec(bf_pad.shape, const),
            pl.BlockSpec(g_q.shape, const),
            pl.BlockSpec(g_kv.shape, const),
            pl.BlockSpec(w_q_all.shape, const),
            pl.BlockSpec(w_kv.shape, const),
            pl.BlockSpec((tm, LANES), lambda i: (i % tiles_per_batch, 0)),
            pl.BlockSpec((tm, LANES), lambda i: (i % tiles_per_batch, 0)),
        ],
        out_specs=[
            pl.BlockSpec((tm, hw), lambda i: (i, 0)),
            pl.BlockSpec((tm, hw), lambda i: (i, 0)),
            pl.BlockSpec((tm, w_kv.shape[1]), lambda i: (i, 0)),
            pl.BlockSpec((tm, LANES), lambda i: (i, 0)),
            pl.BlockSpec((tm, LANES), lambda i: (i, 0)),
            pl.BlockSpec((None, FOX_HEADS, tm), lambda i: (i // tiles_per_batch, 0, i % tiles_per_batch)),
        ],
        out_shape=[
            jax.ShapeDtypeStruct((m, hw), BF16),
            jax.ShapeDtypeStruct((m, hw), BF16),
            jax.ShapeDtypeStruct((m, w_kv.shape[1]), BF16),
            jax.ShapeDtypeStruct((m, LANES), BF16),
            jax.ShapeDtypeStruct((m, LANES), F32),
            jax.ShapeDtypeStruct((n_batch, FOX_HEADS, seq), F32),
        ],
        scratch_shapes=[pltpu.VMEM((1, LANES), F32)],
        compiler_params=_cparams(("arbitrary",)),
        name="prep",
    )(proj, proj_lo, bf_pad, g_q, g_kv, w_q_all, w_kv, cos_t, sin_t)


def _split3(x):
    hi = x.astype(BF16).astype(F32)
    r = x - hi
    mid = r.astype(BF16).astype(F32)
    lo = (r - mid).astype(BF16).astype(F32)
    return hi, mid, lo


def _online_softmax_step(t, m, acc, vt_aug):
    m_new = jnp.maximum(m, jnp.max(t, axis=0, keepdims=True))
    alpha = jnp.exp2(m - m_new)
    p = jnp.exp2(t - m_new).astype(BF16)
    pv = jnp.dot(vt_aug, p, preferred_element_type=F32)
    return m_new, alpha * acc + pv[0:acc.shape[0], :]


def _causal_mask(t, key0, query0, granule):
    r = lax.broadcasted_iota(jnp.int32, t.shape, 0) + key0
    c = lax.broadcasted_iota(jnp.int32, t.shape, 1) + query0
    return jnp.where(r <= (c | (granule - 1)), t, NEG_BIG)


def _flash_key_major(qi, tile, chains, keys, rhs, vt_aug, d_v, granule):
    acc_rows = d_v + 8
    n_rounds = tile // SUB
    n_pairs = len(chains) // 2
    issue_rounds = {g * n_rounds // n_pairs: (2 * g, 2 * g + 1) for g in range(n_pairs)}

    def qk_of(k0, is_diag):
        loaded = {}

        def product(n):
            half, hh = chains[n]
            n_keys = (half + 1) * HALF if is_diag else tile
            if (hh, n_keys) not in loaded:
                loaded[hh, n_keys] = keys(hh, k0, n_keys)
            return jnp.dot(loaded[hh, n_keys], rhs(hh)[:, half * HALF:(half + 1) * HALF],
                           preferred_element_type=F32)
        return product

    def visit(state, tiles):
        st = list(state)
        first = qk_of(*tiles[0])
        ss = [first(n) for n in range(len(chains))]
        for ti, (k0, is_diag) in enumerate(tiles):
            nxt = qk_of(*tiles[ti + 1]) if ti + 1 < len(tiles) else None
            ss_next = [None] * len(chains)
            for sb in range(n_rounds):
                if nxt is not None and sb in issue_rounds:
                    for n in issue_rounds[sb]:
                        ss_next[n] = nxt(n)
                ks = pl.multiple_of(k0 + sb * SUB, SUB)
                for n, (half, hh) in enumerate(chains):
                    if is_diag and sb * SUB >= (half + 1) * HALF:
                        continue
                    t = ss[n][sb * SUB:(sb + 1) * SUB, :]
                    if is_diag and (sb + 1) * SUB > half * HALF:
                        t = _causal_mask(t, sb * SUB, half * HALF, granule)
                    st[n] = _online_softmax_step(t, *st[n], vt_aug(hh, ks))
            ss = ss_next
        return tuple(st)

    def two_full(j, state):
        k0 = pl.multiple_of(2 * j * tile, tile)
        return visit(state, [(k0, False), (pl.multiple_of(k0 + tile, tile), False)])

    init = tuple((jnp.full((1, HALF), NEG_BIG, F32), jnp.zeros((acc_rows, HALF), F32)) for _ in chains)
    state = lax.fori_loop(0, qi // 2, two_full, init)
    q0 = pl.multiple_of(qi * tile, tile)
    state = lax.cond(
        qi % 2 == 1,
        lambda st: visit(st, [(pl.multiple_of(q0 - tile, tile), False), (q0, True)]),
        lambda st: visit(st, [(q0, True)]),
        state)
    return [acc[0:d_v, :] * (1.0 / acc[d_v:d_v + 1, :]) for (_, acc) in state]


def _cast_specs(weights, grid):
    n_steps = math.prod(grid)
    specs, shapes = [], []
    for w in weights:
        rows, cols = w.shape
        slab = rows // n_steps
        assert slab * n_steps == rows and slab % 16 == 0, (w.shape, n_steps)
        specs.append(pl.BlockSpec((slab, cols), lambda a, b, c: ((a * grid[1] + b) * grid[2] + c, 0)))
        shapes.append(jax.ShapeDtypeStruct(w.shape, BF16))
    return specs, shapes


def _cast_slabs(refs):
    n = len(refs) // 2
    for src, dst in zip(refs[:n], refs[n:]):
        dst[...] = src[...].astype(dst.dtype)


def _fox_kernel(q_ref, k_ref, v_ref, cum_ref, cumt_ref, *rest, tq, n_cast):
    cast_in, (o_ref, *cast_out), (vta_ref, kaug_ref) = rest[:n_cast], rest[n_cast:2 * n_cast + 1], rest[2 * n_cast + 1:]
    hp = pl.program_id(1)
    qi = pl.program_id(2)
    seq = k_ref.shape[0]
    dh = FOX_HEAD_DIM
    n_half = tq // HALF
    aug = 6

    @pl.when(qi == 0)
    def _():
        vt = v_ref[...].T
        tail_rows = lax.broadcasted_iota(jnp.int32, (16, seq), 0)
        tail = jnp.where(tail_rows < 8, 1.0, 0.0).astype(BF16)
        for hh in (0, 1):
            vta_ref[hh, 0:dh, :] = vt[hh * dh:(hh + 1) * dh, :]
            vta_ref[hh, dh:dh + 16, :] = tail
        lane = lax.broadcasted_iota(jnp.int32, (seq, LANES), 1)
        cum = cum_ref[...]
        kaug = jnp.zeros((seq, LANES), F32)
        for hh in (0, 1):
            col = jnp.sum(jnp.where(lane == 2 * hp + hh, cum, 0.0), axis=1, keepdims=True)
            neg = jnp.broadcast_to(-col, (seq, LANES))
            for idx, piece in enumerate(_split3(neg)):
                kaug = jnp.where(lane == aug * hh + idx, piece, kaug)
            kaug = jnp.where((lane >= aug * hh + 3) & (lane < aug * hh + 6), 1.0, kaug)
        kaug_ref[...] = kaug.astype(BF16)

    q0 = pl.multiple_of(qi * tq, tq)
    qt = q_ref[...].T
    row = lax.broadcasted_iota(jnp.int32, (LANES, tq), 0)
    rhs = []
    for hh in (0, 1):
        qm = jnp.where((row >= hh * dh) & (row < (hh + 1) * dh), qt, jnp.zeros_like(qt))
        ci = cumt_ref[hh:hh + 1, pl.ds(q0, tq)]
        qa = jnp.where((row >= aug * hh) & (row < aug * hh + 3), 1.0, 0.0)
        for idx, piece in enumerate(_split3(ci)):
            qa = jnp.where(row == aug * hh + 3 + idx, jnp.broadcast_to(piece, (LANES, tq)), qa)
        rhs.append(jnp.concatenate([qm, qa.astype(BF16)], axis=0))
    chains = [(half, hh) for half in range(n_half) for hh in (0, 1)]

    def keys(hh, k0, n):
        return jnp.concatenate([k_ref[pl.ds(k0, n), :], kaug_ref[pl.ds(k0, n), :]], axis=1)

    outs = _flash_key_major(qi, tq, chains, keys, lambda hh: rhs[hh],
                            lambda hh, ks: vta_ref[hh, :, pl.ds(ks, SUB)], dh, granule=1)
    o_t = jnp.concatenate(
        [jnp.concatenate([outs[half * 2 + hh] for half in range(n_half)], axis=1) for hh in (0, 1)], axis=0)
    o_ref[...] = o_t.T.astype(o_ref.dtype)
    _cast_slabs(cast_in + tuple(cast_out))


def _fox_attention(proj, cum, cumt4, seq, q_col0, cast_weights):
    m = proj.shape[0]
    n_batch = m // seq
    tq = 512
    nq = seq // tq
    n_pairs = FOX_HEADS // 2
    qb = q_col0 // LANES
    grid = (n_batch, n_pairs, nq)
    cast_specs, cast_shapes = _cast_specs(cast_weights, grid)
    return pl.pallas_call(
        functools.partial(_fox_kernel, tq=tq, n_cast=len(cast_weights)),
        grid=grid,
        in_specs=[
            pl.BlockSpec((tq, LANES), lambda b, hp, qi: (b * nq + qi, qb + hp)),
            pl.BlockSpec((seq, LANES), lambda b, hp, qi: (b, qb + n_pairs + hp)),
            pl.BlockSpec((seq, LANES), lambda b, hp, qi: (b, qb + 2 * n_pairs + hp)),
            pl.BlockSpec((seq, LANES), lambda b, hp, qi: (b, 0)),
            pl.BlockSpec((None, None, 2, seq), lambda b, hp, qi: (b, hp, 0, 0)),
        ] + cast_specs,
        out_specs=[pl.BlockSpec((tq, LANES), lambda b, hp, qi: (b * nq + qi, hp))] + cast_specs,
        out_shape=[jax.ShapeDtypeStruct((m, FOX_WIDTH), BF16)] + cast_shapes,
        scratch_shapes=[
            pltpu.VMEM((2, FOX_HEAD_DIM + 16, seq), BF16),
            pltpu.VMEM((seq, LANES), BF16),
        ],
        compiler_params=_cparams(("parallel", "parallel", "arbitrary")),
        name="fox_attn",
    )(proj, proj, proj, cum, cumt4, *cast_weights)


def _mla_kernel(qn_ref, qr_ref, kv_ref, kr_ref, *rest, tq, n_cast):
    cast_in, (o_ref, *cast_out), (vta_ref,) = rest[:n_cast], rest[n_cast:2 * n_cast + 1], rest[2 * n_cast + 1:]
    qi = pl.program_id(2)
    seq = kv_ref.shape[0]
    dv = MLA_V_DIM
    n_half = tq // HALF

    @pl.when(qi == 0)
    def _():
        tail_rows = lax.broadcasted_iota(jnp.int32, (16, seq), 0)
        tail = jnp.where(tail_rows < 8, 1.0, 0.0).astype(BF16)
        for hh in (0, 1):
            vta_ref[hh, 0:dv, :] = kv_ref[:, (2 * hh + 1) * LANES:(2 * hh + 2) * LANES].T
            vta_ref[hh, dv:dv + 16, :] = tail

    rhs = [jnp.concatenate([qn_ref[:, hh * LANES:(hh + 1) * LANES],
                            qr_ref[:, hh * LANES:(hh + 1) * LANES]], axis=1).T for hh in (0, 1)]
    chains = [(half, hh) for half in range(n_half) for hh in (0, 1)]

    def keys(hh, k0, n):
        return jnp.concatenate([kv_ref[pl.ds(k0, n), 2 * hh * LANES:(2 * hh + 1) * LANES],
                                kr_ref[pl.ds(k0, n), :]], axis=1)

    outs = _flash_key_major(qi, tq, chains, keys, lambda hh: rhs[hh],
                            lambda hh, ks: vta_ref[hh, :, pl.ds(ks, SUB)], dv, granule=CHUNK)
    o_t = jnp.concatenate(
        [jnp.concatenate([outs[half * 2 + hh] for half in range(n_half)], axis=1) for hh in (0, 1)], axis=0)
    o_ref[...] = o_t.T.astype(o_ref.dtype)
    _cast_slabs(cast_in + tuple(cast_out))


def _mla_attention(qn, qr, kv, kr, seq, cast_weights):
    m = qn.shape[0]
    n_batch = m // seq
    tq = 512
    nq = seq // tq
    pair = 2 * LANES
    grid = (n_batch, MLA_HEADS // 2, nq)
    cast_specs, cast_shapes = _cast_specs(cast_weights, grid)
    return pl.pallas_call(
        functools.partial(_mla_kernel, tq=tq, n_cast=len(cast_weights)),
        grid=grid,
        in_specs=[
            pl.BlockSpec((tq, pair), lambda b, hp, qi: (b * nq + qi, hp)),
            pl.BlockSpec((tq, pair), lambda b, hp, qi: (b * nq + qi, hp)),
            pl.BlockSpec((seq, 2 * pair), lambda b, hp, qi: (b, hp)),
            pl.BlockSpec((seq, LANES), lambda b, hp, qi: (b, 0)),
        ] + cast_specs,
        out_specs=[pl.BlockSpec((tq, pair), lambda b, hp, qi: (b * nq + qi, hp))] + cast_specs,
        out_shape=[jax.ShapeDtypeStruct((m, MLA_HEADS * MLA_V_DIM), BF16)] + cast_shapes,
        scratch_shapes=[pltpu.VMEM((2, MLA_V_DIM + 16, seq), BF16)],
        compiler_params=_cparams(("parallel", "parallel", "arbitrary")),
        name="mla_attn",
    )(qn, qr, kv, kr, *cast_weights)


def _mix_kernel(yf_ref, ym_ref, gf_ref, gm_ref, x_ref, mod_ref, wbf_ref, wbm_ref, wo_ref, g_ref, b_ref,
                o_ref, mg_ref, *, n_chunk, g_off):
    d = o_ref.shape[1]
    yf = yf_ref[...]
    ym = ym_ref[...]

    def gate(ref, c):
        window = ref[:, c * n_chunk:(c + 1) * n_chunk + LANES]
        return jax.nn.sigmoid(window[:, g_off:g_off + n_chunk].astype(F32))

    for c in range(d // n_chunk):
        cols = slice(c * n_chunk, (c + 1) * n_chunk)
        a = jnp.dot(yf, wbf_ref[:, cols], preferred_element_type=F32)
        b = jnp.dot(ym, wbm_ref[:, cols], preferred_element_type=F32)
        merged = gate(gf_ref, c) * a + gate(gm_ref, c) * b
        mg_ref[:, cols] = merged.astype(BF16)
    mix = jnp.dot(mg_ref[...], wo_ref[...], preferred_element_type=F32)
    gate = mod_ref[2:3, :]
    o_ref[...] = _layer_norm(ALPHA * x_ref[...] + gate * mix, g_ref[...], b_ref[...])


def _mix(y_fox, y_mla, proj, gf_col, gm_col, x2d, mod3, w_bf, w_bm, w_o, ln_g, ln_b, seq):
    m, d = x2d.shape
    tm = 512
    tiles_per_batch = seq // tm
    const = lambda i: (0, 0)
    resident = dict(pipeline_mode=pl.Buffered(1))
    g_off = gf_col % LANES
    assert gm_col % LANES == g_off
    g_window = (pl.Element(tm), pl.Element(d + LANES))

    def window_at(col):
        return lambda i: (pl.multiple_of(i * tm, tm), col - g_off)

    return pl.pallas_call(
        functools.partial(_mix_kernel, n_chunk=512, g_off=g_off),
        grid=(m // tm,),
        in_specs=[
            pl.BlockSpec((tm, y_fox.shape[1]), lambda i: (i, 0)),
            pl.BlockSpec((tm, y_mla.shape[1]), lambda i: (i, 0)),
            pl.BlockSpec(g_window, window_at(gf_col)),
            pl.BlockSpec(g_window, window_at(gm_col)),
            pl.BlockSpec((tm, d), lambda i: (i, 0)),
            pl.BlockSpec((None, N_ADA, d), lambda i: (i // tiles_per_batch, 0, 0)),
            pl.BlockSpec(w_bf.shape, const, **resident),
            pl.BlockSpec(w_bm.shape, const, **resident),
            pl.BlockSpec(w_o.shape, const, **resident),
            pl.BlockSpec((1, d), const),
            pl.BlockSpec((1, d), const),
        ],
        out_specs=pl.BlockSpec((tm, d), lambda i: (i, 0)),
        out_shape=jax.ShapeDtypeStruct((m, d), F32),
        scratch_shapes=[pltpu.VMEM((tm, d), BF16)],
        compiler_params=_cparams(("parallel",)),
        name="mix",
    )(y_fox, y_mla, proj, proj, x2d, mod3, w_bf, w_bm, w_o, ln_g, ln_b)


def _mlp_kernel(x_ref, mod_ref, wu_ref, wd_ref, g_ref, b_ref, o_ref, u_ref):
    f = pl.program_id(1)

    @pl.when(f == 0)
    def _():
        shift = mod_ref[3:4, :]
        scale = mod_ref[4:5, :]
        u_ref[...] = (x_ref[...] * (1.0 + scale) + shift).astype(BF16)
        o_ref[...] = jnp.zeros_like(o_ref)

    z = jnp.dot(u_ref[...], wu_ref[...], preferred_element_type=F32)
    h = jnp.square(jnp.maximum(z, 0.0)).astype(BF16)
    o_ref[...] += jnp.dot(h, wd_ref[...], preferred_element_type=F32)

    @pl.when(f == pl.num_programs(1) - 1)
    def _():
        gate = mod_ref[5:6, :]
        o_ref[...] = _layer_norm(ALPHA * x_ref[...] + gate * o_ref[...], g_ref[...], b_ref[...])


def _mlp(x1, mod3, w_up, w_down, ln_g, ln_b, seq):
    m, d = x1.shape
    d_ff = w_up.shape[1]
    tm, tf = 1024, 512
    tiles_per_batch = seq // tm
    return pl.pallas_call(
        _mlp_kernel,
        grid=(m // tm, d_ff // tf),
        in_specs=[
            pl.BlockSpec((tm, d), lambda i, f: (i, 0)),
            pl.BlockSpec((None, N_ADA, d), lambda i, f: (i // tiles_per_batch, 0, 0)),
            pl.BlockSpec((d, tf), lambda i, f: (0, f)),
            pl.BlockSpec((tf, d), lambda i, f: (f, 0)),
            pl.BlockSpec((1, d), lambda i, f: (0, 0)),
            pl.BlockSpec((1, d), lambda i, f: (0, 0)),
        ],
        out_specs=pl.BlockSpec((tm, d), lambda i, f: (i, 0)),
        out_shape=jax.ShapeDtypeStruct((m, d), F32),
        scratch_shapes=[pltpu.VMEM((tm, d), BF16)],
        compiler_params=_cparams(("parallel", "arbitrary")),
        name="mlp",
    )(x1, mod3, w_up, w_down, ln_g, ln_b)


def _pad_cols(w, width):
    return jnp.pad(w, ((0, 0), (0, width - w.shape[1])))


def _swap_halves(w):
    half = w.shape[-1] // 2
    return jnp.concatenate([w[..., half:], w[..., :half]], axis=-1)


def _layout_q_up(w_q_up):
    r = w_q_up.shape[0]
    w3 = w_q_up.reshape(r, MLA_HEADS, MLA_QK_DIM)
    w_nope = w3[:, :, :MLA_NOPE_DIM].reshape(r, MLA_HEADS * MLA_NOPE_DIM)
    w_rope = w3[:, :, MLA_NOPE_DIM:]
    pad = ((0, 0), (0, 0), (0, LANES - MLA_ROPE_DIM))
    w_r = jnp.pad(w_rope, pad).reshape(r, MLA_HEADS * LANES)
    w_rs = jnp.pad(_swap_halves(w_rope), pad).reshape(r, MLA_HEADS * LANES)
    return jnp.concatenate([w_nope, w_r, w_rs], axis=1).astype(BF16)


def _rope_tables(seq):
    pos = np.arange(seq, dtype=np.float64)
    inv_freq = ROPE_THETA ** (-np.arange(0, MLA_ROPE_DIM, 2, dtype=np.float64) / MLA_ROPE_DIM)
    ang = pos[:, None] * inv_freq[None, :]
    cos, sin = np.cos(ang), np.sin(ang)
    pad = np.zeros((seq, LANES - MLA_ROPE_DIM))
    cos_t = np.concatenate([cos, cos, pad], axis=1).astype(np.float32)
    sin_t = np.concatenate([-sin, sin, pad], axis=1).astype(np.float32)
    return jnp.asarray(cos_t), jnp.asarray(sin_t)


def kernel(x, c, w_ada, b_ada, w_in, b_forget, g_q_norm, w_q_up, g_kv_norm, w_kv_up, w_branch_fox,
           w_branch_mla, w_out, ln1_g, ln1_b, w_mlp_up, w_mlp_down, ln2_g, ln2_b):
    n_batch, seq, d = x.shape
    assert w_ada.shape[0] == DEPTH
    m = n_batch * seq
    x2d = x.reshape(m, d)

    mod3 = _ada(c, w_ada[0], b_ada[0]).reshape(n_batch, N_ADA, d)

    small_col = 3 * FOX_WIDTH
    gf_col = small_col + FOX_HEADS + MLA_Q_LORA + MLA_KV_LORA + MLA_ROPE_DIM
    gm_col = gf_col + d
    assert w_in.shape[2] == gm_col + d and small_col == IN_SMALL_TILE * IN_TN and gf_col <= small_col + IN_TN
    proj, proj_lo = _inproj(x2d, mod3, w_in[0], seq)

    cos_t, sin_t = _rope_tables(seq)
    bf_pad = _pad_cols(b_forget[0].reshape(1, FOX_HEADS), LANES)
    qn, qr, kv, kr, cum, cumt = _prep(
        proj, proj_lo, bf_pad, g_q_norm[0].reshape(1, -1), g_kv_norm[0].reshape(1, -1),
        _layout_q_up(w_q_up[0]), w_kv_up[0].astype(BF16), cos_t, sin_t, seq)

    y_fox, w_up_b, w_down_b = _fox_attention(
        proj, cum, cumt.reshape(n_batch, FOX_HEADS // 2, 2, seq), seq, 0, (w_mlp_up[0], w_mlp_down[0]))
    y_mla, w_bf_b, w_bm_b, w_o_b = _mla_attention(
        qn, qr, kv, kr, seq, (w_branch_fox[0], w_branch_mla[0], w_out[0]))

    x1 = _mix(y_fox, y_mla, proj, gf_col, gm_col, x2d, mod3, w_bf_b, w_bm_b, w_o_b,
              ln1_g[0].reshape(1, d), ln1_b[0].reshape(1, d), seq)
    x2 = _mlp(x1, mod3, w_up_b, w_down_b, ln2_g[0].reshape(1, d), ln2_b[0].reshape(1, d), seq)
    return x2.reshape(n_batch, seq, d)
```

```python
import functools
import math

import jax
import jax.numpy as jnp
import numpy as np
from jax import lax
from jax.experimental import pallas as pl
from jax.experimental.pallas import tpu as pltpu

F32 = jnp.float32
BF16 = jnp.bfloat16

FOX_HEADS = 16
FOX_HEAD_DIM = 64
FOX_WIDTH = FOX_HEADS * FOX_HEAD_DIM
MLA_HEADS = 8
MLA_NOPE_DIM = 128
MLA_ROPE_DIM = 64
MLA_V_DIM = 128
MLA_Q_LORA = 512
MLA_KV_LORA = 256
MLA_QK_DIM = MLA_NOPE_DIM + MLA_ROPE_DIM
CHUNK = 64
ROPE_THETA = 10000.0
LN_EPS = 1e-5
RMS_EPS = 1e-6
N_ADA = 6
DEPTH = 1
ALPHA = (2.0 * DEPTH) ** 0.25

LANES = 128
LOG2E = math.log2(math.e)
FOX_QSCALE = FOX_HEAD_DIM ** -0.5 * LOG2E
MLA_QSCALE = MLA_QK_DIM ** -0.5 * LOG2E
NEG_BIG = -1e30
VMEM_LIMIT = 56 * 1024 * 1024

SUB = 128
HALF = 256


def _cparams(semantics):
    return pltpu.CompilerParams(dimension_semantics=semantics, vmem_limit_bytes=VMEM_LIMIT)


def _layer_norm(y, g, b):
    mu = jnp.mean(y, axis=-1, keepdims=True)
    yc = y - mu
    var = jnp.mean(yc * yc, axis=-1, keepdims=True)
    return yc * lax.rsqrt(var + LN_EPS) * g + b


def _rms_norm(c, g):
    return c * lax.rsqrt(jnp.mean(c * c, axis=-1, keepdims=True) + RMS_EPS) * g


def _ada_kernel(ct_ref, w_ref, b_ref, o_ref, *, n_batch, k_chunk):
    d = w_ref.shape[0]
    ct = ct_ref[...]
    s = ct * jax.nn.sigmoid(ct)
    accs = [b_ref[...] for _ in range(n_batch)]
    for kc in range(d // k_chunk):
        rows = slice(kc * k_chunk, (kc + 1) * k_chunk)
        w = w_ref[rows, :]
        for b in range(n_batch):
            accs[b] = accs[b] + jnp.sum(w * s[rows, b:b + 1], axis=0, keepdims=True)
    for b in range(n_batch):
        o_ref[b:b + 1, :] = accs[b]


def _ada(c, w_ada, b_ada):
    n_batch, d = c.shape
    n = w_ada.shape[1]
    tn = 1024
    return pl.pallas_call(
        functools.partial(_ada_kernel, n_batch=n_batch, k_chunk=256),
        grid=(n // tn,),
        in_specs=[
            pl.BlockSpec((d, n_batch), lambda j: (0, 0)),
            pl.BlockSpec((d, tn), lambda j: (0, j)),
            pl.BlockSpec((1, tn), lambda j: (0, j)),
        ],
        out_specs=pl.BlockSpec((n_batch, tn), lambda j: (0, j)),
        out_shape=jax.ShapeDtypeStruct((n_batch, n), F32),
        compiler_params=_cparams(("arbitrary",)),
        name="ada",
    )(c.T, w_ada, b_ada.reshape(1, n))


IN_TN = 1024
IN_SMALL_TILE = 3 * FOX_WIDTH // IN_TN


def _inproj_kernel(x_ref, mod_ref, w_ref, o_ref, lo_ref, u_ref, *, n_cols):
    j = pl.program_id(1)
    last = pl.num_programs(1) - 1

    @pl.when(j == 0)
    def _():
        shift = mod_ref[0:1, :]
        scale = mod_ref[1:2, :]
        u_ref[...] = (x_ref[...] * (1.0 + scale) + shift).astype(BF16)

    def product(wt):
        return lax.dot_general(u_ref[...], wt.astype(BF16), (((1,), (1,)), ((), ())),
                               preferred_element_type=F32)

    @pl.when(j == 0)
    def _():
        o_ref[...] = (product(w_ref[...]) * FOX_QSCALE).astype(o_ref.dtype)

    @pl.when(j == IN_SMALL_TILE)
    def _():
        acc = product(w_ref[...])
        hi = acc.astype(o_ref.dtype)
        o_ref[...] = hi
        lo_ref[...] = (acc - hi.astype(F32)).astype(lo_ref.dtype)

    @pl.when(j == last)
    def _():
        row = lax.broadcasted_iota(jnp.int32, w_ref.shape, 0)
        wt = jnp.where(row < n_cols - last * IN_TN, w_ref[...], 0.0)
        o_ref[...] = product(wt).astype(o_ref.dtype)

    @pl.when((j != 0) & (j != IN_SMALL_TILE) & (j != last))
    def _():
        o_ref[...] = product(w_ref[...]).astype(o_ref.dtype)


def _inproj(x2d, mod3, w_in_t, seq):
    m, d = x2d.shape
    n_cols = w_in_t.shape[0]
    tm = 1024
    n_tiles = pl.cdiv(n_cols, IN_TN)
    tiles_per_batch = seq // tm
    return pl.pallas_call(
        functools.partial(_inproj_kernel, n_cols=n_cols),
        grid=(m // tm, n_tiles),
        in_specs=[
            pl.BlockSpec((tm, d), lambda i, j: (i, 0)),
            pl.BlockSpec((None, N_ADA, d), lambda i, j: (i // tiles_per_batch, 0, 0)),
            pl.BlockSpec((IN_TN, d), lambda i, j: (j, 0)),
        ],
        out_specs=[pl.BlockSpec((tm, IN_TN), lambda i, j: (i, j)),
                   pl.BlockSpec((tm, IN_TN), lambda i, j: (i, 0))],
        out_shape=[jax.ShapeDtypeStruct((m, n_tiles * IN_TN), BF16),
                   jax.ShapeDtypeStruct((m, IN_TN), BF16)],
        scratch_shapes=[pltpu.VMEM((tm, d), BF16)],
        compiler_params=_cparams(("parallel", "arbitrary")),
        name="inproj",
    )(x2d, mod3, w_in_t)


def _log_sigmoid(x):
    return jnp.minimum(x, 0.0) - jnp.log1p(jnp.exp(-jnp.abs(x)))


def _prep_kernel(hi_ref, lo_ref, bf_ref, gq_ref, gkv_ref, wq_ref, wkv_ref, cos_ref, sin_ref,
                 qn_ref, qr_ref, kv_ref, kr_ref, cum_ref, cumt_ref, carry_ref, *, tiles_per_batch):
    i = pl.program_id(0)
    tm = hi_ref.shape[0]
    ps = hi_ref[...].astype(F32) + lo_ref[...].astype(F32)
    c0 = FOX_HEADS
    c1 = c0 + MLA_Q_LORA
    c2 = c1 + MLA_KV_LORA
    c3 = c2 + MLA_ROPE_DIM
    half = MLA_ROPE_DIM // 2
    lane_pad = jnp.zeros((tm, LANES - MLA_ROPE_DIM), F32)
    k_rope = jnp.concatenate([ps[:, c2:c3], lane_pad], axis=1)
    k_rope_partner = jnp.concatenate([ps[:, c2 + half:c3], ps[:, c2:c2 + half], lane_pad], axis=1)
    cos = cos_ref[...]
    sin = sin_ref[...]

    logf = _log_sigmoid(ps[:, 0:LANES] + bf_ref[...]) * LOG2E
    hi = logf.astype(BF16)
    r1 = logf - hi.astype(F32)
    mid = r1.astype(BF16)
    lo = (r1 - mid.astype(F32)).astype(BF16)
    row = lax.broadcasted_iota(jnp.int32, (tm, tm), 0)
    col = lax.broadcasted_iota(jnp.int32, (tm, tm), 1)
    tri = (row >= col).astype(BF16)
    cs = (jnp.dot(tri, hi, preferred_element_type=F32)
          + jnp.dot(tri, mid, preferred_element_type=F32)
          + jnp.dot(tri, lo, preferred_element_type=F32))

    @pl.when(i % tiles_per_batch == 0)
    def _():
        carry_ref[...] = jnp.zeros_like(carry_ref)

    cum = cs + carry_ref[...]
    carry_ref[...] = cum[tm - 1:tm, :]
    cum_ref[...] = cum
    cumt_ref[...] = cum.T[0:FOX_HEADS, :]

    cqn = _rms_norm(ps[:, c0:c1], gq_ref[...]).astype(BF16)
    qa = jnp.dot(cqn, wq_ref[...], preferred_element_type=F32)
    hw = MLA_HEADS * LANES
    qn_ref[...] = (qa[:, 0:hw] * MLA_QSCALE).astype(BF16)
    for h in range(MLA_HEADS):
        a = qa[:, hw + h * LANES: hw + (h + 1) * LANES]
        b = qa[:, 2 * hw + h * LANES: 2 * hw + (h + 1) * LANES]
        qr_ref[:, h * LANES:(h + 1) * LANES] = ((a * cos + b * sin) * MLA_QSCALE).astype(BF16)

    ckvn = _rms_norm(ps[:, c1:c2], gkv_ref[...]).astype(BF16)
    kv_ref[...] = jnp.dot(ckvn, wkv_ref[...], preferred_element_type=F32).astype(BF16)
    kr_ref[...] = (k_rope * cos + k_rope_partner * sin).astype(BF16)


def _prep(proj, proj_lo, bf_pad, g_q, g_kv, w_q_all, w_kv, cos_t, sin_t, seq):
    m = proj.shape[0]
    n_batch = m // seq
    tm = 512
    tiles_per_batch = seq // tm
    hw = MLA_HEADS * LANES
    const = lambda i: (0, 0)
    return pl.pallas_call(
        functools.partial(_prep_kernel, tiles_per_batch=tiles_per_batch),
        grid=(m // tm,),
        in_specs=[
            pl.BlockSpec((tm, IN_TN), lambda i: (i, IN_SMALL_TILE)),
            pl.BlockSpec((tm, IN_TN), lambda i: (i, 0)),
            pl.BlockSpec(bf_pad.shape, const),
            pl.BlockSpec(g_q.shape, const),
            pl.BlockSpec(g_kv.shape, const),
            pl.BlockSpec(w_q_all.shape, const),
            pl.BlockSpec(w_kv.shape, const),
            pl.BlockSpec((tm, LANES), lambda i: (i % tiles_per_batch, 0)),
            pl.BlockSpec((tm, LANES), lambda i: (i % tiles_per_batch, 0)),
        ],
        out_specs=[
            pl.BlockSpec((tm, hw), lambda i: (i, 0)),
            pl.BlockSpec((tm, hw), lambda i: (i, 0)),
            pl.BlockSpec((tm, w_kv.shape[1]), lambda i: (i, 0)),
            pl.BlockSpec((tm, LANES), lambda i: (i, 0)),
            pl.BlockSpec((tm, LANES), lambda i: (i, 0)),
            pl.BlockSpec((None, FOX_HEADS, tm), lambda i: (i // tiles_per_batch, 0, i % tiles_per_batch)),
        ],
        out_shape=[
            jax.ShapeDtypeStruct((m, hw), BF16),
            jax.ShapeDtypeStruct((m, hw), BF16),
            jax.ShapeDtypeStruct((m, w_kv.shape[1]), BF16),
            jax.ShapeDtypeStruct((m, LANES), BF16),
            jax.ShapeDtypeStruct((m, LANES), F32),
            jax.ShapeDtypeStruct((n_batch, FOX_HEADS, seq), F32),
        ],
        scratch_shapes=[pltpu.VMEM((1, LANES), F32)],
        compiler_params=_cparams(("arbitrary",)),
        name="prep",
    )(proj, proj_lo, bf_pad, g_q, g_kv, w_q_all, w_kv, cos_t, sin_t)


def _split3(x):
    hi = x.astype(BF16).astype(F32)
    r = x - hi
    mid = r.astype(BF16).astype(F32)
    lo = (r - mid).astype(BF16).astype(F32)
    return hi, mid, lo


def _online_softmax_step(t, m, acc, vt_aug):
    m_new = jnp.maximum(m, jnp.max(t, axis=0, keepdims=True))
    alpha = jnp.exp2(m - m_new)
    p = jnp.exp2(t - m_new).astype(BF16)
    pv = jnp.dot(vt_aug, p, preferred_element_type=F32)
    return m_new, alpha * acc + pv[0:acc.shape[0], :]


def _causal_mask(t, key0, query0, granule):
    r = lax.broadcasted_iota(jnp.int32, t.shape, 0) + key0
    c = lax.broadcasted_iota(jnp.int32, t.shape, 1) + query0
    return jnp.where(r <= (c | (granule - 1)), t, NEG_BIG)


def _flash_key_major(qi, tile, chains, keys, rhs, vt_aug, d_v, granule):
    acc_rows = d_v + 8
    n_rounds = tile // SUB
    n_pairs = len(chains) // 2
    issue_rounds = {g * n_rounds // n_pairs: (2 * g, 2 * g + 1) for g in range(n_pairs)}

    def qk_of(k0, is_diag):
        loaded = {}

        def product(n):
            half, hh = chains[n]
            n_keys = (half + 1) * HALF if is_diag else tile
            if (hh, n_keys) not in loaded:
                loaded[hh, n_keys] = keys(hh, k0, n_keys)
            return jnp.dot(loaded[hh, n_keys], rhs(hh)[:, half * HALF:(half + 1) * HALF],
                           preferred_element_type=F32)
        return product

    def visit(state, tiles):
        st = list(state)
        first = qk_of(*tiles[0])
        ss = [first(n) for n in range(len(chains))]
        for ti, (k0, is_diag) in enumerate(tiles):
            nxt = qk_of(*tiles[ti + 1]) if ti + 1 < len(tiles) else None
            ss_next = [None] * len(chains)
            for sb in range(n_rounds):
                if nxt is not None and sb in issue_rounds:
                    for n in issue_rounds[sb]:
                        ss_next[n] = nxt(n)
                ks = pl.multiple_of(k0 + sb * SUB, SUB)
                for n, (half, hh) in enumerate(chains):
                    if is_diag and sb * SUB >= (half + 1) * HALF:
                        continue
                    t = ss[n][sb * SUB:(sb + 1) * SUB, :]
                    if is_diag and (sb + 1) * SUB > half * HALF:
                        t = _causal_mask(t, sb * SUB, half * HALF, granule)
                    st[n] = _online_softmax_step(t, *st[n], vt_aug(hh, ks))
            ss = ss_next
        return tuple(st)

    def two_full(j, state):
        k0 = pl.multiple_of(2 * j * tile, tile)
        return visit(state, [(k0, False), (pl.multiple_of(k0 + tile, tile), False)])

    init = tuple((jnp.full((1, HALF), NEG_BIG, F32), jnp.zeros((acc_rows, HALF), F32)) for _ in chains)
    state = lax.fori_loop(0, qi // 2, two_full, init)
    q0 = pl.multiple_of(qi * tile, tile)
    state = lax.cond(
        qi % 2 == 1,
        lambda st: visit(st, [(pl.multiple_of(q0 - tile, tile), False), (q0, True)]),
        lambda st: visit(st, [(q0, True)]),
        state)
    return [acc[0:d_v, :] * (1.0 / acc[d_v:d_v + 1, :]) for (_, acc) in state]


def _cast_specs(weights, grid):
    n_steps = math.prod(grid)
    specs, shapes = [], []
    for w in weights:
        rows, cols = w.shape
        slab = rows // n_steps
        assert slab * n_steps == rows and slab % 16 == 0, (w.shape, n_steps)
        specs.append(pl.BlockSpec((slab, cols), lambda a, b, c: ((a * grid[1] + b) * grid[2] + c, 0)))
        shapes.append(jax.ShapeDtypeStruct(w.shape, BF16))
    return specs, shapes


def _cast_slabs(refs):
    n = len(refs) // 2
    for src, dst in zip(refs[:n], refs[n:]):
        dst[...] = src[...].astype(dst.dtype)


def _fox_kernel(q_ref, k_ref, v_ref, cum_ref, cumt_ref, *rest, tq, n_cast):
    cast_in, (o_ref, *cast_out), (vta_ref, kaug_ref) = rest[:n_cast], rest[n_cast:2 * n_cast + 1], rest[2 * n_cast + 1:]
    hp = pl.program_id(1)
    qi = pl.program_id(2)
    seq = k_ref.shape[0]
    dh = FOX_HEAD_DIM
    n_half = tq // HALF
    aug = 6

    @pl.when(qi == 0)
    def _():
        vt = v_ref[...].T
        tail_rows = lax.broadcasted_iota(jnp.int32, (16, seq), 0)
        tail = jnp.where(tail_rows < 8, 1.0, 0.0).astype(BF16)
        for hh in (0, 1):
            vta_ref[hh, 0:dh, :] = vt[hh * dh:(hh + 1) * dh, :]
            vta_ref[hh, dh:dh + 16, :] = tail
        lane = lax.broadcasted_iota(jnp.int32, (seq, LANES), 1)
        cum = cum_ref[...]
        kaug = jnp.zeros((seq, LANES), F32)
        for hh in (0, 1):
            col = jnp.sum(jnp.where(lane == 2 * hp + hh, cum, 0.0), axis=1, keepdims=True)
            neg = jnp.broadcast_to(-col, (seq, LANES))
            for idx, piece in enumerate(_split3(neg)):
                kaug = jnp.where(lane == aug * hh + idx, piece, kaug)
            kaug = jnp.where((lane >= aug * hh + 3) & (lane < aug * hh + 6), 1.0, kaug)
        kaug_ref[...] = kaug.astype(BF16)

    q0 = pl.multiple_of(qi * tq, tq)
    qt = q_ref[...].T
    row = lax.broadcasted_iota(jnp.int32, (LANES, tq), 0)
    rhs = []
    for hh in (0, 1):
        qm = jnp.where((row >= hh * dh) & (row < (hh + 1) * dh), qt, jnp.zeros_like(qt))
        ci = cumt_ref[hh:hh + 1, pl.ds(q0, tq)]
        qa = jnp.where((row >= aug * hh) & (row < aug * hh + 3), 1.0, 0.0)
        for idx, piece in enumerate(_split3(ci)):
            qa = jnp.where(row == aug * hh + 3 + idx, jnp.broadcast_to(piece, (LANES, tq)), qa)
        rhs.append(jnp.concatenate([qm, qa.astype(BF16)], axis=0))
    chains = [(half, hh) for half in range(n_half) for hh in (0, 1)]

    def keys(hh, k0, n):
        return jnp.concatenate([k_ref[pl.ds(k0, n), :], kaug_ref[pl.ds(k0, n), :]], axis=1)

    outs = _flash_key_major(qi, tq, chains, keys, lambda hh: rhs[hh],
                            lambda hh, ks: vta_ref[hh, :, pl.ds(ks, SUB)], dh, granule=1)
    o_t = jnp.concatenate(
        [jnp.concatenate([outs[half * 2 + hh] for half in range(n_half)], axis=1) for hh in (0, 1)], axis=0)
    o_ref[...] = o_t.T.astype(o_ref.dtype)
    _cast_slabs(cast_in + tuple(cast_out))


def _fox_attention(proj, cum, cumt4, seq, q_col0, cast_weights):
    m = proj.shape[0]
    n_batch = m // seq
    tq = 512
    nq = seq // tq
    n_pairs = FOX_HEADS // 2
    qb = q_col0 // LANES
    grid = (n_batch, n_pairs, nq)
    cast_specs, cast_shapes = _cast_specs(cast_weights, grid)
    return pl.pallas_call(
        functools.partial(_fox_kernel, tq=tq, n_cast=len(cast_weights)),
        grid=grid,
        in_specs=[
            pl.BlockSpec((tq, LANES), lambda b, hp, qi: (b * nq + qi, qb + hp)),
            pl.BlockSpec((seq, LANES), lambda b, hp, qi: (b, qb + n_pairs + hp)),
            pl.BlockSpec((seq, LANES), lambda b, hp, qi: (b, qb + 2 * n_pairs + hp)),
            pl.BlockSpec((seq, LANES), lambda b, hp, qi: (b, 0)),
            pl.BlockSpec((None, None, 2, seq), lambda b, hp, qi: (b, hp, 0, 0)),
        ] + cast_specs,
        out_specs=[pl.BlockSpec((tq, LANES), lambda b, hp, qi: (b * nq + qi, hp))] + cast_specs,
        out_shape=[jax.ShapeDtypeStruct((m, FOX_WIDTH), BF16)] + cast_shapes,
        scratch_shapes=[
            pltpu.VMEM((2, FOX_HEAD_DIM + 16, seq), BF16),
            pltpu.VMEM((seq, LANES), BF16),
        ],
        compiler_params=_cparams(("parallel", "parallel", "arbitrary")),
        name="fox_attn",
    )(proj, proj, proj, cum, cumt4, *cast_weights)


def _mla_kernel(qn_ref, qr_ref, kv_ref, kr_ref, *rest, tq, n_cast):
    cast_in, (o_ref, *cast_out), (vta_ref,) = rest[:n_cast], rest[n_cast:2 * n_cast + 1], rest[2 * n_cast + 1:]
    qi = pl.program_id(2)
    seq = kv_ref.shape[0]
    dv = MLA_V_DIM
    n_half = tq // HALF

    @pl.when(qi == 0)
    def _():
        tail_rows = lax.broadcasted_iota(jnp.int32, (16, seq), 0)
        tail = jnp.where(tail_rows < 8, 1.0, 0.0).astype(BF16)
        for hh in (0, 1):
            vta_ref[hh, 0:dv, :] = kv_ref[:, (2 * hh + 1) * LANES:(2 * hh + 2) * LANES].T
            vta_ref[hh, dv:dv + 16, :] = tail

    rhs = [jnp.concatenate([qn_ref[:, hh * LANES:(hh + 1) * LANES],
                            qr_ref[:, hh * LANES:(hh + 1) * LANES]], axis=1).T for hh in (0, 1)]
    chains = [(half, hh) for half in range(n_half) for hh in (0, 1)]

    def keys(hh, k0, n):
        return jnp.concatenate([kv_ref[pl.ds(k0, n), 2 * hh * LANES:(2 * hh + 1) * LANES],
                                kr_ref[pl.ds(k0, n), :]], axis=1)

    outs = _flash_key_major(qi, tq, chains, keys, lambda hh: rhs[hh],
                            lambda hh, ks: vta_ref[hh, :, pl.ds(ks, SUB)], dv, granule=CHUNK)
    o_t = jnp.concatenate(
        [jnp.concatenate([outs[half * 2 + hh] for half in range(n_half)], axis=1) for hh in (0, 1)], axis=0)
    o_ref[...] = o_t.T.astype(o_ref.dtype)
    _cast_slabs(cast_in + tuple(cast_out))


def _mla_attention(qn, qr, kv, kr, seq, cast_weights):
    m = qn.shape[0]
    n_batch = m // seq
    tq = 512
    nq = seq // tq
    pair = 2 * LANES
    grid = (n_batch, MLA_HEADS // 2, nq)
    cast_specs, cast_shapes = _cast_specs(cast_weights, grid)
    return pl.pallas_call(
        functools.partial(_mla_kernel, tq=tq, n_cast=len(cast_weights)),
        grid=grid,
        in_specs=[
            pl.BlockSpec((tq, pair), lambda b, hp, qi: (b * nq + qi, hp)),
            pl.BlockSpec((tq, pair), lambda b, hp, qi: (b * nq + qi, hp)),
            pl.BlockSpec((seq, 2 * pair), lambda b, hp, qi: (b, hp)),
            pl.BlockSpec((seq, LANES), lambda b, hp, qi: (b, 0)),
        ] + cast_specs,
        out_specs=[pl.BlockSpec((tq, pair), lambda b, hp, qi: (b * nq + qi, hp))] + cast_specs,
        out_shape=[jax.ShapeDtypeStruct((m, MLA_HEADS * MLA_V_DIM), BF16)] + cast_shapes,
        scratch_shapes=[pltpu.VMEM((2, MLA_V_DIM + 16, seq), BF16)],
        compiler_params=_cparams(("parallel", "parallel", "arbitrary")),
        name="mla_attn",
    )(qn, qr, kv, kr, *cast_weights)


def _mix_kernel(yf_ref, ym_ref, gf_ref, gm_ref, x_ref, mod_ref, wbf_ref, wbm_ref, wo_ref, g_ref, b_ref,
                o_ref, mg_ref, *, n_chunk, g_off):
    d = o_ref.shape[1]
    yf = yf_ref[...]
    ym = ym_ref[...]

    def gate(ref, c):
        window = ref[:, c * n_chunk:(c + 1) * n_chunk + LANES]
        return jax.nn.sigmoid(window[:, g_off:g_off + n_chunk].astype(F32))

    for c in range(d // n_chunk):
        cols = slice(c * n_chunk, (c + 1) * n_chunk)
        a = jnp.dot(yf, wbf_ref[:, cols], preferred_element_type=F32)
        b = jnp.dot(ym, wbm_ref[:, cols], preferred_element_type=F32)
        merged = gate(gf_ref, c) * a + gate(gm_ref, c) * b
        mg_ref[:, cols] = merged.astype(BF16)
    mix = jnp.dot(mg_ref[...], wo_ref[...], preferred_element_type=F32)
    gate = mod_ref[2:3, :]
    o_ref[...] = _layer_norm(ALPHA * x_ref[...] + gate * mix, g_ref[...], b_ref[...])


def _mix(y_fox, y_mla, proj, gf_col, gm_col, x2d, mod3, w_bf, w_bm, w_o, ln_g, ln_b, seq):
    m, d = x2d.shape
    tm = 512
    tiles_per_batch = seq // tm
    const = lambda i: (0, 0)
    resident = dict(pipeline_mode=pl.Buffered(1))
    g_off = gf_col % LANES
    assert gm_col % LANES == g_off
    g_window = (pl.Element(tm), pl.Element(d + LANES))

    def window_at(col):
        return lambda i: (pl.multiple_of(i * tm, tm), col - g_off)

    return pl.pallas_call(
        functools.partial(_mix_kernel, n_chunk=512, g_off=g_off),
        grid=(m // tm,),
        in_specs=[
            pl.BlockSpec((tm, y_fox.shape[1]), lambda i: (i, 0)),
            pl.BlockSpec((tm, y_mla.shape[1]), lambda i: (i, 0)),
            pl.BlockSpec(g_window, window_at(gf_col)),
            pl.BlockSpec(g_window, window_at(gm_col)),
            pl.BlockSpec((tm, d), lambda i: (i, 0)),
            pl.BlockSpec((None, N_ADA, d), lambda i: (i // tiles_per_batch, 0, 0)),
            pl.BlockSpec(w_bf.shape, const, **resident),
            pl.BlockSpec(w_bm.shape, const, **resident),
            pl.BlockSpec(w_o.shape, const, **resident),
            pl.BlockSpec((1, d), const),
            pl.BlockSpec((1, d), const),
        ],
        out_specs=pl.BlockSpec((tm, d), lambda i: (i, 0)),
        out_shape=jax.ShapeDtypeStruct((m, d), F32),
        scratch_shapes=[pltpu.VMEM((tm, d), BF16)],
        compiler_params=_cparams(("parallel",)),
        name="mix",
    )(y_fox, y_mla, proj, proj, x2d, mod3, w_bf, w_bm, w_o, ln_g, ln_b)


def _mlp_kernel(x_ref, mod_ref, wu_ref, wd_ref, g_ref, b_ref, o_ref, u_ref):
    f = pl.program_id(1)

    @pl.when(f == 0)
    def _():
        shift = mod_ref[3:4, :]
        scale = mod_ref[4:5, :]
        u_ref[...] = (x_ref[...] * (1.0 + scale) + shift).astype(BF16)
        o_ref[...] = jnp.zeros_like(o_ref)

    z = jnp.dot(u_ref[...], wu_ref[...], preferred_element_type=F32)
    h = jnp.square(jnp.maximum(z, 0.0)).astype(BF16)
    o_ref[...] += jnp.dot(h, wd_ref[...], preferred_element_type=F32)

    @pl.when(f == pl.num_programs(1) - 1)
    def _():
        gate = mod_ref[5:6, :]
        o_ref[...] = _layer_norm(ALPHA * x_ref[...] + gate * o_ref[...], g_ref[...], b_ref[...])


def _mlp(x1, mod3, w_up, w_down, ln_g, ln_b, seq):
    m, d = x1.shape
    d_ff = w_up.shape[1]
    tm, tf = 1024, 512
    tiles_per_batch = seq // tm
    return pl.pallas_call(
        _mlp_kernel,
        grid=(m // tm, d_ff // tf),
        in_specs=[
            pl.BlockSpec((tm, d), lambda i, f: (i, 0)),
            pl.BlockSpec((None, N_ADA, d), lambda i, f: (i // tiles_per_batch, 0, 0)),
            pl.BlockSpec((d, tf), lambda i, f: (0, f)),
            pl.BlockSpec((tf, d), lambda i, f: (f, 0)),
            pl.BlockSpec((1, d), lambda i, f: (0, 0)),
            pl.BlockSpec((1, d), lambda i, f: (0, 0)),
        ],
        out_specs=pl.BlockSpec((tm, d), lambda i, f: (i, 0)),
        out_shape=jax.ShapeDtypeStruct((m, d), F32),
        scratch_shapes=[pltpu.VMEM((tm, d), BF16)],
        compiler_params=_cparams(("parallel", "arbitrary")),
        name="mlp",
    )(x1, mod3, w_up, w_down, ln_g, ln_b)


def _pad_cols(w, width):
    return jnp.pad(w, ((0, 0), (0, width - w.shape[1])))


def _swap_halves(w):
    half = w.shape[-1] // 2
    return jnp.concatenate([w[..., half:], w[..., :half]], axis=-1)


def _layout_q_up(w_q_up):
    r = w_q_up.shape[0]
    w3 = w_q_up.reshape(r, MLA_HEADS, MLA_QK_DIM)
    w_nope = w3[:, :, :MLA_NOPE_DIM].reshape(r, MLA_HEADS * MLA_NOPE_DIM)
    w_rope = w3[:, :, MLA_NOPE_DIM:]
    pad = ((0, 0), (0, 0), (0, LANES - MLA_ROPE_DIM))
    w_r = jnp.pad(w_rope, pad).reshape(r, MLA_HEADS * LANES)
    w_rs = jnp.pad(_swap_halves(w_rope), pad).reshape(r, MLA_HEADS * LANES)
    return jnp.concatenate([w_nope, w_r, w_rs], axis=1).astype(BF16)


def _rope_tables(seq):
    pos = np.arange(seq, dtype=np.float64)
    inv_freq = ROPE_THETA ** (-np.arange(0, MLA_ROPE_DIM, 2, dtype=np.float64) / MLA_ROPE_DIM)
    ang = pos[:, None] * inv_freq[None, :]
    cos, sin = np.cos(ang), np.sin(ang)
    pad = np.zeros((seq, LANES - MLA_ROPE_DIM))
    cos_t = np.concatenate([cos, cos, pad], axis=1).astype(np.float32)
    sin_t = np.concatenate([-sin, sin, pad], axis=1).astype(np.float32)
    return jnp.asarray(cos_t), jnp.asarray(sin_t)


def kernel(x, c, w_ada, b_ada, w_in, b_forget, g_q_norm, w_q_up, g_kv_norm, w_kv_up, w_branch_fox,
           w_branch_mla, w_out, ln1_g, ln1_b, w_mlp_up, w_mlp_down, ln2_g, ln2_b):
    n_batch, seq, d = x.shape
    assert w_ada.shape[0] == DEPTH
    m = n_batch * seq
    x2d = x.reshape(m, d)

    mod3 = _ada(c, w_ada[0], b_ada[0]).reshape(n_batch, N_ADA, d)

    small_col = 3 * FOX_WIDTH
    gf_col = small_col + FOX_HEADS + MLA_Q_LORA + MLA_KV_LORA + MLA_ROPE_DIM
    gm_col = gf_col + d
    assert w_in.shape[2] == gm_col + d and small_col == IN_SMALL_TILE * IN_TN and gf_col <= small_col + IN_TN
    proj, proj_lo = _inproj(x2d, mod3, w_in[0].T, seq)

    cos_t, sin_t = _rope_tables(seq)
    bf_pad = _pad_cols(b_forget[0].reshape(1, FOX_HEADS), LANES)
    qn, qr, kv, kr, cum, cumt = _prep(
        proj, proj_lo, bf_pad, g_q_norm[0].reshape(1, -1), g_kv_norm[0].reshape(1, -1),
        _layout_q_up(w_q_up[0]), w_kv_up[0].astype(BF16), cos_t, sin_t, seq)

    y_fox, w_up_b, w_down_b = _fox_attention(
        proj, cum, cumt.reshape(n_batch, FOX_HEADS // 2, 2, seq), seq, 0, (w_mlp_up[0], w_mlp_down[0]))
    y_mla, w_bf_b, w_bm_b, w_o_b = _mla_attention(
        qn, qr, kv, kr, seq, (w_branch_fox[0], w_branch_mla[0], w_out[0]))

    x1 = _mix(y_fox, y_mla, proj, gf_col, gm_col, x2d, mod3, w_bf_b, w_bm_b, w_o_b,
              ln1_g[0].reshape(1, d), ln1_b[0].reshape(1, d), seq)
    x2 = _mlp(x1, mod3, w_up_b, w_down_b, ln2_g[0].reshape(1, d), ln2_b[0].reshape(1, d), seq)
    return x2.reshape(n_batch, seq, d)
```

```python
import functools
import math

import jax
import jax.numpy as jnp
import numpy as np
from jax import lax
from jax.experimental import pallas as pl
from jax.experimental.pallas import tpu as pltpu

F32 = jnp.float32
BF16 = jnp.bfloat16

FOX_HEADS = 16
FOX_HEAD_DIM = 64
FOX_WIDTH = FOX_HEADS * FOX_HEAD_DIM
MLA_HEADS = 8
MLA_NOPE_DIM = 128
MLA_ROPE_DIM = 64
MLA_V_DIM = 128
MLA_Q_LORA = 512
MLA_KV_LORA = 256
MLA_QK_DIM = MLA_NOPE_DIM + MLA_ROPE_DIM
CHUNK = 64
ROPE_THETA = 10000.0
LN_EPS = 1e-5
RMS_EPS = 1e-6
N_ADA = 6
DEPTH = 1
ALPHA = (2.0 * DEPTH) ** 0.25

LANES = 128
LOG2E = math.log2(math.e)
FOX_QSCALE = FOX_HEAD_DIM ** -0.5 * LOG2E
MLA_QSCALE = MLA_QK_DIM ** -0.5 * LOG2E
NEG_BIG = -1e30
VMEM_LIMIT = 56 * 1024 * 1024

SUB = 128
HALF = 256


def _cparams(semantics):
    return pltpu.CompilerParams(dimension_semantics=semantics, vmem_limit_bytes=VMEM_LIMIT)


def _layer_norm(y, g, b):
    mu = jnp.mean(y, axis=-1, keepdims=True)
    yc = y - mu
    var = jnp.mean(yc * yc, axis=-1, keepdims=True)
    return yc * lax.rsqrt(var + LN_EPS) * g + b


def _rms_norm(c, g):
    return c * lax.rsqrt(jnp.mean(c * c, axis=-1, keepdims=True) + RMS_EPS) * g


def _ada_kernel(ct_ref, w_ref, b_ref, o_ref, *, n_batch, k_chunk):
    d = w_ref.shape[0]
    ct = ct_ref[...]
    s = ct * jax.nn.sigmoid(ct)
    accs = [b_ref[...] for _ in range(n_batch)]
    for kc in range(d // k_chunk):
        rows = slice(kc * k_chunk, (kc + 1) * k_chunk)
        w = w_ref[rows, :]
        for b in range(n_batch):
            accs[b] = accs[b] + jnp.sum(w * s[rows, b:b + 1], axis=0, keepdims=True)
    for b in range(n_batch):
        o_ref[b:b + 1, :] = accs[b]


def _ada(c, w_ada, b_ada):
    n_batch, d = c.shape
    n = w_ada.shape[1]
    tn = 1024
    return pl.pallas_call(
        functools.partial(_ada_kernel, n_batch=n_batch, k_chunk=256),
        grid=(n // tn,),
        in_specs=[
            pl.BlockSpec((d, n_batch), lambda j: (0, 0)),
            pl.BlockSpec((d, tn), lambda j: (0, j)),
            pl.BlockSpec((1, tn), lambda j: (0, j)),
        ],
        out_specs=pl.BlockSpec((n_batch, tn), lambda j: (0, j)),
        out_shape=jax.ShapeDtypeStruct((n_batch, n), F32),
        compiler_params=_cparams(("arbitrary",)),
        name="ada",
    )(c.T, w_ada, b_ada.reshape(1, n))


IN_TN = 1024
IN_SMALL_TILE = 3 * FOX_WIDTH // IN_TN


def _inproj_kernel(x_ref, mod_ref, w_ref, o_ref, lo_ref, u_ref, *, n_cols):
    j = pl.program_id(1)
    last = pl.num_programs(1) - 1

    @pl.when(j == 0)
    def _():
        shift = mod_ref[0:1, :]
        scale = mod_ref[1:2, :]
        u_ref[...] = (x_ref[...] * (1.0 + scale) + shift).astype(BF16)

    def product(wt):
        return lax.dot_general(u_ref[...], wt.astype(BF16), (((1,), (1,)), ((), ())),
                               preferred_element_type=F32)

    @pl.when(j == 0)
    def _():
        o_ref[...] = (product(w_ref[...]) * FOX_QSCALE).astype(o_ref.dtype)

    @pl.when(j == IN_SMALL_TILE)
    def _():
        acc = product(w_ref[...])
        hi = acc.astype(o_ref.dtype)
        o_ref[...] = hi
        lo_ref[...] = (acc - hi.astype(F32)).astype(lo_ref.dtype)

    @pl.when(j == last)
    def _():
        row = lax.broadcasted_iota(jnp.int32, w_ref.shape, 0)
        wt = jnp.where(row < n_cols - last * IN_TN, w_ref[...], 0.0)
        o_ref[...] = product(wt).astype(o_ref.dtype)

    @pl.when((j != 0) & (j != IN_SMALL_TILE) & (j != last))
    def _():
        o_ref[...] = product(w_ref[...]).astype(o_ref.dtype)


def _inproj(x2d, mod3, w_in_t, seq):
    m, d = x2d.shape
    n_cols = w_in_t.shape[0]
    tm = 1024
    n_tiles = pl.cdiv(n_cols, IN_TN)
    tiles_per_batch = seq // tm
    return pl.pallas_call(
        functools.partial(_inproj_kernel, n_cols=n_cols),
        grid=(m // tm, n_tiles),
        in_specs=[
            pl.BlockSpec((tm, d), lambda i, j: (i, 0)),
            pl.BlockSpec((None, N_ADA, d), lambda i, j: (i // tiles_per_batch, 0, 0)),
            pl.BlockSpec((IN_TN, d), lambda i, j: (j, 0)),
        ],
        out_specs=[pl.BlockSpec((tm, IN_TN), lambda i, j: (i, j)),
                   pl.BlockSpec((tm, IN_TN), lambda i, j: (i, 0))],
        out_shape=[jax.ShapeDtypeStruct((m, n_tiles * IN_TN), BF16),
                   jax.ShapeDtypeStruct((m, IN_TN), BF16)],
        scratch_shapes=[pltpu.VMEM((tm, d), BF16)],
        compiler_params=_cparams(("parallel", "arbitrary")),
        name="inproj",
    )(x2d, mod3, w_in_t)


def _log_sigmoid(x):
    return jnp.minimum(x, 0.0) - jnp.log1p(jnp.exp(-jnp.abs(x)))


def _prep_kernel(hi_ref, lo_ref, bf_ref, gq_ref, gkv_ref, wq_ref, wkv_ref, cos_ref, sin_ref,
                 qn_ref, qr_ref, kv_ref, kr_ref, cum_ref, cumt_ref, carry_ref, *, tiles_per_batch):
    i = pl.program_id(0)
    tm = hi_ref.shape[0]
    ps = hi_ref[...].astype(F32) + lo_ref[...].astype(F32)
    c0 = FOX_HEADS
    c1 = c0 + MLA_Q_LORA
    c2 = c1 + MLA_KV_LORA
    c3 = c2 + MLA_ROPE_DIM
    half = MLA_ROPE_DIM // 2
    lane_pad = jnp.zeros((tm, LANES - MLA_ROPE_DIM), F32)
    k_rope = jnp.concatenate([ps[:, c2:c3], lane_pad], axis=1)
    k_rope_partner = jnp.concatenate([ps[:, c2 + half:c3], ps[:, c2:c2 + half], lane_pad], axis=1)
    cos = cos_ref[...]
    sin = sin_ref[...]

    logf = _log_sigmoid(ps[:, 0:LANES] + bf_ref[...]) * LOG2E
    hi = logf.astype(BF16)
    r1 = logf - hi.astype(F32)
    mid = r1.astype(BF16)
    lo = (r1 - mid.astype(F32)).astype(BF16)
    row = lax.broadcasted_iota(jnp.int32, (tm, tm), 0)
    col = lax.broadcasted_iota(jnp.int32, (tm, tm), 1)
    tri = (row >= col).astype(BF16)
    cs = (jnp.dot(tri, hi, preferred_element_type=F32)
          + jnp.dot(tri, mid, preferred_element_type=F32)
          + jnp.dot(tri, lo, preferred_element_type=F32))

    @pl.when(i % tiles_per_batch == 0)
    def _():
        carry_ref[...] = jnp.zeros_like(carry_ref)

    cum = cs + carry_ref[...]
    carry_ref[...] = cum[tm - 1:tm, :]
    cum_ref[...] = cum
    cumt_ref[...] = cum.T[0:FOX_HEADS, :]

    cqn = _rms_norm(ps[:, c0:c1], gq_ref[...]).astype(BF16)
    qa = jnp.dot(cqn, wq_ref[...], preferred_element_type=F32)
    hw = MLA_HEADS * LANES
    qn_ref[...] = (qa[:, 0:hw] * MLA_QSCALE).astype(BF16)
    for h in range(MLA_HEADS):
        a = qa[:, hw + h * LANES: hw + (h + 1) * LANES]
        b = qa[:, 2 * hw + h * LANES: 2 * hw + (h + 1) * LANES]
        qr_ref[:, h * LANES:(h + 1) * LANES] = ((a * cos + b * sin) * MLA_QSCALE).astype(BF16)

    ckvn = _rms_norm(ps[:, c1:c2], gkv_ref[...]).astype(BF16)
    kv_ref[...] = jnp.dot(ckvn, wkv_ref[...], preferred_element_type=F32).astype(BF16)
    kr_ref[...] = (k_rope * cos + k_rope_partner * sin).astype(BF16)


def _prep(proj, proj_lo, bf_pad, g_q, g_kv, w_q_all, w_kv, cos_t, sin_t, seq):
    m = proj.shape[0]
    n_batch = m // seq
    tm = 512
    tiles_per_batch = seq // tm
    hw = MLA_HEADS * LANES
    const = lambda i: (0, 0)
    return pl.pallas_call(
        functools.partial(_prep_kernel, tiles_per_batch=tiles_per_batch),
        grid=(m // tm,),
        in_specs=[
            pl.BlockSpec((tm, IN_TN), lambda i: (i, IN_SMALL_TILE)),
            pl.BlockSpec((tm, IN_TN), lambda i: (i, 0)),
            pl.BlockSpec(bf_pad.shape, const),
            pl.BlockSpec(g_q.shape, const),
            pl.BlockSpec(g_kv.shape, const),
            pl.BlockSpec(w_q_all.shape, const),
            pl.BlockSpec(w_kv.shape, const),
            pl.BlockSpec((tm, LANES), lambda i: (i % tiles_per_batch, 0)),
            pl.BlockSpec((tm, LANES), lambda i: (i % tiles_per_batch, 0)),
        ],
        out_specs=[
            pl.BlockSpec((tm, hw), lambda i: (i, 0)),
            pl.BlockSpec((tm, hw), lambda i: (i, 0)),
            pl.BlockSpec((tm, w_kv.shape[1]), lambda i: (i, 0)),
            pl.BlockSpec((tm, LANES), lambda i: (i, 0)),
            pl.BlockSpec((tm, LANES), lambda i: (i, 0)),
            pl.BlockSpec((None, FOX_HEADS, tm), lambda i: (i // tiles_per_batch, 0, i % tiles_per_batch)),
        ],
        out_shape=[
            jax.ShapeDtypeStruct((m, hw), BF16),
            jax.ShapeDtypeStruct((m, hw), BF16),
            jax.ShapeDtypeStruct((m, w_kv.shape[1]), BF16),
            jax.ShapeDtypeStruct((m, LANES), BF16),
            jax.ShapeDtypeStruct((m, LANES), F32),
            jax.ShapeDtypeStruct((n_batch, FOX_HEADS, seq), F32),
        ],
        scratch_shapes=[pltpu.VMEM((1, LANES), F32)],
        compiler_params=_cparams(("arbitrary",)),
        name="prep",
    )(proj, proj_lo, bf_pad, g_q, g_kv, w_q_all, w_kv, cos_t, sin_t)


def _split3(x):
    hi = x.astype(BF16).astype(F32)
    r = x - hi
    mid = r.astype(BF16).astype(F32)
    lo = (r - mid).astype(BF16).astype(F32)
    return hi, mid, lo


def _online_softmax_step(t, m, acc, vt_aug):
    m_new = jnp.maximum(m, jnp.max(t, axis=0, keepdims=True))
    alpha = jnp.exp2(m - m_new)
    p = jnp.exp2(t - m_new).astype(BF16)
    pv = jnp.dot(vt_aug, p, preferred_element_type=F32)
    return m_new, alpha * acc + pv[0:acc.shape[0], :]


def _causal_mask(t, key0, query0, granule):
    r = lax.broadcasted_iota(jnp.int32, t.shape, 0) + key0
    c = lax.broadcasted_iota(jnp.int32, t.shape, 1) + query0
    return jnp.where(r <= (c | (granule - 1)), t, NEG_BIG)


def _flash_key_major(qi, tile, chains, keys, rhs, vt_aug, d_v, granule):
    acc_rows = d_v + 8
    n_rounds = tile // SUB
    n_pairs = len(chains) // 2
    issue_rounds = {g * n_rounds // n_pairs: (2 * g, 2 * g + 1) for g in range(n_pairs)}

    def qk_of(k0, is_diag):
        loaded = {}

        def product(n):
            half, hh = chains[n]
            n_keys = (half + 1) * HALF if is_diag else tile
            if (hh, n_keys) not in loaded:
                loaded[hh, n_keys] = keys(hh, k0, n_keys)
            return jnp.dot(loaded[hh, n_keys], rhs(hh)[:, half * HALF:(half + 1) * HALF],
                           preferred_element_type=F32)
        return product

    def visit(state, tiles):
        st = list(state)
        first = qk_of(*tiles[0])
        ss = [first(n) for n in range(len(chains))]
        for ti, (k0, is_diag) in enumerate(tiles):
            nxt = qk_of(*tiles[ti + 1]) if ti + 1 < len(tiles) else None
            ss_next = [None] * len(chains)
            for sb in range(n_rounds):
                if nxt is not None and sb in issue_rounds:
                    for n in issue_rounds[sb]:
                        ss_next[n] = nxt(n)
                ks = pl.multiple_of(k0 + sb * SUB, SUB)
                for n, (half, hh) in enumerate(chains):
                    if is_diag and sb * SUB >= (half + 1) * HALF:
                        continue
                    t = ss[n][sb * SUB:(sb + 1) * SUB, :]
                    if is_diag and (sb + 1) * SUB > half * HALF:
                        t = _causal_mask(t, sb * SUB, half * HALF, granule)
                    st[n] = _online_softmax_step(t, *st[n], vt_aug(hh, ks))
            ss = ss_next
        return tuple(st)

    def two_full(j, state):
        k0 = pl.multiple_of(2 * j * tile, tile)
        return visit(state, [(k0, False), (pl.multiple_of(k0 + tile, tile), False)])

    init = tuple((jnp.full((1, HALF), NEG_BIG, F32), jnp.zeros((acc_rows, HALF), F32)) for _ in chains)
    state = lax.fori_loop(0, qi // 2, two_full, init)
    q0 = pl.multiple_of(qi * tile, tile)
    state = lax.cond(
        qi % 2 == 1,
        lambda st: visit(st, [(pl.multiple_of(q0 - tile, tile), False), (q0, True)]),
        lambda st: visit(st, [(q0, True)]),
        state)
    return [acc[0:d_v, :] * (1.0 / acc[d_v:d_v + 1, :]) for (_, acc) in state]


def _cast_specs(weights, grid):
    n_steps = math.prod(grid)
    specs, shapes = [], []
    for w in weights:
        rows, cols = w.shape
        slab = rows // n_steps
        assert slab * n_steps == rows and slab % 16 == 0, (w.shape, n_steps)
        specs.append(pl.BlockSpec((slab, cols), lambda a, b: (a * grid[1] + b, 0)))
        shapes.append(jax.ShapeDtypeStruct(w.shape, BF16))
    return specs, shapes


def _cast_slabs(refs):
    n = len(refs) // 2
    for src, dst in zip(refs[:n], refs[n:]):
        dst[...] = src[...].astype(dst.dtype)


def _stack_chain_outputs(outs, n_half):
    return jnp.concatenate(
        [jnp.concatenate([outs[half * 2 + hh] for half in range(n_half)], axis=1) for hh in (0, 1)], axis=0)


def _fox_kernel(q_ref, k_ref, v_ref, cum_ref, cumt_ref, *rest, tq, n_cast):
    cast_in, (o_ref, *cast_out), (vta_ref, kaug_ref) = rest[:n_cast], rest[n_cast:2 * n_cast + 1], rest[2 * n_cast + 1:]
    hp = pl.program_id(1)
    seq = k_ref.shape[0]
    dh = FOX_HEAD_DIM
    n_half = tq // HALF
    aug = 6

    vt = v_ref[...].T
    tail_rows = lax.broadcasted_iota(jnp.int32, (16, seq), 0)
    tail = jnp.where(tail_rows < 8, 1.0, 0.0).astype(BF16)
    for hh in (0, 1):
        vta_ref[hh, 0:dh, :] = vt[hh * dh:(hh + 1) * dh, :]
        vta_ref[hh, dh:dh + 16, :] = tail
    lane = lax.broadcasted_iota(jnp.int32, (seq, LANES), 1)
    cum = cum_ref[...]
    kaug = jnp.zeros((seq, LANES), F32)
    for hh in (0, 1):
        col = jnp.sum(jnp.where(lane == 2 * hp + hh, cum, 0.0), axis=1, keepdims=True)
        neg = jnp.broadcast_to(-col, (seq, LANES))
        for idx, piece in enumerate(_split3(neg)):
            kaug = jnp.where(lane == aug * hh + idx, piece, kaug)
        kaug = jnp.where((lane >= aug * hh + 3) & (lane < aug * hh + 6), 1.0, kaug)
    kaug_ref[...] = kaug.astype(BF16)

    chains = [(half, hh) for half in range(n_half) for hh in (0, 1)]

    def keys(hh, k0, n):
        return jnp.concatenate([k_ref[pl.ds(k0, n), :], kaug_ref[pl.ds(k0, n), :]], axis=1)

    def query_tile(qi, carry):
        q0 = pl.multiple_of(qi * tq, tq)
        qt = q_ref[pl.ds(q0, tq), :].T
        row = lax.broadcasted_iota(jnp.int32, (LANES, tq), 0)
        rhs = []
        for hh in (0, 1):
            qm = jnp.where((row >= hh * dh) & (row < (hh + 1) * dh), qt, jnp.zeros_like(qt))
            ci = cumt_ref[hh:hh + 1, pl.ds(q0, tq)]
            qa = jnp.where((row >= aug * hh) & (row < aug * hh + 3), 1.0, 0.0)
            for idx, piece in enumerate(_split3(ci)):
                qa = jnp.where(row == aug * hh + 3 + idx, jnp.broadcast_to(piece, (LANES, tq)), qa)
            rhs.append(jnp.concatenate([qm, qa.astype(BF16)], axis=0))
        outs = _flash_key_major(qi, tq, chains, keys, lambda hh: rhs[hh],
                                lambda hh, ks: vta_ref[hh, :, pl.ds(ks, SUB)], dh, granule=1)
        o_ref[pl.ds(q0, tq), :] = _stack_chain_outputs(outs, n_half).T.astype(o_ref.dtype)
        return carry

    lax.fori_loop(0, seq // tq, query_tile, 0)
    _cast_slabs(cast_in + tuple(cast_out))


def _fox_attention(proj, cum, cumt4, seq, q_col0, cast_weights):
    m = proj.shape[0]
    n_batch = m // seq
    n_pairs = FOX_HEADS // 2
    qb = q_col0 // LANES
    grid = (n_batch, n_pairs)
    cast_specs, cast_shapes = _cast_specs(cast_weights, grid)
    return pl.pallas_call(
        functools.partial(_fox_kernel, tq=512, n_cast=len(cast_weights)),
        grid=grid,
        in_specs=[
            pl.BlockSpec((seq, LANES), lambda b, hp: (b, qb + hp)),
            pl.BlockSpec((seq, LANES), lambda b, hp: (b, qb + n_pairs + hp)),
            pl.BlockSpec((seq, LANES), lambda b, hp: (b, qb + 2 * n_pairs + hp)),
            pl.BlockSpec((seq, LANES), lambda b, hp: (b, 0)),
            pl.BlockSpec((None, None, 2, seq), lambda b, hp: (b, hp, 0, 0)),
        ] + cast_specs,
        out_specs=[pl.BlockSpec((seq, LANES), lambda b, hp: (b, hp))] + cast_specs,
        out_shape=[jax.ShapeDtypeStruct((m, FOX_WIDTH), BF16)] + cast_shapes,
        scratch_shapes=[
            pltpu.VMEM((2, FOX_HEAD_DIM + 16, seq), BF16),
            pltpu.VMEM((seq, LANES), BF16),
        ],
        compiler_params=_cparams(("parallel", "parallel")),
        name="fox_attn",
    )(proj, proj, proj, cum, cumt4, *cast_weights)


def _mla_kernel(qn_ref, qr_ref, kv_ref, kr_ref, *rest, tq, n_cast):
    cast_in, (o_ref, *cast_out), (vta_ref,) = rest[:n_cast], rest[n_cast:2 * n_cast + 1], rest[2 * n_cast + 1:]
    seq = kv_ref.shape[0]
    dv = MLA_V_DIM
    n_half = tq // HALF

    tail_rows = lax.broadcasted_iota(jnp.int32, (16, seq), 0)
    tail = jnp.where(tail_rows < 8, 1.0, 0.0).astype(BF16)
    for hh in (0, 1):
        vta_ref[hh, 0:dv, :] = kv_ref[:, (2 * hh + 1) * LANES:(2 * hh + 2) * LANES].T
        vta_ref[hh, dv:dv + 16, :] = tail

    chains = [(half, hh) for half in range(n_half) for hh in (0, 1)]

    def keys(hh, k0, n):
        return jnp.concatenate([kv_ref[pl.ds(k0, n), 2 * hh * LANES:(2 * hh + 1) * LANES],
                                kr_ref[pl.ds(k0, n), :]], axis=1)

    def query_tile(qi, carry):
        q0 = pl.multiple_of(qi * tq, tq)
        rhs = [jnp.concatenate([qn_ref[pl.ds(q0, tq), hh * LANES:(hh + 1) * LANES],
                                qr_ref[pl.ds(q0, tq), hh * LANES:(hh + 1) * LANES]], axis=1).T
               for hh in (0, 1)]
        outs = _flash_key_major(qi, tq, chains, keys, lambda hh: rhs[hh],
                                lambda hh, ks: vta_ref[hh, :, pl.ds(ks, SUB)], dv, granule=CHUNK)
        o_ref[pl.ds(q0, tq), :] = _stack_chain_outputs(outs, n_half).T.astype(o_ref.dtype)
        return carry

    lax.fori_loop(0, seq // tq, query_tile, 0)
    _cast_slabs(cast_in + tuple(cast_out))


def _mla_attention(qn, qr, kv, kr, seq, cast_weights):
    m = qn.shape[0]
    n_batch = m // seq
    pair = 2 * LANES
    grid = (n_batch, MLA_HEADS // 2)
    cast_specs, cast_shapes = _cast_specs(cast_weights, grid)
    return pl.pallas_call(
        functools.partial(_mla_kernel, tq=512, n_cast=len(cast_weights)),
        grid=grid,
        in_specs=[
            pl.BlockSpec((seq, pair), lambda b, hp: (b, hp)),
            pl.BlockSpec((seq, pair), lambda b, hp: (b, hp)),
            pl.BlockSpec((seq, 2 * pair), lambda b, hp: (b, hp)),
            pl.BlockSpec((seq, LANES), lambda b, hp: (b, 0)),
        ] + cast_specs,
        out_specs=[pl.BlockSpec((seq, pair), lambda b, hp: (b, hp))] + cast_specs,
        out_shape=[jax.ShapeDtypeStruct((m, MLA_HEADS * MLA_V_DIM), BF16)] + cast_shapes,
        scratch_shapes=[pltpu.VMEM((2, MLA_V_DIM + 16, seq), BF16)],
        compiler_params=_cparams(("parallel", "parallel")),
        name="mla_attn",
    )(qn, qr, kv, kr, *cast_weights)


def _mix_kernel(yf_ref, ym_ref, gf_ref, gm_ref, x_ref, mod_ref, wbf_ref, wbm_ref, wo_ref, g_ref, b_ref,
                o_ref, mg_ref, *, n_chunk, g_off):
    d = o_ref.shape[1]
    yf = yf_ref[...]
    ym = ym_ref[...]

    def gate(ref, c):
        window = ref[:, c * n_chunk:(c + 1) * n_chunk + LANES]
        return jax.nn.sigmoid(window[:, g_off:g_off + n_chunk].astype(F32))

    for c in range(d // n_chunk):
        cols = slice(c * n_chunk, (c + 1) * n_chunk)
        a = jnp.dot(yf, wbf_ref[:, cols], preferred_element_type=F32)
        b = jnp.dot(ym, wbm_ref[:, cols], preferred_element_type=F32)
        merged = gate(gf_ref, c) * a + gate(gm_ref, c) * b
        mg_ref[:, cols] = merged.astype(BF16)
    mix = jnp.dot(mg_ref[...], wo_ref[...], preferred_element_type=F32)
    gate1 = mod_ref[2:3, :]
    o_ref[...] = _layer_norm(ALPHA * x_ref[...] + gate1 * mix, g_ref[...], b_ref[...])


def _mix(y_fox, y_mla, proj, gf_col, gm_col, x2d, mod3, w_bf, w_bm, w_o, ln_g, ln_b, seq):
    m, d = x2d.shape
    tm = 512
    tiles_per_batch = seq // tm
    const = lambda i: (0, 0)
    resident = dict(pipeline_mode=pl.Buffered(1))
    g_off = gf_col % LANES
    assert gm_col % LANES == g_off
    g_window = (pl.Element(tm), pl.Element(d + LANES))

    def window_at(col):
        return lambda i: (pl.multiple_of(i * tm, tm), col - g_off)

    return pl.pallas_call(
        functools.partial(_mix_kernel, n_chunk=512, g_off=g_off),
        grid=(m // tm,),
        in_specs=[
            pl.BlockSpec((tm, y_fox.shape[1]), lambda i: (i, 0)),
            pl.BlockSpec((tm, y_mla.shape[1]), lambda i: (i, 0)),
            pl.BlockSpec(g_window, window_at(gf_col)),
            pl.BlockSpec(g_window, window_at(gm_col)),
            pl.BlockSpec((tm, d), lambda i: (i, 0)),
            pl.BlockSpec((None, N_ADA, d), lambda i: (i // tiles_per_batch, 0, 0)),
            pl.BlockSpec(w_bf.shape, const, **resident),
            pl.BlockSpec(w_bm.shape, const, **resident),
            pl.BlockSpec(w_o.shape, const, **resident),
            pl.BlockSpec((1, d), const),
            pl.BlockSpec((1, d), const),
        ],
        out_specs=pl.BlockSpec((tm, d), lambda i: (i, 0)),
        out_shape=jax.ShapeDtypeStruct((m, d), F32),
        scratch_shapes=[pltpu.VMEM((tm, d), BF16)],
        compiler_params=_cparams(("parallel",)),
        name="mix",
    )(y_fox, y_mla, proj, proj, x2d, mod3, w_bf, w_bm, w_o, ln_g, ln_b)


def _mlp_kernel(x_ref, mod_ref, wu_ref, wd_ref, g_ref, b_ref, o_ref, u_ref):
    f = pl.program_id(1)

    @pl.when(f == 0)
    def _():
        shift = mod_ref[3:4, :]
        scale = mod_ref[4:5, :]
        u_ref[...] = (x_ref[...] * (1.0 + scale) + shift).astype(BF16)
        o_ref[...] = jnp.zeros_like(o_ref)

    z = jnp.dot(u_ref[...], wu_ref[...], preferred_element_type=F32)
    h = jnp.square(jnp.maximum(z, 0.0)).astype(BF16)
    o_ref[...] += jnp.dot(h, wd_ref[...], preferred_element_type=F32)

    @pl.when(f == pl.num_programs(1) - 1)
    def _():
        gate = mod_ref[5:6, :]
        o_ref[...] = _layer_norm(ALPHA * x_ref[...] + gate * o_ref[...], g_ref[...], b_ref[...])


def _mlp(x1, mod3, w_up, w_down, ln_g, ln_b, seq):
    m, d = x1.shape
    d_ff = w_up.shape[1]
    tm, tf = 1024, 512
    tiles_per_batch = seq // tm
    return pl.pallas_call(
        _mlp_kernel,
        grid=(m // tm, d_ff // tf),
        in_specs=[
            pl.BlockSpec((tm, d), lambda i, f: (i, 0)),
            pl.BlockSpec((None, N_ADA, d), lambda i, f: (i // tiles_per_batch, 0, 0)),
            pl.BlockSpec((d, tf), lambda i, f: (0, f)),
            pl.BlockSpec((tf, d), lambda i, f: (f, 0)),
            pl.BlockSpec((1, d), lambda i, f: (0, 0)),
            pl.BlockSpec((1, d), lambda i, f: (0, 0)),
        ],
        out_specs=pl.BlockSpec((tm, d), lambda i, f: (i, 0)),
        out_shape=jax.ShapeDtypeStruct((m, d), F32),
        scratch_shapes=[pltpu.VMEM((tm, d), BF16)],
        compiler_params=_cparams(("parallel", "arbitrary")),
        name="mlp",
    )(x1, mod3, w_up, w_down, ln_g, ln_b)


def _pad_cols(w, width):
    return jnp.pad(w, ((0, 0), (0, width - w.shape[1])))


def _swap_halves(w):
    half = w.shape[-1] // 2
    return jnp.concatenate([w[..., half:], w[..., :half]], axis=-1)


def _layout_q_up(w_q_up):
    r = w_q_up.shape[0]
    w3 = w_q_up.reshape(r, MLA_HEADS, MLA_QK_DIM)
    w_nope = w3[:, :, :MLA_NOPE_DIM].reshape(r, MLA_HEADS * MLA_NOPE_DIM)
    w_rope = w3[:, :, MLA_NOPE_DIM:]
    pad = ((0, 0), (0, 0), (0, LANES - MLA_ROPE_DIM))
    w_r = jnp.pad(w_rope, pad).reshape(r, MLA_HEADS * LANES)
    w_rs = jnp.pad(_swap_halves(w_rope), pad).reshape(r, MLA_HEADS * LANES)
    return jnp.concatenate([w_nope, w_r, w_rs], axis=1).astype(BF16)


def _rope_tables(seq):
    pos = np.arange(seq, dtype=np.float64)
    inv_freq = ROPE_THETA ** (-np.arange(0, MLA_ROPE_DIM, 2, dtype=np.float64) / MLA_ROPE_DIM)
    ang = pos[:, None] * inv_freq[None, :]
    cos, sin = np.cos(ang), np.sin(ang)
    pad = np.zeros((seq, LANES - MLA_ROPE_DIM))
    cos_t = np.concatenate([cos, cos, pad], axis=1).astype(np.float32)
    sin_t = np.concatenate([-sin, sin, pad], axis=1).astype(np.float32)
    return jnp.asarray(cos_t), jnp.asarray(sin_t)


def kernel(x, c, w_ada, b_ada, w_in, b_forget, g_q_norm, w_q_up, g_kv_norm, w_kv_up, w_branch_fox,
           w_branch_mla, w_out, ln1_g, ln1_b, w_mlp_up, w_mlp_down, ln2_g, ln2_b):
    n_batch, seq, d = x.shape
    assert w_ada.shape[0] == DEPTH
    m = n_batch * seq
    x2d = x.reshape(m, d)

    mod3 = _ada(c, w_ada[0], b_ada[0]).reshape(n_batch, N_ADA, d)

    small_col = 3 * FOX_WIDTH
    gf_col = small_col + FOX_HEADS + MLA_Q_LORA + MLA_KV_LORA + MLA_ROPE_DIM
    gm_col = gf_col + d
    assert w_in.shape[2] == gm_col + d and small_col == IN_SMALL_TILE * IN_TN and gf_col <= small_col + IN_TN
    proj, proj_lo = _inproj(x2d, mod3, w_in[0].T, seq)

    cos_t, sin_t = _rope_tables(seq)
    bf_pad = _pad_cols(b_forget[0].reshape(1, FOX_HEADS), LANES)
    qn, qr, kv, kr, cum, cumt = _prep(
        proj, proj_lo, bf_pad, g_q_norm[0].reshape(1, -1), g_kv_norm[0].reshape(1, -1),
        _layout_q_up(w_q_up[0]), w_kv_up[0].astype(BF16), cos_t, sin_t, seq)

    y_fox, w_up_b, w_down_b = _fox_attention(
        proj, cum, cumt.reshape(n_batch, FOX_HEADS // 2, 2, seq), seq, 0, (w_mlp_up[0], w_mlp_down[0]))
    y_mla, w_bf_b, w_bm_b, w_o_b = _mla_attention(
        qn, qr, kv, kr, seq, (w_branch_fox[0], w_branch_mla[0], w_out[0]))

    x1 = _mix(y_fox, y_mla, proj, gf_col, gm_col, x2d, mod3, w_bf_b, w_bm_b, w_o_b,
              ln1_g[0].reshape(1, d), ln1_b[0].reshape(1, d), seq)
    x2 = _mlp(x1, mod3, w_up_b, w_down_b, ln2_g[0].reshape(1, d), ln2_b[0].reshape(1, d), seq)
    return x2.reshape(n_batch, seq, d)
```

```python
import functools
import math

import jax
import jax.numpy as jnp
import numpy as np
from jax import lax
from jax.experimental import pallas as pl
from jax.experimental.pallas import tpu as pltpu

F32 = jnp.float32
BF16 = jnp.bfloat16

FOX_HEADS = 16
FOX_HEAD_DIM = 64
FOX_WIDTH = FOX_HEADS * FOX_HEAD_DIM
MLA_HEADS = 8
MLA_NOPE_DIM = 128
MLA_ROPE_DIM = 64
MLA_V_DIM = 128
MLA_Q_LORA = 512
MLA_KV_LORA = 256
MLA_QK_DIM = MLA_NOPE_DIM + MLA_ROPE_DIM
CHUNK = 64
ROPE_THETA = 10000.0
LN_EPS = 1e-5
RMS_EPS = 1e-6
N_ADA = 6
DEPTH = 1
ALPHA = (2.0 * DEPTH) ** 0.25

LANES = 128
LOG2E = math.log2(math.e)
FOX_QSCALE = FOX_HEAD_DIM ** -0.5 * LOG2E
MLA_QSCALE = MLA_QK_DIM ** -0.5 * LOG2E
NEG_BIG = -1e30
VMEM_LIMIT = 56 * 1024 * 1024

SUB = 128
HALF = 256


def _cparams(semantics):
    return pltpu.CompilerParams(dimension_semantics=semantics, vmem_limit_bytes=VMEM_LIMIT)


def _layer_norm(y, g, b):
    mu = jnp.mean(y, axis=-1, keepdims=True)
    yc = y - mu
    var = jnp.mean(yc * yc, axis=-1, keepdims=True)
    return yc * lax.rsqrt(var + LN_EPS) * g + b


def _rms_norm(c, g):
    return c * lax.rsqrt(jnp.mean(c * c, axis=-1, keepdims=True) + RMS_EPS) * g


def _ada_kernel(ct_ref, w_ref, b_ref, o_ref, *, n_batch, k_chunk):
    d = w_ref.shape[0]
    ct = ct_ref[...]
    s = ct * jax.nn.sigmoid(ct)
    accs = [b_ref[...] for _ in range(n_batch)]
    for kc in range(d // k_chunk):
        rows = slice(kc * k_chunk, (kc + 1) * k_chunk)
        w = w_ref[rows, :]
        for b in range(n_batch):
            accs[b] = accs[b] + jnp.sum(w * s[rows, b:b + 1], axis=0, keepdims=True)
    for b in range(n_batch):
        o_ref[b:b + 1, :] = accs[b]


def _ada(c, w_ada, b_ada):
    n_batch, d = c.shape
    n = w_ada.shape[1]
    tn = 1024
    return pl.pallas_call(
        functools.partial(_ada_kernel, n_batch=n_batch, k_chunk=256),
        grid=(n // tn,),
        in_specs=[
            pl.BlockSpec((d, n_batch), lambda j: (0, 0)),
            pl.BlockSpec((d, tn), lambda j: (0, j)),
            pl.BlockSpec((1, tn), lambda j: (0, j)),
        ],
        out_specs=pl.BlockSpec((n_batch, tn), lambda j: (0, j)),
        out_shape=jax.ShapeDtypeStruct((n_batch, n), F32),
        compiler_params=_cparams(("arbitrary",)),
        name="ada",
    )(c.T, w_ada, b_ada.reshape(1, n))


IN_TN = 1024
IN_SMALL_TILE = 3 * FOX_WIDTH // IN_TN


def _inproj_kernel(x_ref, mod_ref, w_ref, o_ref, lo_ref, u_ref, *, n_cols):
    j = pl.program_id(1)
    last = pl.num_programs(1) - 1

    @pl.when(j == 0)
    def _():
        shift = mod_ref[0:1, :]
        scale = mod_ref[1:2, :]
        u_ref[...] = (x_ref[...] * (1.0 + scale) + shift).astype(BF16)

    def product(wt):
        return lax.dot_general(u_ref[...], wt.astype(BF16), (((1,), (1,)), ((), ())),
                               preferred_element_type=F32)

    @pl.when(j == 0)
    def _():
        o_ref[...] = (product(w_ref[...]) * FOX_QSCALE).astype(o_ref.dtype)

    @pl.when(j == IN_SMALL_TILE)
    def _():
        acc = product(w_ref[...])
        hi = acc.astype(o_ref.dtype)
        o_ref[...] = hi
        lo_ref[...] = (acc - hi.astype(F32)).astype(lo_ref.dtype)

    @pl.when(j == last)
    def _():
        row = lax.broadcasted_iota(jnp.int32, w_ref.shape, 0)
        wt = jnp.where(row < n_cols - last * IN_TN, w_ref[...], 0.0)
        o_ref[...] = product(wt).astype(o_ref.dtype)

    @pl.when((j != 0) & (j != IN_SMALL_TILE) & (j != last))
    def _():
        o_ref[...] = product(w_ref[...]).astype(o_ref.dtype)


def _inproj(x2d, mod3, w_in_t, seq):
    m, d = x2d.shape
    n_cols = w_in_t.shape[0]
    tm = 1024
    n_tiles = pl.cdiv(n_cols, IN_TN)
    tiles_per_batch = seq // tm
    return pl.pallas_call(
        functools.partial(_inproj_kernel, n_cols=n_cols),
        grid=(m // tm, n_tiles),
        in_specs=[
            pl.BlockSpec((tm, d), lambda i, j: (i, 0)),
            pl.BlockSpec((None, N_ADA, d), lambda i, j: (i // tiles_per_batch, 0, 0)),
            pl.BlockSpec((IN_TN, d), lambda i, j: (j, 0)),
        ],
        out_specs=[pl.BlockSpec((tm, IN_TN), lambda i, j: (i, j)),
                   pl.BlockSpec((tm, IN_TN), lambda i, j: (i, 0))],
        out_shape=[jax.ShapeDtypeStruct((m, n_tiles * IN_TN), BF16),
                   jax.ShapeDtypeStruct((m, IN_TN), BF16)],
        scratch_shapes=[pltpu.VMEM((tm, d), BF16)],
        compiler_params=_cparams(("parallel", "arbitrary")),
        name="inproj",
    )(x2d, mod3, w_in_t)


def _log_sigmoid(x):
    return jnp.minimum(x, 0.0) - jnp.log1p(jnp.exp(-jnp.abs(x)))


def _prep_kernel(hi_ref, lo_ref, bf_ref, gq_ref, gkv_ref, wq_ref, wkv_ref, cos_ref, sin_ref,
                 qn_ref, qr_ref, kv_ref, kr_ref, cum_ref, cumt_ref, carry_ref, *, tiles_per_batch):
    i = pl.program_id(0)
    tm = hi_ref.shape[0]
    ps = hi_ref[...].astype(F32) + lo_ref[...].astype(F32)
    c0 = FOX_HEADS
    c1 = c0 + MLA_Q_LORA
    c2 = c1 + MLA_KV_LORA
    c3 = c2 + MLA_ROPE_DIM
    half = MLA_ROPE_DIM // 2
    lane_pad = jnp.zeros((tm, LANES - MLA_ROPE_DIM), F32)
    k_rope = jnp.concatenate([ps[:, c2:c3], lane_pad], axis=1)
    k_rope_partner = jnp.concatenate([ps[:, c2 + half:c3], ps[:, c2:c2 + half], lane_pad], axis=1)
    cos = cos_ref[...]
    sin = sin_ref[...]

    logf = _log_sigmoid(ps[:, 0:LANES] + bf_ref[...]) * LOG2E
    hi = logf.astype(BF16)
    r1 = logf - hi.astype(F32)
    mid = r1.astype(BF16)
    lo = (r1 - mid.astype(F32)).astype(BF16)
    row = lax.broadcasted_iota(jnp.int32, (tm, tm), 0)
    col = lax.broadcasted_iota(jnp.int32, (tm, tm), 1)
    tri = (row >= col).astype(BF16)
    cs = (jnp.dot(tri, hi, preferred_element_type=F32)
          + jnp.dot(tri, mid, preferred_element_type=F32)
          + jnp.dot(tri, lo, preferred_element_type=F32))

    @pl.when(i % tiles_per_batch == 0)
    def _():
        carry_ref[...] = jnp.zeros_like(carry_ref)

    cum = cs + carry_ref[...]
    carry_ref[...] = cum[tm - 1:tm, :]
    cum_ref[...] = cum
    cumt_ref[...] = cum.T[0:FOX_HEADS, :]

    cqn = _rms_norm(ps[:, c0:c1], gq_ref[...]).astype(BF16)
    qa = jnp.dot(cqn, wq_ref[...], preferred_element_type=F32)
    hw = MLA_HEADS * LANES
    qn_ref[...] = (qa[:, 0:hw] * MLA_QSCALE).astype(BF16)
    for h in range(MLA_HEADS):
        a = qa[:, hw + h * LANES: hw + (h + 1) * LANES]
        b = qa[:, 2 * hw + h * LANES: 2 * hw + (h + 1) * LANES]
        qr_ref[:, h * LANES:(h + 1) * LANES] = ((a * cos + b * sin) * MLA_QSCALE).astype(BF16)

    ckvn = _rms_norm(ps[:, c1:c2], gkv_ref[...]).astype(BF16)
    kv_ref[...] = jnp.dot(ckvn, wkv_ref[...], preferred_element_type=F32).astype(BF16)
    kr_ref[...] = (k_rope * cos + k_rope_partner * sin).astype(BF16)


def _prep(proj, proj_lo, bf_pad, g_q, g_kv, w_q_all, w_kv, cos_t, sin_t, seq):
    m = proj.shape[0]
    n_batch = m // seq
    tm = 512
    tiles_per_batch = seq // tm
    hw = MLA_HEADS * LANES
    const = lambda i: (0, 0)
    return pl.pallas_call(
        functools.partial(_prep_kernel, tiles_per_batch=tiles_per_batch),
        grid=(m // tm,),
        in_specs=[
            pl.BlockSpec((tm, IN_TN), lambda i: (i, IN_SMALL_TILE)),
            pl.BlockSpec((tm, IN_TN), lambda i: (i, 0)),
            pl.BlockSpec(bf_pad.shape, const),
            pl.BlockSpec(g_q.shape, const),
            pl.BlockSpec(g_kv.shape, const),
            pl.BlockSpec(w_q_all.shape, const),
            pl.BlockSpec(w_kv.shape, const),
            pl.BlockSpec((tm, LANES), lambda i: (i % tiles_per_batch, 0)),
            pl.BlockSpec((tm, LANES), lambda i: (i % tiles_per_batch, 0)),
        ],
        out_specs=[
            pl.BlockSpec((tm, hw), lambda i: (i, 0)),
            pl.BlockSpec((tm, hw), lambda i: (i, 0)),
            pl.BlockSpec((tm, w_kv.shape[1]), lambda i: (i, 0)),
            pl.BlockSpec((tm, LANES), lambda i: (i, 0)),
            pl.BlockSpec((tm, LANES), lambda i: (i, 0)),
            pl.BlockSpec((None, FOX_HEADS, tm), lambda i: (i // tiles_per_batch, 0, i % tiles_per_batch)),
        ],
        out_shape=[
            jax.ShapeDtypeStruct((m, hw), BF16),
            jax.ShapeDtypeStruct((m, hw), BF16),
            jax.ShapeDtypeStruct((m, w_kv.shape[1]), BF16),
            jax.ShapeDtypeStruct((m, LANES), BF16),
            jax.ShapeDtypeStruct((m, LANES), F32),
            jax.ShapeDtypeStruct((n_batch, FOX_HEADS, seq), F32),
        ],
        scratch_shapes=[pltpu.VMEM((1, LANES), F32)],
        compiler_params=_cparams(("arbitrary",)),
        name="prep",
    )(proj, proj_lo, bf_pad, g_q, g_kv, w_q_all, w_kv, cos_t, sin_t)


def _split3(x):
    hi = x.astype(BF16).astype(F32)
    r = x - hi
    mid = r.astype(BF16).astype(F32)
    lo = (r - mid).astype(BF16).astype(F32)
    return hi, mid, lo


def _online_softmax_step(t, m, acc, vt_aug):
    m_new = jnp.maximum(m, jnp.max(t, axis=0, keepdims=True))
    alpha = jnp.exp2(m - m_new)
    p = jnp.exp2(t - m_new).astype(BF16)
    pv = jnp.dot(vt_aug, p, preferred_element_type=F32)
    return m_new, alpha * acc + pv[0:acc.shape[0], :]


def _causal_mask(t, key0, query0, granule):
    r = lax.broadcasted_iota(jnp.int32, t.shape, 0) + key0
    c = lax.broadcasted_iota(jnp.int32, t.shape, 1) + query0
    return jnp.where(r <= (c | (granule - 1)), t, NEG_BIG)


def _flash_key_major(qi, tile, chains, keys, rhs, vt_aug, d_v, granule):
    acc_rows = d_v + 8
    n_rounds = tile // SUB
    pairs = [(n, n + 1) for n in range(0, len(chains), 2)]

    def qk_of(k0, is_diag):
        loaded = {}

        def product(n):
            half, hh = chains[n]
            n_keys = (half + 1) * HALF if is_diag else tile
            if (hh, n_keys) not in loaded:
                loaded[hh, n_keys] = keys(hh, k0, n_keys)
            return jnp.dot(loaded[hh, n_keys], rhs(hh)[:, half * HALF:(half + 1) * HALF],
                           preferred_element_type=F32)
        return product

    def visit(state, tiles):
        st = list(state)
        products = [qk_of(*t) for t in tiles]
        phases = [(ti, pair) for ti in range(len(tiles)) for pair in pairs]

        def issue(phase):
            ti, pair = phase
            return {n: products[ti](n) for n in pair}

        cur = issue(phases[0])
        for idx, (ti, pair) in enumerate(phases):
            nxt = issue(phases[idx + 1]) if idx + 1 < len(phases) else None
            k0, is_diag = tiles[ti]
            for sb in range(n_rounds):
                ks = pl.multiple_of(k0 + sb * SUB, SUB)
                for n in pair:
                    half, hh = chains[n]
                    if is_diag and sb * SUB >= (half + 1) * HALF:
                        continue
                    t = cur[n][sb * SUB:(sb + 1) * SUB, :]
                    if is_diag and (sb + 1) * SUB > half * HALF:
                        t = _causal_mask(t, sb * SUB, half * HALF, granule)
                    st[n] = _online_softmax_step(t, *st[n], vt_aug(hh, ks))
            cur = nxt
        return tuple(st)

    def two_full(j, state):
        k0 = pl.multiple_of(2 * j * tile, tile)
        return visit(state, [(k0, False), (pl.multiple_of(k0 + tile, tile), False)])

    init = tuple((jnp.full((1, HALF), NEG_BIG, F32), jnp.zeros((acc_rows, HALF), F32)) for _ in chains)
    state = lax.fori_loop(0, qi // 2, two_full, init)
    q0 = pl.multiple_of(qi * tile, tile)
    state = lax.cond(
        qi % 2 == 1,
        lambda st: visit(st, [(pl.multiple_of(q0 - tile, tile), False), (q0, True)]),
        lambda st: visit(st, [(q0, True)]),
        state)
    return [acc[0:d_v, :] * (1.0 / acc[d_v:d_v + 1, :]) for (_, acc) in state]


def _cast_specs(weights, grid):
    n_steps = math.prod(grid)
    specs, shapes = [], []
    for w in weights:
        rows, cols = w.shape
        slab = rows // n_steps
        assert slab * n_steps == rows and slab % 16 == 0, (w.shape, n_steps)
        specs.append(pl.BlockSpec((slab, cols), lambda a, b: (a * grid[1] + b, 0)))
        shapes.append(jax.ShapeDtypeStruct(w.shape, BF16))
    return specs, shapes


def _cast_slabs(refs):
    n = len(refs) // 2
    for src, dst in zip(refs[:n], refs[n:]):
        dst[...] = src[...].astype(dst.dtype)


def _stack_chain_outputs(outs, n_half):
    return jnp.concatenate(
        [jnp.concatenate([outs[half * 2 + hh] for half in range(n_half)], axis=1) for hh in (0, 1)], axis=0)


def _fox_kernel(q_ref, k_ref, v_ref, cum_ref, cumt_ref, *rest, tq, n_cast):
    cast_in, (o_ref, *cast_out), (vta_ref, kaug_ref) = rest[:n_cast], rest[n_cast:2 * n_cast + 1], rest[2 * n_cast + 1:]
    hp = pl.program_id(1)
    seq = k_ref.shape[0]
    dh = FOX_HEAD_DIM
    n_half = tq // HALF
    aug = 6

    vt = v_ref[...].T
    tail_rows = lax.broadcasted_iota(jnp.int32, (16, seq), 0)
    tail = jnp.where(tail_rows < 8, 1.0, 0.0).astype(BF16)
    for hh in (0, 1):
        vta_ref[hh, 0:dh, :] = vt[hh * dh:(hh + 1) * dh, :]
        vta_ref[hh, dh:dh + 16, :] = tail
    lane = lax.broadcasted_iota(jnp.int32, (seq, LANES), 1)
    cum = cum_ref[...]
    kaug = jnp.zeros((seq, LANES), F32)
    for hh in (0, 1):
        col = jnp.sum(jnp.where(lane == 2 * hp + hh, cum, 0.0), axis=1, keepdims=True)
        neg = jnp.broadcast_to(-col, (seq, LANES))
        for idx, piece in enumerate(_split3(neg)):
            kaug = jnp.where(lane == aug * hh + idx, piece, kaug)
        kaug = jnp.where((lane >= aug * hh + 3) & (lane < aug * hh + 6), 1.0, kaug)
    kaug_ref[...] = kaug.astype(BF16)

    chains = [(half, hh) for half in range(n_half) for hh in (0, 1)]

    def keys(hh, k0, n):
        return jnp.concatenate([k_ref[pl.ds(k0, n), :], kaug_ref[pl.ds(k0, n), :]], axis=1)

    def query_tile(qi, carry):
        q0 = pl.multiple_of(qi * tq, tq)
        qt = q_ref[pl.ds(q0, tq), :].T
        row = lax.broadcasted_iota(jnp.int32, (LANES, tq), 0)
        rhs = []
        for hh in (0, 1):
            qm = jnp.where((row >= hh * dh) & (row < (hh + 1) * dh), qt, jnp.zeros_like(qt))
            ci = cumt_ref[hh:hh + 1, pl.ds(q0, tq)]
            qa = jnp.where((row >= aug * hh) & (row < aug * hh + 3), 1.0, 0.0)
            for idx, piece in enumerate(_split3(ci)):
                qa = jnp.where(row == aug * hh + 3 + idx, jnp.broadcast_to(piece, (LANES, tq)), qa)
            rhs.append(jnp.concatenate([qm, qa.astype(BF16)], axis=0))
        outs = _flash_key_major(qi, tq, chains, keys, lambda hh: rhs[hh],
                                lambda hh, ks: vta_ref[hh, :, pl.ds(ks, SUB)], dh, granule=1)
        o_ref[pl.ds(q0, tq), :] = _stack_chain_outputs(outs, n_half).T.astype(o_ref.dtype)
        return carry

    lax.fori_loop(0, seq // tq, query_tile, 0)
    _cast_slabs(cast_in + tuple(cast_out))


def _fox_attention(proj, cum, cumt4, seq, q_col0, cast_weights):
    m = proj.shape[0]
    n_batch = m // seq
    n_pairs = FOX_HEADS // 2
    qb = q_col0 // LANES
    grid = (n_batch, n_pairs)
    cast_specs, cast_shapes = _cast_specs(cast_weights, grid)
    return pl.pallas_call(
        functools.partial(_fox_kernel, tq=512, n_cast=len(cast_weights)),
        grid=grid,
        in_specs=[
            pl.BlockSpec((seq, LANES), lambda b, hp: (b, qb + hp)),
            pl.BlockSpec((seq, LANES), lambda b, hp: (b, qb + n_pairs + hp)),
            pl.BlockSpec((seq, LANES), lambda b, hp: (b, qb + 2 * n_pairs + hp)),
            pl.BlockSpec((seq, LANES), lambda b, hp: (b, 0)),
            pl.BlockSpec((None, None, 2, seq), lambda b, hp: (b, hp, 0, 0)),
        ] + cast_specs,
        out_specs=[pl.BlockSpec((seq, LANES), lambda b, hp: (b, hp))] + cast_specs,
        out_shape=[jax.ShapeDtypeStruct((m, FOX_WIDTH), BF16)] + cast_shapes,
        scratch_shapes=[
            pltpu.VMEM((2, FOX_HEAD_DIM + 16, seq), BF16),
            pltpu.VMEM((seq, LANES), BF16),
        ],
        compiler_params=_cparams(("parallel", "parallel")),
        name="fox_attn",
    )(proj, proj, proj, cum, cumt4, *cast_weights)


def _mla_kernel(qn_ref, qr_ref, kv_ref, kr_ref, *rest, tq, n_cast):
    cast_in, (o_ref, *cast_out), (vta_ref,) = rest[:n_cast], rest[n_cast:2 * n_cast + 1], rest[2 * n_cast + 1:]
    seq = kv_ref.shape[0]
    dv = MLA_V_DIM
    n_half = tq // HALF

    tail_rows = lax.broadcasted_iota(jnp.int32, (16, seq), 0)
    tail = jnp.where(tail_rows < 8, 1.0, 0.0).astype(BF16)
    for hh in (0, 1):
        vta_ref[hh, 0:dv, :] = kv_ref[:, (2 * hh + 1) * LANES:(2 * hh + 2) * LANES].T
        vta_ref[hh, dv:dv + 16, :] = tail

    chains = [(half, hh) for half in range(n_half) for hh in (0, 1)]

    def keys(hh, k0, n):
        return jnp.concatenate([kv_ref[pl.ds(k0, n), 2 * hh * LANES:(2 * hh + 1) * LANES],
                                kr_ref[pl.ds(k0, n), :]], axis=1)

    def query_tile(qi, carry):
        q0 = pl.multiple_of(qi * tq, tq)
        rhs = [jnp.concatenate([qn_ref[pl.ds(q0, tq), hh * LANES:(hh + 1) * LANES],
                                qr_ref[pl.ds(q0, tq), hh * LANES:(hh + 1) * LANES]], axis=1).T
               for hh in (0, 1)]
        outs = _flash_key_major(qi, tq, chains, keys, lambda hh: rhs[hh],
                                lambda hh, ks: vta_ref[hh, :, pl.ds(ks, SUB)], dv, granule=CHUNK)
        o_ref[pl.ds(q0, tq), :] = _stack_chain_outputs(outs, n_half).T.astype(o_ref.dtype)
        return carry

    lax.fori_loop(0, seq // tq, query_tile, 0)
    _cast_slabs(cast_in + tuple(cast_out))


def _mla_attention(qn, qr, kv, kr, seq, cast_weights):
    m = qn.shape[0]
    n_batch = m // seq
    pair = 2 * LANES
    grid = (n_batch, MLA_HEADS // 2)
    cast_specs, cast_shapes = _cast_specs(cast_weights, grid)
    return pl.pallas_call(
        functools.partial(_mla_kernel, tq=512, n_cast=len(cast_weights)),
        grid=grid,
        in_specs=[
            pl.BlockSpec((seq, pair), lambda b, hp: (b, hp)),
            pl.BlockSpec((seq, pair), lambda b, hp: (b, hp)),
            pl.BlockSpec((seq, 2 * pair), lambda b, hp: (b, hp)),
            pl.BlockSpec((seq, LANES), lambda b, hp: (b, 0)),
        ] + cast_specs,
        out_specs=[pl.BlockSpec((seq, pair), lambda b, hp: (b, hp))] + cast_specs,
        out_shape=[jax.ShapeDtypeStruct((m, MLA_HEADS * MLA_V_DIM), BF16)] + cast_shapes,
        scratch_shapes=[pltpu.VMEM((2, MLA_V_DIM + 16, seq), BF16)],
        compiler_params=_cparams(("parallel", "parallel")),
        name="mla_attn",
    )(qn, qr, kv, kr, *cast_weights)


def _mix_kernel(yf_ref, ym_ref, gf_ref, gm_ref, x_ref, mod_ref, wbf_ref, wbm_ref, wo_ref, g_ref, b_ref,
                o_ref, mg_ref, *, n_chunk, g_off):
    d = o_ref.shape[1]
    yf = yf_ref[...]
    ym = ym_ref[...]

    def gate(ref, c):
        window = ref[:, c * n_chunk:(c + 1) * n_chunk + LANES]
        return jax.nn.sigmoid(window[:, g_off:g_off + n_chunk].astype(F32))

    for c in range(d // n_chunk):
        cols = slice(c * n_chunk, (c + 1) * n_chunk)
        a = jnp.dot(yf, wbf_ref[:, cols], preferred_element_type=F32)
        b = jnp.dot(ym, wbm_ref[:, cols], preferred_element_type=F32)
        merged = gate(gf_ref, c) * a + gate(gm_ref, c) * b
        mg_ref[:, cols] = merged.astype(BF16)
    mix = jnp.dot(mg_ref[...], wo_ref[...], preferred_element_type=F32)
    gate1 = mod_ref[2:3, :]
    o_ref[...] = _layer_norm(ALPHA * x_ref[...] + gate1 * mix, g_ref[...], b_ref[...])


def _mix(y_fox, y_mla, proj, gf_col, gm_col, x2d, mod3, w_bf, w_bm, w_o, ln_g, ln_b, seq):
    m, d = x2d.shape
    tm = 512
    tiles_per_batch = seq // tm
    const = lambda i: (0, 0)
    resident = dict(pipeline_mode=pl.Buffered(1))
    g_off = gf_col % LANES
    assert gm_col % LANES == g_off
    g_window = (pl.Element(tm), pl.Element(d + LANES))

    def window_at(col):
        return lambda i: (pl.multiple_of(i * tm, tm), col - g_off)

    return pl.pallas_call(
        functools.partial(_mix_kernel, n_chunk=512, g_off=g_off),
        grid=(m // tm,),
        in_specs=[
            pl.BlockSpec((tm, y_fox.shape[1]), lambda i: (i, 0)),
            pl.BlockSpec((tm, y_mla.shape[1]), lambda i: (i, 0)),
            pl.BlockSpec(g_window, window_at(gf_col)),
            pl.BlockSpec(g_window, window_at(gm_col)),
            pl.BlockSpec((tm, d), lambda i: (i, 0)),
            pl.BlockSpec((None, N_ADA, d), lambda i: (i // tiles_per_batch, 0, 0)),
            pl.BlockSpec(w_bf.shape, const, **resident),
            pl.BlockSpec(w_bm.shape, const, **resident),
            pl.BlockSpec(w_o.shape, const, **resident),
            pl.BlockSpec((1, d), const),
            pl.BlockSpec((1, d), const),
        ],
        out_specs=pl.BlockSpec((tm, d), lambda i: (i, 0)),
        out_shape=jax.ShapeDtypeStruct((m, d), F32),
        scratch_shapes=[pltpu.VMEM((tm, d), BF16)],
        compiler_params=_cparams(("parallel",)),
        name="mix",
    )(y_fox, y_mla, proj, proj, x2d, mod3, w_bf, w_bm, w_o, ln_g, ln_b)


def _mlp_kernel(x_ref, mod_ref, wu_ref, wd_ref, g_ref, b_ref, o_ref, u_ref):
    f = pl.program_id(1)

    @pl.when(f == 0)
    def _():
        shift = mod_ref[3:4, :]
        scale = mod_ref[4:5, :]
        u_ref[...] = (x_ref[...] * (1.0 + scale) + shift).astype(BF16)
        o_ref[...] = jnp.zeros_like(o_ref)

    z = jnp.dot(u_ref[...], wu_ref[...], preferred_element_type=F32)
    h = jnp.square(jnp.maximum(z, 0.0)).astype(BF16)
    o_ref[...] += jnp.dot(h, wd_ref[...], preferred_element_type=F32)

    @pl.when(f == pl.num_programs(1) - 1)
    def _():
        gate = mod_ref[5:6, :]
        o_ref[...] = _layer_norm(ALPHA * x_ref[...] + gate * o_ref[...], g_ref[...], b_ref[...])


def _mlp(x1, mod3, w_up, w_down, ln_g, ln_b, seq):
    m, d = x1.shape
    d_ff = w_up.shape[1]
    tm, tf = 1024, 512
    tiles_per_batch = seq // tm
    return pl.pallas_call(
        _mlp_kernel,
        grid=(m // tm, d_ff // tf),
        in_specs=[
            pl.BlockSpec((tm, d), lambda i, f: (i, 0)),
            pl.BlockSpec((None, N_ADA, d), lambda i, f: (i // tiles_per_batch, 0, 0)),
            pl.BlockSpec((d, tf), lambda i, f: (0, f)),
            pl.BlockSpec((tf, d), lambda i, f: (f, 0)),
            pl.BlockSpec((1, d), lambda i, f: (0, 0)),
            pl.BlockSpec((1, d), lambda i, f: (0, 0)),
        ],
        out_specs=pl.BlockSpec((tm, d), lambda i, f: (i, 0)),
        out_shape=jax.ShapeDtypeStruct((m, d), F32),
        scratch_shapes=[pltpu.VMEM((tm, d), BF16)],
        compiler_params=_cparams(("parallel", "arbitrary")),
        name="mlp",
    )(x1, mod3, w_up, w_down, ln_g, ln_b)


def _pad_cols(w, width):
    return jnp.pad(w, ((0, 0), (0, width - w.shape[1])))


def _swap_halves(w):
    half = w.shape[-1] // 2
    return jnp.concatenate([w[..., half:], w[..., :half]], axis=-1)


def _layout_q_up(w_q_up):
    r = w_q_up.shape[0]
    w3 = w_q_up.reshape(r, MLA_HEADS, MLA_QK_DIM)
    w_nope = w3[:, :, :MLA_NOPE_DIM].reshape(r, MLA_HEADS * MLA_NOPE_DIM)
    w_rope = w3[:, :, MLA_NOPE_DIM:]
    pad = ((0, 0), (0, 0), (0, LANES - MLA_ROPE_DIM))
    w_r = jnp.pad(w_rope, pad).reshape(r, MLA_HEADS * LANES)
    w_rs = jnp.pad(_swap_halves(w_rope), pad).reshape(r, MLA_HEADS * LANES)
    return jnp.concatenate([w_nope, w_r, w_rs], axis=1).astype(BF16)


def _rope_tables(seq):
    pos = np.arange(seq, dtype=np.float64)
    inv_freq = ROPE_THETA ** (-np.arange(0, MLA_ROPE_DIM, 2, dtype=np.float64) / MLA_ROPE_DIM)
    ang = pos[:, None] * inv_freq[None, :]
    cos, sin = np.cos(ang), np.sin(ang)
    pad = np.zeros((seq, LANES - MLA_ROPE_DIM))
    cos_t = np.concatenate([cos, cos, pad], axis=1).astype(np.float32)
    sin_t = np.concatenate([-sin, sin, pad], axis=1).astype(np.float32)
    return jnp.asarray(cos_t), jnp.asarray(sin_t)


def kernel(x, c, w_ada, b_ada, w_in, b_forget, g_q_norm, w_q_up, g_kv_norm, w_kv_up, w_branch_fox,
           w_branch_mla, w_out, ln1_g, ln1_b, w_mlp_up, w_mlp_down, ln2_g, ln2_b):
    n_batch, seq, d = x.shape
    assert w_ada.shape[0] == DEPTH
    m = n_batch * seq
    x2d = x.reshape(m, d)

    mod3 = _ada(c, w_ada[0], b_ada[0]).reshape(n_batch, N_ADA, d)

    small_col = 3 * FOX_WIDTH
    gf_col = small_col + FOX_HEADS + MLA_Q_LORA + MLA_KV_LORA + MLA_ROPE_DIM
    gm_col = gf_col + d
    assert w_in.shape[2] == gm_col + d and small_col == IN_SMALL_TILE * IN_TN and gf_col <= small_col + IN_TN
    proj, proj_lo = _inproj(x2d, mod3, w_in[0].T, seq)

    cos_t, sin_t = _rope_tables(seq)
    bf_pad = _pad_cols(b_forget[0].reshape(1, FOX_HEADS), LANES)
    qn, qr, kv, kr, cum, cumt = _prep(
        proj, proj_lo, bf_pad, g_q_norm[0].reshape(1, -1), g_kv_norm[0].reshape(1, -1),
        _layout_q_up(w_q_up[0]), w_kv_up[0].astype(BF16), cos_t, sin_t, seq)

    y_fox, w_up_b, w_down_b = _fox_attention(
        proj, cum, cumt.reshape(n_batch, FOX_HEADS // 2, 2, seq), seq, 0, (w_mlp_up[0], w_mlp_down[0]))
    y_mla, w_bf_b, w_bm_b, w_o_b = _mla_attention(
        qn, qr, kv, kr, seq, (w_branch_fox[0], w_branch_mla[0], w_out[0]))

    x1 = _mix(y_fox, y_mla, proj, gf_col, gm_col, x2d, mod3, w_bf_b, w_bm_b, w_o_b,
              ln1_g[0].reshape(1, d), ln1_b[0].reshape(1, d), seq)
    x2 = _mlp(x1, mod3, w_up_b, w_down_b, ln2_g[0].reshape(1, d), ln2_b[0].reshape(1, d), seq)
    return x2.reshape(n_batch, seq, d)
```

```python
import functools
import math

import jax
import jax.numpy as jnp
import numpy as np
from jax import lax
from jax.experimental import pallas as pl
from jax.experimental.pallas import tpu as pltpu

F32 = jnp.float32
BF16 = jnp.bfloat16

FOX_HEADS = 16
FOX_HEAD_DIM = 64
FOX_WIDTH = FOX_HEADS * FOX_HEAD_DIM
MLA_HEADS = 8
MLA_NOPE_DIM = 128
MLA_ROPE_DIM = 64
MLA_V_DIM = 128
MLA_Q_LORA = 512
MLA_KV_LORA = 256
MLA_QK_DIM = MLA_NOPE_DIM + MLA_ROPE_DIM
CHUNK = 64
ROPE_THETA = 10000.0
LN_EPS = 1e-5
RMS_EPS = 1e-6
N_ADA = 6
DEPTH = 1
ALPHA = (2.0 * DEPTH) ** 0.25

LANES = 128
LOG2E = math.log2(math.e)
FOX_QSCALE = FOX_HEAD_DIM ** -0.5 * LOG2E
MLA_QSCALE = MLA_QK_DIM ** -0.5 * LOG2E
NEG_BIG = -1e30
VMEM_LIMIT = 56 * 1024 * 1024

SUB = 128
HALF = 256


def _cparams(semantics):
    return pltpu.CompilerParams(dimension_semantics=semantics, vmem_limit_bytes=VMEM_LIMIT)


def _layer_norm(y, g, b):
    mu = jnp.mean(y, axis=-1, keepdims=True)
    yc = y - mu
    var = jnp.mean(yc * yc, axis=-1, keepdims=True)
    return yc * lax.rsqrt(var + LN_EPS) * g + b


def _rms_norm(c, g):
    return c * lax.rsqrt(jnp.mean(c * c, axis=-1, keepdims=True) + RMS_EPS) * g


def _ada_kernel(ct_ref, w_ref, b_ref, o_ref, *, n_batch, k_chunk):
    d = w_ref.shape[0]
    ct = ct_ref[...]
    s = ct * jax.nn.sigmoid(ct)
    accs = [b_ref[...] for _ in range(n_batch)]
    for kc in range(d // k_chunk):
        rows = slice(kc * k_chunk, (kc + 1) * k_chunk)
        w = w_ref[rows, :]
        for b in range(n_batch):
            accs[b] = accs[b] + jnp.sum(w * s[rows, b:b + 1], axis=0, keepdims=True)
    for b in range(n_batch):
        o_ref[b:b + 1, :] = accs[b]


def _ada(c, w_ada, b_ada):
    n_batch, d = c.shape
    n = w_ada.shape[1]
    tn = 1024
    return pl.pallas_call(
        functools.partial(_ada_kernel, n_batch=n_batch, k_chunk=256),
        grid=(n // tn,),
        in_specs=[
            pl.BlockSpec((d, n_batch), lambda j: (0, 0)),
            pl.BlockSpec((d, tn), lambda j: (0, j)),
            pl.BlockSpec((1, tn), lambda j: (0, j)),
        ],
        out_specs=pl.BlockSpec((n_batch, tn), lambda j: (0, j)),
        out_shape=jax.ShapeDtypeStruct((n_batch, n), F32),
        compiler_params=_cparams(("arbitrary",)),
        name="ada",
    )(c.T, w_ada, b_ada.reshape(1, n))


IN_TN = 1024
IN_SMALL_TILE = 3 * FOX_WIDTH // IN_TN


def _inproj_kernel(x_ref, mod_ref, w_ref, o_ref, lo_ref, u_ref, *, n_cols):
    j = pl.program_id(1)
    last = pl.num_programs(1) - 1

    def product(wt, u=None):
        u = u_ref[...] if u is None else u
        return lax.dot_general(u, wt.astype(BF16), (((1,), (1,)), ((), ())), preferred_element_type=F32)

    @pl.when(j == 0)
    def _():
        shift = mod_ref[0:1, :]
        scale = mod_ref[1:2, :]
        u = (x_ref[...] * (1.0 + scale) + shift).astype(BF16)
        u_ref[...] = u
        o_ref[...] = (product(w_ref[...], u) * FOX_QSCALE).astype(o_ref.dtype)

    @pl.when(j == IN_SMALL_TILE)
    def _():
        acc = product(w_ref[...])
        hi = acc.astype(o_ref.dtype)
        o_ref[...] = hi
        lo_ref[...] = (acc - hi.astype(F32)).astype(lo_ref.dtype)

    @pl.when(j == last)
    def _():
        row = lax.broadcasted_iota(jnp.int32, w_ref.shape, 0)
        wt = jnp.where(row < n_cols - last * IN_TN, w_ref[...], 0.0)
        o_ref[...] = product(wt).astype(o_ref.dtype)

    @pl.when((j != 0) & (j != IN_SMALL_TILE) & (j != last))
    def _():
        o_ref[...] = product(w_ref[...]).astype(o_ref.dtype)


def _inproj(x2d, mod3, w_in_t, seq):
    m, d = x2d.shape
    n_cols = w_in_t.shape[0]
    tm = 1024
    n_tiles = pl.cdiv(n_cols, IN_TN)
    tiles_per_batch = seq // tm
    return pl.pallas_call(
        functools.partial(_inproj_kernel, n_cols=n_cols),
        grid=(m // tm, n_tiles),
        in_specs=[
            pl.BlockSpec((tm, d), lambda i, j: (i, 0)),
            pl.BlockSpec((None, N_ADA, d), lambda i, j: (i // tiles_per_batch, 0, 0)),
            pl.BlockSpec((IN_TN, d), lambda i, j: (j, 0)),
        ],
        out_specs=[pl.BlockSpec((tm, IN_TN), lambda i, j: (i, j)),
                   pl.BlockSpec((tm, IN_TN), lambda i, j: (i, 0))],
        out_shape=[jax.ShapeDtypeStruct((m, n_tiles * IN_TN), BF16),
                   jax.ShapeDtypeStruct((m, IN_TN), BF16)],
        scratch_shapes=[pltpu.VMEM((tm, d), BF16)],
        compiler_params=_cparams(("parallel", "arbitrary")),
        name="inproj",
    )(x2d, mod3, w_in_t)


def _log_sigmoid(x):
    return jnp.minimum(x, 0.0) - jnp.log1p(jnp.exp(-jnp.abs(x)))


def _prep_kernel(hi_ref, lo_ref, bf_ref, gq_ref, gkv_ref, wq_ref, wkv_ref, cos_ref, sin_ref,
                 qn_ref, qr_ref, kv_ref, kr_ref, cum_ref, cumt_ref, carry_ref, *, tiles_per_batch):
    i = pl.program_id(0)
    tm = hi_ref.shape[0]
    ps = hi_ref[...].astype(F32) + lo_ref[...].astype(F32)
    c0 = FOX_HEADS
    c1 = c0 + MLA_Q_LORA
    c2 = c1 + MLA_KV_LORA
    c3 = c2 + MLA_ROPE_DIM
    half = MLA_ROPE_DIM // 2
    lane_pad = jnp.zeros((tm, LANES - MLA_ROPE_DIM), F32)
    k_rope = jnp.concatenate([ps[:, c2:c3], lane_pad], axis=1)
    k_rope_partner = jnp.concatenate([ps[:, c2 + half:c3], ps[:, c2:c2 + half], lane_pad], axis=1)
    cos = cos_ref[...]
    sin = sin_ref[...]

    logf = _log_sigmoid(ps[:, 0:LANES] + bf_ref[...]) * LOG2E
    hi = logf.astype(BF16)
    r1 = logf - hi.astype(F32)
    mid = r1.astype(BF16)
    lo = (r1 - mid.astype(F32)).astype(BF16)
    row = lax.broadcasted_iota(jnp.int32, (tm, tm), 0)
    col = lax.broadcasted_iota(jnp.int32, (tm, tm), 1)
    tri = (row >= col).astype(BF16)
    cs = (jnp.dot(tri, hi, preferred_element_type=F32)
          + jnp.dot(tri, mid, preferred_element_type=F32)
          + jnp.dot(tri, lo, preferred_element_type=F32))

    @pl.when(i % tiles_per_batch == 0)
    def _():
        carry_ref[...] = jnp.zeros_like(carry_ref)

    cum = cs + carry_ref[...]
    carry_ref[...] = cum[tm - 1:tm, :]
    cum_ref[...] = cum
    cumt_ref[...] = cum.T[0:FOX_HEADS, :]

    cqn = _rms_norm(ps[:, c0:c1], gq_ref[...]).astype(BF16)
    qa = jnp.dot(cqn, wq_ref[...], preferred_element_type=F32)
    hw = MLA_HEADS * LANES
    qn_ref[...] = (qa[:, 0:hw] * MLA_QSCALE).astype(BF16)
    for h in range(MLA_HEADS):
        a = qa[:, hw + h * LANES: hw + (h + 1) * LANES]
        b = qa[:, 2 * hw + h * LANES: 2 * hw + (h + 1) * LANES]
        qr_ref[:, h * LANES:(h + 1) * LANES] = ((a * cos + b * sin) * MLA_QSCALE).astype(BF16)

    ckvn = _rms_norm(ps[:, c1:c2], gkv_ref[...]).astype(BF16)
    kv_ref[...] = jnp.dot(ckvn, wkv_ref[...], preferred_element_type=F32).astype(BF16)
    kr_ref[...] = (k_rope * cos + k_rope_partner * sin).astype(BF16)


def _prep(proj, proj_lo, bf_pad, g_q, g_kv, w_q_all, w_kv, cos_t, sin_t, seq):
    m = proj.shape[0]
    n_batch = m // seq
    tm = 512
    tiles_per_batch = seq // tm
    hw = MLA_HEADS * LANES
    const = lambda i: (0, 0)
    return pl.pallas_call(
        functools.partial(_prep_kernel, tiles_per_batch=tiles_per_batch),
        grid=(m // tm,),
        in_specs=[
            pl.BlockSpec((tm, IN_TN), lambda i: (i, IN_SMALL_TILE)),
            pl.BlockSpec((tm, IN_TN), lambda i: (i, 0)),
            pl.BlockSpec(bf_pad.shape, const),
            pl.BlockSpec(g_q.shape, const),
            pl.BlockSpec(g_kv.shape, const),
            pl.BlockSpec(w_q_all.shape, const),
            pl.BlockSpec(w_kv.shape, const),
            pl.BlockSpec((tm, LANES), lambda i: (i % tiles_per_batch, 0)),
            pl.BlockSpec((tm, LANES), lambda i: (i % tiles_per_batch, 0)),
        ],
        out_specs=[
            pl.BlockSpec((tm, hw), lambda i: (i, 0)),
            pl.BlockSpec((tm, hw), lambda i: (i, 0)),
            pl.BlockSpec((tm, w_kv.shape[1]), lambda i: (i, 0)),
            pl.BlockSpec((tm, LANES), lambda i: (i, 0)),
            pl.BlockSpec((tm, LANES), lambda i: (i, 0)),
            pl.BlockSpec((None, FOX_HEADS, tm), lambda i: (i // tiles_per_batch, 0, i % tiles_per_batch)),
        ],
        out_shape=[
            jax.ShapeDtypeStruct((m, hw), BF16),
            jax.ShapeDtypeStruct((m, hw), BF16),
            jax.ShapeDtypeStruct((m, w_kv.shape[1]), BF16),
            jax.ShapeDtypeStruct((m, LANES), BF16),
            jax.ShapeDtypeStruct((m, LANES), F32),
            jax.ShapeDtypeStruct((n_batch, FOX_HEADS, seq), F32),
        ],
        scratch_shapes=[pltpu.VMEM((1, LANES), F32)],
        compiler_params=_cparams(("arbitrary",)),
        name="prep",
    )(proj, proj_lo, bf_pad, g_q, g_kv, w_q_all, w_kv, cos_t, sin_t)


def _split3(x):
    hi = x.astype(BF16).astype(F32)
    r = x - hi
    mid = r.astype(BF16).astype(F32)
    lo = (r - mid).astype(BF16).astype(F32)
    return hi, mid, lo


def _online_softmax_step(t, m, acc, vt_aug):
    m_new = jnp.maximum(m, jnp.max(t, axis=0, keepdims=True))
    alpha = jnp.exp2(m - m_new)
    p = jnp.exp2(t - m_new).astype(BF16)
    pv = jnp.dot(vt_aug, p, preferred_element_type=F32)
    return m_new, alpha * acc + pv[0:acc.shape[0], :]


def _causal_mask(t, key0, query0, granule):
    r = lax.broadcasted_iota(jnp.int32, t.shape, 0) + key0
    c = lax.broadcasted_iota(jnp.int32, t.shape, 1) + query0
    return jnp.where(r <= (c | (granule - 1)), t, NEG_BIG)


def _flash_key_major(qi, tile, chains, keys, rhs, vt_aug, d_v, granule):
    acc_rows = d_v + 8
    n_rounds = tile // SUB
    pairs = [(n, n + 1) for n in range(0, len(chains), 2)]

    def qk_of(k0, is_diag):
        loaded = {}

        def product(n):
            half, hh = chains[n]
            n_keys = (half + 1) * HALF if is_diag else tile
            if (hh, n_keys) not in loaded:
                loaded[hh, n_keys] = keys(hh, k0, n_keys)
            return jnp.dot(loaded[hh, n_keys], rhs(hh)[:, half * HALF:(half + 1) * HALF],
                           preferred_element_type=F32)
        return product

    def visit(state, tiles):
        st = list(state)
        products = [qk_of(*t) for t in tiles]
        phases = [(ti, pair) for ti in range(len(tiles)) for pair in pairs]

        def issue(phase):
            ti, pair = phase
            return {n: products[ti](n) for n in pair}

        cur = issue(phases[0])
        for idx, (ti, pair) in enumerate(phases):
            nxt = issue(phases[idx + 1]) if idx + 1 < len(phases) else None
            k0, is_diag = tiles[ti]
            for sb in range(n_rounds):
                ks = pl.multiple_of(k0 + sb * SUB, SUB)
                for n in pair:
                    half, hh = chains[n]
                    if is_diag and sb * SUB >= (half + 1) * HALF:
                        continue
                    t = cur[n][sb * SUB:(sb + 1) * SUB, :]
                    if is_diag and (sb + 1) * SUB > half * HALF:
                        t = _causal_mask(t, sb * SUB, half * HALF, granule)
                    st[n] = _online_softmax_step(t, *st[n], vt_aug(hh, ks))
            cur = nxt
        return tuple(st)

    def two_full(j, state):
        k0 = pl.multiple_of(2 * j * tile, tile)
        return visit(state, [(k0, False), (pl.multiple_of(k0 + tile, tile), False)])

    init = tuple((jnp.full((1, HALF), NEG_BIG, F32), jnp.zeros((acc_rows, HALF), F32)) for _ in chains)
    state = lax.fori_loop(0, qi // 2, two_full, init)
    q0 = pl.multiple_of(qi * tile, tile)
    state = lax.cond(
        qi % 2 == 1,
        lambda st: visit(st, [(pl.multiple_of(q0 - tile, tile), False), (q0, True)]),
        lambda st: visit(st, [(q0, True)]),
        state)
    return [acc[0:d_v, :] * (1.0 / acc[d_v:d_v + 1, :]) for (_, acc) in state]


def _cast_specs(weights, grid):
    n_steps = math.prod(grid)
    specs, shapes = [], []
    for w in weights:
        rows, cols = w.shape
        slab = rows // n_steps
        assert slab * n_steps == rows and slab % 16 == 0, (w.shape, n_steps)
        specs.append(pl.BlockSpec((slab, cols), lambda a, b: (a * grid[1] + b, 0)))
        shapes.append(jax.ShapeDtypeStruct(w.shape, BF16))
    return specs, shapes


def _cast_slabs(refs):
    n = len(refs) // 2
    for src, dst in zip(refs[:n], refs[n:]):
        dst[...] = src[...].astype(dst.dtype)


def _stack_chain_outputs(outs, n_half):
    return jnp.concatenate(
        [jnp.concatenate([outs[half * 2 + hh] for half in range(n_half)], axis=1) for hh in (0, 1)], axis=0)


def _fox_kernel(q_ref, k_ref, v_ref, cum_ref, cumt_ref, *rest, tq, n_cast):
    cast_in, (o_ref, *cast_out), (vta_ref, kaug_ref) = rest[:n_cast], rest[n_cast:2 * n_cast + 1], rest[2 * n_cast + 1:]
    hp = pl.program_id(1)
    seq = k_ref.shape[0]
    dh = FOX_HEAD_DIM
    n_half = tq // HALF
    aug = 6

    vt = v_ref[...].T
    tail_rows = lax.broadcasted_iota(jnp.int32, (16, seq), 0)
    tail = jnp.where(tail_rows < 8, 1.0, 0.0).astype(BF16)
    for hh in (0, 1):
        vta_ref[hh, 0:dh, :] = vt[hh * dh:(hh + 1) * dh, :]
        vta_ref[hh, dh:dh + 16, :] = tail
    lane = lax.broadcasted_iota(jnp.int32, (seq, LANES), 1)
    cum = cum_ref[...]
    kaug = jnp.zeros((seq, LANES), F32)
    for hh in (0, 1):
        col = jnp.sum(jnp.where(lane == 2 * hp + hh, cum, 0.0), axis=1, keepdims=True)
        neg = jnp.broadcast_to(-col, (seq, LANES))
        for idx, piece in enumerate(_split3(neg)):
            kaug = jnp.where(lane == aug * hh + idx, piece, kaug)
        kaug = jnp.where((lane >= aug * hh + 3) & (lane < aug * hh + 6), 1.0, kaug)
    kaug_ref[...] = kaug.astype(BF16)

    chains = [(half, hh) for half in range(n_half) for hh in (0, 1)]

    def keys(hh, k0, n):
        return jnp.concatenate([k_ref[pl.ds(k0, n), :], kaug_ref[pl.ds(k0, n), :]], axis=1)

    def query_tile(qi, carry):
        q0 = pl.multiple_of(qi * tq, tq)
        qt = q_ref[pl.ds(q0, tq), :].T
        row = lax.broadcasted_iota(jnp.int32, (LANES, tq), 0)
        rhs = []
        for hh in (0, 1):
            qm = jnp.where((row >= hh * dh) & (row < (hh + 1) * dh), qt, jnp.zeros_like(qt))
            ci = cumt_ref[hh:hh + 1, pl.ds(q0, tq)]
            qa = jnp.where((row >= aug * hh) & (row < aug * hh + 3), 1.0, 0.0)
            for idx, piece in enumerate(_split3(ci)):
                qa = jnp.where(row == aug * hh + 3 + idx, jnp.broadcast_to(piece, (LANES, tq)), qa)
            rhs.append(jnp.concatenate([qm, qa.astype(BF16)], axis=0))
        outs = _flash_key_major(qi, tq, chains, keys, lambda hh: rhs[hh],
                                lambda hh, ks: vta_ref[hh, :, pl.ds(ks, SUB)], dh, granule=1)
        o_ref[pl.ds(q0, tq), :] = _stack_chain_outputs(outs, n_half).T.astype(o_ref.dtype)
        return carry

    lax.fori_loop(0, seq // tq, query_tile, 0)
    _cast_slabs(cast_in + tuple(cast_out))


def _fox_attention(proj, cum, cumt4, seq, q_col0, cast_weights):
    m = proj.shape[0]
    n_batch = m // seq
    n_pairs = FOX_HEADS // 2
    qb = q_col0 // LANES
    grid = (n_batch, n_pairs)
    cast_specs, cast_shapes = _cast_specs(cast_weights, grid)
    return pl.pallas_call(
        functools.partial(_fox_kernel, tq=512, n_cast=len(cast_weights)),
        grid=grid,
        in_specs=[
            pl.BlockSpec((seq, LANES), lambda b, hp: (b, qb + hp)),
            pl.BlockSpec((seq, LANES), lambda b, hp: (b, qb + n_pairs + hp)),
            pl.BlockSpec((seq, LANES), lambda b, hp: (b, qb + 2 * n_pairs + hp)),
            pl.BlockSpec((seq, LANES), lambda b, hp: (b, 0)),
            pl.BlockSpec((None, None, 2, seq), lambda b, hp: (b, hp, 0, 0)),
        ] + cast_specs,
        out_specs=[pl.BlockSpec((seq, LANES), lambda b, hp: (b, hp))] + cast_specs,
        out_shape=[jax.ShapeDtypeStruct((m, FOX_WIDTH), BF16)] + cast_shapes,
        scratch_shapes=[
            pltpu.VMEM((2, FOX_HEAD_DIM + 16, seq), BF16),
            pltpu.VMEM((seq, LANES), BF16),
        ],
        compiler_params=_cparams(("parallel", "parallel")),
        name="fox_attn",
    )(proj, proj, proj, cum, cumt4, *cast_weights)


def _mla_kernel(qn_ref, qr_ref, kv_ref, kr_ref, *rest, tq, n_cast):
    cast_in, (o_ref, *cast_out), (vta_ref,) = rest[:n_cast], rest[n_cast:2 * n_cast + 1], rest[2 * n_cast + 1:]
    seq = kv_ref.shape[0]
    dv = MLA_V_DIM
    n_half = tq // HALF

    tail_rows = lax.broadcasted_iota(jnp.int32, (16, seq), 0)
    tail = jnp.where(tail_rows < 8, 1.0, 0.0).astype(BF16)
    for hh in (0, 1):
        vta_ref[hh, 0:dv, :] = kv_ref[:, (2 * hh + 1) * LANES:(2 * hh + 2) * LANES].T
        vta_ref[hh, dv:dv + 16, :] = tail

    chains = [(half, hh) for half in range(n_half) for hh in (0, 1)]

    def keys(hh, k0, n):
        return jnp.concatenate([kv_ref[pl.ds(k0, n), 2 * hh * LANES:(2 * hh + 1) * LANES],
                                kr_ref[pl.ds(k0, n), :]], axis=1)

    def query_tile(qi, carry):
        q0 = pl.multiple_of(qi * tq, tq)
        rhs = [jnp.concatenate([qn_ref[pl.ds(q0, tq), hh * LANES:(hh + 1) * LANES],
                                qr_ref[pl.ds(q0, tq), hh * LANES:(hh + 1) * LANES]], axis=1).T
               for hh in (0, 1)]
        outs = _flash_key_major(qi, tq, chains, keys, lambda hh: rhs[hh],
                                lambda hh, ks: vta_ref[hh, :, pl.ds(ks, SUB)], dv, granule=CHUNK)
        o_ref[pl.ds(q0, tq), :] = _stack_chain_outputs(outs, n_half).T.astype(o_ref.dtype)
        return carry

    lax.fori_loop(0, seq // tq, query_tile, 0)
    _cast_slabs(cast_in + tuple(cast_out))


def _mla_attention(qn, qr, kv, kr, seq, cast_weights):
    m = qn.shape[0]
    n_batch = m // seq
    pair = 2 * LANES
    grid = (n_batch, MLA_HEADS // 2)
    cast_specs, cast_shapes = _cast_specs(cast_weights, grid)
    return pl.pallas_call(
        functools.partial(_mla_kernel, tq=512, n_cast=len(cast_weights)),
        grid=grid,
        in_specs=[
            pl.BlockSpec((seq, pair), lambda b, hp: (b, hp)),
            pl.BlockSpec((seq, pair), lambda b, hp: (b, hp)),
            pl.BlockSpec((seq, 2 * pair), lambda b, hp: (b, hp)),
            pl.BlockSpec((seq, LANES), lambda b, hp: (b, 0)),
        ] + cast_specs,
        out_specs=[pl.BlockSpec((seq, pair), lambda b, hp: (b, hp))] + cast_specs,
        out_shape=[jax.ShapeDtypeStruct((m, MLA_HEADS * MLA_V_DIM), BF16)] + cast_shapes,
        scratch_shapes=[pltpu.VMEM((2, MLA_V_DIM + 16, seq), BF16)],
        compiler_params=_cparams(("parallel", "parallel")),
        name="mla_attn",
    )(qn, qr, kv, kr, *cast_weights)


def _mix_kernel(yf_ref, ym_ref, gf_ref, gm_ref, x_ref, mod_ref, wbf_ref, wbm_ref, wo_ref, g_ref, b_ref,
                o_ref, mg_ref, *, n_chunk, g_off):
    d = o_ref.shape[1]
    yf = yf_ref[...]
    ym = ym_ref[...]

    def gate(ref, c):
        window = ref[:, c * n_chunk:(c + 1) * n_chunk + LANES]
        return jax.nn.sigmoid(window[:, g_off:g_off + n_chunk].astype(F32))

    for c in range(d // n_chunk):
        cols = slice(c * n_chunk, (c + 1) * n_chunk)
        a = jnp.dot(yf, wbf_ref[:, cols], preferred_element_type=F32)
        b = jnp.dot(ym, wbm_ref[:, cols], preferred_element_type=F32)
        merged = gate(gf_ref, c) * a + gate(gm_ref, c) * b
        mg_ref[:, cols] = merged.astype(BF16)
    mix = jnp.dot(mg_ref[...], wo_ref[...], preferred_element_type=F32)
    gate1 = mod_ref[2:3, :]
    o_ref[...] = _layer_norm(ALPHA * x_ref[...] + gate1 * mix, g_ref[...], b_ref[...])


def _mix(y_fox, y_mla, proj, gf_col, gm_col, x2d, mod3, w_bf, w_bm, w_o, ln_g, ln_b, seq):
    m, d = x2d.shape
    tm = 512
    tiles_per_batch = seq // tm
    const = lambda i: (0, 0)
    resident = dict(pipeline_mode=pl.Buffered(1))
    g_off = gf_col % LANES
    assert gm_col % LANES == g_off
    g_window = (pl.Element(tm), pl.Element(d + LANES))

    def window_at(col):
        return lambda i: (pl.multiple_of(i * tm, tm), col - g_off)

    return pl.pallas_call(
        functools.partial(_mix_kernel, n_chunk=512, g_off=g_off),
        grid=(m // tm,),
        in_specs=[
            pl.BlockSpec((tm, y_fox.shape[1]), lambda i: (i, 0)),
            pl.BlockSpec((tm, y_mla.shape[1]), lambda i: (i, 0)),
            pl.BlockSpec(g_window, window_at(gf_col)),
            pl.BlockSpec(g_window, window_at(gm_col)),
            pl.BlockSpec((tm, d), lambda i: (i, 0)),
            pl.BlockSpec((None, N_ADA, d), lambda i: (i // tiles_per_batch, 0, 0)),
            pl.BlockSpec(w_bf.shape, const, **resident),
            pl.BlockSpec(w_bm.shape, const, **resident),
            pl.BlockSpec(w_o.shape, const, **resident),
            pl.BlockSpec((1, d), const),
            pl.BlockSpec((1, d), const),
        ],
        out_specs=pl.BlockSpec((tm, d), lambda i: (i, 0)),
        out_shape=jax.ShapeDtypeStruct((m, d), F32),
        scratch_shapes=[pltpu.VMEM((tm, d), BF16)],
        compiler_params=_cparams(("parallel",)),
        name="mix",
    )(y_fox, y_mla, proj, proj, x2d, mod3, w_bf, w_bm, w_o, ln_g, ln_b)


def _mlp_kernel(x_ref, mod_ref, wu_ref, wd_ref, g_ref, b_ref, o_ref, u_ref):
    f = pl.program_id(1)

    def partial_sum(u):
        z = jnp.dot(u, wu_ref[...], preferred_element_type=F32)
        h = jnp.square(jnp.maximum(z, 0.0)).astype(BF16)
        return jnp.dot(h, wd_ref[...], preferred_element_type=F32)

    @pl.when(f == 0)
    def _():
        shift = mod_ref[3:4, :]
        scale = mod_ref[4:5, :]
        u = (x_ref[...] * (1.0 + scale) + shift).astype(BF16)
        u_ref[...] = u
        o_ref[...] = partial_sum(u)

    @pl.when(f != 0)
    def _():
        o_ref[...] += partial_sum(u_ref[...])

    @pl.when(f == pl.num_programs(1) - 1)
    def _():
        gate = mod_ref[5:6, :]
        o_ref[...] = _layer_norm(ALPHA * x_ref[...] + gate * o_ref[...], g_ref[...], b_ref[...])


def _mlp(x1, mod3, w_up, w_down, ln_g, ln_b, seq):
    m, d = x1.shape
    d_ff = w_up.shape[1]
    tm, tf = 1024, 1024
    tiles_per_batch = seq // tm
    return pl.pallas_call(
        _mlp_kernel,
        grid=(m // tm, d_ff // tf),
        in_specs=[
            pl.BlockSpec((tm, d), lambda i, f: (i, 0), pipeline_mode=pl.Buffered(1)),
            pl.BlockSpec((None, N_ADA, d), lambda i, f: (i // tiles_per_batch, 0, 0)),
            pl.BlockSpec((d, tf), lambda i, f: (0, f)),
            pl.BlockSpec((tf, d), lambda i, f: (f, 0)),
            pl.BlockSpec((1, d), lambda i, f: (0, 0)),
            pl.BlockSpec((1, d), lambda i, f: (0, 0)),
        ],
        out_specs=pl.BlockSpec((tm, d), lambda i, f: (i, 0)),
        out_shape=jax.ShapeDtypeStruct((m, d), F32),
        scratch_shapes=[pltpu.VMEM((tm, d), BF16)],
        compiler_params=_cparams(("parallel", "arbitrary")),
        name="mlp",
    )(x1, mod3, w_up, w_down, ln_g, ln_b)


def _pad_cols(w, width):
    return jnp.pad(w, ((0, 0), (0, width - w.shape[1])))


def _swap_halves(w):
    half = w.shape[-1] // 2
    return jnp.concatenate([w[..., half:], w[..., :half]], axis=-1)


def _layout_q_up(w_q_up):
    r = w_q_up.shape[0]
    w3 = w_q_up.reshape(r, MLA_HEADS, MLA_QK_DIM)
    w_nope = w3[:, :, :MLA_NOPE_DIM].reshape(r, MLA_HEADS * MLA_NOPE_DIM)
    w_rope = w3[:, :, MLA_NOPE_DIM:]
    pad = ((0, 0), (0, 0), (0, LANES - MLA_ROPE_DIM))
    w_r = jnp.pad(w_rope, pad).reshape(r, MLA_HEADS * LANES)
    w_rs = jnp.pad(_swap_halves(w_rope), pad).reshape(r, MLA_HEADS * LANES)
    return jnp.concatenate([w_nope, w_r, w_rs], axis=1).astype(BF16)


def _rope_tables(seq):
    pos = np.arange(seq, dtype=np.float64)
    inv_freq = ROPE_THETA ** (-np.arange(0, MLA_ROPE_DIM, 2, dtype=np.float64) / MLA_ROPE_DIM)
    ang = pos[:, None] * inv_freq[None, :]
    cos, sin = np.cos(ang), np.sin(ang)
    pad = np.zeros((seq, LANES - MLA_ROPE_DIM))
    cos_t = np.concatenate([cos, cos, pad], axis=1).astype(np.float32)
    sin_t = np.concatenate([-sin, sin, pad], axis=1).astype(np.float32)
    return jnp.asarray(cos_t), jnp.asarray(sin_t)


def kernel(x, c, w_ada, b_ada, w_in, b_forget, g_q_norm, w_q_up, g_kv_norm, w_kv_up, w_branch_fox,
           w_branch_mla, w_out, ln1_g, ln1_b, w_mlp_up, w_mlp_down, ln2_g, ln2_b):
    n_batch, seq, d = x.shape
    assert w_ada.shape[0] == DEPTH
    m = n_batch * seq
    x2d = x.reshape(m, d)

    mod3 = _ada(c, w_ada[0], b_ada[0]).reshape(n_batch, N_ADA, d)

    small_col = 3 * FOX_WIDTH
    gf_col = small_col + FOX_HEADS + MLA_Q_LORA + MLA_KV_LORA + MLA_ROPE_DIM
    gm_col = gf_col + d
    assert w_in.shape[2] == gm_col + d and small_col == IN_SMALL_TILE * IN_TN and gf_col <= small_col + IN_TN
    proj, proj_lo = _inproj(x2d, mod3, w_in[0].T, seq)

    cos_t, sin_t = _rope_tables(seq)
    bf_pad = _pad_cols(b_forget[0].reshape(1, FOX_HEADS), LANES)
    qn, qr, kv, kr, cum, cumt = _prep(
        proj, proj_lo, bf_pad, g_q_norm[0].reshape(1, -1), g_kv_norm[0].reshape(1, -1),
        _layout_q_up(w_q_up[0]), w_kv_up[0].astype(BF16), cos_t, sin_t, seq)

    y_fox, w_up_b, w_down_b = _fox_attention(
        proj, cum, cumt.reshape(n_batch, FOX_HEADS // 2, 2, seq), seq, 0, (w_mlp_up[0], w_mlp_down[0]))
    y_mla, w_bf_b, w_bm_b, w_o_b = _mla_attention(
        qn, qr, kv, kr, seq, (w_branch_fox[0], w_branch_mla[0], w_out[0]))

    x1 = _mix(y_fox, y_mla, proj, gf_col, gm_col, x2d, mod3, w_bf_b, w_bm_b, w_o_b,
              ln1_g[0].reshape(1, d), ln1_b[0].reshape(1, d), seq)
    x2 = _mlp(x1, mod3, w_up_b, w_down_b, ln2_g[0].reshape(1, d), ln2_b[0].reshape(1, d), seq)
    return x2.reshape(n_batch, seq, d)
```

```python
import functools
import math

import jax
import jax.numpy as jnp
import numpy as np
from jax import lax
from jax.experimental import pallas as pl
from jax.experimental.pallas import tpu as pltpu

F32 = jnp.float32
BF16 = jnp.bfloat16

FOX_HEADS = 16
FOX_HEAD_DIM = 64
FOX_WIDTH = FOX_HEADS * FOX_HEAD_DIM
MLA_HEADS = 8
MLA_NOPE_DIM = 128
MLA_ROPE_DIM = 64
MLA_V_DIM = 128
MLA_Q_LORA = 512
MLA_KV_LORA = 256
MLA_QK_DIM = MLA_NOPE_DIM + MLA_ROPE_DIM
CHUNK = 64
ROPE_THETA = 10000.0
LN_EPS = 1e-5
RMS_EPS = 1e-6
N_ADA = 6
DEPTH = 1
ALPHA = (2.0 * DEPTH) ** 0.25

LANES = 128
LOG2E = math.log2(math.e)
FOX_QSCALE = FOX_HEAD_DIM ** -0.5 * LOG2E
MLA_QSCALE = MLA_QK_DIM ** -0.5 * LOG2E
NEG_BIG = -1e30
VMEM_LIMIT = 56 * 1024 * 1024

SUB = 128
HALF = 256


def _cparams(semantics):
    return pltpu.CompilerParams(dimension_semantics=semantics, vmem_limit_bytes=VMEM_LIMIT)


def _layer_norm(y, g, b):
    mu = jnp.mean(y, axis=-1, keepdims=True)
    yc = y - mu
    var = jnp.mean(yc * yc, axis=-1, keepdims=True)
    return yc * lax.rsqrt(var + LN_EPS) * g + b


def _rms_norm(c, g):
    return c * lax.rsqrt(jnp.mean(c * c, axis=-1, keepdims=True) + RMS_EPS) * g


def _ada_kernel(ct_ref, w_ref, b_ref, o_ref, *, n_batch, k_chunk):
    d = w_ref.shape[0]
    ct = ct_ref[...]
    s = ct * jax.nn.sigmoid(ct)
    accs = [b_ref[...] for _ in range(n_batch)]
    for kc in range(d // k_chunk):
        rows = slice(kc * k_chunk, (kc + 1) * k_chunk)
        w = w_ref[rows, :]
        for b in range(n_batch):
            accs[b] = accs[b] + jnp.sum(w * s[rows, b:b + 1], axis=0, keepdims=True)
    for b in range(n_batch):
        o_ref[b:b + 1, :] = accs[b]


def _ada(c, w_ada, b_ada):
    n_batch, d = c.shape
    n = w_ada.shape[1]
    tn = 1024
    return pl.pallas_call(
        functools.partial(_ada_kernel, n_batch=n_batch, k_chunk=256),
        grid=(n // tn,),
        in_specs=[
            pl.BlockSpec((d, n_batch), lambda j: (0, 0)),
            pl.BlockSpec((d, tn), lambda j: (0, j)),
            pl.BlockSpec((1, tn), lambda j: (0, j)),
        ],
        out_specs=pl.BlockSpec((n_batch, tn), lambda j: (0, j)),
        out_shape=jax.ShapeDtypeStruct((n_batch, n), F32),
        compiler_params=_cparams(("arbitrary",)),
        name="ada",
    )(c.T, w_ada, b_ada.reshape(1, n))


IN_TN = 1024
IN_SMALL_TILE = 3 * FOX_WIDTH // IN_TN


def _inproj_kernel(x_ref, mod_ref, w_ref, o_ref, lo_ref, u_ref, *, n_cols):
    j = pl.program_id(1)
    last = pl.num_programs(1) - 1

    def product(wt, u=None):
        u = u_ref[...] if u is None else u
        return lax.dot_general(u, wt.astype(BF16), (((1,), (1,)), ((), ())), preferred_element_type=F32)

    @pl.when(j == 0)
    def _():
        shift = mod_ref[0:1, :]
        scale = mod_ref[1:2, :]
        u = (x_ref[...] * (1.0 + scale) + shift).astype(BF16)
        u_ref[...] = u
        o_ref[...] = (product(w_ref[...], u) * FOX_QSCALE).astype(o_ref.dtype)

    @pl.when(j == IN_SMALL_TILE)
    def _():
        acc = product(w_ref[...])
        hi = acc.astype(o_ref.dtype)
        o_ref[...] = hi
        lo_ref[...] = (acc - hi.astype(F32)).astype(lo_ref.dtype)

    @pl.when(j == last)
    def _():
        row = lax.broadcasted_iota(jnp.int32, w_ref.shape, 0)
        wt = jnp.where(row < n_cols - last * IN_TN, w_ref[...], 0.0)
        o_ref[...] = product(wt).astype(o_ref.dtype)

    @pl.when((j != 0) & (j != IN_SMALL_TILE) & (j != last))
    def _():
        o_ref[...] = product(w_ref[...]).astype(o_ref.dtype)


def _inproj(x2d, mod3, w_in_t, seq):
    m, d = x2d.shape
    n_cols = w_in_t.shape[0]
    tm = 1024
    n_tiles = pl.cdiv(n_cols, IN_TN)
    tiles_per_batch = seq // tm
    return pl.pallas_call(
        functools.partial(_inproj_kernel, n_cols=n_cols),
        grid=(m // tm, n_tiles),
        in_specs=[
            pl.BlockSpec((tm, d), lambda i, j: (i, 0)),
            pl.BlockSpec((None, N_ADA, d), lambda i, j: (i // tiles_per_batch, 0, 0)),
            pl.BlockSpec((IN_TN, d), lambda i, j: (j, 0)),
        ],
        out_specs=[pl.BlockSpec((tm, IN_TN), lambda i, j: (i, j)),
                   pl.BlockSpec((tm, IN_TN), lambda i, j: (i, 0))],
        out_shape=[jax.ShapeDtypeStruct((m, n_tiles * IN_TN), BF16),
                   jax.ShapeDtypeStruct((m, IN_TN), BF16)],
        scratch_shapes=[pltpu.VMEM((tm, d), BF16)],
        compiler_params=_cparams(("parallel", "arbitrary")),
        name="inproj",
    )(x2d, mod3, w_in_t)


def _log_sigmoid(x):
    return jnp.minimum(x, 0.0) - jnp.log1p(jnp.exp(-jnp.abs(x)))


def _prep_kernel(hi_ref, lo_ref, bf_ref, gq_ref, gkv_ref, wq_ref, wkv_ref, cos_ref, sin_ref,
                 qn_ref, qr_ref, kv_ref, kr_ref, cum_ref, cumt_ref, carry_ref, *, tiles_per_batch):
    i = pl.program_id(0)
    tm = hi_ref.shape[0]
    ps = hi_ref[...].astype(F32) + lo_ref[...].astype(F32)
    c0 = FOX_HEADS
    c1 = c0 + MLA_Q_LORA
    c2 = c1 + MLA_KV_LORA
    c3 = c2 + MLA_ROPE_DIM
    half = MLA_ROPE_DIM // 2
    lane_pad = jnp.zeros((tm, LANES - MLA_ROPE_DIM), F32)
    k_rope = jnp.concatenate([ps[:, c2:c3], lane_pad], axis=1)
    k_rope_partner = jnp.concatenate([ps[:, c2 + half:c3], ps[:, c2:c2 + half], lane_pad], axis=1)
    cos = cos_ref[...]
    sin = sin_ref[...]

    logf = _log_sigmoid(ps[:, 0:LANES] + bf_ref[...]) * LOG2E
    hi = logf.astype(BF16)
    r1 = logf - hi.astype(F32)
    mid = r1.astype(BF16)
    lo = (r1 - mid.astype(F32)).astype(BF16)
    row = lax.broadcasted_iota(jnp.int32, (tm, tm), 0)
    col = lax.broadcasted_iota(jnp.int32, (tm, tm), 1)
    tri = (row >= col).astype(BF16)
    cs = (jnp.dot(tri, hi, preferred_element_type=F32)
          + jnp.dot(tri, mid, preferred_element_type=F32)
          + jnp.dot(tri, lo, preferred_element_type=F32))

    @pl.when(i % tiles_per_batch == 0)
    def _():
        carry_ref[...] = jnp.zeros_like(carry_ref)

    cum = cs + carry_ref[...]
    carry_ref[...] = cum[tm - 1:tm, :]
    cum_ref[...] = cum
    cumt_ref[...] = cum.T[0:FOX_HEADS, :]

    cqn = _rms_norm(ps[:, c0:c1], gq_ref[...]).astype(BF16)
    qa = jnp.dot(cqn, wq_ref[...], preferred_element_type=F32)
    hw = MLA_HEADS * LANES
    qn_ref[...] = (qa[:, 0:hw] * MLA_QSCALE).astype(BF16)
    for h in range(MLA_HEADS):
        a = qa[:, hw + h * LANES: hw + (h + 1) * LANES]
        b = qa[:, 2 * hw + h * LANES: 2 * hw + (h + 1) * LANES]
        qr_ref[:, h * LANES:(h + 1) * LANES] = ((a * cos + b * sin) * MLA_QSCALE).astype(BF16)

    ckvn = _rms_norm(ps[:, c1:c2], gkv_ref[...]).astype(BF16)
    kv_ref[...] = jnp.dot(ckvn, wkv_ref[...], preferred_element_type=F32).astype(BF16)
    kr_ref[...] = (k_rope * cos + k_rope_partner * sin).astype(BF16)


def _prep(proj, proj_lo, bf_pad, g_q, g_kv, w_q_all, w_kv, cos_t, sin_t, seq):
    m = proj.shape[0]
    n_batch = m // seq
    tm = 512
    tiles_per_batch = seq // tm
    hw = MLA_HEADS * LANES
    const = lambda i: (0, 0)
    return pl.pallas_call(
        functools.partial(_prep_kernel, tiles_per_batch=tiles_per_batch),
        grid=(m // tm,),
        in_specs=[
            pl.BlockSpec((tm, IN_TN), lambda i: (i, IN_SMALL_TILE)),
            pl.BlockSpec((tm, IN_TN), lambda i: (i, 0)),
            pl.BlockSpec(bf_pad.shape, const),
            pl.BlockSpec(g_q.shape, const),
            pl.BlockSpec(g_kv.shape, const),
            pl.BlockSpec(w_q_all.shape, const),
            pl.BlockSpec(w_kv.shape, const),
            pl.BlockSpec((tm, LANES), lambda i: (i % tiles_per_batch, 0)),
            pl.BlockSpec((tm, LANES), lambda i: (i % tiles_per_batch, 0)),
        ],
        out_specs=[
            pl.BlockSpec((tm, hw), lambda i: (i, 0)),
            pl.BlockSpec((tm, hw), lambda i: (i, 0)),
            pl.BlockSpec((tm, w_kv.shape[1]), lambda i: (i, 0)),
            pl.BlockSpec((tm, LANES), lambda i: (i, 0)),
            pl.BlockSpec((tm, LANES), lambda i: (i, 0)),
            pl.BlockSpec((None, FOX_HEADS, tm), lambda i: (i // tiles_per_batch, 0, i % tiles_per_batch)),
        ],
        out_shape=[
            jax.ShapeDtypeStruct((m, hw), BF16),
            jax.ShapeDtypeStruct((m, hw), BF16),
            jax.ShapeDtypeStruct((m, w_kv.shape[1]), BF16),
            jax.ShapeDtypeStruct((m, LANES), BF16),
            jax.ShapeDtypeStruct((m, LANES), F32),
            jax.ShapeDtypeStruct((n_batch, FOX_HEADS, seq), F32),
        ],
        scratch_shapes=[pltpu.VMEM((1, LANES), F32)],
        compiler_params=_cparams(("arbitrary",)),
        name="prep",
    )(proj, proj_lo, bf_pad, g_q, g_kv, w_q_all, w_kv, cos_t, sin_t)


def _split3(x):
    hi = x.astype(BF16).astype(F32)
    r = x - hi
    mid = r.astype(BF16).astype(F32)
    lo = (r - mid).astype(BF16).astype(F32)
    return hi, mid, lo


def _online_softmax_step(t, m, acc, vt_aug):
    m_new = jnp.maximum(m, jnp.max(t, axis=0, keepdims=True))
    alpha = jnp.exp2(m - m_new)
    p = jnp.exp2(t - m_new).astype(BF16)
    pv = jnp.dot(vt_aug, p, preferred_element_type=F32)
    return m_new, alpha * acc + pv[0:acc.shape[0], :]


def _causal_mask(t, key0, query0, granule):
    r = lax.broadcasted_iota(jnp.int32, t.shape, 0) + key0
    c = lax.broadcasted_iota(jnp.int32, t.shape, 1) + query0
    return jnp.where(r <= (c | (granule - 1)), t, NEG_BIG)


def _flash_key_major(qi, tile, chains, keys, rhs, vt_aug, d_v, granule):
    acc_rows = d_v + 8
    n_rounds = tile // SUB
    pairs = [(n, n + 1) for n in range(0, len(chains), 2)]

    def qk_of(k0, is_diag):
        loaded = {}

        def product(n):
            half, hh = chains[n]
            n_keys = (half + 1) * HALF if is_diag else tile
            if (hh, n_keys) not in loaded:
                loaded[hh, n_keys] = keys(hh, k0, n_keys)
            return jnp.dot(loaded[hh, n_keys], rhs(hh)[:, half * HALF:(half + 1) * HALF],
                           preferred_element_type=F32)
        return product

    def visit(state, tiles):
        st = list(state)
        products = [qk_of(*t) for t in tiles]
        phases = [(ti, pair) for ti in range(len(tiles)) for pair in pairs]

        def issue(phase):
            ti, pair = phase
            return {n: products[ti](n) for n in pair}

        cur = issue(phases[0])
        for idx, (ti, pair) in enumerate(phases):
            nxt = issue(phases[idx + 1]) if idx + 1 < len(phases) else None
            k0, is_diag = tiles[ti]
            for sb in range(n_rounds):
                ks = pl.multiple_of(k0 + sb * SUB, SUB)
                for n in pair:
                    half, hh = chains[n]
                    if is_diag and sb * SUB >= (half + 1) * HALF:
                        continue
                    t = cur[n][sb * SUB:(sb + 1) * SUB, :]
                    if is_diag and (sb + 1) * SUB > half * HALF:
                        t = _causal_mask(t, sb * SUB, half * HALF, granule)
                    st[n] = _online_softmax_step(t, *st[n], vt_aug(hh, ks))
            cur = nxt
        return tuple(st)

    def two_full(j, state):
        k0 = pl.multiple_of(2 * j * tile, tile)
        return visit(state, [(k0, False), (pl.multiple_of(k0 + tile, tile), False)])

    init = tuple((jnp.full((1, HALF), NEG_BIG, F32), jnp.zeros((acc_rows, HALF), F32)) for _ in chains)
    state = lax.fori_loop(0, qi // 2, two_full, init)
    q0 = pl.multiple_of(qi * tile, tile)
    state = lax.cond(
        qi % 2 == 1,
        lambda st: visit(st, [(pl.multiple_of(q0 - tile, tile), False), (q0, True)]),
        lambda st: visit(st, [(q0, True)]),
        state)
    return [acc[0:d_v, :] * (1.0 / acc[d_v:d_v + 1, :]) for (_, acc) in state]


def _cast_specs(weights, grid):
    n_steps = math.prod(grid)
    specs, shapes = [], []
    for w in weights:
        rows, cols = w.shape
        slab = rows // n_steps
        assert slab * n_steps == rows and slab % 16 == 0, (w.shape, n_steps)
        specs.append(pl.BlockSpec((slab, cols), lambda a, b: (a * grid[1] + b, 0)))
        shapes.append(jax.ShapeDtypeStruct(w.shape, BF16))
    return specs, shapes


def _cast_slabs(refs):
    n = len(refs) // 2
    for src, dst in zip(refs[:n], refs[n:]):
        dst[...] = src[...].astype(dst.dtype)


def _stack_chain_outputs(outs, n_half):
    return jnp.concatenate(
        [jnp.concatenate([outs[half * 2 + hh] for half in range(n_half)], axis=1) for hh in (0, 1)], axis=0)


def _fox_kernel(q_ref, k_ref, v_ref, cum_ref, cumt_ref, *rest, tq, n_cast):
    cast_in, (o_ref, *cast_out), (vta_ref, kaug_ref) = rest[:n_cast], rest[n_cast:2 * n_cast + 1], rest[2 * n_cast + 1:]
    hp = pl.program_id(1)
    seq = k_ref.shape[0]
    dh = FOX_HEAD_DIM
    n_half = tq // HALF
    aug = 6

    vt = v_ref[...].T
    tail_rows = lax.broadcasted_iota(jnp.int32, (16, seq), 0)
    tail = jnp.where(tail_rows < 8, 1.0, 0.0).astype(BF16)
    for hh in (0, 1):
        vta_ref[hh, 0:dh, :] = vt[hh * dh:(hh + 1) * dh, :]
        vta_ref[hh, dh:dh + 16, :] = tail
    lane = lax.broadcasted_iota(jnp.int32, (seq, LANES), 1)
    cum = cum_ref[...]
    kaug = jnp.zeros((seq, LANES), F32)
    for hh in (0, 1):
        col = jnp.sum(jnp.where(lane == 2 * hp + hh, cum, 0.0), axis=1, keepdims=True)
        neg = jnp.broadcast_to(-col, (seq, LANES))
        for idx, piece in enumerate(_split3(neg)):
            kaug = jnp.where(lane == aug * hh + idx, piece, kaug)
        kaug = jnp.where((lane >= aug * hh + 3) & (lane < aug * hh + 6), 1.0, kaug)
    kaug_ref[...] = kaug.astype(BF16)

    chains = [(half, hh) for half in range(n_half) for hh in (0, 1)]

    def keys(hh, k0, n):
        return jnp.concatenate([k_ref[pl.ds(k0, n), :], kaug_ref[pl.ds(k0, n), :]], axis=1)

    def query_tile(qi, carry):
        q0 = pl.multiple_of(qi * tq, tq)
        qt = q_ref[pl.ds(q0, tq), :].T
        row = lax.broadcasted_iota(jnp.int32, (LANES, tq), 0)
        rhs = []
        for hh in (0, 1):
            qm = jnp.where((row >= hh * dh) & (row < (hh + 1) * dh), qt, jnp.zeros_like(qt))
            ci = cumt_ref[hh:hh + 1, pl.ds(q0, tq)]
            qa = jnp.where((row >= aug * hh) & (row < aug * hh + 3), 1.0, 0.0)
            for idx, piece in enumerate(_split3(ci)):
                qa = jnp.where(row == aug * hh + 3 + idx, jnp.broadcast_to(piece, (LANES, tq)), qa)
            rhs.append(jnp.concatenate([qm, qa.astype(BF16)], axis=0))
        outs = _flash_key_major(qi, tq, chains, keys, lambda hh: rhs[hh],
                                lambda hh, ks: vta_ref[hh, :, pl.ds(ks, SUB)], dh, granule=1)
        o_ref[pl.ds(q0, tq), :] = _stack_chain_outputs(outs, n_half).T.astype(o_ref.dtype)
        return carry

    lax.fori_loop(0, seq // tq, query_tile, 0)
    _cast_slabs(cast_in + tuple(cast_out))


def _fox_attention(proj, cum, cumt4, seq, q_col0, cast_weights):
    m = proj.shape[0]
    n_batch = m // seq
    n_pairs = FOX_HEADS // 2
    qb = q_col0 // LANES
    grid = (n_batch, n_pairs)
    cast_specs, cast_shapes = _cast_specs(cast_weights, grid)
    return pl.pallas_call(
        functools.partial(_fox_kernel, tq=512, n_cast=len(cast_weights)),
        grid=grid,
        in_specs=[
            pl.BlockSpec((seq, LANES), lambda b, hp: (b, qb + hp)),
            pl.BlockSpec((seq, LANES), lambda b, hp: (b, qb + n_pairs + hp)),
            pl.BlockSpec((seq, LANES), lambda b, hp: (b, qb + 2 * n_pairs + hp)),
            pl.BlockSpec((seq, LANES), lambda b, hp: (b, 0)),
            pl.BlockSpec((None, None, 2, seq), lambda b, hp: (b, hp, 0, 0)),
        ] + cast_specs,
        out_specs=[pl.BlockSpec((seq, LANES), lambda b, hp: (b, hp))] + cast_specs,
        out_shape=[jax.ShapeDtypeStruct((m, FOX_WIDTH), BF16)] + cast_shapes,
        scratch_shapes=[
            pltpu.VMEM((2, FOX_HEAD_DIM + 16, seq), BF16),
            pltpu.VMEM((seq, LANES), BF16),
        ],
        compiler_params=_cparams(("parallel", "parallel")),
        name="fox_attn",
    )(proj, proj, proj, cum, cumt4, *cast_weights)


def _mla_kernel(qn_ref, qr_ref, kv_ref, kr_ref, *rest, tq, n_cast):
    cast_in, (o_ref, *cast_out), (vta_ref,) = rest[:n_cast], rest[n_cast:2 * n_cast + 1], rest[2 * n_cast + 1:]
    seq = kv_ref.shape[0]
    dv = MLA_V_DIM
    n_half = tq // HALF

    tail_rows = lax.broadcasted_iota(jnp.int32, (16, seq), 0)
    tail = jnp.where(tail_rows < 8, 1.0, 0.0).astype(BF16)
    for hh in (0, 1):
        vta_ref[hh, 0:dv, :] = kv_ref[:, (2 * hh + 1) * LANES:(2 * hh + 2) * LANES].T
        vta_ref[hh, dv:dv + 16, :] = tail

    chains = [(half, hh) for half in range(n_half) for hh in (0, 1)]

    def keys(hh, k0, n):
        return jnp.concatenate([kv_ref[pl.ds(k0, n), 2 * hh * LANES:(2 * hh + 1) * LANES],
                                kr_ref[pl.ds(k0, n), :]], axis=1)

    def query_tile(qi, carry):
        q0 = pl.multiple_of(qi * tq, tq)
        rhs = [jnp.concatenate([qn_ref[pl.ds(q0, tq), hh * LANES:(hh + 1) * LANES],
                                qr_ref[pl.ds(q0, tq), hh * LANES:(hh + 1) * LANES]], axis=1).T
               for hh in (0, 1)]
        outs = _flash_key_major(qi, tq, chains, keys, lambda hh: rhs[hh],
                                lambda hh, ks: vta_ref[hh, :, pl.ds(ks, SUB)], dv, granule=CHUNK)
        o_ref[pl.ds(q0, tq), :] = _stack_chain_outputs(outs, n_half).T.astype(o_ref.dtype)
        return carry

    lax.fori_loop(0, seq // tq, query_tile, 0)
    _cast_slabs(cast_in + tuple(cast_out))


def _mla_attention(qn, qr, kv, kr, seq, cast_weights):
    m = qn.shape[0]
    n_batch = m // seq
    pair = 2 * LANES
    grid = (n_batch, MLA_HEADS // 2)
    cast_specs, cast_shapes = _cast_specs(cast_weights, grid)
    return pl.pallas_call(
        functools.partial(_mla_kernel, tq=512, n_cast=len(cast_weights)),
        grid=grid,
        in_specs=[
            pl.BlockSpec((seq, pair), lambda b, hp: (b, hp)),
            pl.BlockSpec((seq, pair), lambda b, hp: (b, hp)),
            pl.BlockSpec((seq, 2 * pair), lambda b, hp: (b, hp)),
            pl.BlockSpec((seq, LANES), lambda b, hp: (b, 0)),
        ] + cast_specs,
        out_specs=[pl.BlockSpec((seq, pair), lambda b, hp: (b, hp))] + cast_specs,
        out_shape=[jax.ShapeDtypeStruct((m, MLA_HEADS * MLA_V_DIM), BF16)] + cast_shapes,
        scratch_shapes=[pltpu.VMEM((2, MLA_V_DIM + 16, seq), BF16)],
        compiler_params=_cparams(("parallel", "parallel")),
        name="mla_attn",
    )(qn, qr, kv, kr, *cast_weights)


def _mix_kernel(yf_ref, ym_ref, gf_ref, gm_ref, x_ref, mod_ref, wbf_ref, wbm_ref, wo_ref, g_ref, b_ref,
                o_ref, mg_ref, *, n_chunk, g_off):
    d = o_ref.shape[1]
    yf = yf_ref[...]
    ym = ym_ref[...]

    def gate(ref, c):
        window = ref[:, c * n_chunk:(c + 1) * n_chunk + LANES]
        return jax.nn.sigmoid(window[:, g_off:g_off + n_chunk].astype(F32))

    for c in range(d // n_chunk):
        cols = slice(c * n_chunk, (c + 1) * n_chunk)
        a = jnp.dot(yf, wbf_ref[:, cols], preferred_element_type=F32)
        b = jnp.dot(ym, wbm_ref[:, cols], preferred_element_type=F32)
        merged = gate(gf_ref, c) * a + gate(gm_ref, c) * b
        mg_ref[:, cols] = merged.astype(BF16)
    mix = jnp.dot(mg_ref[...], wo_ref[...], preferred_element_type=F32)
    gate1 = mod_ref[2:3, :]
    o_ref[...] = _layer_norm(ALPHA * x_ref[...] + gate1 * mix, g_ref[...], b_ref[...])


def _mix(y_fox, y_mla, proj, gf_col, gm_col, x2d, mod3, w_bf, w_bm, w_o, ln_g, ln_b, seq):
    m, d = x2d.shape
    tm = 512
    tiles_per_batch = seq // tm
    const = lambda i: (0, 0)
    resident = dict(pipeline_mode=pl.Buffered(1))
    g_off = gf_col % LANES
    assert gm_col % LANES == g_off
    g_window = (pl.Element(tm), pl.Element(d + LANES))

    def window_at(col):
        return lambda i: (pl.multiple_of(i * tm, tm), col - g_off)

    return pl.pallas_call(
        functools.partial(_mix_kernel, n_chunk=512, g_off=g_off),
        grid=(m // tm,),
        in_specs=[
            pl.BlockSpec((tm, y_fox.shape[1]), lambda i: (i, 0)),
            pl.BlockSpec((tm, y_mla.shape[1]), lambda i: (i, 0)),
            pl.BlockSpec(g_window, window_at(gf_col)),
            pl.BlockSpec(g_window, window_at(gm_col)),
            pl.BlockSpec((tm, d), lambda i: (i, 0)),
            pl.BlockSpec((None, N_ADA, d), lambda i: (i // tiles_per_batch, 0, 0)),
            pl.BlockSpec(w_bf.shape, const, **resident),
            pl.BlockSpec(w_bm.shape, const, **resident),
            pl.BlockSpec(w_o.shape, const, **resident),
            pl.BlockSpec((1, d), const),
            pl.BlockSpec((1, d), const),
        ],
        out_specs=pl.BlockSpec((tm, d), lambda i: (i, 0)),
        out_shape=jax.ShapeDtypeStruct((m, d), F32),
        scratch_shapes=[pltpu.VMEM((tm, d), BF16)],
        compiler_params=_cparams(("parallel",)),
        name="mix",
    )(y_fox, y_mla, proj, proj, x2d, mod3, w_bf, w_bm, w_o, ln_g, ln_b)


def _mlp_kernel(x_ref, mod_ref, wu_ref, wd_ref, g_ref, b_ref, o_ref, u_ref):
    f = pl.program_id(1)

    def partial_sum(u):
        z = jnp.dot(u, wu_ref[...], preferred_element_type=F32)
        h = jnp.square(jnp.maximum(z, 0.0)).astype(BF16)
        return jnp.dot(h, wd_ref[...], preferred_element_type=F32)

    @pl.when(f == 0)
    def _():
        shift = mod_ref[3:4, :]
        scale = mod_ref[4:5, :]
        u = (x_ref[...] * (1.0 + scale) + shift).astype(BF16)
        u_ref[...] = u
        o_ref[...] = partial_sum(u)

    @pl.when(f != 0)
    def _():
        o_ref[...] += partial_sum(u_ref[...])

    @pl.when(f == pl.num_programs(1) - 1)
    def _():
        gate = mod_ref[5:6, :]
        o_ref[...] = _layer_norm(ALPHA * x_ref[...] + gate * o_ref[...], g_ref[...], b_ref[...])


def _mlp(x1, mod3, w_up, w_down, ln_g, ln_b, seq):
    m, d = x1.shape
    d_ff = w_up.shape[1]
    tm, tf = 1024, 512
    tiles_per_batch = seq // tm
    return pl.pallas_call(
        _mlp_kernel,
        grid=(m // tm, d_ff // tf),
        in_specs=[
            pl.BlockSpec((tm, d), lambda i, f: (i, 0)),
            pl.BlockSpec((None, N_ADA, d), lambda i, f: (i // tiles_per_batch, 0, 0)),
            pl.BlockSpec((d, tf), lambda i, f: (0, f)),
            pl.BlockSpec((tf, d), lambda i, f: (f, 0)),
            pl.BlockSpec((1, d), lambda i, f: (0, 0)),
            pl.BlockSpec((1, d), lambda i, f: (0, 0)),
        ],
        out_specs=pl.BlockSpec((tm, d), lambda i, f: (i, 0)),
        out_shape=jax.ShapeDtypeStruct((m, d), F32),
        scratch_shapes=[pltpu.VMEM((tm, d), BF16)],
        compiler_params=_cparams(("parallel", "arbitrary")),
        name="mlp",
    )(x1, mod3, w_up, w_down, ln_g, ln_b)


def _pad_cols(w, width):
    return jnp.pad(w, ((0, 0), (0, width - w.shape[1])))


def _swap_halves(w):
    half = w.shape[-1] // 2
    return jnp.concatenate([w[..., half:], w[..., :half]], axis=-1)


def _layout_q_up(w_q_up):
    r = w_q_up.shape[0]
    w3 = w_q_up.reshape(r, MLA_HEADS, MLA_QK_DIM)
    w_nope = w3[:, :, :MLA_NOPE_DIM].reshape(r, MLA_HEADS * MLA_NOPE_DIM)
    w_rope = w3[:, :, MLA_NOPE_DIM:]
    pad = ((0, 0), (0, 0), (0, LANES - MLA_ROPE_DIM))
    w_r = jnp.pad(w_rope, pad).reshape(r, MLA_HEADS * LANES)
    w_rs = jnp.pad(_swap_halves(w_rope), pad).reshape(r, MLA_HEADS * LANES)
    return jnp.concatenate([w_nope, w_r, w_rs], axis=1).astype(BF16)


def _rope_tables(seq):
    pos = np.arange(seq, dtype=np.float64)
    inv_freq = ROPE_THETA ** (-np.arange(0, MLA_ROPE_DIM, 2, dtype=np.float64) / MLA_ROPE_DIM)
    ang = pos[:, None] * inv_freq[None, :]
    cos, sin = np.cos(ang), np.sin(ang)
    pad = np.zeros((seq, LANES - MLA_ROPE_DIM))
    cos_t = np.concatenate([cos, cos, pad], axis=1).astype(np.float32)
    sin_t = np.concatenate([-sin, sin, pad], axis=1).astype(np.float32)
    return jnp.asarray(cos_t), jnp.asarray(sin_t)


def kernel(x, c, w_ada, b_ada, w_in, b_forget, g_q_norm, w_q_up, g_kv_norm, w_kv_up, w_branch_fox,
           w_branch_mla, w_out, ln1_g, ln1_b, w_mlp_up, w_mlp_down, ln2_g, ln2_b):
    n_batch, seq, d = x.shape
    assert w_ada.shape[0] == DEPTH
    m = n_batch * seq
    x2d = x.reshape(m, d)

    mod3 = _ada(c, w_ada[0], b_ada[0]).reshape(n_batch, N_ADA, d)

    small_col = 3 * FOX_WIDTH
    gf_col = small_col + FOX_HEADS + MLA_Q_LORA + MLA_KV_LORA + MLA_ROPE_DIM
    gm_col = gf_col + d
    assert w_in.shape[2] == gm_col + d and small_col == IN_SMALL_TILE * IN_TN and gf_col <= small_col + IN_TN
    proj, proj_lo = _inproj(x2d, mod3, w_in[0].T, seq)

    cos_t, sin_t = _rope_tables(seq)
    bf_pad = _pad_cols(b_forget[0].reshape(1, FOX_HEADS), LANES)
    qn, qr, kv, kr, cum, cumt = _prep(
        proj, proj_lo, bf_pad, g_q_norm[0].reshape(1, -1), g_kv_norm[0].reshape(1, -1),
        _layout_q_up(w_q_up[0]), w_kv_up[0].astype(BF16), cos_t, sin_t, seq)

    y_fox, w_up_b, w_down_b = _fox_attention(
        proj, cum, cumt.reshape(n_batch, FOX_HEADS // 2, 2, seq), seq, 0, (w_mlp_up[0], w_mlp_down[0]))
    y_mla, w_bf_b, w_bm_b, w_o_b = _mla_attention(
        qn, qr, kv, kr, seq, (w_branch_fox[0], w_branch_mla[0], w_out[0]))

    x1 = _mix(y_fox, y_mla, proj, gf_col, gm_col, x2d, mod3, w_bf_b, w_bm_b, w_o_b,
              ln1_g[0].reshape(1, d), ln1_b[0].reshape(1, d), seq)
    x2 = _mlp(x1, mod3, w_up_b, w_down_b, ln2_g[0].reshape(1, d), ln2_b[0].reshape(1, d), seq)
    return x2.reshape(n_batch, seq, d)
```

```python
import functools
import math

import jax
import jax.numpy as jnp
import numpy as np
from jax import lax
from jax.experimental import pallas as pl
from jax.experimental.pallas import tpu as pltpu

F32 = jnp.float32
BF16 = jnp.bfloat16

FOX_HEADS = 16
FOX_HEAD_DIM = 64
FOX_WIDTH = FOX_HEADS * FOX_HEAD_DIM
MLA_HEADS = 8
MLA_NOPE_DIM = 128
MLA_ROPE_DIM = 64
MLA_V_DIM = 128
MLA_Q_LORA = 512
MLA_KV_LORA = 256
MLA_QK_DIM = MLA_NOPE_DIM + MLA_ROPE_DIM
CHUNK = 64
ROPE_THETA = 10000.0
LN_EPS = 1e-5
RMS_EPS = 1e-6
N_ADA = 6
DEPTH = 1
ALPHA = (2.0 * DEPTH) ** 0.25

LANES = 128
LOG2E = math.log2(math.e)
FOX_QSCALE = FOX_HEAD_DIM ** -0.5 * LOG2E
MLA_QSCALE = MLA_QK_DIM ** -0.5 * LOG2E
NEG_BIG = -1e30
VMEM_LIMIT = 56 * 1024 * 1024

SUB = 128
HALF = 256


def _cparams(semantics):
    return pltpu.CompilerParams(dimension_semantics=semantics, vmem_limit_bytes=VMEM_LIMIT)


def _layer_norm(y, g, b):
    mu = jnp.mean(y, axis=-1, keepdims=True)
    yc = y - mu
    var = jnp.mean(yc * yc, axis=-1, keepdims=True)
    return yc * lax.rsqrt(var + LN_EPS) * g + b


def _rms_norm(c, g):
    return c * lax.rsqrt(jnp.mean(c * c, axis=-1, keepdims=True) + RMS_EPS) * g


def _ada_kernel(ct_ref, w_ref, b_ref, o_ref, *, n_batch, k_chunk):
    d = w_ref.shape[0]
    ct = ct_ref[...]
    s = ct * jax.nn.sigmoid(ct)
    accs = [b_ref[...] for _ in range(n_batch)]
    for kc in range(d // k_chunk):
        rows = slice(kc * k_chunk, (kc + 1) * k_chunk)
        w = w_ref[rows, :]
        for b in range(n_batch):
            accs[b] = accs[b] + jnp.sum(w * s[rows, b:b + 1], axis=0, keepdims=True)
    for b in range(n_batch):
        o_ref[b:b + 1, :] = accs[b]


def _ada(c, w_ada, b_ada):
    n_batch, d = c.shape
    n = w_ada.shape[1]
    tn = 1024
    return pl.pallas_call(
        functools.partial(_ada_kernel, n_batch=n_batch, k_chunk=256),
        grid=(n // tn,),
        in_specs=[
            pl.BlockSpec((d, n_batch), lambda j: (0, 0)),
            pl.BlockSpec((d, tn), lambda j: (0, j)),
            pl.BlockSpec((1, tn), lambda j: (0, j)),
        ],
        out_specs=pl.BlockSpec((n_batch, tn), lambda j: (0, j)),
        out_shape=jax.ShapeDtypeStruct((n_batch, n), F32),
        compiler_params=_cparams(("arbitrary",)),
        name="ada",
    )(c.T, w_ada, b_ada.reshape(1, n))


IN_TN = 1024
IN_SMALL_TILE = 3 * FOX_WIDTH // IN_TN


def _inproj_kernel(x_ref, mod_ref, w_ref, o_ref, lo_ref, u_ref, *, n_cols):
    j = pl.program_id(1)
    last = pl.num_programs(1) - 1

    def product(wt, u=None):
        u = u_ref[...] if u is None else u
        return lax.dot_general(u, wt.astype(BF16), (((1,), (1,)), ((), ())), preferred_element_type=F32)

    @pl.when(j == 0)
    def _():
        shift = mod_ref[0:1, :]
        scale = mod_ref[1:2, :]
        u = (x_ref[...] * (1.0 + scale) + shift).astype(BF16)
        u_ref[...] = u
        o_ref[...] = (product(w_ref[...], u) * FOX_QSCALE).astype(o_ref.dtype)

    @pl.when(j == IN_SMALL_TILE)
    def _():
        acc = product(w_ref[...])
        hi = acc.astype(o_ref.dtype)
        o_ref[...] = hi
        lo_ref[...] = (acc - hi.astype(F32)).astype(lo_ref.dtype)

    @pl.when(j == last)
    def _():
        row = lax.broadcasted_iota(jnp.int32, w_ref.shape, 0)
        wt = jnp.where(row < n_cols - last * IN_TN, w_ref[...], 0.0)
        o_ref[...] = product(wt).astype(o_ref.dtype)

    @pl.when((j != 0) & (j != IN_SMALL_TILE) & (j != last))
    def _():
        o_ref[...] = product(w_ref[...]).astype(o_ref.dtype)


def _inproj(x2d, mod3, w_in_t, seq):
    m, d = x2d.shape
    n_cols = w_in_t.shape[0]
    tm = 1024
    n_tiles = pl.cdiv(n_cols, IN_TN)
    tiles_per_batch = seq // tm
    return pl.pallas_call(
        functools.partial(_inproj_kernel, n_cols=n_cols),
        grid=(m // tm, n_tiles),
        in_specs=[
            pl.BlockSpec((tm, d), lambda i, j: (i, 0)),
            pl.BlockSpec((None, N_ADA, d), lambda i, j: (i // tiles_per_batch, 0, 0)),
            pl.BlockSpec((IN_TN, d), lambda i, j: (j, 0)),
        ],
        out_specs=[pl.BlockSpec((tm, IN_TN), lambda i, j: (i, j)),
                   pl.BlockSpec((tm, IN_TN), lambda i, j: (i, 0))],
        out_shape=[jax.ShapeDtypeStruct((m, n_tiles * IN_TN), BF16),
                   jax.ShapeDtypeStruct((m, IN_TN), BF16)],
        scratch_shapes=[pltpu.VMEM((tm, d), BF16)],
        compiler_params=_cparams(("parallel", "arbitrary")),
        name="inproj",
    )(x2d, mod3, w_in_t)


def _log_sigmoid(x):
    return jnp.minimum(x, 0.0) - jnp.log1p(jnp.exp(-jnp.abs(x)))


def _prep_kernel(hi_ref, lo_ref, bf_ref, gq_ref, gkv_ref, wq_ref, wkv_ref, cos_ref, sin_ref,
                 qn_ref, qr_ref, kv_ref, kr_ref, cumt_ref, carry_ref, *, tiles_per_batch):
    i = pl.program_id(0)
    tm = hi_ref.shape[0]
    ps = hi_ref[...].astype(F32) + lo_ref[...].astype(F32)
    c0 = FOX_HEADS
    c1 = c0 + MLA_Q_LORA
    c2 = c1 + MLA_KV_LORA
    c3 = c2 + MLA_ROPE_DIM
    half = MLA_ROPE_DIM // 2
    lane_pad = jnp.zeros((tm, LANES - MLA_ROPE_DIM), F32)
    k_rope = jnp.concatenate([ps[:, c2:c3], lane_pad], axis=1)
    k_rope_partner = jnp.concatenate([ps[:, c2 + half:c3], ps[:, c2:c2 + half], lane_pad], axis=1)
    cos = cos_ref[...]
    sin = sin_ref[...]

    logf = _log_sigmoid(ps[:, 0:LANES] + bf_ref[...]) * LOG2E
    hi = logf.astype(BF16)
    r1 = logf - hi.astype(F32)
    mid = r1.astype(BF16)
    lo = (r1 - mid.astype(F32)).astype(BF16)
    row = lax.broadcasted_iota(jnp.int32, (tm, tm), 0)
    col = lax.broadcasted_iota(jnp.int32, (tm, tm), 1)
    tri = (row >= col).astype(BF16)
    cs = (jnp.dot(tri, hi, preferred_element_type=F32)
          + jnp.dot(tri, mid, preferred_element_type=F32)
          + jnp.dot(tri, lo, preferred_element_type=F32))

    @pl.when(i % tiles_per_batch == 0)
    def _():
        carry_ref[...] = jnp.zeros_like(carry_ref)

    cum = cs + carry_ref[...]
    carry_ref[...] = cum[tm - 1:tm, :]
    cumt_ref[...] = cum.T[0:FOX_HEADS, :]

    cqn = _rms_norm(ps[:, c0:c1], gq_ref[...]).astype(BF16)
    qa = jnp.dot(cqn, wq_ref[...], preferred_element_type=F32)
    hw = MLA_HEADS * LANES
    qn_ref[...] = (qa[:, 0:hw] * MLA_QSCALE).astype(BF16)
    for h in range(MLA_HEADS):
        a = qa[:, hw + h * LANES: hw + (h + 1) * LANES]
        b = qa[:, 2 * hw + h * LANES: 2 * hw + (h + 1) * LANES]
        qr_ref[:, h * LANES:(h + 1) * LANES] = ((a * cos + b * sin) * MLA_QSCALE).astype(BF16)

    ckvn = _rms_norm(ps[:, c1:c2], gkv_ref[...]).astype(BF16)
    kv_ref[...] = jnp.dot(ckvn, wkv_ref[...], preferred_element_type=F32).astype(BF16)
    kr_ref[...] = (k_rope * cos + k_rope_partner * sin).astype(BF16)


def _prep(proj, proj_lo, bf_pad, g_q, g_kv, w_q_all, w_kv, cos_t, sin_t, seq):
    m = proj.shape[0]
    n_batch = m // seq
    tm = 512
    tiles_per_batch = seq // tm
    hw = MLA_HEADS * LANES
    const = lambda i: (0, 0)
    return pl.pallas_call(
        functools.partial(_prep_kernel, tiles_per_batch=tiles_per_batch),
        grid=(m // tm,),
        in_specs=[
            pl.BlockSpec((tm, IN_TN), lambda i: (i, IN_SMALL_TILE)),
            pl.BlockSpec((tm, IN_TN), lambda i: (i, 0)),
            pl.BlockSpec(bf_pad.shape, const),
            pl.BlockSpec(g_q.shape, const),
            pl.BlockSpec(g_kv.shape, const),
            pl.BlockSpec(w_q_all.shape, const),
            pl.BlockSpec(w_kv.shape, const),
            pl.BlockSpec((tm, LANES), lambda i: (i % tiles_per_batch, 0)),
            pl.BlockSpec((tm, LANES), lambda i: (i % tiles_per_batch, 0)),
        ],
        out_specs=[
            pl.BlockSpec((tm, hw), lambda i: (i, 0)),
            pl.BlockSpec((tm, hw), lambda i: (i, 0)),
            pl.BlockSpec((tm, w_kv.shape[1]), lambda i: (i, 0)),
            pl.BlockSpec((tm, LANES), lambda i: (i, 0)),
            pl.BlockSpec((None, FOX_HEADS, tm), lambda i: (i // tiles_per_batch, 0, i % tiles_per_batch)),
        ],
        out_shape=[
            jax.ShapeDtypeStruct((m, hw), BF16),
            jax.ShapeDtypeStruct((m, hw), BF16),
            jax.ShapeDtypeStruct((m, w_kv.shape[1]), BF16),
            jax.ShapeDtypeStruct((m, LANES), BF16),
            jax.ShapeDtypeStruct((n_batch, FOX_HEADS, seq), F32),
        ],
        scratch_shapes=[pltpu.VMEM((1, LANES), F32)],
        compiler_params=_cparams(("arbitrary",)),
        name="prep",
    )(proj, proj_lo, bf_pad, g_q, g_kv, w_q_all, w_kv, cos_t, sin_t)


def _split3(x):
    hi = x.astype(BF16).astype(F32)
    r = x - hi
    mid = r.astype(BF16).astype(F32)
    lo = (r - mid).astype(BF16).astype(F32)
    return hi, mid, lo


def _online_softmax_step(t, m, acc, vt_aug):
    m_new = jnp.maximum(m, jnp.max(t, axis=0, keepdims=True))
    alpha = jnp.exp2(m - m_new)
    p = jnp.exp2(t - m_new).astype(BF16)
    pv = jnp.dot(vt_aug, p, preferred_element_type=F32)
    return m_new, alpha * acc + pv[0:acc.shape[0], :]


def _causal_mask(t, key0, query0, granule):
    r = lax.broadcasted_iota(jnp.int32, t.shape, 0) + key0
    c = lax.broadcasted_iota(jnp.int32, t.shape, 1) + query0
    return jnp.where(r <= (c | (granule - 1)), t, NEG_BIG)


def _flash_key_major(qi, tile, chains, keys, rhs, vt_aug, d_v, granule):
    acc_rows = d_v + 8
    n_rounds = tile // SUB
    pairs = [(n, n + 1) for n in range(0, len(chains), 2)]

    def qk_of(k0, is_diag):
        loaded = {}

        def product(n):
            half, hh = chains[n]
            n_keys = (half + 1) * HALF if is_diag else tile
            if (hh, n_keys) not in loaded:
                loaded[hh, n_keys] = keys(hh, k0, n_keys)
            return jnp.dot(loaded[hh, n_keys], rhs(hh)[:, half * HALF:(half + 1) * HALF],
                           preferred_element_type=F32)
        return product

    def visit(state, tiles):
        st = list(state)
        products = [qk_of(*t) for t in tiles]
        phases = [(ti, pair) for ti in range(len(tiles)) for pair in pairs]

        def issue(phase):
            ti, pair = phase
            return {n: products[ti](n) for n in pair}

        cur = issue(phases[0])
        for idx, (ti, pair) in enumerate(phases):
            nxt = issue(phases[idx + 1]) if idx + 1 < len(phases) else None
            k0, is_diag = tiles[ti]
            for sb in range(n_rounds):
                ks = pl.multiple_of(k0 + sb * SUB, SUB)
                for n in pair:
                    half, hh = chains[n]
                    if is_diag and sb * SUB >= (half + 1) * HALF:
                        continue
                    t = cur[n][sb * SUB:(sb + 1) * SUB, :]
                    if is_diag and (sb + 1) * SUB > half * HALF:
                        t = _causal_mask(t, sb * SUB, half * HALF, granule)
                    st[n] = _online_softmax_step(t, *st[n], vt_aug(hh, ks))
            cur = nxt
        return tuple(st)

    def two_full(j, state):
        k0 = pl.multiple_of(2 * j * tile, tile)
        return visit(state, [(k0, False), (pl.multiple_of(k0 + tile, tile), False)])

    init = tuple((jnp.full((1, HALF), NEG_BIG, F32), jnp.zeros((acc_rows, HALF), F32)) for _ in chains)
    state = lax.fori_loop(0, qi // 2, two_full, init)
    q0 = pl.multiple_of(qi * tile, tile)
    state = lax.cond(
        qi % 2 == 1,
        lambda st: visit(st, [(pl.multiple_of(q0 - tile, tile), False), (q0, True)]),
        lambda st: visit(st, [(q0, True)]),
        state)
    return [acc[0:d_v, :] * (1.0 / acc[d_v:d_v + 1, :]) for (_, acc) in state]


def _cast_specs(weights, grid):
    n_steps = math.prod(grid)
    specs, shapes = [], []
    for w in weights:
        rows, cols = w.shape
        slab = rows // n_steps
        assert slab * n_steps == rows and slab % 16 == 0, (w.shape, n_steps)
        specs.append(pl.BlockSpec((slab, cols), lambda a, b: (a * grid[1] + b, 0)))
        shapes.append(jax.ShapeDtypeStruct(w.shape, BF16))
    return specs, shapes


def _cast_slabs(refs):
    n = len(refs) // 2
    for src, dst in zip(refs[:n], refs[n:]):
        dst[...] = src[...].astype(dst.dtype)


def _stack_chain_outputs(outs, n_half):
    return jnp.concatenate(
        [jnp.concatenate([outs[half * 2 + hh] for half in range(n_half)], axis=1) for hh in (0, 1)], axis=0)


AUG_ROWS = 16


def _bias_rows(n, entries):
    r = lax.broadcasted_iota(jnp.int32, (AUG_ROWS, n), 0)
    out = jnp.zeros((AUG_ROWS, n), F32)
    for row, val in entries.items():
        out = jnp.where(r == row, val, out)
    return out.astype(BF16)


def _fox_kernel(q_ref, k_ref, v_ref, cumt_ref, *rest, tq, n_cast):
    cast_in, (o_ref, *cast_out), (vta_ref, kaug_ref) = rest[:n_cast], rest[n_cast:2 * n_cast + 1], rest[2 * n_cast + 1:]
    seq = k_ref.shape[0]
    dh = FOX_HEAD_DIM
    n_half = tq // HALF
    aug = 6

    vt = v_ref[...].T
    tail_rows = lax.broadcasted_iota(jnp.int32, (16, seq), 0)
    tail = jnp.where(tail_rows < 8, 1.0, 0.0).astype(BF16)
    for hh in (0, 1):
        vta_ref[hh, 0:dh, :] = vt[hh * dh:(hh + 1) * dh, :]
        vta_ref[hh, dh:dh + 16, :] = tail
    key_rows = {}
    for hh in (0, 1):
        for idx, piece in enumerate(_split3(-cumt_ref[hh:hh + 1, :])):
            key_rows[aug * hh + idx] = piece
            key_rows[aug * hh + 3 + idx] = 1.0
    kaug_t = jnp.concatenate([_bias_rows(seq, key_rows), jnp.zeros((LANES - AUG_ROWS, seq), BF16)], axis=0)
    kaug_ref[...] = kaug_t.T

    chains = [(half, hh) for half in range(n_half) for hh in (0, 1)]

    def keys(hh, k0, n):
        return jnp.concatenate([k_ref[pl.ds(k0, n), :], kaug_ref[pl.ds(k0, n), :]], axis=1)

    def query_tile(qi, carry):
        q0 = pl.multiple_of(qi * tq, tq)
        qt = q_ref[pl.ds(q0, tq), :].T
        row = lax.broadcasted_iota(jnp.int32, (LANES, tq), 0)
        pad = jnp.zeros((LANES - AUG_ROWS, tq), BF16)
        rhs = []
        for hh in (0, 1):
            qm = jnp.where((row >= hh * dh) & (row < (hh + 1) * dh), qt, jnp.zeros_like(qt))
            query_rows = {}
            for idx, piece in enumerate(_split3(cumt_ref[hh:hh + 1, pl.ds(q0, tq)])):
                query_rows[aug * hh + idx] = 1.0
                query_rows[aug * hh + 3 + idx] = piece
            rhs.append(jnp.concatenate([qm, _bias_rows(tq, query_rows), pad], axis=0))
        outs = _flash_key_major(qi, tq, chains, keys, lambda hh: rhs[hh],
                                lambda hh, ks: vta_ref[hh, :, pl.ds(ks, SUB)], dh, granule=1)
        o_ref[pl.ds(q0, tq), :] = _stack_chain_outputs(outs, n_half).T.astype(o_ref.dtype)
        return carry

    lax.fori_loop(0, seq // tq, query_tile, 0)
    _cast_slabs(cast_in + tuple(cast_out))


def _fox_attention(proj, cumt4, seq, q_col0, cast_weights):
    m = proj.shape[0]
    n_batch = m // seq
    n_pairs = FOX_HEADS // 2
    qb = q_col0 // LANES
    grid = (n_batch, n_pairs)
    cast_specs, cast_shapes = _cast_specs(cast_weights, grid)
    return pl.pallas_call(
        functools.partial(_fox_kernel, tq=512, n_cast=len(cast_weights)),
        grid=grid,
        in_specs=[
            pl.BlockSpec((seq, LANES), lambda b, hp: (b, qb + hp)),
            pl.BlockSpec((seq, LANES), lambda b, hp: (b, qb + n_pairs + hp)),
            pl.BlockSpec((seq, LANES), lambda b, hp: (b, qb + 2 * n_pairs + hp)),
            pl.BlockSpec((None, None, 2, seq), lambda b, hp: (b, hp, 0, 0)),
        ] + cast_specs,
        out_specs=[pl.BlockSpec((seq, LANES), lambda b, hp: (b, hp))] + cast_specs,
        out_shape=[jax.ShapeDtypeStruct((m, FOX_WIDTH), BF16)] + cast_shapes,
        scratch_shapes=[
            pltpu.VMEM((2, FOX_HEAD_DIM + 16, seq), BF16),
            pltpu.VMEM((seq, LANES), BF16),
        ],
        compiler_params=_cparams(("parallel", "parallel")),
        name="fox_attn",
    )(proj, proj, proj, cumt4, *cast_weights)


def _mla_kernel(qn_ref, qr_ref, kv_ref, kr_ref, *rest, tq, n_cast):
    cast_in, (o_ref, *cast_out), (vta_ref,) = rest[:n_cast], rest[n_cast:2 * n_cast + 1], rest[2 * n_cast + 1:]
    seq = kv_ref.shape[0]
    dv = MLA_V_DIM
    n_half = tq // HALF

    tail_rows = lax.broadcasted_iota(jnp.int32, (16, seq), 0)
    tail = jnp.where(tail_rows < 8, 1.0, 0.0).astype(BF16)
    for hh in (0, 1):
        vta_ref[hh, 0:dv, :] = kv_ref[:, (2 * hh + 1) * LANES:(2 * hh + 2) * LANES].T
        vta_ref[hh, dv:dv + 16, :] = tail

    chains = [(half, hh) for half in range(n_half) for hh in (0, 1)]

    def keys(hh, k0, n):
        return jnp.concatenate([kv_ref[pl.ds(k0, n), 2 * hh * LANES:(2 * hh + 1) * LANES],
                                kr_ref[pl.ds(k0, n), :]], axis=1)

    def query_tile(qi, carry):
        q0 = pl.multiple_of(qi * tq, tq)
        rhs = [jnp.concatenate([qn_ref[pl.ds(q0, tq), hh * LANES:(hh + 1) * LANES],
                                qr_ref[pl.ds(q0, tq), hh * LANES:(hh + 1) * LANES]], axis=1).T
               for hh in (0, 1)]
        outs = _flash_key_major(qi, tq, chains, keys, lambda hh: rhs[hh],
                                lambda hh, ks: vta_ref[hh, :, pl.ds(ks, SUB)], dv, granule=CHUNK)
        o_ref[pl.ds(q0, tq), :] = _stack_chain_outputs(outs, n_half).T.astype(o_ref.dtype)
        return carry

    lax.fori_loop(0, seq // tq, query_tile, 0)
    _cast_slabs(cast_in + tuple(cast_out))


def _mla_attention(qn, qr, kv, kr, seq, cast_weights):
    m = qn.shape[0]
    n_batch = m // seq
    pair = 2 * LANES
    grid = (n_batch, MLA_HEADS // 2)
    cast_specs, cast_shapes = _cast_specs(cast_weights, grid)
    return pl.pallas_call(
        functools.partial(_mla_kernel, tq=512, n_cast=len(cast_weights)),
        grid=grid,
        in_specs=[
            pl.BlockSpec((seq, pair), lambda b, hp: (b, hp)),
            pl.BlockSpec((seq, pair), lambda b, hp: (b, hp)),
            pl.BlockSpec((seq, 2 * pair), lambda b, hp: (b, hp)),
            pl.BlockSpec((seq, LANES), lambda b, hp: (b, 0)),
        ] + cast_specs,
        out_specs=[pl.BlockSpec((seq, pair), lambda b, hp: (b, hp))] + cast_specs,
        out_shape=[jax.ShapeDtypeStruct((m, MLA_HEADS * MLA_V_DIM), BF16)] + cast_shapes,
        scratch_shapes=[pltpu.VMEM((2, MLA_V_DIM + 16, seq), BF16)],
        compiler_params=_cparams(("parallel", "parallel")),
        name="mla_attn",
    )(qn, qr, kv, kr, *cast_weights)


def _mix_kernel(yf_ref, ym_ref, gf_ref, gm_ref, x_ref, mod_ref, wbf_ref, wbm_ref, wo_ref, g_ref, b_ref,
                o_ref, mg_ref, *, n_chunk, g_off):
    d = o_ref.shape[1]
    yf = yf_ref[...]
    ym = ym_ref[...]

    def gate(ref, c):
        window = ref[:, c * n_chunk:(c + 1) * n_chunk + LANES]
        return jax.nn.sigmoid(window[:, g_off:g_off + n_chunk].astype(F32))

    for c in range(d // n_chunk):
        cols = slice(c * n_chunk, (c + 1) * n_chunk)
        a = jnp.dot(yf, wbf_ref[:, cols], preferred_element_type=F32)
        b = jnp.dot(ym, wbm_ref[:, cols], preferred_element_type=F32)
        merged = gate(gf_ref, c) * a + gate(gm_ref, c) * b
        mg_ref[:, cols] = merged.astype(BF16)
    mix = jnp.dot(mg_ref[...], wo_ref[...], preferred_element_type=F32)
    gate1 = mod_ref[2:3, :]
    o_ref[...] = _layer_norm(ALPHA * x_ref[...] + gate1 * mix, g_ref[...], b_ref[...])


def _mix(y_fox, y_mla, proj, gf_col, gm_col, x2d, mod3, w_bf, w_bm, w_o, ln_g, ln_b, seq):
    m, d = x2d.shape
    tm = 512
    tiles_per_batch = seq // tm
    const = lambda i: (0, 0)
    resident = dict(pipeline_mode=pl.Buffered(1))
    g_off = gf_col % LANES
    assert gm_col % LANES == g_off
    g_window = (pl.Element(tm), pl.Element(d + LANES))

    def window_at(col):
        return lambda i: (pl.multiple_of(i * tm, tm), col - g_off)

    return pl.pallas_call(
        functools.partial(_mix_kernel, n_chunk=512, g_off=g_off),
        grid=(m // tm,),
        in_specs=[
            pl.BlockSpec((tm, y_fox.shape[1]), lambda i: (i, 0)),
            pl.BlockSpec((tm, y_mla.shape[1]), lambda i: (i, 0)),
            pl.BlockSpec(g_window, window_at(gf_col)),
            pl.BlockSpec(g_window, window_at(gm_col)),
            pl.BlockSpec((tm, d), lambda i: (i, 0)),
            pl.BlockSpec((None, N_ADA, d), lambda i: (i // tiles_per_batch, 0, 0)),
            pl.BlockSpec(w_bf.shape, const, **resident),
            pl.BlockSpec(w_bm.shape, const, **resident),
            pl.BlockSpec(w_o.shape, const, **resident),
            pl.BlockSpec((1, d), const),
            pl.BlockSpec((1, d), const),
        ],
        out_specs=pl.BlockSpec((tm, d), lambda i: (i, 0)),
        out_shape=jax.ShapeDtypeStruct((m, d), F32),
        scratch_shapes=[pltpu.VMEM((tm, d), BF16)],
        compiler_params=_cparams(("parallel",)),
        name="mix",
    )(y_fox, y_mla, proj, proj, x2d, mod3, w_bf, w_bm, w_o, ln_g, ln_b)


def _mlp_kernel(x_ref, mod_ref, wu_ref, wd_ref, g_ref, b_ref, o_ref, u_ref):
    f = pl.program_id(1)

    def partial_sum(u):
        z = jnp.dot(u, wu_ref[...], preferred_element_type=F32)
        h = jnp.square(jnp.maximum(z, 0.0)).astype(BF16)
        return jnp.dot(h, wd_ref[...], preferred_element_type=F32)

    @pl.when(f == 0)
    def _():
        shift = mod_ref[3:4, :]
        scale = mod_ref[4:5, :]
        u = (x_ref[...] * (1.0 + scale) + shift).astype(BF16)
        u_ref[...] = u
        o_ref[...] = partial_sum(u)

    @pl.when(f != 0)
    def _():
        o_ref[...] += partial_sum(u_ref[...])

    @pl.when(f == pl.num_programs(1) - 1)
    def _():
        gate = mod_ref[5:6, :]
        o_ref[...] = _layer_norm(ALPHA * x_ref[...] + gate * o_ref[...], g_ref[...], b_ref[...])


def _mlp(x1, mod3, w_up, w_down, ln_g, ln_b, seq):
    m, d = x1.shape
    d_ff = w_up.shape[1]
    tm, tf = 1024, 512
    tiles_per_batch = seq // tm
    return pl.pallas_call(
        _mlp_kernel,
        grid=(m // tm, d_ff // tf),
        in_specs=[
            pl.BlockSpec((tm, d), lambda i, f: (i, 0)),
            pl.BlockSpec((None, N_ADA, d), lambda i, f: (i // tiles_per_batch, 0, 0)),
            pl.BlockSpec((d, tf), lambda i, f: (0, f)),
            pl.BlockSpec((tf, d), lambda i, f: (f, 0)),
            pl.BlockSpec((1, d), lambda i, f: (0, 0)),
            pl.BlockSpec((1, d), lambda i, f: (0, 0)),
        ],
        out_specs=pl.BlockSpec((tm, d), lambda i, f: (i, 0)),
        out_shape=jax.ShapeDtypeStruct((m, d), F32),
        scratch_shapes=[pltpu.VMEM((tm, d), BF16)],
        compiler_params=_cparams(("parallel", "arbitrary")),
        name="mlp",
    )(x1, mod3, w_up, w_down, ln_g, ln_b)


def _pad_cols(w, width):
    return jnp.pad(w, ((0, 0), (0, width - w.shape[1])))


def _swap_halves(w):
    half = w.shape[-1] // 2
    return jnp.concatenate([w[..., half:], w[..., :half]], axis=-1)


def _layout_q_up(w_q_up):
    r = w_q_up.shape[0]
    w3 = w_q_up.reshape(r, MLA_HEADS, MLA_QK_DIM)
    w_nope = w3[:, :, :MLA_NOPE_DIM].reshape(r, MLA_HEADS * MLA_NOPE_DIM)
    w_rope = w3[:, :, MLA_NOPE_DIM:]
    pad = ((0, 0), (0, 0), (0, LANES - MLA_ROPE_DIM))
    w_r = jnp.pad(w_rope, pad).reshape(r, MLA_HEADS * LANES)
    w_rs = jnp.pad(_swap_halves(w_rope), pad).reshape(r, MLA_HEADS * LANES)
    return jnp.concatenate([w_nope, w_r, w_rs], axis=1).astype(BF16)


def _rope_tables(seq):
    pos = np.arange(seq, dtype=np.float64)
    inv_freq = ROPE_THETA ** (-np.arange(0, MLA_ROPE_DIM, 2, dtype=np.float64) / MLA_ROPE_DIM)
    ang = pos[:, None] * inv_freq[None, :]
    cos, sin = np.cos(ang), np.sin(ang)
    pad = np.zeros((seq, LANES - MLA_ROPE_DIM))
    cos_t = np.concatenate([cos, cos, pad], axis=1).astype(np.float32)
    sin_t = np.concatenate([-sin, sin, pad], axis=1).astype(np.float32)
    return jnp.asarray(cos_t), jnp.asarray(sin_t)


def kernel(x, c, w_ada, b_ada, w_in, b_forget, g_q_norm, w_q_up, g_kv_norm, w_kv_up, w_branch_fox,
           w_branch_mla, w_out, ln1_g, ln1_b, w_mlp_up, w_mlp_down, ln2_g, ln2_b):
    n_batch, seq, d = x.shape
    assert w_ada.shape[0] == DEPTH
    m = n_batch * seq
    x2d = x.reshape(m, d)

    mod3 = _ada(c, w_ada[0], b_ada[0]).reshape(n_batch, N_ADA, d)

    small_col = 3 * FOX_WIDTH
    gf_col = small_col + FOX_HEADS + MLA_Q_LORA + MLA_KV_LORA + MLA_ROPE_DIM
    gm_col = gf_col + d
    assert w_in.shape[2] == gm_col + d and small_col == IN_SMALL_TILE * IN_TN and gf_col <= small_col + IN_TN
    proj, proj_lo = _inproj(x2d, mod3, w_in[0].T, seq)

    cos_t, sin_t = _rope_tables(seq)
    bf_pad = _pad_cols(b_forget[0].reshape(1, FOX_HEADS), LANES)
    qn, qr, kv, kr, cumt = _prep(
        proj, proj_lo, bf_pad, g_q_norm[0].reshape(1, -1), g_kv_norm[0].reshape(1, -1),
        _layout_q_up(w_q_up[0]), w_kv_up[0].astype(BF16), cos_t, sin_t, seq)

    y_fox, w_up_b, w_down_b = _fox_attention(
        proj, cumt.reshape(n_batch, FOX_HEADS // 2, 2, seq), seq, 0, (w_mlp_up[0], w_mlp_down[0]))
    y_mla, w_bf_b, w_bm_b, w_o_b = _mla_attention(
        qn, qr, kv, kr, seq, (w_branch_fox[0], w_branch_mla[0], w_out[0]))

    x1 = _mix(y_fox, y_mla, proj, gf_col, gm_col, x2d, mod3, w_bf_b, w_bm_b, w_o_b,
              ln1_g[0].reshape(1, d), ln1_b[0].reshape(1, d), seq)
    x2 = _mlp(x1, mod3, w_up_b, w_down_b, ln2_g[0].reshape(1, d), ln2_b[0].reshape(1, d), seq)
    return x2.reshape(n_batch, seq, d)
```

```python
import functools
import math

import jax
import jax.numpy as jnp
import numpy as np
from jax import lax
from jax.experimental import pallas as pl
from jax.experimental.pallas import tpu as pltpu

F32 = jnp.float32
BF16 = jnp.bfloat16

FOX_HEADS = 16
FOX_HEAD_DIM = 64
FOX_WIDTH = FOX_HEADS * FOX_HEAD_DIM
MLA_HEADS = 8
MLA_NOPE_DIM = 128
MLA_ROPE_DIM = 64
MLA_V_DIM = 128
MLA_Q_LORA = 512
MLA_KV_LORA = 256
MLA_QK_DIM = MLA_NOPE_DIM + MLA_ROPE_DIM
CHUNK = 64
ROPE_THETA = 10000.0
LN_EPS = 1e-5
RMS_EPS = 1e-6
N_ADA = 6
DEPTH = 1
ALPHA = (2.0 * DEPTH) ** 0.25

LANES = 128
LOG2E = math.log2(math.e)
FOX_QSCALE = FOX_HEAD_DIM ** -0.5 * LOG2E
MLA_QSCALE = MLA_QK_DIM ** -0.5 * LOG2E
NEG_BIG = -1e30
VMEM_LIMIT = 56 * 1024 * 1024

SUB = 128
HALF = 256


def _cparams(semantics):
    return pltpu.CompilerParams(dimension_semantics=semantics, vmem_limit_bytes=VMEM_LIMIT)


def _layer_norm(y, g, b):
    mu = jnp.mean(y, axis=-1, keepdims=True)
    yc = y - mu
    var = jnp.mean(yc * yc, axis=-1, keepdims=True)
    return yc * lax.rsqrt(var + LN_EPS) * g + b


def _rms_norm(c, g):
    return c * lax.rsqrt(jnp.mean(c * c, axis=-1, keepdims=True) + RMS_EPS) * g


def _ada_kernel(ct_ref, w_ref, b_ref, o_ref, *, n_batch, k_chunk):
    d = w_ref.shape[0]
    ct = ct_ref[...]
    s = ct * jax.nn.sigmoid(ct)
    accs = [b_ref[...] for _ in range(n_batch)]
    for kc in range(d // k_chunk):
        rows = slice(kc * k_chunk, (kc + 1) * k_chunk)
        w = w_ref[rows, :]
        for b in range(n_batch):
            accs[b] = accs[b] + jnp.sum(w * s[rows, b:b + 1], axis=0, keepdims=True)
    for b in range(n_batch):
        o_ref[b:b + 1, :] = accs[b]


def _ada(c, w_ada, b_ada):
    n_batch, d = c.shape
    n = w_ada.shape[1]
    tn = 1024
    return pl.pallas_call(
        functools.partial(_ada_kernel, n_batch=n_batch, k_chunk=256),
        grid=(n // tn,),
        in_specs=[
            pl.BlockSpec((d, n_batch), lambda j: (0, 0)),
            pl.BlockSpec((d, tn), lambda j: (0, j)),
            pl.BlockSpec((1, tn), lambda j: (0, j)),
        ],
        out_specs=pl.BlockSpec((n_batch, tn), lambda j: (0, j)),
        out_shape=jax.ShapeDtypeStruct((n_batch, n), F32),
        compiler_params=_cparams(("arbitrary",)),
        name="ada",
    )(c.T, w_ada, b_ada.reshape(1, n))


IN_TN = 1024
IN_SMALL_TILE = 3 * FOX_WIDTH // IN_TN


def _inproj_kernel(x_ref, mod_ref, w_ref, o_ref, lo_ref, u_ref, *, n_cols):
    j = pl.program_id(1)
    last = pl.num_programs(1) - 1

    def product(wt, u=None):
        u = u_ref[...] if u is None else u
        return lax.dot_general(u, wt.astype(BF16), (((1,), (1,)), ((), ())), preferred_element_type=F32)

    @pl.when(j == 0)
    def _():
        shift = mod_ref[0:1, :]
        scale = mod_ref[1:2, :]
        u = (x_ref[...] * (1.0 + scale) + shift).astype(BF16)
        u_ref[...] = u
        o_ref[...] = (product(w_ref[...], u) * FOX_QSCALE).astype(o_ref.dtype)

    @pl.when(j == IN_SMALL_TILE)
    def _():
        acc = product(w_ref[...])
        hi = acc.astype(o_ref.dtype)
        o_ref[...] = hi
        lo_ref[...] = (acc - hi.astype(F32)).astype(lo_ref.dtype)

    @pl.when(j == last)
    def _():
        row = lax.broadcasted_iota(jnp.int32, w_ref.shape, 0)
        wt = jnp.where(row < n_cols - last * IN_TN, w_ref[...], 0.0)
        o_ref[...] = product(wt).astype(o_ref.dtype)

    @pl.when((j != 0) & (j != IN_SMALL_TILE) & (j != last))
    def _():
        o_ref[...] = product(w_ref[...]).astype(o_ref.dtype)


def _inproj(x2d, mod3, w_in_t, seq):
    m, d = x2d.shape
    n_cols = w_in_t.shape[0]
    tm = 1024
    n_tiles = pl.cdiv(n_cols, IN_TN)
    tiles_per_batch = seq // tm
    return pl.pallas_call(
        functools.partial(_inproj_kernel, n_cols=n_cols),
        grid=(m // tm, n_tiles),
        in_specs=[
            pl.BlockSpec((tm, d), lambda i, j: (i, 0)),
            pl.BlockSpec((None, N_ADA, d), lambda i, j: (i // tiles_per_batch, 0, 0)),
            pl.BlockSpec((IN_TN, d), lambda i, j: (j, 0)),
        ],
        out_specs=[pl.BlockSpec((tm, IN_TN), lambda i, j: (i, j)),
                   pl.BlockSpec((tm, IN_TN), lambda i, j: (i, 0))],
        out_shape=[jax.ShapeDtypeStruct((m, n_tiles * IN_TN), BF16),
                   jax.ShapeDtypeStruct((m, IN_TN), BF16)],
        scratch_shapes=[pltpu.VMEM((tm, d), BF16)],
        compiler_params=_cparams(("parallel", "arbitrary")),
        name="inproj",
    )(x2d, mod3, w_in_t)


def _log_sigmoid(x):
    return jnp.minimum(x, 0.0) - jnp.log1p(jnp.exp(-jnp.abs(x)))


def _prep_kernel(hi_ref, lo_ref, bf_ref, gq_ref, gkv_ref, wq_ref, wkv_ref, cos_ref, sin_ref,
                 qn_ref, qr_ref, kv_ref, kr_ref, cumt_ref, carry_ref, *, tiles_per_batch):
    i = pl.program_id(0)
    tm = hi_ref.shape[0]
    ps = hi_ref[...].astype(F32) + lo_ref[...].astype(F32)
    c0 = FOX_HEADS
    c1 = c0 + MLA_Q_LORA
    c2 = c1 + MLA_KV_LORA
    c3 = c2 + MLA_ROPE_DIM
    half = MLA_ROPE_DIM // 2
    lane_pad = jnp.zeros((tm, LANES - MLA_ROPE_DIM), F32)
    k_rope = jnp.concatenate([ps[:, c2:c3], lane_pad], axis=1)
    k_rope_partner = jnp.concatenate([ps[:, c2 + half:c3], ps[:, c2:c2 + half], lane_pad], axis=1)
    cos = cos_ref[...]
    sin = sin_ref[...]

    logf = _log_sigmoid(ps[:, 0:LANES] + bf_ref[...]) * LOG2E
    hi = logf.astype(BF16)
    r1 = logf - hi.astype(F32)
    mid = r1.astype(BF16)
    lo = (r1 - mid.astype(F32)).astype(BF16)
    row = lax.broadcasted_iota(jnp.int32, (tm, tm), 0)
    col = lax.broadcasted_iota(jnp.int32, (tm, tm), 1)
    tri = (row >= col).astype(BF16)
    cs = (jnp.dot(tri, hi, preferred_element_type=F32)
          + jnp.dot(tri, mid, preferred_element_type=F32)
          + jnp.dot(tri, lo, preferred_element_type=F32))

    @pl.when(i % tiles_per_batch == 0)
    def _():
        carry_ref[...] = jnp.zeros_like(carry_ref)

    cum = cs + carry_ref[...]
    carry_ref[...] = cum[tm - 1:tm, :]
    cumt_ref[...] = cum.T[0:FOX_HEADS, :]

    cqn = _rms_norm(ps[:, c0:c1], gq_ref[...]).astype(BF16)
    qa = jnp.dot(cqn, wq_ref[...], preferred_element_type=F32)
    hw = MLA_HEADS * LANES
    qn_ref[...] = (qa[:, 0:hw] * MLA_QSCALE).astype(BF16)
    for h in range(MLA_HEADS):
        a = qa[:, hw + h * LANES: hw + (h + 1) * LANES]
        b = qa[:, 2 * hw + h * LANES: 2 * hw + (h + 1) * LANES]
        qr_ref[:, h * LANES:(h + 1) * LANES] = ((a * cos + b * sin) * MLA_QSCALE).astype(BF16)

    ckvn = _rms_norm(ps[:, c1:c2], gkv_ref[...]).astype(BF16)
    kv_ref[...] = jnp.dot(ckvn, wkv_ref[...], preferred_element_type=F32).astype(BF16)
    kr_ref[...] = (k_rope * cos + k_rope_partner * sin).astype(BF16)


def _prep(proj, proj_lo, bf_pad, g_q, g_kv, w_q_all, w_kv, cos_t, sin_t, seq):
    m = proj.shape[0]
    n_batch = m // seq
    tm = 512
    tiles_per_batch = seq // tm
    hw = MLA_HEADS * LANES
    const = lambda i: (0, 0)
    return pl.pallas_call(
        functools.partial(_prep_kernel, tiles_per_batch=tiles_per_batch),
        grid=(m // tm,),
        in_specs=[
            pl.BlockSpec((tm, IN_TN), lambda i: (i, IN_SMALL_TILE)),
            pl.BlockSpec((tm, IN_TN), lambda i: (i, 0)),
            pl.BlockSpec(bf_pad.shape, const),
            pl.BlockSpec(g_q.shape, const),
            pl.BlockSpec(g_kv.shape, const),
            pl.BlockSpec(w_q_all.shape, const),
            pl.BlockSpec(w_kv.shape, const),
            pl.BlockSpec((tm, LANES), lambda i: (i % tiles_per_batch, 0)),
            pl.BlockSpec((tm, LANES), lambda i: (i % tiles_per_batch, 0)),
        ],
        out_specs=[
            pl.BlockSpec((tm, hw), lambda i: (i, 0)),
            pl.BlockSpec((tm, hw), lambda i: (i, 0)),
            pl.BlockSpec((tm, w_kv.shape[1]), lambda i: (i, 0)),
            pl.BlockSpec((tm, LANES), lambda i: (i, 0)),
            pl.BlockSpec((None, FOX_HEADS, tm), lambda i: (i // tiles_per_batch, 0, i % tiles_per_batch)),
        ],
        out_shape=[
            jax.ShapeDtypeStruct((m, hw), BF16),
            jax.ShapeDtypeStruct((m, hw), BF16),
            jax.ShapeDtypeStruct((m, w_kv.shape[1]), BF16),
            jax.ShapeDtypeStruct((m, LANES), BF16),
            jax.ShapeDtypeStruct((n_batch, FOX_HEADS, seq), F32),
        ],
        scratch_shapes=[pltpu.VMEM((1, LANES), F32)],
        compiler_params=_cparams(("arbitrary",)),
        name="prep",
    )(proj, proj_lo, bf_pad, g_q, g_kv, w_q_all, w_kv, cos_t, sin_t)


def _split3(x):
    hi = x.astype(BF16).astype(F32)
    r = x - hi
    mid = r.astype(BF16).astype(F32)
    lo = (r - mid).astype(BF16).astype(F32)
    return hi, mid, lo


def _online_softmax_step(t, m, acc, vt_aug):
    m_new = jnp.maximum(m, jnp.max(t, axis=0, keepdims=True))
    alpha = jnp.exp2(m - m_new)
    p = jnp.exp2(t - m_new).astype(BF16)
    pv = jnp.dot(vt_aug, p, preferred_element_type=F32)
    return m_new, alpha * acc + pv[0:acc.shape[0], :]


def _causal_mask(t, key0, query0, granule):
    r = lax.broadcasted_iota(jnp.int32, t.shape, 0) + key0
    c = lax.broadcasted_iota(jnp.int32, t.shape, 1) + query0
    return jnp.where(r <= (c | (granule - 1)), t, NEG_BIG)


def _flash_key_major(qi, tile, chains, keys, rhs, vt_aug, d_v, granule, finish):
    acc_rows = d_v + 8
    n_rounds = tile // SUB
    pairs = [(n, n + 1) for n in range(0, len(chains), 2)]

    def qk_of(k0, is_diag):
        loaded = {}

        def product(n):
            half, hh = chains[n]
            n_keys = (half + 1) * HALF if is_diag else tile
            if (hh, n_keys) not in loaded:
                loaded[hh, n_keys] = keys(hh, k0, n_keys)
            return jnp.dot(loaded[hh, n_keys], rhs(hh)[:, half * HALF:(half + 1) * HALF],
                           preferred_element_type=F32)
        return product

    def visit(state, tiles):
        st = list(state)
        products = [qk_of(*t) for t in tiles]
        phases = [(ti, pair) for ti in range(len(tiles)) for pair in pairs]

        def issue(phase):
            ti, pair = phase
            return {n: products[ti](n) for n in pair}

        cur = issue(phases[0])
        for idx, (ti, pair) in enumerate(phases):
            nxt = issue(phases[idx + 1]) if idx + 1 < len(phases) else None
            k0, is_diag = tiles[ti]
            for sb in range(n_rounds):
                ks = pl.multiple_of(k0 + sb * SUB, SUB)
                for n in pair:
                    half, hh = chains[n]
                    if is_diag and sb * SUB >= (half + 1) * HALF:
                        continue
                    t = cur[n][sb * SUB:(sb + 1) * SUB, :]
                    if is_diag and (sb + 1) * SUB > half * HALF:
                        t = _causal_mask(t, sb * SUB, half * HALF, granule)
                    st[n] = _online_softmax_step(t, *st[n], vt_aug(hh, ks))
            cur = nxt
        return tuple(st)

    def two_full(j, state):
        k0 = pl.multiple_of(2 * j * tile, tile)
        return visit(state, [(k0, False), (pl.multiple_of(k0 + tile, tile), False)])

    init = tuple((jnp.full((1, HALF), NEG_BIG, F32), jnp.zeros((acc_rows, HALF), F32)) for _ in chains)
    state = lax.fori_loop(0, qi // 2, two_full, init)
    q0 = pl.multiple_of(qi * tile, tile)
    def last_tiles(tiles):
        def branch(st):
            outs = [acc[0:d_v, :] * (1.0 / acc[d_v:d_v + 1, :]) for (_, acc) in visit(st, tiles)]
            return finish(outs)
        return branch

    return lax.cond(qi % 2 == 1,
                    last_tiles([(pl.multiple_of(q0 - tile, tile), False), (q0, True)]),
                    last_tiles([(q0, True)]),
                    state)


def _cast_specs(weights, grid):
    n_steps = math.prod(grid)
    specs, shapes = [], []
    for w in weights:
        rows, cols = w.shape
        slab = rows // n_steps
        assert slab * n_steps == rows and slab % 16 == 0, (w.shape, n_steps)
        specs.append(pl.BlockSpec((slab, cols), lambda a, b: (a * grid[1] + b, 0)))
        shapes.append(jax.ShapeDtypeStruct(w.shape, BF16))
    return specs, shapes


def _cast_slabs(refs):
    n = len(refs) // 2
    for src, dst in zip(refs[:n], refs[n:]):
        dst[...] = src[...].astype(dst.dtype)


def _stack_chain_outputs(outs, n_half):
    return jnp.concatenate(
        [jnp.concatenate([outs[half * 2 + hh] for half in range(n_half)], axis=1) for hh in (0, 1)], axis=0)


AUG_ROWS = 16


def _bias_rows(n, entries):
    r = lax.broadcasted_iota(jnp.int32, (AUG_ROWS, n), 0)
    out = jnp.zeros((AUG_ROWS, n), F32)
    for row, val in entries.items():
        out = jnp.where(r == row, val, out)
    return out.astype(BF16)


def _fox_kernel(q_ref, k_ref, v_ref, cumt_ref, *rest, tq, n_cast):
    cast_in, (o_ref, *cast_out), (vta_ref, kaug_ref) = rest[:n_cast], rest[n_cast:2 * n_cast + 1], rest[2 * n_cast + 1:]
    seq = k_ref.shape[0]
    dh = FOX_HEAD_DIM
    n_half = tq // HALF
    aug = 6

    vt = v_ref[...].T
    tail_rows = lax.broadcasted_iota(jnp.int32, (16, seq), 0)
    tail = jnp.where(tail_rows < 8, 1.0, 0.0).astype(BF16)
    for hh in (0, 1):
        vta_ref[hh, 0:dh, :] = vt[hh * dh:(hh + 1) * dh, :]
        vta_ref[hh, dh:dh + 16, :] = tail
    key_rows = {}
    for hh in (0, 1):
        for idx, piece in enumerate(_split3(-cumt_ref[hh:hh + 1, :])):
            key_rows[aug * hh + idx] = piece
            key_rows[aug * hh + 3 + idx] = 1.0
    kaug_t = jnp.concatenate([_bias_rows(seq, key_rows), jnp.zeros((LANES - AUG_ROWS, seq), BF16)], axis=0)
    kaug_ref[...] = kaug_t.T

    chains = [(half, hh) for half in range(n_half) for hh in (0, 1)]

    def keys(hh, k0, n):
        return jnp.concatenate([k_ref[pl.ds(k0, n), :], kaug_ref[pl.ds(k0, n), :]], axis=1)

    def make_rhs(qi):
        q0 = pl.multiple_of(qi * tq, tq)
        qt = q_ref[pl.ds(q0, tq), :].T
        row = lax.broadcasted_iota(jnp.int32, (LANES, tq), 0)
        pad = jnp.zeros((LANES - AUG_ROWS, tq), BF16)
        rhs = []
        for hh in (0, 1):
            qm = jnp.where((row >= hh * dh) & (row < (hh + 1) * dh), qt, jnp.zeros_like(qt))
            query_rows = {}
            for idx, piece in enumerate(_split3(cumt_ref[hh:hh + 1, pl.ds(q0, tq)])):
                query_rows[aug * hh + idx] = 1.0
                query_rows[aug * hh + 3 + idx] = piece
            rhs.append(jnp.concatenate([qm, _bias_rows(tq, query_rows), pad], axis=0))
        return tuple(rhs)

    n_tiles = seq // tq

    def query_tile(qi, rhs):
        def finish(outs):
            q0 = pl.multiple_of(qi * tq, tq)
            o_ref[pl.ds(q0, tq), :] = _stack_chain_outputs(outs, n_half).T.astype(o_ref.dtype)
            return make_rhs(jnp.minimum(qi + 1, n_tiles - 1))

        return _flash_key_major(qi, tq, chains, keys, lambda hh: rhs[hh],
                                lambda hh, ks: vta_ref[hh, :, pl.ds(ks, SUB)], dh, 1, finish)

    lax.fori_loop(0, n_tiles, query_tile, make_rhs(0))
    _cast_slabs(cast_in + tuple(cast_out))


def _fox_attention(proj, cumt4, seq, q_col0, cast_weights):
    m = proj.shape[0]
    n_batch = m // seq
    n_pairs = FOX_HEADS // 2
    qb = q_col0 // LANES
    grid = (n_batch, n_pairs)
    cast_specs, cast_shapes = _cast_specs(cast_weights, grid)
    return pl.pallas_call(
        functools.partial(_fox_kernel, tq=512, n_cast=len(cast_weights)),
        grid=grid,
        in_specs=[
            pl.BlockSpec((seq, LANES), lambda b, hp: (b, qb + hp)),
            pl.BlockSpec((seq, LANES), lambda b, hp: (b, qb + n_pairs + hp)),
            pl.BlockSpec((seq, LANES), lambda b, hp: (b, qb + 2 * n_pairs + hp)),
            pl.BlockSpec((None, None, 2, seq), lambda b, hp: (b, hp, 0, 0)),
        ] + cast_specs,
        out_specs=[pl.BlockSpec((seq, LANES), lambda b, hp: (b, hp))] + cast_specs,
        out_shape=[jax.ShapeDtypeStruct((m, FOX_WIDTH), BF16)] + cast_shapes,
        scratch_shapes=[
            pltpu.VMEM((2, FOX_HEAD_DIM + 16, seq), BF16),
            pltpu.VMEM((seq, LANES), BF16),
        ],
        compiler_params=_cparams(("parallel", "parallel")),
        name="fox_attn",
    )(proj, proj, proj, cumt4, *cast_weights)


def _mla_kernel(qn_ref, qr_ref, kv_ref, kr_ref, *rest, tq, n_cast):
    cast_in, (o_ref, *cast_out), (vta_ref,) = rest[:n_cast], rest[n_cast:2 * n_cast + 1], rest[2 * n_cast + 1:]
    seq = kv_ref.shape[0]
    dv = MLA_V_DIM
    n_half = tq // HALF

    tail_rows = lax.broadcasted_iota(jnp.int32, (16, seq), 0)
    tail = jnp.where(tail_rows < 8, 1.0, 0.0).astype(BF16)
    for hh in (0, 1):
        vta_ref[hh, 0:dv, :] = kv_ref[:, (2 * hh + 1) * LANES:(2 * hh + 2) * LANES].T
        vta_ref[hh, dv:dv + 16, :] = tail

    chains = [(half, hh) for half in range(n_half) for hh in (0, 1)]

    def keys(hh, k0, n):
        return jnp.concatenate([kv_ref[pl.ds(k0, n), 2 * hh * LANES:(2 * hh + 1) * LANES],
                                kr_ref[pl.ds(k0, n), :]], axis=1)

    def make_rhs(qi):
        q0 = pl.multiple_of(qi * tq, tq)
        return tuple(jnp.concatenate([qn_ref[pl.ds(q0, tq), hh * LANES:(hh + 1) * LANES],
                                      qr_ref[pl.ds(q0, tq), hh * LANES:(hh + 1) * LANES]], axis=1).T
                     for hh in (0, 1))

    n_tiles = seq // tq

    def query_tile(qi, rhs):
        def finish(outs):
            q0 = pl.multiple_of(qi * tq, tq)
            o_ref[pl.ds(q0, tq), :] = _stack_chain_outputs(outs, n_half).T.astype(o_ref.dtype)
            return make_rhs(jnp.minimum(qi + 1, n_tiles - 1))

        return _flash_key_major(qi, tq, chains, keys, lambda hh: rhs[hh],
                                lambda hh, ks: vta_ref[hh, :, pl.ds(ks, SUB)], dv, CHUNK, finish)

    lax.fori_loop(0, n_tiles, query_tile, make_rhs(0))
    _cast_slabs(cast_in + tuple(cast_out))


def _mla_attention(qn, qr, kv, kr, seq, cast_weights):
    m = qn.shape[0]
    n_batch = m // seq
    pair = 2 * LANES
    grid = (n_batch, MLA_HEADS // 2)
    cast_specs, cast_shapes = _cast_specs(cast_weights, grid)
    return pl.pallas_call(
        functools.partial(_mla_kernel, tq=512, n_cast=len(cast_weights)),
        grid=grid,
        in_specs=[
            pl.BlockSpec((seq, pair), lambda b, hp: (b, hp)),
            pl.BlockSpec((seq, pair), lambda b, hp: (b, hp)),
            pl.BlockSpec((seq, 2 * pair), lambda b, hp: (b, hp)),
            pl.BlockSpec((seq, LANES), lambda b, hp: (b, 0)),
        ] + cast_specs,
        out_specs=[pl.BlockSpec((seq, pair), lambda b, hp: (b, hp))] + cast_specs,
        out_shape=[jax.ShapeDtypeStruct((m, MLA_HEADS * MLA_V_DIM), BF16)] + cast_shapes,
        scratch_shapes=[pltpu.VMEM((2, MLA_V_DIM + 16, seq), BF16)],
        compiler_params=_cparams(("parallel", "parallel")),
        name="mla_attn",
    )(qn, qr, kv, kr, *cast_weights)


def _mix_kernel(yf_ref, ym_ref, gf_ref, gm_ref, x_ref, mod_ref, wbf_ref, wbm_ref, wo_ref, g_ref, b_ref,
                o_ref, mg_ref, *, n_chunk, g_off):
    d = o_ref.shape[1]
    yf = yf_ref[...]
    ym = ym_ref[...]

    def gate(ref, c):
        window = ref[:, c * n_chunk:(c + 1) * n_chunk + LANES]
        return jax.nn.sigmoid(window[:, g_off:g_off + n_chunk].astype(F32))

    for c in range(d // n_chunk):
        cols = slice(c * n_chunk, (c + 1) * n_chunk)
        a = jnp.dot(yf, wbf_ref[:, cols], preferred_element_type=F32)
        b = jnp.dot(ym, wbm_ref[:, cols], preferred_element_type=F32)
        merged = gate(gf_ref, c) * a + gate(gm_ref, c) * b
        mg_ref[:, cols] = merged.astype(BF16)
    mix = jnp.dot(mg_ref[...], wo_ref[...], preferred_element_type=F32)
    gate1 = mod_ref[2:3, :]
    o_ref[...] = _layer_norm(ALPHA * x_ref[...] + gate1 * mix, g_ref[...], b_ref[...])


def _mix(y_fox, y_mla, proj, gf_col, gm_col, x2d, mod3, w_bf, w_bm, w_o, ln_g, ln_b, seq):
    m, d = x2d.shape
    tm = 512
    tiles_per_batch = seq // tm
    const = lambda i: (0, 0)
    resident = dict(pipeline_mode=pl.Buffered(1))
    g_off = gf_col % LANES
    assert gm_col % LANES == g_off
    g_window = (pl.Element(tm), pl.Element(d + LANES))

    def window_at(col):
        return lambda i: (pl.multiple_of(i * tm, tm), col - g_off)

    return pl.pallas_call(
        functools.partial(_mix_kernel, n_chunk=512, g_off=g_off),
        grid=(m // tm,),
        in_specs=[
            pl.BlockSpec((tm, y_fox.shape[1]), lambda i: (i, 0)),
            pl.BlockSpec((tm, y_mla.shape[1]), lambda i: (i, 0)),
            pl.BlockSpec(g_window, window_at(gf_col)),
            pl.BlockSpec(g_window, window_at(gm_col)),
            pl.BlockSpec((tm, d), lambda i: (i, 0)),
            pl.BlockSpec((None, N_ADA, d), lambda i: (i // tiles_per_batch, 0, 0)),
            pl.BlockSpec(w_bf.shape, const, **resident),
            pl.BlockSpec(w_bm.shape, const, **resident),
            pl.BlockSpec(w_o.shape, const, **resident),
            pl.BlockSpec((1, d), const),
            pl.BlockSpec((1, d), const),
        ],
        out_specs=pl.BlockSpec((tm, d), lambda i: (i, 0)),
        out_shape=jax.ShapeDtypeStruct((m, d), F32),
        scratch_shapes=[pltpu.VMEM((tm, d), BF16)],
        compiler_params=_cparams(("parallel",)),
        name="mix",
    )(y_fox, y_mla, proj, proj, x2d, mod3, w_bf, w_bm, w_o, ln_g, ln_b)


def _mlp_kernel(x_ref, mod_ref, wu_ref, wd_ref, g_ref, b_ref, o_ref, u_ref):
    f = pl.program_id(1)

    def partial_sum(u):
        z = jnp.dot(u, wu_ref[...], preferred_element_type=F32)
        h = jnp.square(jnp.maximum(z, 0.0)).astype(BF16)
        return jnp.dot(h, wd_ref[...], preferred_element_type=F32)

    @pl.when(f == 0)
    def _():
        shift = mod_ref[3:4, :]
        scale = mod_ref[4:5, :]
        u = (x_ref[...] * (1.0 + scale) + shift).astype(BF16)
        u_ref[...] = u
        o_ref[...] = partial_sum(u)

    @pl.when(f != 0)
    def _():
        o_ref[...] += partial_sum(u_ref[...])

    @pl.when(f == pl.num_programs(1) - 1)
    def _():
        gate = mod_ref[5:6, :]
        o_ref[...] = _layer_norm(ALPHA * x_ref[...] + gate * o_ref[...], g_ref[...], b_ref[...])


def _mlp(x1, mod3, w_up, w_down, ln_g, ln_b, seq):
    m, d = x1.shape
    d_ff = w_up.shape[1]
    tm, tf = 1024, 512
    tiles_per_batch = seq // tm
    return pl.pallas_call(
        _mlp_kernel,
        grid=(m // tm, d_ff // tf),
        in_specs=[
            pl.BlockSpec((tm, d), lambda i, f: (i, 0)),
            pl.BlockSpec((None, N_ADA, d), lambda i, f: (i // tiles_per_batch, 0, 0)),
            pl.BlockSpec((d, tf), lambda i, f: (0, f)),
            pl.BlockSpec((tf, d), lambda i, f: (f, 0)),
            pl.BlockSpec((1, d), lambda i, f: (0, 0)),
            pl.BlockSpec((1, d), lambda i, f: (0, 0)),
        ],
        out_specs=pl.BlockSpec((tm, d), lambda i, f: (i, 0)),
        out_shape=jax.ShapeDtypeStruct((m, d), F32),
        scratch_shapes=[pltpu.VMEM((tm, d), BF16)],
        compiler_params=_cparams(("parallel", "arbitrary")),
        name="mlp",
    )(x1, mod3, w_up, w_down, ln_g, ln_b)


def _pad_cols(w, width):
    return jnp.pad(w, ((0, 0), (0, width - w.shape[1])))


def _swap_halves(w):
    half = w.shape[-1] // 2
    return jnp.concatenate([w[..., half:], w[..., :half]], axis=-1)


def _layout_q_up(w_q_up):
    r = w_q_up.shape[0]
    w3 = w_q_up.reshape(r, MLA_HEADS, MLA_QK_DIM)
    w_nope = w3[:, :, :MLA_NOPE_DIM].reshape(r, MLA_HEADS * MLA_NOPE_DIM)
    w_rope = w3[:, :, MLA_NOPE_DIM:]
    pad = ((0, 0), (0, 0), (0, LANES - MLA_ROPE_DIM))
    w_r = jnp.pad(w_rope, pad).reshape(r, MLA_HEADS * LANES)
    w_rs = jnp.pad(_swap_halves(w_rope), pad).reshape(r, MLA_HEADS * LANES)
    return jnp.concatenate([w_nope, w_r, w_rs], axis=1).astype(BF16)


def _rope_tables(seq):
    pos = np.arange(seq, dtype=np.float64)
    inv_freq = ROPE_THETA ** (-np.arange(0, MLA_ROPE_DIM, 2, dtype=np.float64) / MLA_ROPE_DIM)
    ang = pos[:, None] * inv_freq[None, :]
    cos, sin = np.cos(ang), np.sin(ang)
    pad = np.zeros((seq, LANES - MLA_ROPE_DIM))
    cos_t = np.concatenate([cos, cos, pad], axis=1).astype(np.float32)
    sin_t = np.concatenate([-sin, sin, pad], axis=1).astype(np.float32)
    return jnp.asarray(cos_t), jnp.asarray(sin_t)


def kernel(x, c, w_ada, b_ada, w_in, b_forget, g_q_norm, w_q_up, g_kv_norm, w_kv_up, w_branch_fox,
           w_branch_mla, w_out, ln1_g, ln1_b, w_mlp_up, w_mlp_down, ln2_g, ln2_b):
    n_batch, seq, d = x.shape
    assert w_ada.shape[0] == DEPTH
    m = n_batch * seq
    x2d = x.reshape(m, d)

    mod3 = _ada(c, w_ada[0], b_ada[0]).reshape(n_batch, N_ADA, d)

    small_col = 3 * FOX_WIDTH
    gf_col = small_col + FOX_HEADS + MLA_Q_LORA + MLA_KV_LORA + MLA_ROPE_DIM
    gm_col = gf_col + d
    assert w_in.shape[2] == gm_col + d and small_col == IN_SMALL_TILE * IN_TN and gf_col <= small_col + IN_TN
    proj, proj_lo = _inproj(x2d, mod3, w_in[0].T, seq)

    cos_t, sin_t = _rope_tables(seq)
    bf_pad = _pad_cols(b_forget[0].reshape(1, FOX_HEADS), LANES)
    qn, qr, kv, kr, cumt = _prep(
        proj, proj_lo, bf_pad, g_q_norm[0].reshape(1, -1), g_kv_norm[0].reshape(1, -1),
        _layout_q_up(w_q_up[0]), w_kv_up[0].astype(BF16), cos_t, sin_t, seq)

    y_fox, w_up_b, w_down_b = _fox_attention(
        proj, cumt.reshape(n_batch, FOX_HEADS // 2, 2, seq), seq, 0, (w_mlp_up[0], w_mlp_down[0]))
    y_mla, w_bf_b, w_bm_b, w_o_b = _mla_attention(
        qn, qr, kv, kr, seq, (w_branch_fox[0], w_branch_mla[0], w_out[0]))

    x1 = _mix(y_fox, y_mla, proj, gf_col, gm_col, x2d, mod3, w_bf_b, w_bm_b, w_o_b,
              ln1_g[0].reshape(1, d), ln1_b[0].reshape(1, d), seq)
    x2 = _mlp(x1, mod3, w_up_b, w_down_b, ln2_g[0].reshape(1, d), ln2_b[0].reshape(1, d), seq)
    return x2.reshape(n_batch, seq, d)
```

```python
import functools
import math

import jax
import jax.numpy as jnp
import numpy as np
from jax import lax
from jax.experimental import pallas as pl
from jax.experimental.pallas import tpu as pltpu

F32 = jnp.float32
BF16 = jnp.bfloat16

FOX_HEADS = 16
FOX_HEAD_DIM = 64
FOX_WIDTH = FOX_HEADS * FOX_HEAD_DIM
MLA_HEADS = 8
MLA_NOPE_DIM = 128
MLA_ROPE_DIM = 64
MLA_V_DIM = 128
MLA_Q_LORA = 512
MLA_KV_LORA = 256
MLA_QK_DIM = MLA_NOPE_DIM + MLA_ROPE_DIM
CHUNK = 64
ROPE_THETA = 10000.0
LN_EPS = 1e-5
RMS_EPS = 1e-6
N_ADA = 6
DEPTH = 1
ALPHA = (2.0 * DEPTH) ** 0.25

LANES = 128
LOG2E = math.log2(math.e)
FOX_QSCALE = FOX_HEAD_DIM ** -0.5 * LOG2E
MLA_QSCALE = MLA_QK_DIM ** -0.5 * LOG2E
NEG_BIG = -1e30
VMEM_LIMIT = 56 * 1024 * 1024

SUB = 128
HALF = 256


def _cparams(semantics):
    return pltpu.CompilerParams(dimension_semantics=semantics, vmem_limit_bytes=VMEM_LIMIT)


def _layer_norm(y, g, b):
    mu = jnp.mean(y, axis=-1, keepdims=True)
    yc = y - mu
    var = jnp.mean(yc * yc, axis=-1, keepdims=True)
    return yc * lax.rsqrt(var + LN_EPS) * g + b


def _rms_norm(c, g):
    return c * lax.rsqrt(jnp.mean(c * c, axis=-1, keepdims=True) + RMS_EPS) * g


def _ada_kernel(ct_ref, w_ref, b_ref, o_ref, *, n_batch, k_chunk):
    d = w_ref.shape[0]
    ct = ct_ref[...]
    s = ct * jax.nn.sigmoid(ct)
    accs = [b_ref[...] for _ in range(n_batch)]
    for kc in range(d // k_chunk):
        rows = slice(kc * k_chunk, (kc + 1) * k_chunk)
        w = w_ref[rows, :]
        for b in range(n_batch):
            accs[b] = accs[b] + jnp.sum(w * s[rows, b:b + 1], axis=0, keepdims=True)
    for b in range(n_batch):
        o_ref[b:b + 1, :] = accs[b]


def _ada(c, w_ada, b_ada):
    n_batch, d = c.shape
    n = w_ada.shape[1]
    tn = 1024
    return pl.pallas_call(
        functools.partial(_ada_kernel, n_batch=n_batch, k_chunk=256),
        grid=(n // tn,),
        in_specs=[
            pl.BlockSpec((d, n_batch), lambda j: (0, 0)),
            pl.BlockSpec((d, tn), lambda j: (0, j)),
            pl.BlockSpec((1, tn), lambda j: (0, j)),
        ],
        out_specs=pl.BlockSpec((n_batch, tn), lambda j: (0, j)),
        out_shape=jax.ShapeDtypeStruct((n_batch, n), F32),
        compiler_params=_cparams(("arbitrary",)),
        name="ada",
    )(c.T, w_ada, b_ada.reshape(1, n))


IN_TN = 1024
IN_SMALL_TILE = 3 * FOX_WIDTH // IN_TN
NARROW_F = (0, LANES)
NARROW_CQ = (NARROW_F[1], NARROW_F[1] + MLA_Q_LORA)
NARROW_CKV = (NARROW_CQ[1], NARROW_CQ[1] + MLA_KV_LORA)
NARROW_KR = (NARROW_CKV[1], NARROW_CKV[1] + LANES)
assert NARROW_KR[1] == IN_TN and 2 * MLA_ROPE_DIM == LANES


def _narrow_segments(acc):
    c0 = FOX_HEADS
    c1 = c0 + MLA_Q_LORA
    c2 = c1 + MLA_KV_LORA
    c3 = c2 + MLA_ROPE_DIM
    half = MLA_ROPE_DIM // 2
    return jnp.concatenate([acc[:, 0:LANES], acc[:, c0:c1], acc[:, c1:c2], acc[:, c2:c3],
                            acc[:, c2 + half:c3], acc[:, c2:c2 + half]], axis=1)


def _inproj_kernel(x_ref, mod_ref, w_ref, o_ref, hi_ref, lo_ref, u_ref, *, n_cols):
    j = pl.program_id(1)
    last = pl.num_programs(1) - 1

    def product(wt, u=None):
        u = u_ref[...] if u is None else u
        return lax.dot_general(u, wt.astype(BF16), (((1,), (1,)), ((), ())), preferred_element_type=F32)

    @pl.when(j == 0)
    def _():
        shift = mod_ref[0:1, :]
        scale = mod_ref[1:2, :]
        u = (x_ref[...] * (1.0 + scale) + shift).astype(BF16)
        u_ref[...] = u
        o_ref[...] = (product(w_ref[...], u) * FOX_QSCALE).astype(o_ref.dtype)

    @pl.when(j == IN_SMALL_TILE)
    def _():
        acc = product(w_ref[...])
        o_ref[...] = acc.astype(o_ref.dtype)
        narrow = _narrow_segments(acc)
        hi = narrow.astype(hi_ref.dtype)
        hi_ref[...] = hi
        lo_ref[...] = (narrow - hi.astype(F32)).astype(lo_ref.dtype)

    @pl.when(j == last)
    def _():
        row = lax.broadcasted_iota(jnp.int32, w_ref.shape, 0)
        wt = jnp.where(row < n_cols - last * IN_TN, w_ref[...], 0.0)
        o_ref[...] = product(wt).astype(o_ref.dtype)

    @pl.when((j != 0) & (j != IN_SMALL_TILE) & (j != last))
    def _():
        o_ref[...] = product(w_ref[...]).astype(o_ref.dtype)


def _inproj(x2d, mod3, w_in_t, seq):
    m, d = x2d.shape
    n_cols = w_in_t.shape[0]
    tm = 1024
    n_tiles = pl.cdiv(n_cols, IN_TN)
    tiles_per_batch = seq // tm
    return pl.pallas_call(
        functools.partial(_inproj_kernel, n_cols=n_cols),
        grid=(m // tm, n_tiles),
        in_specs=[
            pl.BlockSpec((tm, d), lambda i, j: (i, 0)),
            pl.BlockSpec((None, N_ADA, d), lambda i, j: (i // tiles_per_batch, 0, 0)),
            pl.BlockSpec((IN_TN, d), lambda i, j: (j, 0)),
        ],
        out_specs=[pl.BlockSpec((tm, IN_TN), lambda i, j: (i, j)),
                   pl.BlockSpec((tm, IN_TN), lambda i, j: (i, 0)),
                   pl.BlockSpec((tm, IN_TN), lambda i, j: (i, 0))],
        out_shape=[jax.ShapeDtypeStruct((m, n_tiles * IN_TN), BF16),
                   jax.ShapeDtypeStruct((m, IN_TN), BF16),
                   jax.ShapeDtypeStruct((m, IN_TN), BF16)],
        scratch_shapes=[pltpu.VMEM((tm, d), BF16)],
        compiler_params=_cparams(("parallel", "arbitrary")),
        name="inproj",
    )(x2d, mod3, w_in_t)


def _log_sigmoid(x):
    return jnp.minimum(x, 0.0) - jnp.log1p(jnp.exp(-jnp.abs(x)))


def _prep_kernel(hi_ref, lo_ref, bf_ref, gq_ref, gkv_ref, wq_ref, wkv_ref, cos_ref, sin_ref,
                 qn_ref, qr_ref, kv_ref, kr_ref, cumt_ref, carry_ref, *, tiles_per_batch):
    i = pl.program_id(0)
    tm = hi_ref.shape[0]
    ps = hi_ref[...].astype(F32) + lo_ref[...].astype(F32)
    k_rope = ps[:, NARROW_KR[0]:NARROW_KR[1]]
    k_rope_partner = pltpu.roll(k_rope, MLA_ROPE_DIM, axis=1)
    cos = cos_ref[...]
    sin = sin_ref[...]

    logf = _log_sigmoid(ps[:, NARROW_F[0]:NARROW_F[1]] + bf_ref[...]) * LOG2E
    hi = logf.astype(BF16)
    r1 = logf - hi.astype(F32)
    mid = r1.astype(BF16)
    lo = (r1 - mid.astype(F32)).astype(BF16)
    row = lax.broadcasted_iota(jnp.int32, (tm, tm), 0)
    col = lax.broadcasted_iota(jnp.int32, (tm, tm), 1)
    tri = (row >= col).astype(BF16)
    cs = (jnp.dot(tri, hi, preferred_element_type=F32)
          + jnp.dot(tri, mid, preferred_element_type=F32)
          + jnp.dot(tri, lo, preferred_element_type=F32))

    @pl.when(i % tiles_per_batch == 0)
    def _():
        carry_ref[...] = jnp.zeros_like(carry_ref)

    cum = cs + carry_ref[...]
    carry_ref[...] = cum[tm - 1:tm, :]
    cumt_ref[...] = cum.T[0:FOX_HEADS, :]

    cqn = _rms_norm(ps[:, NARROW_CQ[0]:NARROW_CQ[1]], gq_ref[...]).astype(BF16)
    qa = jnp.dot(cqn, wq_ref[...], preferred_element_type=F32)
    hw = MLA_HEADS * LANES
    qn_ref[...] = (qa[:, 0:hw] * MLA_QSCALE).astype(BF16)
    for h in range(MLA_HEADS):
        a = qa[:, hw + h * LANES: hw + (h + 1) * LANES]
        b = qa[:, 2 * hw + h * LANES: 2 * hw + (h + 1) * LANES]
        qr_ref[:, h * LANES:(h + 1) * LANES] = ((a * cos + b * sin) * MLA_QSCALE).astype(BF16)

    ckvn = _rms_norm(ps[:, NARROW_CKV[0]:NARROW_CKV[1]], gkv_ref[...]).astype(BF16)
    kv_ref[...] = jnp.dot(ckvn, wkv_ref[...], preferred_element_type=F32).astype(BF16)
    kr_ref[...] = (k_rope * cos + k_rope_partner * sin).astype(BF16)


def _prep(narrow_hi, narrow_lo, bf_pad, g_q, g_kv, w_q_all, w_kv, cos_t, sin_t, seq):
    m = narrow_hi.shape[0]
    n_batch = m // seq
    tm = 512
    tiles_per_batch = seq // tm
    hw = MLA_HEADS * LANES
    const = lambda i: (0, 0)
    return pl.pallas_call(
        functools.partial(_prep_kernel, tiles_per_batch=tiles_per_batch),
        grid=(m // tm,),
        in_specs=[
            pl.BlockSpec((tm, IN_TN), lambda i: (i, 0)),
            pl.BlockSpec((tm, IN_TN), lambda i: (i, 0)),
            pl.BlockSpec(bf_pad.shape, const),
            pl.BlockSpec(g_q.shape, const),
            pl.BlockSpec(g_kv.shape, const),
            pl.BlockSpec(w_q_all.shape, const),
            pl.BlockSpec(w_kv.shape, const),
            pl.BlockSpec((tm, LANES), lambda i: (i % tiles_per_batch, 0)),
            pl.BlockSpec((tm, LANES), lambda i: (i % tiles_per_batch, 0)),
        ],
        out_specs=[
            pl.BlockSpec((tm, hw), lambda i: (i, 0)),
            pl.BlockSpec((tm, hw), lambda i: (i, 0)),
            pl.BlockSpec((tm, w_kv.shape[1]), lambda i: (i, 0)),
            pl.BlockSpec((tm, LANES), lambda i: (i, 0)),
            pl.BlockSpec((None, FOX_HEADS, tm), lambda i: (i // tiles_per_batch, 0, i % tiles_per_batch)),
        ],
        out_shape=[
            jax.ShapeDtypeStruct((m, hw), BF16),
            jax.ShapeDtypeStruct((m, hw), BF16),
            jax.ShapeDtypeStruct((m, w_kv.shape[1]), BF16),
            jax.ShapeDtypeStruct((m, LANES), BF16),
            jax.ShapeDtypeStruct((n_batch, FOX_HEADS, seq), F32),
        ],
        scratch_shapes=[pltpu.VMEM((1, LANES), F32)],
        compiler_params=_cparams(("arbitrary",)),
        name="prep",
    )(narrow_hi, narrow_lo, bf_pad, g_q, g_kv, w_q_all, w_kv, cos_t, sin_t)


def _split3(x):
    hi = x.astype(BF16).astype(F32)
    r = x - hi
    mid = r.astype(BF16).astype(F32)
    lo = (r - mid).astype(BF16).astype(F32)
    return hi, mid, lo


def _online_softmax_step(t, m, acc, vt_aug):
    m_new = jnp.maximum(m, jnp.max(t, axis=0, keepdims=True))
    alpha = jnp.exp2(m - m_new)
    p = jnp.exp2(t - m_new).astype(BF16)
    pv = jnp.dot(vt_aug, p, preferred_element_type=F32)
    return m_new, alpha * acc + pv[0:acc.shape[0], :]


def _causal_mask(t, key0, query0, granule):
    r = lax.broadcasted_iota(jnp.int32, t.shape, 0) + key0
    c = lax.broadcasted_iota(jnp.int32, t.shape, 1) + query0
    return jnp.where(r <= (c | (granule - 1)), t, NEG_BIG)


def _flash_key_major(qi, tile, chains, keys, rhs, vt_aug, d_v, granule, finish):
    acc_rows = d_v + 8
    n_rounds = tile // SUB
    pairs = [(n, n + 1) for n in range(0, len(chains), 2)]

    def qk_of(k0, is_diag):
        loaded = {}

        def product(n):
            half, hh = chains[n]
            n_keys = (half + 1) * HALF if is_diag else tile
            if (hh, n_keys) not in loaded:
                loaded[hh, n_keys] = keys(hh, k0, n_keys)
            return jnp.dot(loaded[hh, n_keys], rhs(hh)[:, half * HALF:(half + 1) * HALF],
                           preferred_element_type=F32)
        return product

    def visit(state, tiles):
        st = list(state)
        products = [qk_of(*t) for t in tiles]
        phases = [(ti, pair) for ti in range(len(tiles)) for pair in pairs]

        def issue(phase):
            ti, pair = phase
            return {n: products[ti](n) for n in pair}

        cur = issue(phases[0])
        for idx, (ti, pair) in enumerate(phases):
            nxt = issue(phases[idx + 1]) if idx + 1 < len(phases) else None
            k0, is_diag = tiles[ti]
            for sb in range(n_rounds):
                ks = pl.multiple_of(k0 + sb * SUB, SUB)
                for n in pair:
                    half, hh = chains[n]
                    if is_diag and sb * SUB >= (half + 1) * HALF:
                        continue
                    t = cur[n][sb * SUB:(sb + 1) * SUB, :]
                    if is_diag and (sb + 1) * SUB > half * HALF:
                        t = _causal_mask(t, sb * SUB, half * HALF, granule)
                    st[n] = _online_softmax_step(t, *st[n], vt_aug(hh, ks))
            cur = nxt
        return tuple(st)

    def two_full(j, state):
        k0 = pl.multiple_of(2 * j * tile, tile)
        return visit(state, [(k0, False), (pl.multiple_of(k0 + tile, tile), False)])

    init = tuple((jnp.full((1, HALF), NEG_BIG, F32), jnp.zeros((acc_rows, HALF), F32)) for _ in chains)
    state = lax.fori_loop(0, qi // 2, two_full, init)
    q0 = pl.multiple_of(qi * tile, tile)
    def last_tiles(tiles):
        def branch(st):
            outs = [acc[0:d_v, :] * (1.0 / acc[d_v:d_v + 1, :]) for (_, acc) in visit(st, tiles)]
            return finish(outs)
        return branch

    return lax.cond(qi % 2 == 1,
                    last_tiles([(pl.multiple_of(q0 - tile, tile), False), (q0, True)]),
                    last_tiles([(q0, True)]),
                    state)


def _cast_specs(weights, grid):
    n_steps = math.prod(grid)
    specs, shapes = [], []
    for w in weights:
        rows, cols = w.shape
        slab = rows // n_steps
        assert slab * n_steps == rows and slab % 16 == 0, (w.shape, n_steps)
        specs.append(pl.BlockSpec((slab, cols), lambda a, b: (a * grid[1] + b, 0)))
        shapes.append(jax.ShapeDtypeStruct(w.shape, BF16))
    return specs, shapes


def _cast_slabs(refs):
    n = len(refs) // 2
    for src, dst in zip(refs[:n], refs[n:]):
        dst[...] = src[...].astype(dst.dtype)


def _stack_chain_outputs(outs, n_half):
    return jnp.concatenate(
        [jnp.concatenate([outs[half * 2 + hh] for half in range(n_half)], axis=1) for hh in (0, 1)], axis=0)


AUG_ROWS = 16


def _bias_rows(n, entries):
    r = lax.broadcasted_iota(jnp.int32, (AUG_ROWS, n), 0)
    out = jnp.zeros((AUG_ROWS, n), F32)
    for row, val in entries.items():
        out = jnp.where(r == row, val, out)
    return out.astype(BF16)


def _fox_kernel(q_ref, k_ref, v_ref, cumt_ref, *rest, tq, n_cast):
    cast_in, (o_ref, *cast_out), (vta_ref, kaug_ref) = rest[:n_cast], rest[n_cast:2 * n_cast + 1], rest[2 * n_cast + 1:]
    seq = k_ref.shape[0]
    dh = FOX_HEAD_DIM
    n_half = tq // HALF
    aug = 6

    vt = v_ref[...].T
    tail_rows = lax.broadcasted_iota(jnp.int32, (16, seq), 0)
    tail = jnp.where(tail_rows < 8, 1.0, 0.0).astype(BF16)
    for hh in (0, 1):
        vta_ref[hh, 0:dh, :] = vt[hh * dh:(hh + 1) * dh, :]
        vta_ref[hh, dh:dh + 16, :] = tail
    key_rows = {}
    for hh in (0, 1):
        for idx, piece in enumerate(_split3(-cumt_ref[hh:hh + 1, :])):
            key_rows[aug * hh + idx] = piece
            key_rows[aug * hh + 3 + idx] = 1.0
    kaug_t = jnp.concatenate([_bias_rows(seq, key_rows), jnp.zeros((LANES - AUG_ROWS, seq), BF16)], axis=0)
    kaug_ref[...] = kaug_t.T

    chains = [(half, hh) for half in range(n_half) for hh in (0, 1)]

    def keys(hh, k0, n):
        return jnp.concatenate([k_ref[pl.ds(k0, n), :], kaug_ref[pl.ds(k0, n), :]], axis=1)

    def make_rhs(qi):
        q0 = pl.multiple_of(qi * tq, tq)
        qt = q_ref[pl.ds(q0, tq), :].T
        row = lax.broadcasted_iota(jnp.int32, (LANES, tq), 0)
        pad = jnp.zeros((LANES - AUG_ROWS, tq), BF16)
        rhs = []
        for hh in (0, 1):
            qm = jnp.where((row >= hh * dh) & (row < (hh + 1) * dh), qt, jnp.zeros_like(qt))
            query_rows = {}
            for idx, piece in enumerate(_split3(cumt_ref[hh:hh + 1, pl.ds(q0, tq)])):
                query_rows[aug * hh + idx] = 1.0
                query_rows[aug * hh + 3 + idx] = piece
            rhs.append(jnp.concatenate([qm, _bias_rows(tq, query_rows), pad], axis=0))
        return tuple(rhs)

    n_tiles = seq // tq

    def query_tile(qi, rhs):
        def finish(outs):
            q0 = pl.multiple_of(qi * tq, tq)
            o_ref[pl.ds(q0, tq), :] = _stack_chain_outputs(outs, n_half).T.astype(o_ref.dtype)
            return make_rhs(jnp.minimum(qi + 1, n_tiles - 1))

        return _flash_key_major(qi, tq, chains, keys, lambda hh: rhs[hh],
                                lambda hh, ks: vta_ref[hh, :, pl.ds(ks, SUB)], dh, 1, finish)

    lax.fori_loop(0, n_tiles, query_tile, make_rhs(0))
    _cast_slabs(cast_in + tuple(cast_out))


def _fox_attention(proj, cumt4, seq, q_col0, cast_weights):
    m = proj.shape[0]
    n_batch = m // seq
    n_pairs = FOX_HEADS // 2
    qb = q_col0 // LANES
    grid = (n_batch, n_pairs)
    cast_specs, cast_shapes = _cast_specs(cast_weights, grid)
    return pl.pallas_call(
        functools.partial(_fox_kernel, tq=512, n_cast=len(cast_weights)),
        grid=grid,
        in_specs=[
            pl.BlockSpec((seq, LANES), lambda b, hp: (b, qb + hp)),
            pl.BlockSpec((seq, LANES), lambda b, hp: (b, qb + n_pairs + hp)),
            pl.BlockSpec((seq, LANES), lambda b, hp: (b, qb + 2 * n_pairs + hp)),
            pl.BlockSpec((None, None, 2, seq), lambda b, hp: (b, hp, 0, 0)),
        ] + cast_specs,
        out_specs=[pl.BlockSpec((seq, LANES), lambda b, hp: (b, hp))] + cast_specs,
        out_shape=[jax.ShapeDtypeStruct((m, FOX_WIDTH), BF16)] + cast_shapes,
        scratch_shapes=[
            pltpu.VMEM((2, FOX_HEAD_DIM + 16, seq), BF16),
            pltpu.VMEM((seq, LANES), BF16),
        ],
        compiler_params=_cparams(("parallel", "parallel")),
        name="fox_attn",
    )(proj, proj, proj, cumt4, *cast_weights)


def _mla_kernel(qn_ref, qr_ref, kv_ref, kr_ref, *rest, tq, n_cast):
    cast_in, (o_ref, *cast_out), (vta_ref,) = rest[:n_cast], rest[n_cast:2 * n_cast + 1], rest[2 * n_cast + 1:]
    seq = kv_ref.shape[0]
    dv = MLA_V_DIM
    n_half = tq // HALF

    tail_rows = lax.broadcasted_iota(jnp.int32, (16, seq), 0)
    tail = jnp.where(tail_rows < 8, 1.0, 0.0).astype(BF16)
    for hh in (0, 1):
        vta_ref[hh, 0:dv, :] = kv_ref[:, (2 * hh + 1) * LANES:(2 * hh + 2) * LANES].T
        vta_ref[hh, dv:dv + 16, :] = tail

    chains = [(half, hh) for half in range(n_half) for hh in (0, 1)]

    def keys(hh, k0, n):
        return jnp.concatenate([kv_ref[pl.ds(k0, n), 2 * hh * LANES:(2 * hh + 1) * LANES],
                                kr_ref[pl.ds(k0, n), :]], axis=1)

    def make_rhs(qi):
        q0 = pl.multiple_of(qi * tq, tq)
        return tuple(jnp.concatenate([qn_ref[pl.ds(q0, tq), hh * LANES:(hh + 1) * LANES],
                                      qr_ref[pl.ds(q0, tq), hh * LANES:(hh + 1) * LANES]], axis=1).T
                     for hh in (0, 1))

    n_tiles = seq // tq

    def query_tile(qi, rhs):
        def finish(outs):
            q0 = pl.multiple_of(qi * tq, tq)
            o_ref[pl.ds(q0, tq), :] = _stack_chain_outputs(outs, n_half).T.astype(o_ref.dtype)
            return make_rhs(jnp.minimum(qi + 1, n_tiles - 1))

        return _flash_key_major(qi, tq, chains, keys, lambda hh: rhs[hh],
                                lambda hh, ks: vta_ref[hh, :, pl.ds(ks, SUB)], dv, CHUNK, finish)

    lax.fori_loop(0, n_tiles, query_tile, make_rhs(0))
    _cast_slabs(cast_in + tuple(cast_out))


def _mla_attention(qn, qr, kv, kr, seq, cast_weights):
    m = qn.shape[0]
    n_batch = m // seq
    pair = 2 * LANES
    grid = (n_batch, MLA_HEADS // 2)
    cast_specs, cast_shapes = _cast_specs(cast_weights, grid)
    return pl.pallas_call(
        functools.partial(_mla_kernel, tq=512, n_cast=len(cast_weights)),
        grid=grid,
        in_specs=[
            pl.BlockSpec((seq, pair), lambda b, hp: (b, hp)),
            pl.BlockSpec((seq, pair), lambda b, hp: (b, hp)),
            pl.BlockSpec((seq, 2 * pair), lambda b, hp: (b, hp)),
            pl.BlockSpec((seq, LANES), lambda b, hp: (b, 0)),
        ] + cast_specs,
        out_specs=[pl.BlockSpec((seq, pair), lambda b, hp: (b, hp))] + cast_specs,
        out_shape=[jax.ShapeDtypeStruct((m, MLA_HEADS * MLA_V_DIM), BF16)] + cast_shapes,
        scratch_shapes=[pltpu.VMEM((2, MLA_V_DIM + 16, seq), BF16)],
        compiler_params=_cparams(("parallel", "parallel")),
        name="mla_attn",
    )(qn, qr, kv, kr, *cast_weights)


def _mix_kernel(yf_ref, ym_ref, gf_ref, gm_ref, x_ref, mod_ref, wbf_ref, wbm_ref, wo_ref, g_ref, b_ref,
                o_ref, mg_ref, *, n_chunk, g_off):
    d = o_ref.shape[1]
    yf = yf_ref[...]
    ym = ym_ref[...]

    def gate(ref, c):
        window = ref[:, c * n_chunk:(c + 1) * n_chunk + LANES]
        return jax.nn.sigmoid(window[:, g_off:g_off + n_chunk].astype(F32))

    for c in range(d // n_chunk):
        cols = slice(c * n_chunk, (c + 1) * n_chunk)
        a = jnp.dot(yf, wbf_ref[:, cols], preferred_element_type=F32)
        b = jnp.dot(ym, wbm_ref[:, cols], preferred_element_type=F32)
        merged = gate(gf_ref, c) * a + gate(gm_ref, c) * b
        mg_ref[:, cols] = merged.astype(BF16)
    mix = jnp.dot(mg_ref[...], wo_ref[...], preferred_element_type=F32)
    gate1 = mod_ref[2:3, :]
    o_ref[...] = _layer_norm(ALPHA * x_ref[...] + gate1 * mix, g_ref[...], b_ref[...])


def _mix(y_fox, y_mla, proj, gf_col, gm_col, x2d, mod3, w_bf, w_bm, w_o, ln_g, ln_b, seq):
    m, d = x2d.shape
    tm = 512
    tiles_per_batch = seq // tm
    const = lambda i: (0, 0)
    resident = dict(pipeline_mode=pl.Buffered(1))
    g_off = gf_col % LANES
    assert gm_col % LANES == g_off
    g_window = (pl.Element(tm), pl.Element(d + LANES))

    def window_at(col):
        return lambda i: (pl.multiple_of(i * tm, tm), col - g_off)

    return pl.pallas_call(
        functools.partial(_mix_kernel, n_chunk=512, g_off=g_off),
        grid=(m // tm,),
        in_specs=[
            pl.BlockSpec((tm, y_fox.shape[1]), lambda i: (i, 0)),
            pl.BlockSpec((tm, y_mla.shape[1]), lambda i: (i, 0)),
            pl.BlockSpec(g_window, window_at(gf_col)),
            pl.BlockSpec(g_window, window_at(gm_col)),
            pl.BlockSpec((tm, d), lambda i: (i, 0)),
            pl.BlockSpec((None, N_ADA, d), lambda i: (i // tiles_per_batch, 0, 0)),
            pl.BlockSpec(w_bf.shape, const, **resident),
            pl.BlockSpec(w_bm.shape, const, **resident),
            pl.BlockSpec(w_o.shape, const, **resident),
            pl.BlockSpec((1, d), const),
            pl.BlockSpec((1, d), const),
        ],
        out_specs=pl.BlockSpec((tm, d), lambda i: (i, 0)),
        out_shape=jax.ShapeDtypeStruct((m, d), F32),
        scratch_shapes=[pltpu.VMEM((tm, d), BF16)],
        compiler_params=_cparams(("parallel",)),
        name="mix",
    )(y_fox, y_mla, proj, proj, x2d, mod3, w_bf, w_bm, w_o, ln_g, ln_b)


def _mlp_kernel(x_ref, mod_ref, wu_ref, wd_ref, g_ref, b_ref, o_ref, u_ref):
    f = pl.program_id(1)

    def partial_sum(u):
        z = jnp.dot(u, wu_ref[...], preferred_element_type=F32)
        h = jnp.square(jnp.maximum(z, 0.0)).astype(BF16)
        return jnp.dot(h, wd_ref[...], preferred_element_type=F32)

    @pl.when(f == 0)
    def _():
        shift = mod_ref[3:4, :]
        scale = mod_ref[4:5, :]
        u = (x_ref[...] * (1.0 + scale) + shift).astype(BF16)
        u_ref[...] = u
        o_ref[...] = partial_sum(u)

    @pl.when(f != 0)
    def _():
        o_ref[...] += partial_sum(u_ref[...])

    @pl.when(f == pl.num_programs(1) - 1)
    def _():
        gate = mod_ref[5:6, :]
        o_ref[...] = _layer_norm(ALPHA * x_ref[...] + gate * o_ref[...], g_ref[...], b_ref[...])


def _mlp(x1, mod3, w_up, w_down, ln_g, ln_b, seq):
    m, d = x1.shape
    d_ff = w_up.shape[1]
    tm, tf = 1024, 512
    tiles_per_batch = seq // tm
    return pl.pallas_call(
        _mlp_kernel,
        grid=(m // tm, d_ff // tf),
        in_specs=[
            pl.BlockSpec((tm, d), lambda i, f: (i, 0)),
            pl.BlockSpec((None, N_ADA, d), lambda i, f: (i // tiles_per_batch, 0, 0)),
            pl.BlockSpec((d, tf), lambda i, f: (0, f)),
            pl.BlockSpec((tf, d), lambda i, f: (f, 0)),
            pl.BlockSpec((1, d), lambda i, f: (0, 0)),
            pl.BlockSpec((1, d), lambda i, f: (0, 0)),
        ],
        out_specs=pl.BlockSpec((tm, d), lambda i, f: (i, 0)),
        out_shape=jax.ShapeDtypeStruct((m, d), F32),
        scratch_shapes=[pltpu.VMEM((tm, d), BF16)],
        compiler_params=_cparams(("parallel", "arbitrary")),
        name="mlp",
    )(x1, mod3, w_up, w_down, ln_g, ln_b)


def _pad_cols(w, width):
    return jnp.pad(w, ((0, 0), (0, width - w.shape[1])))


def _swap_halves(w):
    half = w.shape[-1] // 2
    return jnp.concatenate([w[..., half:], w[..., :half]], axis=-1)


def _layout_q_up(w_q_up):
    r = w_q_up.shape[0]
    w3 = w_q_up.reshape(r, MLA_HEADS, MLA_QK_DIM)
    w_nope = w3[:, :, :MLA_NOPE_DIM].reshape(r, MLA_HEADS * MLA_NOPE_DIM)
    w_rope = w3[:, :, MLA_NOPE_DIM:]
    pad = ((0, 0), (0, 0), (0, LANES - MLA_ROPE_DIM))
    w_r = jnp.pad(w_rope, pad).reshape(r, MLA_HEADS * LANES)
    w_rs = jnp.pad(_swap_halves(w_rope), pad).reshape(r, MLA_HEADS * LANES)
    return jnp.concatenate([w_nope, w_r, w_rs], axis=1).astype(BF16)


def _rope_tables(seq):
    pos = np.arange(seq, dtype=np.float64)
    inv_freq = ROPE_THETA ** (-np.arange(0, MLA_ROPE_DIM, 2, dtype=np.float64) / MLA_ROPE_DIM)
    ang = pos[:, None] * inv_freq[None, :]
    cos, sin = np.cos(ang), np.sin(ang)
    pad = np.zeros((seq, LANES - MLA_ROPE_DIM))
    cos_t = np.concatenate([cos, cos, pad], axis=1).astype(np.float32)
    sin_t = np.concatenate([-sin, sin, pad], axis=1).astype(np.float32)
    return jnp.asarray(cos_t), jnp.asarray(sin_t)


def kernel(x, c, w_ada, b_ada, w_in, b_forget, g_q_norm, w_q_up, g_kv_norm, w_kv_up, w_branch_fox,
           w_branch_mla, w_out, ln1_g, ln1_b, w_mlp_up, w_mlp_down, ln2_g, ln2_b):
    n_batch, seq, d = x.shape
    assert w_ada.shape[0] == DEPTH
    m = n_batch * seq
    x2d = x.reshape(m, d)

    mod3 = _ada(c, w_ada[0], b_ada[0]).reshape(n_batch, N_ADA, d)

    small_col = 3 * FOX_WIDTH
    gf_col = small_col + FOX_HEADS + MLA_Q_LORA + MLA_KV_LORA + MLA_ROPE_DIM
    gm_col = gf_col + d
    assert w_in.shape[2] == gm_col + d and small_col == IN_SMALL_TILE * IN_TN and gf_col <= small_col + IN_TN
    proj, narrow_hi, narrow_lo = _inproj(x2d, mod3, w_in[0].T, seq)

    cos_t, sin_t = _rope_tables(seq)
    bf_pad = _pad_cols(b_forget[0].reshape(1, FOX_HEADS), LANES)
    qn, qr, kv, kr, cumt = _prep(
        narrow_hi, narrow_lo, bf_pad, g_q_norm[0].reshape(1, -1), g_kv_norm[0].reshape(1, -1),
        _layout_q_up(w_q_up[0]), w_kv_up[0].astype(BF16), cos_t, sin_t, seq)

    y_fox, w_up_b, w_down_b = _fox_attention(
        proj, cumt.reshape(n_batch, FOX_HEADS // 2, 2, seq), seq, 0, (w_mlp_up[0], w_mlp_down[0]))
    y_mla, w_bf_b, w_bm_b, w_o_b = _mla_attention(
        qn, qr, kv, kr, seq, (w_branch_fox[0], w_branch_mla[0], w_out[0]))

    x1 = _mix(y_fox, y_mla, proj, gf_col, gm_col, x2d, mod3, w_bf_b, w_bm_b, w_o_b,
              ln1_g[0].reshape(1, d), ln1_b[0].reshape(1, d), seq)
    x2 = _mlp(x1, mod3, w_up_b, w_down_b, ln2_g[0].reshape(1, d), ln2_b[0].reshape(1, d), seq)
    return x2.reshape(n_batch, seq, d)
```

```python
import functools
import math

import jax
import jax.numpy as jnp
import numpy as np
from jax import lax
from jax.experimental import pallas as pl
from jax.experimental.pallas import tpu as pltpu

F32 = jnp.float32
BF16 = jnp.bfloat16

FOX_HEADS = 16
FOX_HEAD_DIM = 64
FOX_WIDTH = FOX_HEADS * FOX_HEAD_DIM
MLA_HEADS = 8
MLA_NOPE_DIM = 128
MLA_ROPE_DIM = 64
MLA_V_DIM = 128
MLA_Q_LORA = 512
MLA_KV_LORA = 256
MLA_QK_DIM = MLA_NOPE_DIM + MLA_ROPE_DIM
CHUNK = 64
ROPE_THETA = 10000.0
LN_EPS = 1e-5
RMS_EPS = 1e-6
N_ADA = 6
DEPTH = 1
ALPHA = (2.0 * DEPTH) ** 0.25

LANES = 128
LOG2E = math.log2(math.e)
FOX_QSCALE = FOX_HEAD_DIM ** -0.5 * LOG2E
MLA_QSCALE = MLA_QK_DIM ** -0.5 * LOG2E
NEG_BIG = -1e30
VMEM_LIMIT = 56 * 1024 * 1024

SUB = 128
HALF = 256


def _cparams(semantics):
    return pltpu.CompilerParams(dimension_semantics=semantics, vmem_limit_bytes=VMEM_LIMIT)


def _layer_norm(y, g, b):
    mu = jnp.mean(y, axis=-1, keepdims=True)
    yc = y - mu
    var = jnp.mean(yc * yc, axis=-1, keepdims=True)
    return yc * lax.rsqrt(var + LN_EPS) * g + b


def _rms_norm(c, g):
    return c * lax.rsqrt(jnp.mean(c * c, axis=-1, keepdims=True) + RMS_EPS) * g


def _ada_kernel(ct_ref, w_ref, b_ref, o_ref, *, n_batch, k_chunk):
    d = w_ref.shape[0]
    ct = ct_ref[...]
    s = ct * jax.nn.sigmoid(ct)
    accs = [b_ref[...] for _ in range(n_batch)]
    for kc in range(d // k_chunk):
        rows = slice(kc * k_chunk, (kc + 1) * k_chunk)
        w = w_ref[rows, :]
        for b in range(n_batch):
            accs[b] = accs[b] + jnp.sum(w * s[rows, b:b + 1], axis=0, keepdims=True)
    for b in range(n_batch):
        o_ref[b:b + 1, :] = accs[b]


def _ada(c, w_ada, b_ada):
    n_batch, d = c.shape
    n = w_ada.shape[1]
    tn = 1024
    return pl.pallas_call(
        functools.partial(_ada_kernel, n_batch=n_batch, k_chunk=256),
        grid=(n // tn,),
        in_specs=[
            pl.BlockSpec((d, n_batch), lambda j: (0, 0)),
            pl.BlockSpec((d, tn), lambda j: (0, j)),
            pl.BlockSpec((1, tn), lambda j: (0, j)),
        ],
        out_specs=pl.BlockSpec((n_batch, tn), lambda j: (0, j)),
        out_shape=jax.ShapeDtypeStruct((n_batch, n), F32),
        compiler_params=_cparams(("arbitrary",)),
        name="ada",
    )(c.T, w_ada, b_ada.reshape(1, n))


IN_TN = 1024
IN_SMALL_TILE = 3 * FOX_WIDTH // IN_TN
NARROW_F = (0, LANES)
NARROW_CQ = (NARROW_F[1], NARROW_F[1] + MLA_Q_LORA)
NARROW_CKV = (NARROW_CQ[1], NARROW_CQ[1] + MLA_KV_LORA)
NARROW_KR = (NARROW_CKV[1], NARROW_CKV[1] + LANES)
assert NARROW_KR[1] == IN_TN and 2 * MLA_ROPE_DIM == LANES


def _narrow_segments(acc):
    c0 = FOX_HEADS
    c1 = c0 + MLA_Q_LORA
    c2 = c1 + MLA_KV_LORA
    c3 = c2 + MLA_ROPE_DIM
    half = MLA_ROPE_DIM // 2
    return jnp.concatenate([acc[:, 0:LANES], acc[:, c0:c1], acc[:, c1:c2], acc[:, c2:c3],
                            acc[:, c2 + half:c3], acc[:, c2:c2 + half]], axis=1)


def _inproj_kernel(x_ref, mod_ref, w_ref, o_ref, hi_ref, lo_ref, u_ref, *, n_cols):
    j = pl.program_id(1)
    last = pl.num_programs(1) - 1

    def product(wt, u=None):
        u = u_ref[...] if u is None else u
        return lax.dot_general(u, wt.astype(BF16), (((1,), (1,)), ((), ())), preferred_element_type=F32)

    @pl.when(j == 0)
    def _():
        shift = mod_ref[0:1, :]
        scale = mod_ref[1:2, :]
        u = (x_ref[...] * (1.0 + scale) + shift).astype(BF16)
        u_ref[...] = u
        o_ref[...] = (product(w_ref[...], u) * FOX_QSCALE).astype(o_ref.dtype)

    @pl.when(j == IN_SMALL_TILE)
    def _():
        acc = product(w_ref[...])
        o_ref[...] = acc.astype(o_ref.dtype)
        narrow = _narrow_segments(acc)
        hi = narrow.astype(hi_ref.dtype)
        hi_ref[...] = hi
        lo_ref[...] = (narrow - hi.astype(F32)).astype(lo_ref.dtype)

    @pl.when(j == last)
    def _():
        row = lax.broadcasted_iota(jnp.int32, w_ref.shape, 0)
        wt = jnp.where(row < n_cols - last * IN_TN, w_ref[...], 0.0)
        o_ref[...] = product(wt).astype(o_ref.dtype)

    @pl.when((j != 0) & (j != IN_SMALL_TILE) & (j != last))
    def _():
        o_ref[...] = product(w_ref[...]).astype(o_ref.dtype)


def _inproj(x2d, mod3, w_in_t, seq):
    m, d = x2d.shape
    n_cols = w_in_t.shape[0]
    tm = 1024
    n_tiles = pl.cdiv(n_cols, IN_TN)
    tiles_per_batch = seq // tm
    return pl.pallas_call(
        functools.partial(_inproj_kernel, n_cols=n_cols),
        grid=(m // tm, n_tiles),
        in_specs=[
            pl.BlockSpec((tm, d), lambda i, j: (i, 0)),
            pl.BlockSpec((None, N_ADA, d), lambda i, j: (i // tiles_per_batch, 0, 0)),
            pl.BlockSpec((IN_TN, d), lambda i, j: (j, 0)),
        ],
        out_specs=[pl.BlockSpec((tm, IN_TN), lambda i, j: (i, j)),
                   pl.BlockSpec((tm, IN_TN), lambda i, j: (i, 0)),
                   pl.BlockSpec((tm, IN_TN), lambda i, j: (i, 0))],
        out_shape=[jax.ShapeDtypeStruct((m, n_tiles * IN_TN), BF16),
                   jax.ShapeDtypeStruct((m, IN_TN), BF16),
                   jax.ShapeDtypeStruct((m, IN_TN), BF16)],
        scratch_shapes=[pltpu.VMEM((tm, d), BF16)],
        compiler_params=_cparams(("parallel", "arbitrary")),
        name="inproj",
    )(x2d, mod3, w_in_t)


def _log_sigmoid(x):
    return jnp.minimum(x, 0.0) - jnp.log1p(jnp.exp(-jnp.abs(x)))


def _prep_kernel(hi_ref, lo_ref, bf_ref, gq_ref, gkv_ref, wq_ref, wkv_ref, cos_ref, sin_ref,
                 qn_ref, qr_ref, kv_ref, kr_ref, cumt_ref, carry_ref, *, tiles_per_batch):
    i = pl.program_id(0)
    tm = hi_ref.shape[0]
    ps = hi_ref[...].astype(F32) + lo_ref[...].astype(F32)
    k_rope = ps[:, NARROW_KR[0]:NARROW_KR[1]]
    k_rope_partner = pltpu.roll(k_rope, MLA_ROPE_DIM, axis=1)
    cos = cos_ref[...]
    sin = sin_ref[...]

    logf = _log_sigmoid(ps[:, NARROW_F[0]:NARROW_F[1]] + bf_ref[...]) * LOG2E
    hi = logf.astype(BF16)
    r1 = logf - hi.astype(F32)
    mid = r1.astype(BF16)
    lo = (r1 - mid.astype(F32)).astype(BF16)
    row = lax.broadcasted_iota(jnp.int32, (tm, tm), 0)
    col = lax.broadcasted_iota(jnp.int32, (tm, tm), 1)
    tri = (row >= col).astype(BF16)
    cs = (jnp.dot(tri, hi, preferred_element_type=F32)
          + jnp.dot(tri, mid, preferred_element_type=F32)
          + jnp.dot(tri, lo, preferred_element_type=F32))

    @pl.when(i % tiles_per_batch == 0)
    def _():
        carry_ref[...] = jnp.zeros_like(carry_ref)

    cum = cs + carry_ref[...]
    carry_ref[...] = cum[tm - 1:tm, :]
    cumt_ref[...] = cum.T[0:FOX_HEADS, :]

    cqn = _rms_norm(ps[:, NARROW_CQ[0]:NARROW_CQ[1]], gq_ref[...]).astype(BF16)
    qa = jnp.dot(cqn, wq_ref[...], preferred_element_type=F32)
    hw = MLA_HEADS * LANES
    qn_ref[...] = (qa[:, 0:hw] * MLA_QSCALE).astype(BF16)
    for h in range(MLA_HEADS):
        a = qa[:, hw + h * LANES: hw + (h + 1) * LANES]
        b = qa[:, 2 * hw + h * LANES: 2 * hw + (h + 1) * LANES]
        qr_ref[:, h * LANES:(h + 1) * LANES] = ((a * cos + b * sin) * MLA_QSCALE).astype(BF16)

    ckvn = _rms_norm(ps[:, NARROW_CKV[0]:NARROW_CKV[1]], gkv_ref[...]).astype(BF16)
    kv_ref[...] = jnp.dot(ckvn, wkv_ref[...], preferred_element_type=F32).astype(BF16)
    kr_ref[...] = (k_rope * cos + k_rope_partner * sin).astype(BF16)


def _prep(narrow_hi, narrow_lo, bf_pad, g_q, g_kv, w_q_all, w_kv, cos_t, sin_t, seq):
    m = narrow_hi.shape[0]
    n_batch = m // seq
    tm = 512
    tiles_per_batch = seq // tm
    hw = MLA_HEADS * LANES
    const = lambda i: (0, 0)
    return pl.pallas_call(
        functools.partial(_prep_kernel, tiles_per_batch=tiles_per_batch),
        grid=(m // tm,),
        in_specs=[
            pl.BlockSpec((tm, IN_TN), lambda i: (i, 0)),
            pl.BlockSpec((tm, IN_TN), lambda i: (i, 0)),
            pl.BlockSpec(bf_pad.shape, const),
            pl.BlockSpec(g_q.shape, const),
            pl.BlockSpec(g_kv.shape, const),
            pl.BlockSpec(w_q_all.shape, const),
            pl.BlockSpec(w_kv.shape, const),
            pl.BlockSpec((tm, LANES), lambda i: (i % tiles_per_batch, 0)),
            pl.BlockSpec((tm, LANES), lambda i: (i % tiles_per_batch, 0)),
        ],
        out_specs=[
            pl.BlockSpec((tm, hw), lambda i: (i, 0)),
            pl.BlockSpec((tm, hw), lambda i: (i, 0)),
            pl.BlockSpec((tm, w_kv.shape[1]), lambda i: (i, 0)),
            pl.BlockSpec((tm, LANES), lambda i: (i, 0)),
            pl.BlockSpec((None, FOX_HEADS, tm), lambda i: (i // tiles_per_batch, 0, i % tiles_per_batch)),
        ],
        out_shape=[
            jax.ShapeDtypeStruct((m, hw), BF16),
            jax.ShapeDtypeStruct((m, hw), BF16),
            jax.ShapeDtypeStruct((m, w_kv.shape[1]), BF16),
            jax.ShapeDtypeStruct((m, LANES), BF16),
            jax.ShapeDtypeStruct((n_batch, FOX_HEADS, seq), F32),
        ],
        scratch_shapes=[pltpu.VMEM((1, LANES), F32)],
        compiler_params=_cparams(("arbitrary",)),
        name="prep",
    )(narrow_hi, narrow_lo, bf_pad, g_q, g_kv, w_q_all, w_kv, cos_t, sin_t)


def _split3(x):
    hi = x.astype(BF16).astype(F32)
    r = x - hi
    mid = r.astype(BF16).astype(F32)
    lo = (r - mid).astype(BF16).astype(F32)
    return hi, mid, lo


def _online_softmax_step(t, m, acc, vt_aug):
    m_new = jnp.maximum(m, jnp.max(t, axis=0, keepdims=True))
    alpha = jnp.exp2(m - m_new)
    p = jnp.exp2(t - m_new).astype(BF16)
    pv = jnp.dot(vt_aug, p, preferred_element_type=F32)
    return m_new, alpha * acc + pv[0:acc.shape[0], :]


def _causal_mask(t, key0, query0, granule):
    r = lax.broadcasted_iota(jnp.int32, t.shape, 0) + key0
    c = lax.broadcasted_iota(jnp.int32, t.shape, 1) + query0
    return jnp.where(r <= (c | (granule - 1)), t, NEG_BIG)


def _flash_key_major(qi, tile, chains, keys, rhs, vt_aug, d_v, granule, finish):
    acc_rows = d_v + 8
    n_rounds = tile // SUB
    pairs = [(n, n + 1) for n in range(0, len(chains), 2)]

    def qk_of(k0, is_diag):
        loaded = {}

        def product(n):
            half, hh = chains[n]
            n_keys = (half + 1) * HALF if is_diag else tile
            if (hh, n_keys) not in loaded:
                loaded[hh, n_keys] = keys(hh, k0, n_keys)
            return jnp.dot(loaded[hh, n_keys], rhs(hh)[:, half * HALF:(half + 1) * HALF],
                           preferred_element_type=F32)
        return product

    def visit(state, tiles):
        st = list(state)
        products = [qk_of(*t) for t in tiles]
        phases = [(ti, pair) for ti in range(len(tiles)) for pair in pairs]

        def issue(phase):
            ti, pair = phase
            return {n: products[ti](n) for n in pair}

        cur = issue(phases[0])
        for idx, (ti, pair) in enumerate(phases):
            nxt = issue(phases[idx + 1]) if idx + 1 < len(phases) else None
            k0, is_diag = tiles[ti]
            for sb in range(n_rounds):
                ks = pl.multiple_of(k0 + sb * SUB, SUB)
                for n in pair:
                    half, hh = chains[n]
                    if is_diag and sb * SUB >= (half + 1) * HALF:
                        continue
                    t = cur[n][sb * SUB:(sb + 1) * SUB, :]
                    if is_diag and (sb + 1) * SUB > half * HALF:
                        t = _causal_mask(t, sb * SUB, half * HALF, granule)
                    st[n] = _online_softmax_step(t, *st[n], vt_aug(hh, ks))
            cur = nxt
        return tuple(st)

    def two_full(j, state):
        k0 = pl.multiple_of(2 * j * tile, tile)
        return visit(state, [(k0, False), (pl.multiple_of(k0 + tile, tile), False)])

    init = tuple((jnp.full((1, HALF), NEG_BIG, F32), jnp.zeros((acc_rows, HALF), F32)) for _ in chains)
    state = lax.fori_loop(0, qi // 2, two_full, init)
    q0 = pl.multiple_of(qi * tile, tile)
    def last_tiles(tiles):
        def branch(st):
            outs = [acc[0:d_v, :] * (1.0 / acc[d_v:d_v + 1, :]) for (_, acc) in visit(st, tiles)]
            return finish(outs)
        return branch

    return lax.cond(qi % 2 == 1,
                    last_tiles([(pl.multiple_of(q0 - tile, tile), False), (q0, True)]),
                    last_tiles([(q0, True)]),
                    state)


def _cast_specs(weights, grid):
    n_steps = math.prod(grid)
    specs, shapes = [], []
    for w in weights:
        rows, cols = w.shape
        slab = rows // n_steps
        assert slab * n_steps == rows and slab % 16 == 0, (w.shape, n_steps)
        specs.append(pl.BlockSpec((slab, cols), lambda a, b: (a * grid[1] + b, 0)))
        shapes.append(jax.ShapeDtypeStruct(w.shape, BF16))
    return specs, shapes


def _cast_slabs(refs):
    n = len(refs) // 2
    for src, dst in zip(refs[:n], refs[n:]):
        dst[...] = src[...].astype(dst.dtype)


def _stack_chain_outputs(outs, n_half):
    return jnp.concatenate(
        [jnp.concatenate([outs[half * 2 + hh] for half in range(n_half)], axis=1) for hh in (0, 1)], axis=0)


AUG_ROWS = 16


def _bias_rows(n, entries):
    r = lax.broadcasted_iota(jnp.int32, (AUG_ROWS, n), 0)
    out = jnp.zeros((AUG_ROWS, n), F32)
    for row, val in entries.items():
        out = jnp.where(r == row, val, out)
    return out.astype(BF16)


def _fox_kernel(q_ref, k_ref, v_ref, cumt_ref, *rest, tq, n_cast):
    cast_in, (o_ref, *cast_out), (vta_ref, kaug_ref) = rest[:n_cast], rest[n_cast:2 * n_cast + 1], rest[2 * n_cast + 1:]
    seq = k_ref.shape[0]
    dh = FOX_HEAD_DIM
    n_half = tq // HALF
    aug = 6

    vt = v_ref[...].T
    tail_rows = lax.broadcasted_iota(jnp.int32, (16, seq), 0)
    tail = jnp.where(tail_rows < 8, 1.0, 0.0).astype(BF16)
    for hh in (0, 1):
        vta_ref[hh, 0:dh, :] = vt[hh * dh:(hh + 1) * dh, :]
        vta_ref[hh, dh:dh + 16, :] = tail
    key_rows = {}
    for hh in (0, 1):
        for idx, piece in enumerate(_split3(-cumt_ref[hh:hh + 1, :])):
            key_rows[aug * hh + idx] = piece
            key_rows[aug * hh + 3 + idx] = 1.0
    kaug_t = jnp.concatenate([_bias_rows(seq, key_rows), jnp.zeros((LANES - AUG_ROWS, seq), BF16)], axis=0)
    kaug_ref[...] = kaug_t.T

    chains = [(half, hh) for half in range(n_half) for hh in (0, 1)]

    def keys(hh, k0, n):
        return jnp.concatenate([k_ref[pl.ds(k0, n), :], kaug_ref[pl.ds(k0, n), :]], axis=1)

    def make_rhs(qi):
        q0 = pl.multiple_of(qi * tq, tq)
        qt = q_ref[pl.ds(q0, tq), :].T
        row = lax.broadcasted_iota(jnp.int32, (LANES, tq), 0)
        pad = jnp.zeros((LANES - AUG_ROWS, tq), BF16)
        rhs = []
        for hh in (0, 1):
            qm = jnp.where((row >= hh * dh) & (row < (hh + 1) * dh), qt, jnp.zeros_like(qt))
            query_rows = {}
            for idx, piece in enumerate(_split3(cumt_ref[hh:hh + 1, pl.ds(q0, tq)])):
                query_rows[aug * hh + idx] = 1.0
                query_rows[aug * hh + 3 + idx] = piece
            rhs.append(jnp.concatenate([qm, _bias_rows(tq, query_rows), pad], axis=0))
        return tuple(rhs)

    n_tiles = seq // tq

    def query_tile(qi, rhs):
        def finish(outs):
            q0 = pl.multiple_of(qi * tq, tq)
            o_ref[pl.ds(q0, tq), :] = _stack_chain_outputs(outs, n_half).T.astype(o_ref.dtype)
            return make_rhs(jnp.minimum(qi + 1, n_tiles - 1))

        return _flash_key_major(qi, tq, chains, keys, lambda hh: rhs[hh],
                                lambda hh, ks: vta_ref[hh, :, pl.ds(ks, SUB)], dh, 1, finish)

    lax.fori_loop(0, n_tiles, query_tile, make_rhs(0))
    _cast_slabs(cast_in + tuple(cast_out))


def _fox_attention(proj, cumt4, seq, q_col0, cast_weights):
    m = proj.shape[0]
    n_batch = m // seq
    n_pairs = FOX_HEADS // 2
    qb = q_col0 // LANES
    grid = (n_batch, n_pairs)
    cast_specs, cast_shapes = _cast_specs(cast_weights, grid)
    return pl.pallas_call(
        functools.partial(_fox_kernel, tq=512, n_cast=len(cast_weights)),
        grid=grid,
        in_specs=[
            pl.BlockSpec((seq, LANES), lambda b, hp: (b, qb + hp)),
            pl.BlockSpec((seq, LANES), lambda b, hp: (b, qb + n_pairs + hp)),
            pl.BlockSpec((seq, LANES), lambda b, hp: (b, qb + 2 * n_pairs + hp)),
            pl.BlockSpec((None, None, 2, seq), lambda b, hp: (b, hp, 0, 0)),
        ] + cast_specs,
        out_specs=[pl.BlockSpec((seq, LANES), lambda b, hp: (b, hp))] + cast_specs,
        out_shape=[jax.ShapeDtypeStruct((m, FOX_WIDTH), BF16)] + cast_shapes,
        scratch_shapes=[
            pltpu.VMEM((2, FOX_HEAD_DIM + 16, seq), BF16),
            pltpu.VMEM((seq, LANES), BF16),
        ],
        compiler_params=_cparams(("parallel", "parallel")),
        name="fox_attn",
    )(proj, proj, proj, cumt4, *cast_weights)


def _mla_kernel(qn_ref, qr_ref, kv_ref, kr_ref, *rest, tq, n_cast):
    cast_in, (o_ref, *cast_out), (vta_ref,) = rest[:n_cast], rest[n_cast:2 * n_cast + 1], rest[2 * n_cast + 1:]
    seq = kv_ref.shape[0]
    dv = MLA_V_DIM
    n_half = tq // HALF

    tail_rows = lax.broadcasted_iota(jnp.int32, (16, seq), 0)
    tail = jnp.where(tail_rows < 8, 1.0, 0.0).astype(BF16)
    for hh in (0, 1):
        vta_ref[hh, 0:dv, :] = kv_ref[:, (2 * hh + 1) * LANES:(2 * hh + 2) * LANES].T
        vta_ref[hh, dv:dv + 16, :] = tail

    chains = [(half, hh) for half in range(n_half) for hh in (0, 1)]

    def keys(hh, k0, n):
        return jnp.concatenate([kv_ref[pl.ds(k0, n), 2 * hh * LANES:(2 * hh + 1) * LANES],
                                kr_ref[pl.ds(k0, n), :]], axis=1)

    def make_rhs(qi):
        q0 = pl.multiple_of(qi * tq, tq)
        return tuple(jnp.concatenate([qn_ref[pl.ds(q0, tq), hh * LANES:(hh + 1) * LANES],
                                      qr_ref[pl.ds(q0, tq), hh * LANES:(hh + 1) * LANES]], axis=1).T
                     for hh in (0, 1))

    n_tiles = seq // tq

    def query_tile(qi, rhs):
        def finish(outs):
            q0 = pl.multiple_of(qi * tq, tq)
            o_ref[pl.ds(q0, tq), :] = _stack_chain_outputs(outs, n_half).T.astype(o_ref.dtype)
            return make_rhs(jnp.minimum(qi + 1, n_tiles - 1))

        return _flash_key_major(qi, tq, chains, keys, lambda hh: rhs[hh],
                                lambda hh, ks: vta_ref[hh, :, pl.ds(ks, SUB)], dv, CHUNK, finish)

    lax.fori_loop(0, n_tiles, query_tile, make_rhs(0))
    _cast_slabs(cast_in + tuple(cast_out))


def _mla_attention(qn, qr, kv, kr, seq, cast_weights):
    m = qn.shape[0]
    n_batch = m // seq
    pair = 2 * LANES
    grid = (n_batch, MLA_HEADS // 2)
    cast_specs, cast_shapes = _cast_specs(cast_weights, grid)
    return pl.pallas_call(
        functools.partial(_mla_kernel, tq=512, n_cast=len(cast_weights)),
        grid=grid,
        in_specs=[
            pl.BlockSpec((seq, pair), lambda b, hp: (b, hp)),
            pl.BlockSpec((seq, pair), lambda b, hp: (b, hp)),
            pl.BlockSpec((seq, 2 * pair), lambda b, hp: (b, hp)),
            pl.BlockSpec((seq, LANES), lambda b, hp: (b, 0)),
        ] + cast_specs,
        out_specs=[pl.BlockSpec((seq, pair), lambda b, hp: (b, hp))] + cast_specs,
        out_shape=[jax.ShapeDtypeStruct((m, MLA_HEADS * MLA_V_DIM), BF16)] + cast_shapes,
        scratch_shapes=[pltpu.VMEM((2, MLA_V_DIM + 16, seq), BF16)],
        compiler_params=_cparams(("parallel", "parallel")),
        name="mla_attn",
    )(qn, qr, kv, kr, *cast_weights)


MIX_LN_CHUNKS = 2


def _mix_kernel(yf_ref, ym_ref, gf_ref, gm_ref, x_ref, mod_ref, wbf_ref, wbm_ref, wo_ref, g_ref, b_ref,
                o_ref, mg_ref, *, n_chunk, g_off):
    d = o_ref.shape[1]
    yf = yf_ref[...]
    ym = ym_ref[...]

    def gate(ref, c):
        window = ref[:, c * n_chunk:(c + 1) * n_chunk + LANES]
        return jax.nn.sigmoid(window[:, g_off:g_off + n_chunk].astype(F32))

    for c in range(d // n_chunk):
        cols = slice(c * n_chunk, (c + 1) * n_chunk)
        a = jnp.dot(yf, wbf_ref[:, cols], preferred_element_type=F32)
        b = jnp.dot(ym, wbm_ref[:, cols], preferred_element_type=F32)
        merged = gate(gf_ref, c) * a + gate(gm_ref, c) * b
        mg_ref[:, cols] = merged.astype(BF16)
    gate1 = mod_ref[2:3, :]
    chunk = o_ref.shape[0] // MIX_LN_CHUNKS
    for r in range(MIX_LN_CHUNKS):
        rows = slice(r * chunk, (r + 1) * chunk)
        mix = jnp.dot(mg_ref[rows, :], wo_ref[...], preferred_element_type=F32)
        o_ref[rows, :] = _layer_norm(ALPHA * x_ref[rows, :] + gate1 * mix, g_ref[...], b_ref[...])


def _mix(y_fox, y_mla, proj, gf_col, gm_col, x2d, mod3, w_bf, w_bm, w_o, ln_g, ln_b, seq):
    m, d = x2d.shape
    tm = 512
    tiles_per_batch = seq // tm
    const = lambda i: (0, 0)
    resident = dict(pipeline_mode=pl.Buffered(1))
    g_off = gf_col % LANES
    assert gm_col % LANES == g_off
    g_window = (pl.Element(tm), pl.Element(d + LANES))

    def window_at(col):
        return lambda i: (pl.multiple_of(i * tm, tm), col - g_off)

    return pl.pallas_call(
        functools.partial(_mix_kernel, n_chunk=512, g_off=g_off),
        grid=(m // tm,),
        in_specs=[
            pl.BlockSpec((tm, y_fox.shape[1]), lambda i: (i, 0)),
            pl.BlockSpec((tm, y_mla.shape[1]), lambda i: (i, 0)),
            pl.BlockSpec(g_window, window_at(gf_col)),
            pl.BlockSpec(g_window, window_at(gm_col)),
            pl.BlockSpec((tm, d), lambda i: (i, 0)),
            pl.BlockSpec((None, N_ADA, d), lambda i: (i // tiles_per_batch, 0, 0)),
            pl.BlockSpec(w_bf.shape, const, **resident),
            pl.BlockSpec(w_bm.shape, const, **resident),
            pl.BlockSpec(w_o.shape, const, **resident),
            pl.BlockSpec((1, d), const),
            pl.BlockSpec((1, d), const),
        ],
        out_specs=pl.BlockSpec((tm, d), lambda i: (i, 0)),
        out_shape=jax.ShapeDtypeStruct((m, d), F32),
        scratch_shapes=[pltpu.VMEM((tm, d), BF16)],
        compiler_params=_cparams(("parallel",)),
        name="mix",
    )(y_fox, y_mla, proj, proj, x2d, mod3, w_bf, w_bm, w_o, ln_g, ln_b)


MLP_LN_CHUNKS = 4


def _mlp_kernel(x_ref, mod_ref, wu_ref, wd_ref, g_ref, b_ref, o_ref, u_ref):
    f = pl.program_id(1)

    def partial_sum(u):
        z = jnp.dot(u, wu_ref[...], preferred_element_type=F32)
        h = jnp.square(jnp.maximum(z, 0.0)).astype(BF16)
        return jnp.dot(h, wd_ref[...], preferred_element_type=F32)

    @pl.when(f == 0)
    def _():
        shift = mod_ref[3:4, :]
        scale = mod_ref[4:5, :]
        u = (x_ref[...] * (1.0 + scale) + shift).astype(BF16)
        u_ref[...] = u
        o_ref[...] = partial_sum(u)

    last = pl.num_programs(1) - 1

    @pl.when((f != 0) & (f != last))
    def _():
        o_ref[...] += partial_sum(u_ref[...])

    @pl.when(f == last)
    def _():
        gate = mod_ref[5:6, :]
        chunk = o_ref.shape[0] // MLP_LN_CHUNKS
        for r in range(MLP_LN_CHUNKS):
            rows = slice(r * chunk, (r + 1) * chunk)
            h = o_ref[rows, :] + partial_sum(u_ref[rows, :])
            o_ref[rows, :] = _layer_norm(ALPHA * x_ref[rows, :] + gate * h, g_ref[...], b_ref[...])


def _mlp(x1, mod3, w_up, w_down, ln_g, ln_b, seq):
    m, d = x1.shape
    d_ff = w_up.shape[1]
    tm, tf = 1024, 512
    tiles_per_batch = seq // tm
    return pl.pallas_call(
        _mlp_kernel,
        grid=(m // tm, d_ff // tf),
        in_specs=[
            pl.BlockSpec((tm, d), lambda i, f: (i, 0)),
            pl.BlockSpec((None, N_ADA, d), lambda i, f: (i // tiles_per_batch, 0, 0)),
            pl.BlockSpec((d, tf), lambda i, f: (0, f)),
            pl.BlockSpec((tf, d), lambda i, f: (f, 0)),
            pl.BlockSpec((1, d), lambda i, f: (0, 0)),
            pl.BlockSpec((1, d), lambda i, f: (0, 0)),
        ],
        out_specs=pl.BlockSpec((tm, d), lambda i, f: (i, 0)),
        out_shape=jax.ShapeDtypeStruct((m, d), F32),
        scratch_shapes=[pltpu.VMEM((tm, d), BF16)],
        compiler_params=_cparams(("parallel", "arbitrary")),
        name="mlp",
    )(x1, mod3, w_up, w_down, ln_g, ln_b)


def _pad_cols(w, width):
    return jnp.pad(w, ((0, 0), (0, width - w.shape[1])))


def _swap_halves(w):
    half = w.shape[-1] // 2
    return jnp.concatenate([w[..., half:], w[..., :half]], axis=-1)


def _layout_q_up(w_q_up):
    r = w_q_up.shape[0]
    w3 = w_q_up.reshape(r, MLA_HEADS, MLA_QK_DIM)
    w_nope = w3[:, :, :MLA_NOPE_DIM].reshape(r, MLA_HEADS * MLA_NOPE_DIM)
    w_rope = w3[:, :, MLA_NOPE_DIM:]
    pad = ((0, 0), (0, 0), (0, LANES - MLA_ROPE_DIM))
    w_r = jnp.pad(w_rope, pad).reshape(r, MLA_HEADS * LANES)
    w_rs = jnp.pad(_swap_halves(w_rope), pad).reshape(r, MLA_HEADS * LANES)
    return jnp.concatenate([w_nope, w_r, w_rs], axis=1).astype(BF16)


def _rope_tables(seq):
    pos = np.arange(seq, dtype=np.float64)
    inv_freq = ROPE_THETA ** (-np.arange(0, MLA_ROPE_DIM, 2, dtype=np.float64) / MLA_ROPE_DIM)
    ang = pos[:, None] * inv_freq[None, :]
    cos, sin = np.cos(ang), np.sin(ang)
    pad = np.zeros((seq, LANES - MLA_ROPE_DIM))
    cos_t = np.concatenate([cos, cos, pad], axis=1).astype(np.float32)
    sin_t = np.concatenate([-sin, sin, pad], axis=1).astype(np.float32)
    return jnp.asarray(cos_t), jnp.asarray(sin_t)


def kernel(x, c, w_ada, b_ada, w_in, b_forget, g_q_norm, w_q_up, g_kv_norm, w_kv_up, w_branch_fox,
           w_branch_mla, w_out, ln1_g, ln1_b, w_mlp_up, w_mlp_down, ln2_g, ln2_b):
    n_batch, seq, d = x.shape
    assert w_ada.shape[0] == DEPTH
    m = n_batch * seq
    x2d = x.reshape(m, d)

    mod3 = _ada(c, w_ada[0], b_ada[0]).reshape(n_batch, N_ADA, d)

    small_col = 3 * FOX_WIDTH
    gf_col = small_col + FOX_HEADS + MLA_Q_LORA + MLA_KV_LORA + MLA_ROPE_DIM
    gm_col = gf_col + d
    assert w_in.shape[2] == gm_col + d and small_col == IN_SMALL_TILE * IN_TN and gf_col <= small_col + IN_TN
    proj, narrow_hi, narrow_lo = _inproj(x2d, mod3, w_in[0].T, seq)

    cos_t, sin_t = _rope_tables(seq)
    bf_pad = _pad_cols(b_forget[0].reshape(1, FOX_HEADS), LANES)
    qn, qr, kv, kr, cumt = _prep(
        narrow_hi, narrow_lo, bf_pad, g_q_norm[0].reshape(1, -1), g_kv_norm[0].reshape(1, -1),
        _layout_q_up(w_q_up[0]), w_kv_up[0].astype(BF16), cos_t, sin_t, seq)

    y_fox, w_up_b, w_down_b = _fox_attention(
        proj, cumt.reshape(n_batch, FOX_HEADS // 2, 2, seq), seq, 0, (w_mlp_up[0], w_mlp_down[0]))
    y_mla, w_bf_b, w_bm_b, w_o_b = _mla_attention(
        qn, qr, kv, kr, seq, (w_branch_fox[0], w_branch_mla[0], w_out[0]))

    x1 = _mix(y_fox, y_mla, proj, gf_col, gm_col, x2d, mod3, w_bf_b, w_bm_b, w_o_b,
              ln1_g[0].reshape(1, d), ln1_b[0].reshape(1, d), seq)
    x2 = _mlp(x1, mod3, w_up_b, w_down_b, ln2_g[0].reshape(1, d), ln2_b[0].reshape(1, d), seq)
    return x2.reshape(n_batch, seq, d)
```

```python
import functools
import math

import jax
import jax.numpy as jnp
import numpy as np
from jax import lax
from jax.experimental import pallas as pl
from jax.experimental.pallas import tpu as pltpu

F32 = jnp.float32
BF16 = jnp.bfloat16

FOX_HEADS = 16
FOX_HEAD_DIM = 64
FOX_WIDTH = FOX_HEADS * FOX_HEAD_DIM
MLA_HEADS = 8
MLA_NOPE_DIM = 128
MLA_ROPE_DIM = 64
MLA_V_DIM = 128
MLA_Q_LORA = 512
MLA_KV_LORA = 256
MLA_QK_DIM = MLA_NOPE_DIM + MLA_ROPE_DIM
CHUNK = 64
ROPE_THETA = 10000.0
LN_EPS = 1e-5
RMS_EPS = 1e-6
N_ADA = 6
DEPTH = 1
ALPHA = (2.0 * DEPTH) ** 0.25

LANES = 128
LOG2E = math.log2(math.e)
FOX_QSCALE = FOX_HEAD_DIM ** -0.5 * LOG2E
MLA_QSCALE = MLA_QK_DIM ** -0.5 * LOG2E
NEG_BIG = -1e30
VMEM_LIMIT = 56 * 1024 * 1024

SUB = 128
HALF = 256


def _cparams(semantics):
    return pltpu.CompilerParams(dimension_semantics=semantics, vmem_limit_bytes=VMEM_LIMIT)


def _layer_norm(y, g, b):
    mu = jnp.mean(y, axis=-1, keepdims=True)
    yc = y - mu
    var = jnp.mean(yc * yc, axis=-1, keepdims=True)
    return yc * lax.rsqrt(var + LN_EPS) * g + b


def _rms_norm(c, g):
    return c * lax.rsqrt(jnp.mean(c * c, axis=-1, keepdims=True) + RMS_EPS) * g


def _ada_kernel(ct_ref, w_ref, b_ref, o_ref, *, n_batch, k_chunk):
    d = w_ref.shape[0]
    ct = ct_ref[...]
    s = ct * jax.nn.sigmoid(ct)
    accs = [b_ref[...] for _ in range(n_batch)]
    for kc in range(d // k_chunk):
        rows = slice(kc * k_chunk, (kc + 1) * k_chunk)
        w = w_ref[rows, :]
        for b in range(n_batch):
            accs[b] = accs[b] + jnp.sum(w * s[rows, b:b + 1], axis=0, keepdims=True)
    for b in range(n_batch):
        o_ref[b:b + 1, :] = accs[b]


def _ada(c, w_ada, b_ada):
    n_batch, d = c.shape
    n = w_ada.shape[1]
    tn = 1024
    return pl.pallas_call(
        functools.partial(_ada_kernel, n_batch=n_batch, k_chunk=256),
        grid=(n // tn,),
        in_specs=[
            pl.BlockSpec((d, n_batch), lambda j: (0, 0)),
            pl.BlockSpec((d, tn), lambda j: (0, j)),
            pl.BlockSpec((1, tn), lambda j: (0, j)),
        ],
        out_specs=pl.BlockSpec((n_batch, tn), lambda j: (0, j)),
        out_shape=jax.ShapeDtypeStruct((n_batch, n), F32),
        compiler_params=_cparams(("arbitrary",)),
        name="ada",
    )(c.T, w_ada, b_ada.reshape(1, n))


IN_TN = 1024
IN_SMALL_TILE = 3 * FOX_WIDTH // IN_TN
NARROW_F = (0, LANES)
NARROW_CQ = (NARROW_F[1], NARROW_F[1] + MLA_Q_LORA)
NARROW_CKV = (NARROW_CQ[1], NARROW_CQ[1] + MLA_KV_LORA)
NARROW_KR = (NARROW_CKV[1], NARROW_CKV[1] + LANES)
assert NARROW_KR[1] == IN_TN and 2 * MLA_ROPE_DIM == LANES


def _narrow_segments(acc):
    c0 = FOX_HEADS
    c1 = c0 + MLA_Q_LORA
    c2 = c1 + MLA_KV_LORA
    c3 = c2 + MLA_ROPE_DIM
    half = MLA_ROPE_DIM // 2
    return jnp.concatenate([acc[:, 0:LANES], acc[:, c0:c1], acc[:, c1:c2], acc[:, c2:c3],
                            acc[:, c2 + half:c3], acc[:, c2:c2 + half]], axis=1)


def _inproj_kernel(x_ref, mod_ref, w_ref, o_ref, hi_ref, lo_ref, u_ref, *, n_cols):
    j = pl.program_id(1)
    last = pl.num_programs(1) - 1

    def product(wt, u=None):
        u = u_ref[...] if u is None else u
        return lax.dot_general(u, wt.astype(BF16), (((1,), (1,)), ((), ())), preferred_element_type=F32)

    @pl.when(j == 0)
    def _():
        shift = mod_ref[0:1, :]
        scale = mod_ref[1:2, :]
        u = (x_ref[...] * (1.0 + scale) + shift).astype(BF16)
        u_ref[...] = u
        o_ref[...] = (product(w_ref[...], u) * FOX_QSCALE).astype(o_ref.dtype)

    @pl.when(j == IN_SMALL_TILE)
    def _():
        acc = product(w_ref[...])
        o_ref[...] = acc.astype(o_ref.dtype)
        narrow = _narrow_segments(acc)
        hi = narrow.astype(hi_ref.dtype)
        hi_ref[...] = hi
        lo_ref[...] = (narrow - hi.astype(F32)).astype(lo_ref.dtype)

    @pl.when(j == last)
    def _():
        row = lax.broadcasted_iota(jnp.int32, w_ref.shape, 0)
        wt = jnp.where(row < n_cols - last * IN_TN, w_ref[...], 0.0)
        o_ref[...] = product(wt).astype(o_ref.dtype)

    @pl.when((j != 0) & (j != IN_SMALL_TILE) & (j != last))
    def _():
        o_ref[...] = product(w_ref[...]).astype(o_ref.dtype)


def _inproj(x2d, mod3, w_in_t, seq):
    m, d = x2d.shape
    n_cols = w_in_t.shape[0]
    tm = 1024
    n_tiles = pl.cdiv(n_cols, IN_TN)
    tiles_per_batch = seq // tm
    return pl.pallas_call(
        functools.partial(_inproj_kernel, n_cols=n_cols),
        grid=(m // tm, n_tiles),
        in_specs=[
            pl.BlockSpec((tm, d), lambda i, j: (i, 0)),
            pl.BlockSpec((None, N_ADA, d), lambda i, j: (i // tiles_per_batch, 0, 0)),
            pl.BlockSpec((IN_TN, d), lambda i, j: (j, 0)),
        ],
        out_specs=[pl.BlockSpec((tm, IN_TN), lambda i, j: (i, j)),
                   pl.BlockSpec((tm, IN_TN), lambda i, j: (i, 0)),
                   pl.BlockSpec((tm, IN_TN), lambda i, j: (i, 0))],
        out_shape=[jax.ShapeDtypeStruct((m, n_tiles * IN_TN), BF16),
                   jax.ShapeDtypeStruct((m, IN_TN), BF16),
                   jax.ShapeDtypeStruct((m, IN_TN), BF16)],
        scratch_shapes=[pltpu.VMEM((tm, d), BF16)],
        compiler_params=_cparams(("parallel", "arbitrary")),
        name="inproj",
    )(x2d, mod3, w_in_t)


def _log_sigmoid(x):
    return jnp.minimum(x, 0.0) - jnp.log1p(jnp.exp(-jnp.abs(x)))


def _prep_kernel(hi_ref, lo_ref, bf_ref, gq_ref, gkv_ref, wq_ref, wkv_ref, cos_ref, sin_ref,
                 qn_ref, qr_ref, kv_ref, kr_ref, cumt_ref, carry_ref, *, tiles_per_batch):
    i = pl.program_id(0)
    tm = hi_ref.shape[0]
    ps = hi_ref[...].astype(F32) + lo_ref[...].astype(F32)
    k_rope = ps[:, NARROW_KR[0]:NARROW_KR[1]]
    k_rope_partner = pltpu.roll(k_rope, MLA_ROPE_DIM, axis=1)
    cos = cos_ref[...]
    sin = sin_ref[...]

    logf = _log_sigmoid(ps[:, NARROW_F[0]:NARROW_F[1]] + bf_ref[...]) * LOG2E
    hi = logf.astype(BF16)
    r1 = logf - hi.astype(F32)
    mid = r1.astype(BF16)
    lo = (r1 - mid.astype(F32)).astype(BF16)
    row = lax.broadcasted_iota(jnp.int32, (tm, tm), 0)
    col = lax.broadcasted_iota(jnp.int32, (tm, tm), 1)
    tri = (row >= col).astype(BF16)
    cs = (jnp.dot(tri, hi, preferred_element_type=F32)
          + jnp.dot(tri, mid, preferred_element_type=F32)
          + jnp.dot(tri, lo, preferred_element_type=F32))

    @pl.when(i % tiles_per_batch == 0)
    def _():
        carry_ref[...] = jnp.zeros_like(carry_ref)

    cum = cs + carry_ref[...]
    carry_ref[...] = cum[tm - 1:tm, :]
    cumt_ref[...] = cum.T[0:FOX_HEADS, :]

    cqn = _rms_norm(ps[:, NARROW_CQ[0]:NARROW_CQ[1]], gq_ref[...]).astype(BF16)
    qa = jnp.dot(cqn, wq_ref[...], preferred_element_type=F32)
    hw = MLA_HEADS * LANES
    qn_ref[...] = (qa[:, 0:hw] * MLA_QSCALE).astype(BF16)
    for h in range(MLA_HEADS):
        a = qa[:, hw + h * LANES: hw + (h + 1) * LANES]
        b = qa[:, 2 * hw + h * LANES: 2 * hw + (h + 1) * LANES]
        qr_ref[:, h * LANES:(h + 1) * LANES] = ((a * cos + b * sin) * MLA_QSCALE).astype(BF16)

    ckvn = _rms_norm(ps[:, NARROW_CKV[0]:NARROW_CKV[1]], gkv_ref[...]).astype(BF16)
    kv_ref[...] = jnp.dot(ckvn, wkv_ref[...], preferred_element_type=F32).astype(BF16)
    kr_ref[...] = (k_rope * cos + k_rope_partner * sin).astype(BF16)


def _prep(narrow_hi, narrow_lo, bf_pad, g_q, g_kv, w_q_all, w_kv, cos_t, sin_t, seq):
    m = narrow_hi.shape[0]
    n_batch = m // seq
    tm = 512
    tiles_per_batch = seq // tm
    hw = MLA_HEADS * LANES
    const = lambda i: (0, 0)
    return pl.pallas_call(
        functools.partial(_prep_kernel, tiles_per_batch=tiles_per_batch),
        grid=(m // tm,),
        in_specs=[
            pl.BlockSpec((tm, IN_TN), lambda i: (i, 0)),
            pl.BlockSpec((tm, IN_TN), lambda i: (i, 0)),
            pl.BlockSpec(bf_pad.shape, const),
            pl.BlockSpec(g_q.shape, const),
            pl.BlockSpec(g_kv.shape, const),
            pl.BlockSpec(w_q_all.shape, const),
            pl.BlockSpec(w_kv.shape, const),
            pl.BlockSpec((tm, LANES), lambda i: (i % tiles_per_batch, 0)),
            pl.BlockSpec((tm, LANES), lambda i: (i % tiles_per_batch, 0)),
        ],
        out_specs=[
            pl.BlockSpec((tm, hw), lambda i: (i, 0)),
            pl.BlockSpec((tm, hw), lambda i: (i, 0)),
            pl.BlockSpec((tm, w_kv.shape[1]), lambda i: (i, 0)),
            pl.BlockSpec((tm, LANES), lambda i: (i, 0)),
            pl.BlockSpec((None, FOX_HEADS, tm), lambda i: (i // tiles_per_batch, 0, i % tiles_per_batch)),
        ],
        out_shape=[
            jax.ShapeDtypeStruct((m, hw), BF16),
            jax.ShapeDtypeStruct((m, hw), BF16),
            jax.ShapeDtypeStruct((m, w_kv.shape[1]), BF16),
            jax.ShapeDtypeStruct((m, LANES), BF16),
            jax.ShapeDtypeStruct((n_batch, FOX_HEADS, seq), F32),
        ],
        scratch_shapes=[pltpu.VMEM((1, LANES), F32)],
        compiler_params=_cparams(("arbitrary",)),
        name="prep",
    )(narrow_hi, narrow_lo, bf_pad, g_q, g_kv, w_q_all, w_kv, cos_t, sin_t)


def _split3(x):
    hi = x.astype(BF16).astype(F32)
    r = x - hi
    mid = r.astype(BF16).astype(F32)
    lo = (r - mid).astype(BF16).astype(F32)
    return hi, mid, lo


def _online_softmax_step(t, m, acc, vt_aug):
    m_new = jnp.maximum(m, jnp.max(t, axis=0, keepdims=True))
    alpha = jnp.exp2(m - m_new)
    p = jnp.exp2(t - m_new).astype(BF16)
    pv = jnp.dot(vt_aug, p, preferred_element_type=F32)
    return m_new, alpha * acc + pv[0:acc.shape[0], :]


def _causal_mask(t, key0, query0, granule):
    r = lax.broadcasted_iota(jnp.int32, t.shape, 0) + key0
    c = lax.broadcasted_iota(jnp.int32, t.shape, 1) + query0
    return jnp.where(r <= (c | (granule - 1)), t, NEG_BIG)


Q_TILES = 2
FULL, DIAG, SKIP = "full", "diag", "skip"


def _flash_key_major(j, tile, keys, rhs, vt_aug, d_v, granule, finish):
    acc_rows = d_v + 8
    n_rounds = tile // SUB
    chains = [(qt, half, hh) for qt in range(Q_TILES) for half in range(tile // HALF) for hh in (0, 1)]
    pairs = [(n, n + 1) for n in range(0, len(chains), 2)]

    def qk_of(k0, modes):
        loaded = {}

        def product(n):
            qt, half, hh = chains[n]
            n_keys = (half + 1) * HALF if modes[qt] == DIAG else tile
            if (hh, n_keys) not in loaded:
                loaded[hh, n_keys] = keys(hh, k0, n_keys)
            return jnp.dot(loaded[hh, n_keys], rhs(qt, hh)[:, half * HALF:(half + 1) * HALF],
                           preferred_element_type=F32)
        return product

    def visit(state, tiles):
        st = list(state)
        products = [qk_of(*t) for t in tiles]
        phases = [(ti, pair) for ti, (_, modes) in enumerate(tiles) for pair in pairs
                  if modes[chains[pair[0]][0]] != SKIP]

        def issue(phase):
            ti, pair = phase
            return {n: products[ti](n) for n in pair}

        cur = issue(phases[0])
        for idx, (ti, pair) in enumerate(phases):
            nxt = issue(phases[idx + 1]) if idx + 1 < len(phases) else None
            k0, modes = tiles[ti]
            for sb in range(n_rounds):
                ks = pl.multiple_of(k0 + sb * SUB, SUB)
                for n in pair:
                    qt, half, hh = chains[n]
                    is_diag = modes[qt] == DIAG
                    if is_diag and sb * SUB >= (half + 1) * HALF:
                        continue
                    t = cur[n][sb * SUB:(sb + 1) * SUB, :]
                    if is_diag and (sb + 1) * SUB > half * HALF:
                        t = _causal_mask(t, sb * SUB, half * HALF, granule)
                    st[n] = _online_softmax_step(t, *st[n], vt_aug(hh, ks))
            cur = nxt
        return tuple(st)

    everyone = (FULL,) * Q_TILES

    def two_full(i, state):
        k0 = pl.multiple_of(2 * i * tile, tile)
        return visit(state, [(k0, everyone), (pl.multiple_of(k0 + tile, tile), everyone)])

    assert Q_TILES % 2 == 0
    init = tuple((jnp.full((1, HALF), NEG_BIG, F32), jnp.zeros((acc_rows, HALF), F32)) for _ in chains)
    state = lax.fori_loop(0, j * (Q_TILES // 2), two_full, init)
    q0 = j * (Q_TILES * tile)
    last = [(pl.multiple_of(q0 + t * tile, tile),
             tuple(SKIP if qt < t else DIAG if qt == t else FULL for qt in range(Q_TILES)))
            for t in range(Q_TILES)]
    return finish([acc[0:d_v, :] * (1.0 / acc[d_v:d_v + 1, :]) for (_, acc) in visit(state, last)])


def _cast_specs(weights, grid):
    n_steps = math.prod(grid)
    specs, shapes = [], []
    for w in weights:
        rows, cols = w.shape
        slab = rows // n_steps
        assert slab * n_steps == rows and slab % 16 == 0, (w.shape, n_steps)
        specs.append(pl.BlockSpec((slab, cols), lambda a, b: (a * grid[1] + b, 0)))
        shapes.append(jax.ShapeDtypeStruct(w.shape, BF16))
    return specs, shapes


def _cast_slabs(refs):
    n = len(refs) // 2
    for src, dst in zip(refs[:n], refs[n:]):
        dst[...] = src[...].astype(dst.dtype)


def _stack_chain_outputs(outs, n_half):
    return jnp.concatenate(
        [jnp.concatenate([outs[half * 2 + hh] for half in range(n_half)], axis=1) for hh in (0, 1)], axis=0)


AUG_ROWS = 16


def _bias_rows(n, entries):
    r = lax.broadcasted_iota(jnp.int32, (AUG_ROWS, n), 0)
    out = jnp.zeros((AUG_ROWS, n), F32)
    for row, val in entries.items():
        out = jnp.where(r == row, val, out)
    return out.astype(BF16)


def _fox_kernel(q_ref, k_ref, v_ref, cumt_ref, *rest, tq, n_cast):
    cast_in, (o_ref, *cast_out), (vta_ref, kaug_ref) = rest[:n_cast], rest[n_cast:2 * n_cast + 1], rest[2 * n_cast + 1:]
    seq = k_ref.shape[0]
    dh = FOX_HEAD_DIM
    n_half = tq // HALF
    aug = 6

    vt = v_ref[...].T
    tail_rows = lax.broadcasted_iota(jnp.int32, (16, seq), 0)
    tail = jnp.where(tail_rows < 8, 1.0, 0.0).astype(BF16)
    for hh in (0, 1):
        vta_ref[hh, 0:dh, :] = vt[hh * dh:(hh + 1) * dh, :]
        vta_ref[hh, dh:dh + 16, :] = tail
    key_rows = {}
    for hh in (0, 1):
        for idx, piece in enumerate(_split3(-cumt_ref[hh:hh + 1, :])):
            key_rows[aug * hh + idx] = piece
            key_rows[aug * hh + 3 + idx] = 1.0
    kaug_t = jnp.concatenate([_bias_rows(seq, key_rows), jnp.zeros((LANES - AUG_ROWS, seq), BF16)], axis=0)
    kaug_ref[...] = kaug_t.T

    def keys(hh, k0, n):
        return jnp.concatenate([k_ref[pl.ds(k0, n), :], kaug_ref[pl.ds(k0, n), :]], axis=1)

    def make_rhs(j):
        row = lax.broadcasted_iota(jnp.int32, (LANES, tq), 0)
        pad = jnp.zeros((LANES - AUG_ROWS, tq), BF16)
        out = []
        for t in range(Q_TILES):
            q0 = pl.multiple_of((j * Q_TILES + t) * tq, tq)
            qt = q_ref[pl.ds(q0, tq), :].T
            rhs = []
            for hh in (0, 1):
                qm = jnp.where((row >= hh * dh) & (row < (hh + 1) * dh), qt, jnp.zeros_like(qt))
                query_rows = {}
                for idx, piece in enumerate(_split3(cumt_ref[hh:hh + 1, pl.ds(q0, tq)])):
                    query_rows[aug * hh + idx] = 1.0
                    query_rows[aug * hh + 3 + idx] = piece
                rhs.append(jnp.concatenate([qm, _bias_rows(tq, query_rows), pad], axis=0))
            out.append(tuple(rhs))
        return tuple(out)

    n_super = seq // (Q_TILES * tq)
    per_tile = 2 * n_half

    def super_tile(j, rhs):
        def finish(outs):
            for t in range(Q_TILES):
                q0 = pl.multiple_of((j * Q_TILES + t) * tq, tq)
                tile_outs = outs[t * per_tile:(t + 1) * per_tile]
                o_ref[pl.ds(q0, tq), :] = _stack_chain_outputs(tile_outs, n_half).T.astype(o_ref.dtype)
            return make_rhs(jnp.minimum(j + 1, n_super - 1))

        return _flash_key_major(j, tq, keys, lambda t, hh: rhs[t][hh],
                                lambda hh, ks: vta_ref[hh, :, pl.ds(ks, SUB)], dh, 1, finish)

    lax.fori_loop(0, n_super, super_tile, make_rhs(0))
    _cast_slabs(cast_in + tuple(cast_out))


def _fox_attention(proj, cumt4, seq, q_col0, cast_weights):
    m = proj.shape[0]
    n_batch = m // seq
    n_pairs = FOX_HEADS // 2
    qb = q_col0 // LANES
    grid = (n_batch, n_pairs)
    cast_specs, cast_shapes = _cast_specs(cast_weights, grid)
    return pl.pallas_call(
        functools.partial(_fox_kernel, tq=512, n_cast=len(cast_weights)),
        grid=grid,
        in_specs=[
            pl.BlockSpec((seq, LANES), lambda b, hp: (b, qb + hp)),
            pl.BlockSpec((seq, LANES), lambda b, hp: (b, qb + n_pairs + hp)),
            pl.BlockSpec((seq, LANES), lambda b, hp: (b, qb + 2 * n_pairs + hp)),
            pl.BlockSpec((None, None, 2, seq), lambda b, hp: (b, hp, 0, 0)),
        ] + cast_specs,
        out_specs=[pl.BlockSpec((seq, LANES), lambda b, hp: (b, hp))] + cast_specs,
        out_shape=[jax.ShapeDtypeStruct((m, FOX_WIDTH), BF16)] + cast_shapes,
        scratch_shapes=[
            pltpu.VMEM((2, FOX_HEAD_DIM + 16, seq), BF16),
            pltpu.VMEM((seq, LANES), BF16),
        ],
        compiler_params=_cparams(("parallel", "parallel")),
        name="fox_attn",
    )(proj, proj, proj, cumt4, *cast_weights)


def _mla_kernel(qn_ref, qr_ref, kv_ref, kr_ref, *rest, tq, n_cast):
    cast_in, (o_ref, *cast_out), (vta_ref,) = rest[:n_cast], rest[n_cast:2 * n_cast + 1], rest[2 * n_cast + 1:]
    seq = kv_ref.shape[0]
    dv = MLA_V_DIM
    n_half = tq // HALF

    tail_rows = lax.broadcasted_iota(jnp.int32, (16, seq), 0)
    tail = jnp.where(tail_rows < 8, 1.0, 0.0).astype(BF16)
    for hh in (0, 1):
        vta_ref[hh, 0:dv, :] = kv_ref[:, (2 * hh + 1) * LANES:(2 * hh + 2) * LANES].T
        vta_ref[hh, dv:dv + 16, :] = tail

    def keys(hh, k0, n):
        return jnp.concatenate([kv_ref[pl.ds(k0, n), 2 * hh * LANES:(2 * hh + 1) * LANES],
                                kr_ref[pl.ds(k0, n), :]], axis=1)

    def make_rhs(j):
        out = []
        for t in range(Q_TILES):
            q0 = pl.multiple_of((j * Q_TILES + t) * tq, tq)
            out.append(tuple(jnp.concatenate([qn_ref[pl.ds(q0, tq), hh * LANES:(hh + 1) * LANES],
                                              qr_ref[pl.ds(q0, tq), hh * LANES:(hh + 1) * LANES]], axis=1).T
                             for hh in (0, 1)))
        return tuple(out)

    n_super = seq // (Q_TILES * tq)
    per_tile = 2 * n_half

    def super_tile(j, rhs):
        def finish(outs):
            for t in range(Q_TILES):
                q0 = pl.multiple_of((j * Q_TILES + t) * tq, tq)
                tile_outs = outs[t * per_tile:(t + 1) * per_tile]
                o_ref[pl.ds(q0, tq), :] = _stack_chain_outputs(tile_outs, n_half).T.astype(o_ref.dtype)
            return make_rhs(jnp.minimum(j + 1, n_super - 1))

        return _flash_key_major(j, tq, keys, lambda t, hh: rhs[t][hh],
                                lambda hh, ks: vta_ref[hh, :, pl.ds(ks, SUB)], dv, CHUNK, finish)

    lax.fori_loop(0, n_super, super_tile, make_rhs(0))
    _cast_slabs(cast_in + tuple(cast_out))


def _mla_attention(qn, qr, kv, kr, seq, cast_weights):
    m = qn.shape[0]
    n_batch = m // seq
    pair = 2 * LANES
    grid = (n_batch, MLA_HEADS // 2)
    cast_specs, cast_shapes = _cast_specs(cast_weights, grid)
    return pl.pallas_call(
        functools.partial(_mla_kernel, tq=512, n_cast=len(cast_weights)),
        grid=grid,
        in_specs=[
            pl.BlockSpec((seq, pair), lambda b, hp: (b, hp)),
            pl.BlockSpec((seq, pair), lambda b, hp: (b, hp)),
            pl.BlockSpec((seq, 2 * pair), lambda b, hp: (b, hp)),
            pl.BlockSpec((seq, LANES), lambda b, hp: (b, 0)),
        ] + cast_specs,
        out_specs=[pl.BlockSpec((seq, pair), lambda b, hp: (b, hp))] + cast_specs,
        out_shape=[jax.ShapeDtypeStruct((m, MLA_HEADS * MLA_V_DIM), BF16)] + cast_shapes,
        scratch_shapes=[pltpu.VMEM((2, MLA_V_DIM + 16, seq), BF16)],
        compiler_params=_cparams(("parallel", "parallel")),
        name="mla_attn",
    )(qn, qr, kv, kr, *cast_weights)


def _mix_kernel(yf_ref, ym_ref, gf_ref, gm_ref, x_ref, mod_ref, wbf_ref, wbm_ref, wo_ref, g_ref, b_ref,
                o_ref, mg_ref, *, n_chunk, g_off):
    d = o_ref.shape[1]
    yf = yf_ref[...]
    ym = ym_ref[...]

    def gate(ref, c):
        window = ref[:, c * n_chunk:(c + 1) * n_chunk + LANES]
        return jax.nn.sigmoid(window[:, g_off:g_off + n_chunk].astype(F32))

    for c in range(d // n_chunk):
        cols = slice(c * n_chunk, (c + 1) * n_chunk)
        a = jnp.dot(yf, wbf_ref[:, cols], preferred_element_type=F32)
        b = jnp.dot(ym, wbm_ref[:, cols], preferred_element_type=F32)
        merged = gate(gf_ref, c) * a + gate(gm_ref, c) * b
        mg_ref[:, cols] = merged.astype(BF16)
    mix = jnp.dot(mg_ref[...], wo_ref[...], preferred_element_type=F32)
    gate1 = mod_ref[2:3, :]
    o_ref[...] = _layer_norm(ALPHA * x_ref[...] + gate1 * mix, g_ref[...], b_ref[...])


def _mix(y_fox, y_mla, proj, gf_col, gm_col, x2d, mod3, w_bf, w_bm, w_o, ln_g, ln_b, seq):
    m, d = x2d.shape
    tm = 512
    tiles_per_batch = seq // tm
    const = lambda i: (0, 0)
    resident = dict(pipeline_mode=pl.Buffered(1))
    g_off = gf_col % LANES
    assert gm_col % LANES == g_off
    g_window = (pl.Element(tm), pl.Element(d + LANES))

    def window_at(col):
        return lambda i: (pl.multiple_of(i * tm, tm), col - g_off)

    return pl.pallas_call(
        functools.partial(_mix_kernel, n_chunk=512, g_off=g_off),
        grid=(m // tm,),
        in_specs=[
            pl.BlockSpec((tm, y_fox.shape[1]), lambda i: (i, 0)),
            pl.BlockSpec((tm, y_mla.shape[1]), lambda i: (i, 0)),
            pl.BlockSpec(g_window, window_at(gf_col)),
            pl.BlockSpec(g_window, window_at(gm_col)),
            pl.BlockSpec((tm, d), lambda i: (i, 0)),
            pl.BlockSpec((None, N_ADA, d), lambda i: (i // tiles_per_batch, 0, 0)),
            pl.BlockSpec(w_bf.shape, const, **resident),
            pl.BlockSpec(w_bm.shape, const, **resident),
            pl.BlockSpec(w_o.shape, const, **resident),
            pl.BlockSpec((1, d), const),
            pl.BlockSpec((1, d), const),
        ],
        out_specs=pl.BlockSpec((tm, d), lambda i: (i, 0)),
        out_shape=jax.ShapeDtypeStruct((m, d), F32),
        scratch_shapes=[pltpu.VMEM((tm, d), BF16)],
        compiler_params=_cparams(("parallel",)),
        name="mix",
    )(y_fox, y_mla, proj, proj, x2d, mod3, w_bf, w_bm, w_o, ln_g, ln_b)


def _mlp_kernel(x_ref, mod_ref, wu_ref, wd_ref, g_ref, b_ref, o_ref, u_ref):
    f = pl.program_id(1)

    def partial_sum(u):
        z = jnp.dot(u, wu_ref[...], preferred_element_type=F32)
        h = jnp.square(jnp.maximum(z, 0.0)).astype(BF16)
        return jnp.dot(h, wd_ref[...], preferred_element_type=F32)

    @pl.when(f == 0)
    def _():
        shift = mod_ref[3:4, :]
        scale = mod_ref[4:5, :]
        u = (x_ref[...] * (1.0 + scale) + shift).astype(BF16)
        u_ref[...] = u
        o_ref[...] = partial_sum(u)

    @pl.when(f != 0)
    def _():
        o_ref[...] += partial_sum(u_ref[...])

    @pl.when(f == pl.num_programs(1) - 1)
    def _():
        gate = mod_ref[5:6, :]
        o_ref[...] = _layer_norm(ALPHA * x_ref[...] + gate * o_ref[...], g_ref[...], b_ref[...])


def _mlp(x1, mod3, w_up, w_down, ln_g, ln_b, seq):
    m, d = x1.shape
    d_ff = w_up.shape[1]
    tm, tf = 1024, 512
    tiles_per_batch = seq // tm
    return pl.pallas_call(
        _mlp_kernel,
        grid=(m // tm, d_ff // tf),
        in_specs=[
            pl.BlockSpec((tm, d), lambda i, f: (i, 0)),
            pl.BlockSpec((None, N_ADA, d), lambda i, f: (i // tiles_per_batch, 0, 0)),
            pl.BlockSpec((d, tf), lambda i, f: (0, f)),
            pl.BlockSpec((tf, d), lambda i, f: (f, 0)),
            pl.BlockSpec((1, d), lambda i, f: (0, 0)),
            pl.BlockSpec((1, d), lambda i, f: (0, 0)),
        ],
        out_specs=pl.BlockSpec((tm, d), lambda i, f: (i, 0)),
        out_shape=jax.ShapeDtypeStruct((m, d), F32),
        scratch_shapes=[pltpu.VMEM((tm, d), BF16)],
        compiler_params=_cparams(("parallel", "arbitrary")),
        name="mlp",
    )(x1, mod3, w_up, w_down, ln_g, ln_b)


def _pad_cols(w, width):
    return jnp.pad(w, ((0, 0), (0, width - w.shape[1])))


def _swap_halves(w):
    half = w.shape[-1] // 2
    return jnp.concatenate([w[..., half:], w[..., :half]], axis=-1)


def _layout_q_up(w_q_up):
    r = w_q_up.shape[0]
    w3 = w_q_up.reshape(r, MLA_HEADS, MLA_QK_DIM)
    w_nope = w3[:, :, :MLA_NOPE_DIM].reshape(r, MLA_HEADS * MLA_NOPE_DIM)
    w_rope = w3[:, :, MLA_NOPE_DIM:]
    pad = ((0, 0), (0, 0), (0, LANES - MLA_ROPE_DIM))
    w_r = jnp.pad(w_rope, pad).reshape(r, MLA_HEADS * LANES)
    w_rs = jnp.pad(_swap_halves(w_rope), pad).reshape(r, MLA_HEADS * LANES)
    return jnp.concatenate([w_nope, w_r, w_rs], axis=1).astype(BF16)


def _rope_tables(seq):
    pos = np.arange(seq, dtype=np.float64)
    inv_freq = ROPE_THETA ** (-np.arange(0, MLA_ROPE_DIM, 2, dtype=np.float64) / MLA_ROPE_DIM)
    ang = pos[:, None] * inv_freq[None, :]
    cos, sin = np.cos(ang), np.sin(ang)
    pad = np.zeros((seq, LANES - MLA_ROPE_DIM))
    cos_t = np.concatenate([cos, cos, pad], axis=1).astype(np.float32)
    sin_t = np.concatenate([-sin, sin, pad], axis=1).astype(np.float32)
    return jnp.asarray(cos_t), jnp.asarray(sin_t)


def kernel(x, c, w_ada, b_ada, w_in, b_forget, g_q_norm, w_q_up, g_kv_norm, w_kv_up, w_branch_fox,
           w_branch_mla, w_out, ln1_g, ln1_b, w_mlp_up, w_mlp_down, ln2_g, ln2_b):
    n_batch, seq, d = x.shape
    assert w_ada.shape[0] == DEPTH
    m = n_batch * seq
    x2d = x.reshape(m, d)

    mod3 = _ada(c, w_ada[0], b_ada[0]).reshape(n_batch, N_ADA, d)

    small_col = 3 * FOX_WIDTH
    gf_col = small_col + FOX_HEADS + MLA_Q_LORA + MLA_KV_LORA + MLA_ROPE_DIM
    gm_col = gf_col + d
    assert w_in.shape[2] == gm_col + d and small_col == IN_SMALL_TILE * IN_TN and gf_col <= small_col + IN_TN
    proj, narrow_hi, narrow_lo = _inproj(x2d, mod3, w_in[0].T, seq)

    cos_t, sin_t = _rope_tables(seq)
    bf_pad = _pad_cols(b_forget[0].reshape(1, FOX_HEADS), LANES)
    qn, qr, kv, kr, cumt = _prep(
        narrow_hi, narrow_lo, bf_pad, g_q_norm[0].reshape(1, -1), g_kv_norm[0].reshape(1, -1),
        _layout_q_up(w_q_up[0]), w_kv_up[0].astype(BF16), cos_t, sin_t, seq)

    y_fox, w_up_b, w_down_b = _fox_attention(
        proj, cumt.reshape(n_batch, FOX_HEADS // 2, 2, seq), seq, 0, (w_mlp_up[0], w_mlp_down[0]))
    y_mla, w_bf_b, w_bm_b, w_o_b = _mla_attention(
        qn, qr, kv, kr, seq, (w_branch_fox[0], w_branch_mla[0], w_out[0]))

    x1 = _mix(y_fox, y_mla, proj, gf_col, gm_col, x2d, mod3, w_bf_b, w_bm_b, w_o_b,
              ln1_g[0].reshape(1, d), ln1_b[0].reshape(1, d), seq)
    x2 = _mlp(x1, mod3, w_up_b, w_down_b, ln2_g[0].reshape(1, d), ln2_b[0].reshape(1, d), seq)
    return x2.reshape(n_batch, seq, d)
```

```python
import functools
import math

import jax
import jax.numpy as jnp
import numpy as np
from jax import lax
from jax.experimental import pallas as pl
from jax.experimental.pallas import tpu as pltpu

F32 = jnp.float32
BF16 = jnp.bfloat16

FOX_HEADS = 16
FOX_HEAD_DIM = 64
FOX_WIDTH = FOX_HEADS * FOX_HEAD_DIM
MLA_HEADS = 8
MLA_NOPE_DIM = 128
MLA_ROPE_DIM = 64
MLA_V_DIM = 128
MLA_Q_LORA = 512
MLA_KV_LORA = 256
MLA_QK_DIM = MLA_NOPE_DIM + MLA_ROPE_DIM
CHUNK = 64
ROPE_THETA = 10000.0
LN_EPS = 1e-5
RMS_EPS = 1e-6
N_ADA = 6
DEPTH = 1
ALPHA = (2.0 * DEPTH) ** 0.25

LANES = 128
LOG2E = math.log2(math.e)
FOX_QSCALE = FOX_HEAD_DIM ** -0.5 * LOG2E
MLA_QSCALE = MLA_QK_DIM ** -0.5 * LOG2E
NEG_BIG = -1e30
VMEM_LIMIT = 56 * 1024 * 1024

SUB = 128
HALF = 256


def _cparams(semantics):
    return pltpu.CompilerParams(dimension_semantics=semantics, vmem_limit_bytes=VMEM_LIMIT)


def _layer_norm(y, g, b):
    mu = jnp.mean(y, axis=-1, keepdims=True)
    yc = y - mu
    var = jnp.mean(yc * yc, axis=-1, keepdims=True)
    return yc * lax.rsqrt(var + LN_EPS) * g + b


def _rms_norm(c, g):
    return c * lax.rsqrt(jnp.mean(c * c, axis=-1, keepdims=True) + RMS_EPS) * g


def _ada_kernel(ct_ref, w_ref, b_ref, o_ref, *, n_batch, k_chunk):
    d = w_ref.shape[0]
    ct = ct_ref[...]
    s = ct * jax.nn.sigmoid(ct)
    accs = [b_ref[...] for _ in range(n_batch)]
    for kc in range(d // k_chunk):
        rows = slice(kc * k_chunk, (kc + 1) * k_chunk)
        w = w_ref[rows, :]
        for b in range(n_batch):
            accs[b] = accs[b] + jnp.sum(w * s[rows, b:b + 1], axis=0, keepdims=True)
    for b in range(n_batch):
        o_ref[b:b + 1, :] = accs[b]


def _ada(c, w_ada, b_ada):
    n_batch, d = c.shape
    n = w_ada.shape[1]
    tn = 1024
    return pl.pallas_call(
        functools.partial(_ada_kernel, n_batch=n_batch, k_chunk=256),
        grid=(n // tn,),
        in_specs=[
            pl.BlockSpec((d, n_batch), lambda j: (0, 0)),
            pl.BlockSpec((d, tn), lambda j: (0, j)),
            pl.BlockSpec((1, tn), lambda j: (0, j)),
        ],
        out_specs=pl.BlockSpec((n_batch, tn), lambda j: (0, j)),
        out_shape=jax.ShapeDtypeStruct((n_batch, n), F32),
        compiler_params=_cparams(("arbitrary",)),
        name="ada",
    )(c.T, w_ada, b_ada.reshape(1, n))


IN_TN = 1024
IN_SMALL_TILE = 3 * FOX_WIDTH // IN_TN
NARROW_F = (0, LANES)
NARROW_CQ = (NARROW_F[1], NARROW_F[1] + MLA_Q_LORA)
NARROW_CKV = (NARROW_CQ[1], NARROW_CQ[1] + MLA_KV_LORA)
NARROW_KR = (NARROW_CKV[1], NARROW_CKV[1] + LANES)
assert NARROW_KR[1] == IN_TN and 2 * MLA_ROPE_DIM == LANES


def _narrow_segments(acc):
    c0 = FOX_HEADS
    c1 = c0 + MLA_Q_LORA
    c2 = c1 + MLA_KV_LORA
    c3 = c2 + MLA_ROPE_DIM
    half = MLA_ROPE_DIM // 2
    return jnp.concatenate([acc[:, 0:LANES], acc[:, c0:c1], acc[:, c1:c2], acc[:, c2:c3],
                            acc[:, c2 + half:c3], acc[:, c2:c2 + half]], axis=1)


def _inproj_kernel(x_ref, mod_ref, w_ref, o_ref, hi_ref, lo_ref, u_ref, *, n_cols):
    j = pl.program_id(1)
    last = pl.num_programs(1) - 1

    def product(wt, u=None):
        u = u_ref[...] if u is None else u
        return lax.dot_general(u, wt.astype(BF16), (((1,), (1,)), ((), ())), preferred_element_type=F32)

    @pl.when(j == 0)
    def _():
        shift = mod_ref[0:1, :]
        scale = mod_ref[1:2, :]
        u = (x_ref[...] * (1.0 + scale) + shift).astype(BF16)
        u_ref[...] = u
        o_ref[...] = (product(w_ref[...], u) * FOX_QSCALE).astype(o_ref.dtype)

    @pl.when(j == IN_SMALL_TILE)
    def _():
        acc = product(w_ref[...])
        o_ref[...] = acc.astype(o_ref.dtype)
        narrow = _narrow_segments(acc)
        hi = narrow.astype(hi_ref.dtype)
        hi_ref[...] = hi
        lo_ref[...] = (narrow - hi.astype(F32)).astype(lo_ref.dtype)

    @pl.when(j == last)
    def _():
        row = lax.broadcasted_iota(jnp.int32, w_ref.shape, 0)
        wt = jnp.where(row < n_cols - last * IN_TN, w_ref[...], 0.0)
        o_ref[...] = product(wt).astype(o_ref.dtype)

    @pl.when((j != 0) & (j != IN_SMALL_TILE) & (j != last))
    def _():
        o_ref[...] = product(w_ref[...]).astype(o_ref.dtype)


def _inproj(x2d, mod3, w_in_t, seq):
    m, d = x2d.shape
    n_cols = w_in_t.shape[0]
    tm = 1024
    n_tiles = pl.cdiv(n_cols, IN_TN)
    tiles_per_batch = seq // tm
    return pl.pallas_call(
        functools.partial(_inproj_kernel, n_cols=n_cols),
        grid=(m // tm, n_tiles),
        in_specs=[
            pl.BlockSpec((tm, d), lambda i, j: (i, 0)),
            pl.BlockSpec((None, N_ADA, d), lambda i, j: (i // tiles_per_batch, 0, 0)),
            pl.BlockSpec((IN_TN, d), lambda i, j: (j, 0)),
        ],
        out_specs=[pl.BlockSpec((tm, IN_TN), lambda i, j: (i, j)),
                   pl.BlockSpec((tm, IN_TN), lambda i, j: (i, 0)),
                   pl.BlockSpec((tm, IN_TN), lambda i, j: (i, 0))],
        out_shape=[jax.ShapeDtypeStruct((m, n_tiles * IN_TN), BF16),
                   jax.ShapeDtypeStruct((m, IN_TN), BF16),
                   jax.ShapeDtypeStruct((m, IN_TN), BF16)],
        scratch_shapes=[pltpu.VMEM((tm, d), BF16)],
        compiler_params=_cparams(("parallel", "arbitrary")),
        name="inproj",
    )(x2d, mod3, w_in_t)


def _log_sigmoid(x):
    return jnp.minimum(x, 0.0) - jnp.log1p(jnp.exp(-jnp.abs(x)))


def _prep_kernel(hi_ref, lo_ref, bf_ref, gq_ref, gkv_ref, wq_ref, wkv_ref, cos_ref, sin_ref,
                 qn_ref, qr_ref, kv_ref, kr_ref, cumt_ref, carry_ref, *, tiles_per_batch):
    i = pl.program_id(0)
    tm = hi_ref.shape[0]
    ps = hi_ref[...].astype(F32) + lo_ref[...].astype(F32)
    k_rope = ps[:, NARROW_KR[0]:NARROW_KR[1]]
    k_rope_partner = pltpu.roll(k_rope, MLA_ROPE_DIM, axis=1)
    cos = cos_ref[...]
    sin = sin_ref[...]

    logf = _log_sigmoid(ps[:, NARROW_F[0]:NARROW_F[1]] + bf_ref[...]) * LOG2E
    hi = logf.astype(BF16)
    r1 = logf - hi.astype(F32)
    mid = r1.astype(BF16)
    lo = (r1 - mid.astype(F32)).astype(BF16)
    row = lax.broadcasted_iota(jnp.int32, (tm, tm), 0)
    col = lax.broadcasted_iota(jnp.int32, (tm, tm), 1)
    tri = (row >= col).astype(BF16)
    cs = (jnp.dot(tri, hi, preferred_element_type=F32)
          + jnp.dot(tri, mid, preferred_element_type=F32)
          + jnp.dot(tri, lo, preferred_element_type=F32))

    @pl.when(i % tiles_per_batch == 0)
    def _():
        carry_ref[...] = jnp.zeros_like(carry_ref)

    cum = cs + carry_ref[...]
    carry_ref[...] = cum[tm - 1:tm, :]
    cumt_ref[...] = cum.T[0:FOX_HEADS, :]

    cqn = _rms_norm(ps[:, NARROW_CQ[0]:NARROW_CQ[1]], gq_ref[...]).astype(BF16)
    qa = jnp.dot(cqn, wq_ref[...], preferred_element_type=F32)
    hw = MLA_HEADS * LANES
    qn_ref[...] = (qa[:, 0:hw] * MLA_QSCALE).astype(BF16)
    for h in range(MLA_HEADS):
        a = qa[:, hw + h * LANES: hw + (h + 1) * LANES]
        b = qa[:, 2 * hw + h * LANES: 2 * hw + (h + 1) * LANES]
        qr_ref[:, h * LANES:(h + 1) * LANES] = ((a * cos + b * sin) * MLA_QSCALE).astype(BF16)

    ckvn = _rms_norm(ps[:, NARROW_CKV[0]:NARROW_CKV[1]], gkv_ref[...]).astype(BF16)
    kv_ref[...] = jnp.dot(ckvn, wkv_ref[...], preferred_element_type=F32).astype(BF16)
    kr_ref[...] = (k_rope * cos + k_rope_partner * sin).astype(BF16)


def _prep(narrow_hi, narrow_lo, bf_pad, g_q, g_kv, w_q_all, w_kv, cos_t, sin_t, seq):
    m = narrow_hi.shape[0]
    n_batch = m // seq
    tm = 512
    tiles_per_batch = seq // tm
    hw = MLA_HEADS * LANES
    const = lambda i: (0, 0)
    return pl.pallas_call(
        functools.partial(_prep_kernel, tiles_per_batch=tiles_per_batch),
        grid=(m // tm,),
        in_specs=[
            pl.BlockSpec((tm, IN_TN), lambda i: (i, 0)),
            pl.BlockSpec((tm, IN_TN), lambda i: (i, 0)),
            pl.BlockSpec(bf_pad.shape, const),
            pl.BlockSpec(g_q.shape, const),
            pl.BlockSpec(g_kv.shape, const),
            pl.BlockSpec(w_q_all.shape, const),
            pl.BlockSpec(w_kv.shape, const),
            pl.BlockSpec((tm, LANES), lambda i: (i % tiles_per_batch, 0)),
            pl.BlockSpec((tm, LANES), lambda i: (i % tiles_per_batch, 0)),
        ],
        out_specs=[
            pl.BlockSpec((tm, hw), lambda i: (i, 0)),
            pl.BlockSpec((tm, hw), lambda i: (i, 0)),
            pl.BlockSpec((tm, w_kv.shape[1]), lambda i: (i, 0)),
            pl.BlockSpec((tm, LANES), lambda i: (i, 0)),
            pl.BlockSpec((None, FOX_HEADS, tm), lambda i: (i // tiles_per_batch, 0, i % tiles_per_batch)),
        ],
        out_shape=[
            jax.ShapeDtypeStruct((m, hw), BF16),
            jax.ShapeDtypeStruct((m, hw), BF16),
            jax.ShapeDtypeStruct((m, w_kv.shape[1]), BF16),
            jax.ShapeDtypeStruct((m, LANES), BF16),
            jax.ShapeDtypeStruct((n_batch, FOX_HEADS, seq), F32),
        ],
        scratch_shapes=[pltpu.VMEM((1, LANES), F32)],
        compiler_params=_cparams(("arbitrary",)),
        name="prep",
    )(narrow_hi, narrow_lo, bf_pad, g_q, g_kv, w_q_all, w_kv, cos_t, sin_t)


def _split3(x):
    hi = x.astype(BF16).astype(F32)
    r = x - hi
    mid = r.astype(BF16).astype(F32)
    lo = (r - mid).astype(BF16).astype(F32)
    return hi, mid, lo


def _online_softmax_step(t, m, acc, vt_aug):
    m_new = jnp.maximum(m, jnp.max(t, axis=0, keepdims=True))
    alpha = jnp.exp2(m - m_new)
    p = jnp.exp2(t - m_new).astype(BF16)
    pv = jnp.dot(vt_aug, p, preferred_element_type=F32)
    return m_new, alpha * acc + pv[0:acc.shape[0], :]


def _causal_mask(t, key0, query0, granule):
    r = lax.broadcasted_iota(jnp.int32, t.shape, 0) + key0
    c = lax.broadcasted_iota(jnp.int32, t.shape, 1) + query0
    return jnp.where(r <= (c | (granule - 1)), t, NEG_BIG)


Q_TILES = 4
FULL, DIAG, SKIP = "full", "diag", "skip"


def _flash_key_major(j, tile, keys, rhs, vt_aug, d_v, granule, finish):
    acc_rows = d_v + 8
    n_rounds = tile // SUB
    chains = [(qt, half, hh) for qt in range(Q_TILES) for half in range(tile // HALF) for hh in (0, 1)]
    pairs = [(n, n + 1) for n in range(0, len(chains), 2)]

    def qk_of(k0, modes):
        loaded = {}

        def product(n):
            qt, half, hh = chains[n]
            n_keys = (half + 1) * HALF if modes[qt] == DIAG else tile
            if (hh, n_keys) not in loaded:
                loaded[hh, n_keys] = keys(hh, k0, n_keys)
            return jnp.dot(loaded[hh, n_keys], rhs(qt, hh)[:, half * HALF:(half + 1) * HALF],
                           preferred_element_type=F32)
        return product

    def visit(state, tiles):
        st = list(state)
        products = [qk_of(*t) for t in tiles]
        phases = [(ti, pair) for ti, (_, modes) in enumerate(tiles) for pair in pairs
                  if modes[chains[pair[0]][0]] != SKIP]

        def issue(phase):
            ti, pair = phase
            return {n: products[ti](n) for n in pair}

        cur = issue(phases[0])
        for idx, (ti, pair) in enumerate(phases):
            nxt = issue(phases[idx + 1]) if idx + 1 < len(phases) else None
            k0, modes = tiles[ti]
            for sb in range(n_rounds):
                ks = pl.multiple_of(k0 + sb * SUB, SUB)
                for n in pair:
                    qt, half, hh = chains[n]
                    is_diag = modes[qt] == DIAG
                    if is_diag and sb * SUB >= (half + 1) * HALF:
                        continue
                    t = cur[n][sb * SUB:(sb + 1) * SUB, :]
                    if is_diag and (sb + 1) * SUB > half * HALF:
                        t = _causal_mask(t, sb * SUB, half * HALF, granule)
                    st[n] = _online_softmax_step(t, *st[n], vt_aug(hh, ks))
            cur = nxt
        return tuple(st)

    everyone = (FULL,) * Q_TILES

    def two_full(i, state):
        k0 = pl.multiple_of(2 * i * tile, tile)
        return visit(state, [(k0, everyone), (pl.multiple_of(k0 + tile, tile), everyone)])

    assert Q_TILES % 2 == 0
    init = tuple((jnp.full((1, HALF), NEG_BIG, F32), jnp.zeros((acc_rows, HALF), F32)) for _ in chains)
    state = lax.fori_loop(0, j * (Q_TILES // 2), two_full, init)
    q0 = j * (Q_TILES * tile)
    last = [(pl.multiple_of(q0 + t * tile, tile),
             tuple(SKIP if qt < t else DIAG if qt == t else FULL for qt in range(Q_TILES)))
            for t in range(Q_TILES)]
    return finish([acc[0:d_v, :] * (1.0 / acc[d_v:d_v + 1, :]) for (_, acc) in visit(state, last)])


def _cast_specs(weights, grid):
    n_steps = math.prod(grid)
    specs, shapes = [], []
    for w in weights:
        rows, cols = w.shape
        slab = rows // n_steps
        assert slab * n_steps == rows and slab % 16 == 0, (w.shape, n_steps)
        specs.append(pl.BlockSpec((slab, cols), lambda a, b: (a * grid[1] + b, 0)))
        shapes.append(jax.ShapeDtypeStruct(w.shape, BF16))
    return specs, shapes


def _cast_slabs(refs):
    n = len(refs) // 2
    for src, dst in zip(refs[:n], refs[n:]):
        dst[...] = src[...].astype(dst.dtype)


def _stack_chain_outputs(outs, n_half):
    return jnp.concatenate(
        [jnp.concatenate([outs[half * 2 + hh] for half in range(n_half)], axis=1) for hh in (0, 1)], axis=0)


AUG_ROWS = 16


def _bias_rows(n, entries):
    r = lax.broadcasted_iota(jnp.int32, (AUG_ROWS, n), 0)
    out = jnp.zeros((AUG_ROWS, n), F32)
    for row, val in entries.items():
        out = jnp.where(r == row, val, out)
    return out.astype(BF16)


def _fox_kernel(q_ref, k_ref, v_ref, cumt_ref, *rest, tq, n_cast):
    cast_in, (o_ref, *cast_out), (vta_ref, kaug_ref) = rest[:n_cast], rest[n_cast:2 * n_cast + 1], rest[2 * n_cast + 1:]
    seq = k_ref.shape[0]
    dh = FOX_HEAD_DIM
    n_half = tq // HALF
    aug = 6

    vt = v_ref[...].T
    tail_rows = lax.broadcasted_iota(jnp.int32, (16, seq), 0)
    tail = jnp.where(tail_rows < 8, 1.0, 0.0).astype(BF16)
    for hh in (0, 1):
        vta_ref[hh, 0:dh, :] = vt[hh * dh:(hh + 1) * dh, :]
        vta_ref[hh, dh:dh + 16, :] = tail
    key_rows = {}
    for hh in (0, 1):
        for idx, piece in enumerate(_split3(-cumt_ref[hh:hh + 1, :])):
            key_rows[aug * hh + idx] = piece
            key_rows[aug * hh + 3 + idx] = 1.0
    kaug_t = jnp.concatenate([_bias_rows(seq, key_rows), jnp.zeros((LANES - AUG_ROWS, seq), BF16)], axis=0)
    kaug_ref[...] = kaug_t.T

    def keys(hh, k0, n):
        return jnp.concatenate([k_ref[pl.ds(k0, n), :], kaug_ref[pl.ds(k0, n), :]], axis=1)

    def make_rhs(j):
        row = lax.broadcasted_iota(jnp.int32, (LANES, tq), 0)
        pad = jnp.zeros((LANES - AUG_ROWS, tq), BF16)
        out = []
        for t in range(Q_TILES):
            q0 = pl.multiple_of((j * Q_TILES + t) * tq, tq)
            qt = q_ref[pl.ds(q0, tq), :].T
            rhs = []
            for hh in (0, 1):
                qm = jnp.where((row >= hh * dh) & (row < (hh + 1) * dh), qt, jnp.zeros_like(qt))
                query_rows = {}
                for idx, piece in enumerate(_split3(cumt_ref[hh:hh + 1, pl.ds(q0, tq)])):
                    query_rows[aug * hh + idx] = 1.0
                    query_rows[aug * hh + 3 + idx] = piece
                rhs.append(jnp.concatenate([qm, _bias_rows(tq, query_rows), pad], axis=0))
            out.append(tuple(rhs))
        return tuple(out)

    n_super = seq // (Q_TILES * tq)
    per_tile = 2 * n_half

    def super_tile(j, rhs):
        def finish(outs):
            for t in range(Q_TILES):
                q0 = pl.multiple_of((j * Q_TILES + t) * tq, tq)
                tile_outs = outs[t * per_tile:(t + 1) * per_tile]
                o_ref[pl.ds(q0, tq), :] = _stack_chain_outputs(tile_outs, n_half).T.astype(o_ref.dtype)
            return make_rhs(jnp.minimum(j + 1, n_super - 1))

        return _flash_key_major(j, tq, keys, lambda t, hh: rhs[t][hh],
                                lambda hh, ks: vta_ref[hh, :, pl.ds(ks, SUB)], dh, 1, finish)

    lax.fori_loop(0, n_super, super_tile, make_rhs(0))
    _cast_slabs(cast_in + tuple(cast_out))


def _fox_attention(proj, cumt4, seq, q_col0, cast_weights):
    m = proj.shape[0]
    n_batch = m // seq
    n_pairs = FOX_HEADS // 2
    qb = q_col0 // LANES
    grid = (n_batch, n_pairs)
    cast_specs, cast_shapes = _cast_specs(cast_weights, grid)
    return pl.pallas_call(
        functools.partial(_fox_kernel, tq=512, n_cast=len(cast_weights)),
        grid=grid,
        in_specs=[
            pl.BlockSpec((seq, LANES), lambda b, hp: (b, qb + hp)),
            pl.BlockSpec((seq, LANES), lambda b, hp: (b, qb + n_pairs + hp)),
            pl.BlockSpec((seq, LANES), lambda b, hp: (b, qb + 2 * n_pairs + hp)),
            pl.BlockSpec((None, None, 2, seq), lambda b, hp: (b, hp, 0, 0)),
        ] + cast_specs,
        out_specs=[pl.BlockSpec((seq, LANES), lambda b, hp: (b, hp))] + cast_specs,
        out_shape=[jax.ShapeDtypeStruct((m, FOX_WIDTH), BF16)] + cast_shapes,
        scratch_shapes=[
            pltpu.VMEM((2, FOX_HEAD_DIM + 16, seq), BF16),
            pltpu.VMEM((seq, LANES), BF16),
        ],
        compiler_params=_cparams(("parallel", "parallel")),
        name="fox_attn",
    )(proj, proj, proj, cumt4, *cast_weights)


def _mla_kernel(qn_ref, qr_ref, kv_ref, kr_ref, *rest, tq, n_cast):
    cast_in, (o_ref, *cast_out), (vta_ref,) = rest[:n_cast], rest[n_cast:2 * n_cast + 1], rest[2 * n_cast + 1:]
    seq = kv_ref.shape[0]
    dv = MLA_V_DIM
    n_half = tq // HALF

    tail_rows = lax.broadcasted_iota(jnp.int32, (16, seq), 0)
    tail = jnp.where(tail_rows < 8, 1.0, 0.0).astype(BF16)
    for hh in (0, 1):
        vta_ref[hh, 0:dv, :] = kv_ref[:, (2 * hh + 1) * LANES:(2 * hh + 2) * LANES].T
        vta_ref[hh, dv:dv + 16, :] = tail

    def keys(hh, k0, n):
        return jnp.concatenate([kv_ref[pl.ds(k0, n), 2 * hh * LANES:(2 * hh + 1) * LANES],
                                kr_ref[pl.ds(k0, n), :]], axis=1)

    def make_rhs(j):
        out = []
        for t in range(Q_TILES):
            q0 = pl.multiple_of((j * Q_TILES + t) * tq, tq)
            out.append(tuple(jnp.concatenate([qn_ref[pl.ds(q0, tq), hh * LANES:(hh + 1) * LANES],
                                              qr_ref[pl.ds(q0, tq), hh * LANES:(hh + 1) * LANES]], axis=1).T
                             for hh in (0, 1)))
        return tuple(out)

    n_super = seq // (Q_TILES * tq)
    per_tile = 2 * n_half

    def super_tile(j, rhs):
        def finish(outs):
            for t in range(Q_TILES):
                q0 = pl.multiple_of((j * Q_TILES + t) * tq, tq)
                tile_outs = outs[t * per_tile:(t + 1) * per_tile]
                o_ref[pl.ds(q0, tq), :] = _stack_chain_outputs(tile_outs, n_half).T.astype(o_ref.dtype)
            return make_rhs(jnp.minimum(j + 1, n_super - 1))

        return _flash_key_major(j, tq, keys, lambda t, hh: rhs[t][hh],
                                lambda hh, ks: vta_ref[hh, :, pl.ds(ks, SUB)], dv, CHUNK, finish)

    lax.fori_loop(0, n_super, super_tile, make_rhs(0))
    _cast_slabs(cast_in + tuple(cast_out))


def _mla_attention(qn, qr, kv, kr, seq, cast_weights):
    m = qn.shape[0]
    n_batch = m // seq
    pair = 2 * LANES
    grid = (n_batch, MLA_HEADS // 2)
    cast_specs, cast_shapes = _cast_specs(cast_weights, grid)
    return pl.pallas_call(
        functools.partial(_mla_kernel, tq=512, n_cast=len(cast_weights)),
        grid=grid,
        in_specs=[
            pl.BlockSpec((seq, pair), lambda b, hp: (b, hp)),
            pl.BlockSpec((seq, pair), lambda b, hp: (b, hp)),
            pl.BlockSpec((seq, 2 * pair), lambda b, hp: (b, hp)),
            pl.BlockSpec((seq, LANES), lambda b, hp: (b, 0)),
        ] + cast_specs,
        out_specs=[pl.BlockSpec((seq, pair), lambda b, hp: (b, hp))] + cast_specs,
        out_shape=[jax.ShapeDtypeStruct((m, MLA_HEADS * MLA_V_DIM), BF16)] + cast_shapes,
        scratch_shapes=[pltpu.VMEM((2, MLA_V_DIM + 16, seq), BF16)],
        compiler_params=_cparams(("parallel", "parallel")),
        name="mla_attn",
    )(qn, qr, kv, kr, *cast_weights)


def _mix_kernel(yf_ref, ym_ref, gf_ref, gm_ref, x_ref, mod_ref, wbf_ref, wbm_ref, wo_ref, g_ref, b_ref,
                o_ref, mg_ref, *, n_chunk, g_off):
    d = o_ref.shape[1]
    yf = yf_ref[...]
    ym = ym_ref[...]

    def gate(ref, c):
        window = ref[:, c * n_chunk:(c + 1) * n_chunk + LANES]
        return jax.nn.sigmoid(window[:, g_off:g_off + n_chunk].astype(F32))

    for c in range(d // n_chunk):
        cols = slice(c * n_chunk, (c + 1) * n_chunk)
        a = jnp.dot(yf, wbf_ref[:, cols], preferred_element_type=F32)
        b = jnp.dot(ym, wbm_ref[:, cols], preferred_element_type=F32)
        merged = gate(gf_ref, c) * a + gate(gm_ref, c) * b
        mg_ref[:, cols] = merged.astype(BF16)
    mix = jnp.dot(mg_ref[...], wo_ref[...], preferred_element_type=F32)
    gate1 = mod_ref[2:3, :]
    o_ref[...] = _layer_norm(ALPHA * x_ref[...] + gate1 * mix, g_ref[...], b_ref[...])


def _mix(y_fox, y_mla, proj, gf_col, gm_col, x2d, mod3, w_bf, w_bm, w_o, ln_g, ln_b, seq):
    m, d = x2d.shape
    tm = 512
    tiles_per_batch = seq // tm
    const = lambda i: (0, 0)
    resident = dict(pipeline_mode=pl.Buffered(1))
    g_off = gf_col % LANES
    assert gm_col % LANES == g_off
    g_window = (pl.Element(tm), pl.Element(d + LANES))

    def window_at(col):
        return lambda i: (pl.multiple_of(i * tm, tm), col - g_off)

    return pl.pallas_call(
        functools.partial(_mix_kernel, n_chunk=512, g_off=g_off),
        grid=(m // tm,),
        in_specs=[
            pl.BlockSpec((tm, y_fox.shape[1]), lambda i: (i, 0)),
            pl.BlockSpec((tm, y_mla.shape[1]), lambda i: (i, 0)),
            pl.BlockSpec(g_window, window_at(gf_col)),
            pl.BlockSpec(g_window, window_at(gm_col)),
            pl.BlockSpec((tm, d), lambda i: (i, 0)),
            pl.BlockSpec((None, N_ADA, d), lambda i: (i // tiles_per_batch, 0, 0)),
            pl.BlockSpec(w_bf.shape, const, **resident),
            pl.BlockSpec(w_bm.shape, const, **resident),
            pl.BlockSpec(w_o.shape, const, **resident),
            pl.BlockSpec((1, d), const),
            pl.BlockSpec((1, d), const),
        ],
        out_specs=pl.BlockSpec((tm, d), lambda i: (i, 0)),
        out_shape=jax.ShapeDtypeStruct((m, d), F32),
        scratch_shapes=[pltpu.VMEM((tm, d), BF16)],
        compiler_params=_cparams(("parallel",)),
        name="mix",
    )(y_fox, y_mla, proj, proj, x2d, mod3, w_bf, w_bm, w_o, ln_g, ln_b)


def _mlp_kernel(x_ref, mod_ref, wu_ref, wd_ref, g_ref, b_ref, o_ref, u_ref):
    f = pl.program_id(1)

    def partial_sum(u):
        z = jnp.dot(u, wu_ref[...], preferred_element_type=F32)
        h = jnp.square(jnp.maximum(z, 0.0)).astype(BF16)
        return jnp.dot(h, wd_ref[...], preferred_element_type=F32)

    @pl.when(f == 0)
    def _():
        shift = mod_ref[3:4, :]
        scale = mod_ref[4:5, :]
        u = (x_ref[...] * (1.0 + scale) + shift).astype(BF16)
        u_ref[...] = u
        o_ref[...] = partial_sum(u)

    @pl.when(f != 0)
    def _():
        o_ref[...] += partial_sum(u_ref[...])

    @pl.when(f == pl.num_programs(1) - 1)
    def _():
        gate = mod_ref[5:6, :]
        o_ref[...] = _layer_norm(ALPHA * x_ref[...] + gate * o_ref[...], g_ref[...], b_ref[...])


def _mlp(x1, mod3, w_up, w_down, ln_g, ln_b, seq):
    m, d = x1.shape
    d_ff = w_up.shape[1]
    tm, tf = 1024, 512
    tiles_per_batch = seq // tm
    return pl.pallas_call(
        _mlp_kernel,
        grid=(m // tm, d_ff // tf),
        in_specs=[
            pl.BlockSpec((tm, d), lambda i, f: (i, 0)),
            pl.BlockSpec((None, N_ADA, d), lambda i, f: (i // tiles_per_batch, 0, 0)),
            pl.BlockSpec((d, tf), lambda i, f: (0, f)),
            pl.BlockSpec((tf, d), lambda i, f: (f, 0)),
            pl.BlockSpec((1, d), lambda i, f: (0, 0)),
            pl.BlockSpec((1, d), lambda i, f: (0, 0)),
        ],
        out_specs=pl.BlockSpec((tm, d), lambda i, f: (i, 0)),
        out_shape=jax.ShapeDtypeStruct((m, d), F32),
        scratch_shapes=[pltpu.VMEM((tm, d), BF16)],
        compiler_params=_cparams(("parallel", "arbitrary")),
        name="mlp",
    )(x1, mod3, w_up, w_down, ln_g, ln_b)


def _pad_cols(w, width):
    return jnp.pad(w, ((0, 0), (0, width - w.shape[1])))


def _swap_halves(w):
    half = w.shape[-1] // 2
    return jnp.concatenate([w[..., half:], w[..., :half]], axis=-1)


def _layout_q_up(w_q_up):
    r = w_q_up.shape[0]
    w3 = w_q_up.reshape(r, MLA_HEADS, MLA_QK_DIM)
    w_nope = w3[:, :, :MLA_NOPE_DIM].reshape(r, MLA_HEADS * MLA_NOPE_DIM)
    w_rope = w3[:, :, MLA_NOPE_DIM:]
    pad = ((0, 0), (0, 0), (0, LANES - MLA_ROPE_DIM))
    w_r = jnp.pad(w_rope, pad).reshape(r, MLA_HEADS * LANES)
    w_rs = jnp.pad(_swap_halves(w_rope), pad).reshape(r, MLA_HEADS * LANES)
    return jnp.concatenate([w_nope, w_r, w_rs], axis=1).astype(BF16)


def _rope_tables(seq):
    pos = np.arange(seq, dtype=np.float64)
    inv_freq = ROPE_THETA ** (-np.arange(0, MLA_ROPE_DIM, 2, dtype=np.float64) / MLA_ROPE_DIM)
    ang = pos[:, None] * inv_freq[None, :]
    cos, sin = np.cos(ang), np.sin(ang)
    pad = np.zeros((seq, LANES - MLA_ROPE_DIM))
    cos_t = np.concatenate([cos, cos, pad], axis=1).astype(np.float32)
    sin_t = np.concatenate([-sin, sin, pad], axis=1).astype(np.float32)
    return jnp.asarray(cos_t), jnp.asarray(sin_t)


def kernel(x, c, w_ada, b_ada, w_in, b_forget, g_q_norm, w_q_up, g_kv_norm, w_kv_up, w_branch_fox,
           w_branch_mla, w_out, ln1_g, ln1_b, w_mlp_up, w_mlp_down, ln2_g, ln2_b):
    n_batch, seq, d = x.shape
    assert w_ada.shape[0] == DEPTH
    m = n_batch * seq
    x2d = x.reshape(m, d)

    mod3 = _ada(c, w_ada[0], b_ada[0]).reshape(n_batch, N_ADA, d)

    small_col = 3 * FOX_WIDTH
    gf_col = small_col + FOX_HEADS + MLA_Q_LORA + MLA_KV_LORA + MLA_ROPE_DIM
    gm_col = gf_col + d
    assert w_in.shape[2] == gm_col + d and small_col == IN_SMALL_TILE * IN_TN and gf_col <= small_col + IN_TN
    proj, narrow_hi, narrow_lo = _inproj(x2d, mod3, w_in[0].T, seq)

    cos_t, sin_t = _rope_tables(seq)
    bf_pad = _pad_cols(b_forget[0].reshape(1, FOX_HEADS), LANES)
    qn, qr, kv, kr, cumt = _prep(
        narrow_hi, narrow_lo, bf_pad, g_q_norm[0].reshape(1, -1), g_kv_norm[0].reshape(1, -1),
        _layout_q_up(w_q_up[0]), w_kv_up[0].astype(BF16), cos_t, sin_t, seq)

    y_fox, w_up_b, w_down_b = _fox_attention(
        proj, cumt.reshape(n_batch, FOX_HEADS // 2, 2, seq), seq, 0, (w_mlp_up[0], w_mlp_down[0]))
    y_mla, w_bf_b, w_bm_b, w_o_b = _mla_attention(
        qn, qr, kv, kr, seq, (w_branch_fox[0], w_branch_mla[0], w_out[0]))

    x1 = _mix(y_fox, y_mla, proj, gf_col, gm_col, x2d, mod3, w_bf_b, w_bm_b, w_o_b,
              ln1_g[0].reshape(1, d), ln1_b[0].reshape(1, d), seq)
    x2 = _mlp(x1, mod3, w_up_b, w_down_b, ln2_g[0].reshape(1, d), ln2_b[0].reshape(1, d), seq)
    return x2.reshape(n_batch, seq, d)
```

```python
import functools
import math

import jax
import jax.numpy as jnp
import numpy as np
from jax import lax
from jax.experimental import pallas as pl
from jax.experimental.pallas import tpu as pltpu

F32 = jnp.float32
BF16 = jnp.bfloat16

FOX_HEADS = 16
FOX_HEAD_DIM = 64
FOX_WIDTH = FOX_HEADS * FOX_HEAD_DIM
MLA_HEADS = 8
MLA_NOPE_DIM = 128
MLA_ROPE_DIM = 64
MLA_V_DIM = 128
MLA_Q_LORA = 512
MLA_KV_LORA = 256
MLA_QK_DIM = MLA_NOPE_DIM + MLA_ROPE_DIM
CHUNK = 64
ROPE_THETA = 10000.0
LN_EPS = 1e-5
RMS_EPS = 1e-6
N_ADA = 6
DEPTH = 1
ALPHA = (2.0 * DEPTH) ** 0.25

LANES = 128
LOG2E = math.log2(math.e)
FOX_QSCALE = FOX_HEAD_DIM ** -0.5 * LOG2E
MLA_QSCALE = MLA_QK_DIM ** -0.5 * LOG2E
NEG_BIG = -1e30
VMEM_LIMIT = 56 * 1024 * 1024

SUB = 128
HALF = 256


def _cparams(semantics):
    return pltpu.CompilerParams(dimension_semantics=semantics, vmem_limit_bytes=VMEM_LIMIT)


def _layer_norm(y, g, b):
    mu = jnp.mean(y, axis=-1, keepdims=True)
    yc = y - mu
    var = jnp.mean(yc * yc, axis=-1, keepdims=True)
    return yc * lax.rsqrt(var + LN_EPS) * g + b


def _rms_norm(c, g):
    return c * lax.rsqrt(jnp.mean(c * c, axis=-1, keepdims=True) + RMS_EPS) * g


def _ada_kernel(ct_ref, w_ref, b_ref, o_ref, *, n_batch, k_chunk):
    d = w_ref.shape[0]
    ct = ct_ref[...]
    s = ct * jax.nn.sigmoid(ct)
    accs = [b_ref[...] for _ in range(n_batch)]
    for kc in range(d // k_chunk):
        rows = slice(kc * k_chunk, (kc + 1) * k_chunk)
        w = w_ref[rows, :]
        for b in range(n_batch):
            accs[b] = accs[b] + jnp.sum(w * s[rows, b:b + 1], axis=0, keepdims=True)
    for b in range(n_batch):
        o_ref[b:b + 1, :] = accs[b]


def _ada(c, w_ada, b_ada):
    n_batch, d = c.shape
    n = w_ada.shape[1]
    tn = 1024
    return pl.pallas_call(
        functools.partial(_ada_kernel, n_batch=n_batch, k_chunk=256),
        grid=(n // tn,),
        in_specs=[
            pl.BlockSpec((d, n_batch), lambda j: (0, 0)),
            pl.BlockSpec((d, tn), lambda j: (0, j)),
            pl.BlockSpec((1, tn), lambda j: (0, j)),
        ],
        out_specs=pl.BlockSpec((n_batch, tn), lambda j: (0, j)),
        out_shape=jax.ShapeDtypeStruct((n_batch, n), F32),
        compiler_params=_cparams(("arbitrary",)),
        name="ada",
    )(c.T, w_ada, b_ada.reshape(1, n))


IN_TN = 1024
IN_SMALL_TILE = 3 * FOX_WIDTH // IN_TN
NARROW_F = (0, LANES)
NARROW_CQ = (NARROW_F[1], NARROW_F[1] + MLA_Q_LORA)
NARROW_CKV = (NARROW_CQ[1], NARROW_CQ[1] + MLA_KV_LORA)
NARROW_KR = (NARROW_CKV[1], NARROW_CKV[1] + LANES)
assert NARROW_KR[1] == IN_TN and 2 * MLA_ROPE_DIM == LANES


def _narrow_segments(acc):
    c0 = FOX_HEADS
    c1 = c0 + MLA_Q_LORA
    c2 = c1 + MLA_KV_LORA
    c3 = c2 + MLA_ROPE_DIM
    half = MLA_ROPE_DIM // 2
    return jnp.concatenate([acc[:, 0:LANES], acc[:, c0:c1], acc[:, c1:c2], acc[:, c2:c3],
                            acc[:, c2 + half:c3], acc[:, c2:c2 + half]], axis=1)


def _inproj_kernel(x_ref, mod_ref, w_ref, o_ref, hi_ref, lo_ref, u_ref, *, n_cols):
    j = pl.program_id(1)
    last = pl.num_programs(1) - 1

    def product(wt, u=None):
        u = u_ref[...] if u is None else u
        return lax.dot_general(u, wt.astype(BF16), (((1,), (1,)), ((), ())), preferred_element_type=F32)

    @pl.when(j == 0)
    def _():
        shift = mod_ref[0:1, :]
        scale = mod_ref[1:2, :]
        u = (x_ref[...] * (1.0 + scale) + shift).astype(BF16)
        u_ref[...] = u
        o_ref[...] = (product(w_ref[...], u) * FOX_QSCALE).astype(o_ref.dtype)

    @pl.when(j == IN_SMALL_TILE)
    def _():
        acc = product(w_ref[...])
        o_ref[...] = acc.astype(o_ref.dtype)
        narrow = _narrow_segments(acc)
        hi = narrow.astype(hi_ref.dtype)
        hi_ref[...] = hi
        lo_ref[...] = (narrow - hi.astype(F32)).astype(lo_ref.dtype)

    @pl.when(j == last)
    def _():
        row = lax.broadcasted_iota(jnp.int32, w_ref.shape, 0)
        wt = jnp.where(row < n_cols - last * IN_TN, w_ref[...], 0.0)
        o_ref[...] = product(wt).astype(o_ref.dtype)

    @pl.when((j != 0) & (j != IN_SMALL_TILE) & (j != last))
    def _():
        o_ref[...] = product(w_ref[...]).astype(o_ref.dtype)


def _inproj(x2d, mod3, w_in_t, seq):
    m, d = x2d.shape
    n_cols = w_in_t.shape[0]
    tm = 1024
    n_tiles = pl.cdiv(n_cols, IN_TN)
    tiles_per_batch = seq // tm
    return pl.pallas_call(
        functools.partial(_inproj_kernel, n_cols=n_cols),
        grid=(m // tm, n_tiles),
        in_specs=[
            pl.BlockSpec((tm, d), lambda i, j: (i, 0)),
            pl.BlockSpec((None, N_ADA, d), lambda i, j: (i // tiles_per_batch, 0, 0)),
            pl.BlockSpec((IN_TN, d), lambda i, j: (j, 0)),
        ],
        out_specs=[pl.BlockSpec((tm, IN_TN), lambda i, j: (i, j)),
                   pl.BlockSpec((tm, IN_TN), lambda i, j: (i, 0)),
                   pl.BlockSpec((tm, IN_TN), lambda i, j: (i, 0))],
        out_shape=[jax.ShapeDtypeStruct((m, n_tiles * IN_TN), BF16),
                   jax.ShapeDtypeStruct((m, IN_TN), BF16),
                   jax.ShapeDtypeStruct((m, IN_TN), BF16)],
        scratch_shapes=[pltpu.VMEM((tm, d), BF16)],
        compiler_params=_cparams(("parallel", "arbitrary")),
        name="inproj",
    )(x2d, mod3, w_in_t)


def _log_sigmoid(x):
    return jnp.minimum(x, 0.0) - jnp.log1p(jnp.exp(-jnp.abs(x)))


def _prep_kernel(hi_ref, lo_ref, bf_ref, gq_ref, gkv_ref, wq_ref, wkv_ref, cos_ref, sin_ref,
                 qn_ref, qr_ref, kv_ref, kr_ref, cumt_ref, carry_ref, *, tiles_per_batch):
    i = pl.program_id(0)
    tm = hi_ref.shape[0]
    ps = hi_ref[...].astype(F32) + lo_ref[...].astype(F32)
    k_rope = ps[:, NARROW_KR[0]:NARROW_KR[1]]
    k_rope_partner = pltpu.roll(k_rope, MLA_ROPE_DIM, axis=1)
    cos = cos_ref[...]
    sin = sin_ref[...]

    logf = _log_sigmoid(ps[:, NARROW_F[0]:NARROW_F[1]] + bf_ref[...]) * LOG2E
    hi = logf.astype(BF16)
    r1 = logf - hi.astype(F32)
    mid = r1.astype(BF16)
    lo = (r1 - mid.astype(F32)).astype(BF16)
    row = lax.broadcasted_iota(jnp.int32, (tm, tm), 0)
    col = lax.broadcasted_iota(jnp.int32, (tm, tm), 1)
    tri = (row >= col).astype(BF16)
    cs = (jnp.dot(tri, hi, preferred_element_type=F32)
          + jnp.dot(tri, mid, preferred_element_type=F32)
          + jnp.dot(tri, lo, preferred_element_type=F32))

    @pl.when(i % tiles_per_batch == 0)
    def _():
        carry_ref[...] = jnp.zeros_like(carry_ref)

    cum = cs + carry_ref[...]
    carry_ref[...] = cum[tm - 1:tm, :]
    cumt_ref[...] = cum.T[0:FOX_HEADS, :]

    cqn = _rms_norm(ps[:, NARROW_CQ[0]:NARROW_CQ[1]], gq_ref[...]).astype(BF16)
    qa = jnp.dot(cqn, wq_ref[...], preferred_element_type=F32)
    hw = MLA_HEADS * LANES
    qn_ref[...] = (qa[:, 0:hw] * MLA_QSCALE).astype(BF16)
    for h in range(MLA_HEADS):
        a = qa[:, hw + h * LANES: hw + (h + 1) * LANES]
        b = qa[:, 2 * hw + h * LANES: 2 * hw + (h + 1) * LANES]
        qr_ref[:, h * LANES:(h + 1) * LANES] = ((a * cos + b * sin) * MLA_QSCALE).astype(BF16)

    ckvn = _rms_norm(ps[:, NARROW_CKV[0]:NARROW_CKV[1]], gkv_ref[...]).astype(BF16)
    kv_ref[...] = jnp.dot(ckvn, wkv_ref[...], preferred_element_type=F32).astype(BF16)
    kr_ref[...] = (k_rope * cos + k_rope_partner * sin).astype(BF16)


def _prep(narrow_hi, narrow_lo, bf_pad, g_q, g_kv, w_q_all, w_kv, cos_t, sin_t, seq):
    m = narrow_hi.shape[0]
    n_batch = m // seq
    tm = 512
    tiles_per_batch = seq // tm
    hw = MLA_HEADS * LANES
    const = lambda i: (0, 0)
    return pl.pallas_call(
        functools.partial(_prep_kernel, tiles_per_batch=tiles_per_batch),
        grid=(m // tm,),
        in_specs=[
            pl.BlockSpec((tm, IN_TN), lambda i: (i, 0)),
            pl.BlockSpec((tm, IN_TN), lambda i: (i, 0)),
            pl.BlockSpec(bf_pad.shape, const),
            pl.BlockSpec(g_q.shape, const),
            pl.BlockSpec(g_kv.shape, const),
            pl.BlockSpec(w_q_all.shape, const),
            pl.BlockSpec(w_kv.shape, const),
            pl.BlockSpec((tm, LANES), lambda i: (i % tiles_per_batch, 0)),
            pl.BlockSpec((tm, LANES), lambda i: (i % tiles_per_batch, 0)),
        ],
        out_specs=[
            pl.BlockSpec((tm, hw), lambda i: (i, 0)),
            pl.BlockSpec((tm, hw), lambda i: (i, 0)),
            pl.BlockSpec((tm, w_kv.shape[1]), lambda i: (i, 0)),
            pl.BlockSpec((tm, LANES), lambda i: (i, 0)),
            pl.BlockSpec((None, FOX_HEADS, tm), lambda i: (i // tiles_per_batch, 0, i % tiles_per_batch)),
        ],
        out_shape=[
            jax.ShapeDtypeStruct((m, hw), BF16),
            jax.ShapeDtypeStruct((m, hw), BF16),
            jax.ShapeDtypeStruct((m, w_kv.shape[1]), BF16),
            jax.ShapeDtypeStruct((m, LANES), BF16),
            jax.ShapeDtypeStruct((n_batch, FOX_HEADS, seq), F32),
        ],
        scratch_shapes=[pltpu.VMEM((1, LANES), F32)],
        compiler_params=_cparams(("arbitrary",)),
        name="prep",
    )(narrow_hi, narrow_lo, bf_pad, g_q, g_kv, w_q_all, w_kv, cos_t, sin_t)


def _split3(x):
    hi = x.astype(BF16).astype(F32)
    r = x - hi
    mid = r.astype(BF16).astype(F32)
    lo = (r - mid).astype(BF16).astype(F32)
    return hi, mid, lo


def _online_softmax_step(t, m, acc, vt_aug):
    m_new = jnp.maximum(m, jnp.max(t, axis=0, keepdims=True))
    alpha = jnp.exp2(m - m_new)
    p = jnp.exp2(t - m_new).astype(BF16)
    pv = jnp.dot(vt_aug, p, preferred_element_type=F32)
    return m_new, alpha * acc + pv[0:acc.shape[0], :]


def _causal_mask(t, key0, query0, granule):
    r = lax.broadcasted_iota(jnp.int32, t.shape, 0) + key0
    c = lax.broadcasted_iota(jnp.int32, t.shape, 1) + query0
    return jnp.where(r <= (c | (granule - 1)), t, NEG_BIG)


Q_TILES = 8
FULL, DIAG, SKIP = "full", "diag", "skip"


def _flash_key_major(j, tile, keys, rhs, vt_aug, d_v, granule, finish):
    acc_rows = d_v + 8
    n_rounds = tile // SUB
    chains = [(qt, half, hh) for qt in range(Q_TILES) for half in range(tile // HALF) for hh in (0, 1)]
    pairs = [(n, n + 1) for n in range(0, len(chains), 2)]

    def qk_of(k0, modes):
        loaded = {}

        def product(n):
            qt, half, hh = chains[n]
            n_keys = (half + 1) * HALF if modes[qt] == DIAG else tile
            if (hh, n_keys) not in loaded:
                loaded[hh, n_keys] = keys(hh, k0, n_keys)
            return jnp.dot(loaded[hh, n_keys], rhs(qt, hh)[:, half * HALF:(half + 1) * HALF],
                           preferred_element_type=F32)
        return product

    def visit(state, tiles):
        st = list(state)
        products = [qk_of(*t) for t in tiles]
        phases = [(ti, pair) for ti, (_, modes) in enumerate(tiles) for pair in pairs
                  if modes[chains[pair[0]][0]] != SKIP]

        def issue(phase):
            ti, pair = phase
            return {n: products[ti](n) for n in pair}

        cur = issue(phases[0])
        for idx, (ti, pair) in enumerate(phases):
            nxt = issue(phases[idx + 1]) if idx + 1 < len(phases) else None
            k0, modes = tiles[ti]
            for sb in range(n_rounds):
                ks = pl.multiple_of(k0 + sb * SUB, SUB)
                for n in pair:
                    qt, half, hh = chains[n]
                    is_diag = modes[qt] == DIAG
                    if is_diag and sb * SUB >= (half + 1) * HALF:
                        continue
                    t = cur[n][sb * SUB:(sb + 1) * SUB, :]
                    if is_diag and (sb + 1) * SUB > half * HALF:
                        t = _causal_mask(t, sb * SUB, half * HALF, granule)
                    st[n] = _online_softmax_step(t, *st[n], vt_aug(hh, ks))
            cur = nxt
        return tuple(st)

    everyone = (FULL,) * Q_TILES

    def two_full(i, state):
        k0 = pl.multiple_of(2 * i * tile, tile)
        return visit(state, [(k0, everyone), (pl.multiple_of(k0 + tile, tile), everyone)])

    assert Q_TILES % 2 == 0
    init = tuple((jnp.full((1, HALF), NEG_BIG, F32), jnp.zeros((acc_rows, HALF), F32)) for _ in chains)
    state = lax.fori_loop(0, j * (Q_TILES // 2), two_full, init)
    q0 = j * (Q_TILES * tile)
    last = [(pl.multiple_of(q0 + t * tile, tile),
             tuple(SKIP if qt < t else DIAG if qt == t else FULL for qt in range(Q_TILES)))
            for t in range(Q_TILES)]
    return finish([acc[0:d_v, :] * (1.0 / acc[d_v:d_v + 1, :]) for (_, acc) in visit(state, last)])


def _cast_specs(weights, grid):
    n_steps = math.prod(grid)
    specs, shapes = [], []
    for w in weights:
        rows, cols = w.shape
        slab = rows // n_steps
        assert slab * n_steps == rows and slab % 16 == 0, (w.shape, n_steps)
        specs.append(pl.BlockSpec((slab, cols), lambda a, b: (a * grid[1] + b, 0)))
        shapes.append(jax.ShapeDtypeStruct(w.shape, BF16))
    return specs, shapes


def _cast_slabs(refs):
    n = len(refs) // 2
    for src, dst in zip(refs[:n], refs[n:]):
        dst[...] = src[...].astype(dst.dtype)


def _stack_chain_outputs(outs, n_half):
    return jnp.concatenate(
        [jnp.concatenate([outs[half * 2 + hh] for half in range(n_half)], axis=1) for hh in (0, 1)], axis=0)


AUG_ROWS = 16


def _bias_rows(n, entries):
    r = lax.broadcasted_iota(jnp.int32, (AUG_ROWS, n), 0)
    out = jnp.zeros((AUG_ROWS, n), F32)
    for row, val in entries.items():
        out = jnp.where(r == row, val, out)
    return out.astype(BF16)


def _fox_kernel(q_ref, k_ref, v_ref, cumt_ref, *rest, tq, n_cast):
    cast_in, (o_ref, *cast_out), (vta_ref, kaug_ref) = rest[:n_cast], rest[n_cast:2 * n_cast + 1], rest[2 * n_cast + 1:]
    seq = k_ref.shape[0]
    dh = FOX_HEAD_DIM
    n_half = tq // HALF
    aug = 6

    vt = v_ref[...].T
    tail_rows = lax.broadcasted_iota(jnp.int32, (16, seq), 0)
    tail = jnp.where(tail_rows < 8, 1.0, 0.0).astype(BF16)
    for hh in (0, 1):
        vta_ref[hh, 0:dh, :] = vt[hh * dh:(hh + 1) * dh, :]
        vta_ref[hh, dh:dh + 16, :] = tail
    key_rows = {}
    for hh in (0, 1):
        for idx, piece in enumerate(_split3(-cumt_ref[hh:hh + 1, :])):
            key_rows[aug * hh + idx] = piece
            key_rows[aug * hh + 3 + idx] = 1.0
    kaug_t = jnp.concatenate([_bias_rows(seq, key_rows), jnp.zeros((LANES - AUG_ROWS, seq), BF16)], axis=0)
    kaug_ref[...] = kaug_t.T

    def keys(hh, k0, n):
        return jnp.concatenate([k_ref[pl.ds(k0, n), :], kaug_ref[pl.ds(k0, n), :]], axis=1)

    def make_rhs(j):
        row = lax.broadcasted_iota(jnp.int32, (LANES, tq), 0)
        pad = jnp.zeros((LANES - AUG_ROWS, tq), BF16)
        out = []
        for t in range(Q_TILES):
            q0 = pl.multiple_of((j * Q_TILES + t) * tq, tq)
            qt = q_ref[pl.ds(q0, tq), :].T
            rhs = []
            for hh in (0, 1):
                qm = jnp.where((row >= hh * dh) & (row < (hh + 1) * dh), qt, jnp.zeros_like(qt))
                query_rows = {}
                for idx, piece in enumerate(_split3(cumt_ref[hh:hh + 1, pl.ds(q0, tq)])):
                    query_rows[aug * hh + idx] = 1.0
                    query_rows[aug * hh + 3 + idx] = piece
                rhs.append(jnp.concatenate([qm, _bias_rows(tq, query_rows), pad], axis=0))
            out.append(tuple(rhs))
        return tuple(out)

    n_super = seq // (Q_TILES * tq)
    per_tile = 2 * n_half

    def super_tile(j, rhs):
        def finish(outs):
            for t in range(Q_TILES):
                q0 = pl.multiple_of((j * Q_TILES + t) * tq, tq)
                tile_outs = outs[t * per_tile:(t + 1) * per_tile]
                o_ref[pl.ds(q0, tq), :] = _stack_chain_outputs(tile_outs, n_half).T.astype(o_ref.dtype)
            return make_rhs(jnp.minimum(j + 1, n_super - 1))

        return _flash_key_major(j, tq, keys, lambda t, hh: rhs[t][hh],
                                lambda hh, ks: vta_ref[hh, :, pl.ds(ks, SUB)], dh, 1, finish)

    lax.fori_loop(0, n_super, super_tile, make_rhs(0))
    _cast_slabs(cast_in + tuple(cast_out))


def _fox_attention(proj, cumt4, seq, q_col0, cast_weights):
    m = proj.shape[0]
    n_batch = m // seq
    n_pairs = FOX_HEADS // 2
    qb = q_col0 // LANES
    grid = (n_batch, n_pairs)
    cast_specs, cast_shapes = _cast_specs(cast_weights, grid)
    return pl.pallas_call(
        functools.partial(_fox_kernel, tq=512, n_cast=len(cast_weights)),
        grid=grid,
        in_specs=[
            pl.BlockSpec((seq, LANES), lambda b, hp: (b, qb + hp)),
            pl.BlockSpec((seq, LANES), lambda b, hp: (b, qb + n_pairs + hp)),
            pl.BlockSpec((seq, LANES), lambda b, hp: (b, qb + 2 * n_pairs + hp)),
            pl.BlockSpec((None, None, 2, seq), lambda b, hp: (b, hp, 0, 0)),
        ] + cast_specs,
        out_specs=[pl.BlockSpec((seq, LANES), lambda b, hp: (b, hp))] + cast_specs,
        out_shape=[jax.ShapeDtypeStruct((m, FOX_WIDTH), BF16)] + cast_shapes,
        scratch_shapes=[
            pltpu.VMEM((2, FOX_HEAD_DIM + 16, seq), BF16),
            pltpu.VMEM((seq, LANES), BF16),
        ],
        compiler_params=_cparams(("parallel", "parallel")),
        name="fox_attn",
    )(proj, proj, proj, cumt4, *cast_weights)


def _mla_kernel(qn_ref, qr_ref, kv_ref, kr_ref, *rest, tq, n_cast):
    cast_in, (o_ref, *cast_out), (vta_ref,) = rest[:n_cast], rest[n_cast:2 * n_cast + 1], rest[2 * n_cast + 1:]
    seq = kv_ref.shape[0]
    dv = MLA_V_DIM
    n_half = tq // HALF

    tail_rows = lax.broadcasted_iota(jnp.int32, (16, seq), 0)
    tail = jnp.where(tail_rows < 8, 1.0, 0.0).astype(BF16)
    for hh in (0, 1):
        vta_ref[hh, 0:dv, :] = kv_ref[:, (2 * hh + 1) * LANES:(2 * hh + 2) * LANES].T
        vta_ref[hh, dv:dv + 16, :] = tail

    def keys(hh, k0, n):
        return jnp.concatenate([kv_ref[pl.ds(k0, n), 2 * hh * LANES:(2 * hh + 1) * LANES],
                                kr_ref[pl.ds(k0, n), :]], axis=1)

    def make_rhs(j):
        out = []
        for t in range(Q_TILES):
            q0 = pl.multiple_of((j * Q_TILES + t) * tq, tq)
            out.append(tuple(jnp.concatenate([qn_ref[pl.ds(q0, tq), hh * LANES:(hh + 1) * LANES],
                                              qr_ref[pl.ds(q0, tq), hh * LANES:(hh + 1) * LANES]], axis=1).T
                             for hh in (0, 1)))
        return tuple(out)

    n_super = seq // (Q_TILES * tq)
    per_tile = 2 * n_half

    def super_tile(j, rhs):
        def finish(outs):
            for t in range(Q_TILES):
                q0 = pl.multiple_of((j * Q_TILES + t) * tq, tq)
                tile_outs = outs[t * per_tile:(t + 1) * per_tile]
                o_ref[pl.ds(q0, tq), :] = _stack_chain_outputs(tile_outs, n_half).T.astype(o_ref.dtype)
            return make_rhs(jnp.minimum(j + 1, n_super - 1))

        return _flash_key_major(j, tq, keys, lambda t, hh: rhs[t][hh],
                                lambda hh, ks: vta_ref[hh, :, pl.ds(ks, SUB)], dv, CHUNK, finish)

    lax.fori_loop(0, n_super, super_tile, make_rhs(0))
    _cast_slabs(cast_in + tuple(cast_out))


def _mla_attention(qn, qr, kv, kr, seq, cast_weights):
    m = qn.shape[0]
    n_batch = m // seq
    pair = 2 * LANES
    grid = (n_batch, MLA_HEADS // 2)
    cast_specs, cast_shapes = _cast_specs(cast_weights, grid)
    return pl.pallas_call(
        functools.partial(_mla_kernel, tq=512, n_cast=len(cast_weights)),
        grid=grid,
        in_specs=[
            pl.BlockSpec((seq, pair), lambda b, hp: (b, hp)),
            pl.BlockSpec((seq, pair), lambda b, hp: (b, hp)),
            pl.BlockSpec((seq, 2 * pair), lambda b, hp: (b, hp)),
            pl.BlockSpec((seq, LANES), lambda b, hp: (b, 0)),
        ] + cast_specs,
        out_specs=[pl.BlockSpec((seq, pair), lambda b, hp: (b, hp))] + cast_specs,
        out_shape=[jax.ShapeDtypeStruct((m, MLA_HEADS * MLA_V_DIM), BF16)] + cast_shapes,
        scratch_shapes=[pltpu.VMEM((2, MLA_V_DIM + 16, seq), BF16)],
        compiler_params=_cparams(("parallel", "parallel")),
        name="mla_attn",
    )(qn, qr, kv, kr, *cast_weights)


def _mix_kernel(yf_ref, ym_ref, gf_ref, gm_ref, x_ref, mod_ref, wbf_ref, wbm_ref, wo_ref, g_ref, b_ref,
                o_ref, mg_ref, *, n_chunk, g_off):
    d = o_ref.shape[1]
    yf = yf_ref[...]
    ym = ym_ref[...]

    def gate(ref, c):
        window = ref[:, c * n_chunk:(c + 1) * n_chunk + LANES]
        return jax.nn.sigmoid(window[:, g_off:g_off + n_chunk].astype(F32))

    for c in range(d // n_chunk):
        cols = slice(c * n_chunk, (c + 1) * n_chunk)
        a = jnp.dot(yf, wbf_ref[:, cols], preferred_element_type=F32)
        b = jnp.dot(ym, wbm_ref[:, cols], preferred_element_type=F32)
        merged = gate(gf_ref, c) * a + gate(gm_ref, c) * b
        mg_ref[:, cols] = merged.astype(BF16)
    mix = jnp.dot(mg_ref[...], wo_ref[...], preferred_element_type=F32)
    gate1 = mod_ref[2:3, :]
    o_ref[...] = _layer_norm(ALPHA * x_ref[...] + gate1 * mix, g_ref[...], b_ref[...])


def _mix(y_fox, y_mla, proj, gf_col, gm_col, x2d, mod3, w_bf, w_bm, w_o, ln_g, ln_b, seq):
    m, d = x2d.shape
    tm = 512
    tiles_per_batch = seq // tm
    const = lambda i: (0, 0)
    resident = dict(pipeline_mode=pl.Buffered(1))
    g_off = gf_col % LANES
    assert gm_col % LANES == g_off
    g_window = (pl.Element(tm), pl.Element(d + LANES))

    def window_at(col):
        return lambda i: (pl.multiple_of(i * tm, tm), col - g_off)

    return pl.pallas_call(
        functools.partial(_mix_kernel, n_chunk=512, g_off=g_off),
        grid=(m // tm,),
        in_specs=[
            pl.BlockSpec((tm, y_fox.shape[1]), lambda i: (i, 0)),
            pl.BlockSpec((tm, y_mla.shape[1]), lambda i: (i, 0)),
            pl.BlockSpec(g_window, window_at(gf_col)),
            pl.BlockSpec(g_window, window_at(gm_col)),
            pl.BlockSpec((tm, d), lambda i: (i, 0)),
            pl.BlockSpec((None, N_ADA, d), lambda i: (i // tiles_per_batch, 0, 0)),
            pl.BlockSpec(w_bf.shape, const, **resident),
            pl.BlockSpec(w_bm.shape, const, **resident),
            pl.BlockSpec(w_o.shape, const, **resident),
            pl.BlockSpec((1, d), const),
            pl.BlockSpec((1, d), const),
        ],
        out_specs=pl.BlockSpec((tm, d), lambda i: (i, 0)),
        out_shape=jax.ShapeDtypeStruct((m, d), F32),
        scratch_shapes=[pltpu.VMEM((tm, d), BF16)],
        compiler_params=_cparams(("parallel",)),
        name="mix",
    )(y_fox, y_mla, proj, proj, x2d, mod3, w_bf, w_bm, w_o, ln_g, ln_b)


def _mlp_kernel(x_ref, mod_ref, wu_ref, wd_ref, g_ref, b_ref, o_ref, u_ref):
    f = pl.program_id(1)

    def partial_sum(u):
        z = jnp.dot(u, wu_ref[...], preferred_element_type=F32)
        h = jnp.square(jnp.maximum(z, 0.0)).astype(BF16)
        return jnp.dot(h, wd_ref[...], preferred_element_type=F32)

    @pl.when(f == 0)
    def _():
        shift = mod_ref[3:4, :]
        scale = mod_ref[4:5, :]
        u = (x_ref[...] * (1.0 + scale) + shift).astype(BF16)
        u_ref[...] = u
        o_ref[...] = partial_sum(u)

    @pl.when(f != 0)
    def _():
        o_ref[...] += partial_sum(u_ref[...])

    @pl.when(f == pl.num_programs(1) - 1)
    def _():
        gate = mod_ref[5:6, :]
        o_ref[...] = _layer_norm(ALPHA * x_ref[...] + gate * o_ref[...], g_ref[...], b_ref[...])


def _mlp(x1, mod3, w_up, w_down, ln_g, ln_b, seq):
    m, d = x1.shape
    d_ff = w_up.shape[1]
    tm, tf = 1024, 512
    tiles_per_batch = seq // tm
    return pl.pallas_call(
        _mlp_kernel,
        grid=(m // tm, d_ff // tf),
        in_specs=[
            pl.BlockSpec((tm, d), lambda i, f: (i, 0)),
            pl.BlockSpec((None, N_ADA, d), lambda i, f: (i // tiles_per_batch, 0, 0)),
            pl.BlockSpec((d, tf), lambda i, f: (0, f)),
            pl.BlockSpec((tf, d), lambda i, f: (f, 0)),
            pl.BlockSpec((1, d), lambda i, f: (0, 0)),
            pl.BlockSpec((1, d), lambda i, f: (0, 0)),
        ],
        out_specs=pl.BlockSpec((tm, d), lambda i, f: (i, 0)),
        out_shape=jax.ShapeDtypeStruct((m, d), F32),
        scratch_shapes=[pltpu.VMEM((tm, d), BF16)],
        compiler_params=_cparams(("parallel", "arbitrary")),
        name="mlp",
    )(x1, mod3, w_up, w_down, ln_g, ln_b)


def _pad_cols(w, width):
    return jnp.pad(w, ((0, 0), (0, width - w.shape[1])))


def _swap_halves(w):
    half = w.shape[-1] // 2
    return jnp.concatenate([w[..., half:], w[..., :half]], axis=-1)


def _layout_q_up(w_q_up):
    r = w_q_up.shape[0]
    w3 = w_q_up.reshape(r, MLA_HEADS, MLA_QK_DIM)
    w_nope = w3[:, :, :MLA_NOPE_DIM].reshape(r, MLA_HEADS * MLA_NOPE_DIM)
    w_rope = w3[:, :, MLA_NOPE_DIM:]
    pad = ((0, 0), (0, 0), (0, LANES - MLA_ROPE_DIM))
    w_r = jnp.pad(w_rope, pad).reshape(r, MLA_HEADS * LANES)
    w_rs = jnp.pad(_swap_halves(w_rope), pad).reshape(r, MLA_HEADS * LANES)
    return jnp.concatenate([w_nope, w_r, w_rs], axis=1).astype(BF16)


def _rope_tables(seq):
    pos = np.arange(seq, dtype=np.float64)
    inv_freq = ROPE_THETA ** (-np.arange(0, MLA_ROPE_DIM, 2, dtype=np.float64) / MLA_ROPE_DIM)
    ang = pos[:, None] * inv_freq[None, :]
    cos, sin = np.cos(ang), np.sin(ang)
    pad = np.zeros((seq, LANES - MLA_ROPE_DIM))
    cos_t = np.concatenate([cos, cos, pad], axis=1).astype(np.float32)
    sin_t = np.concatenate([-sin, sin, pad], axis=1).astype(np.float32)
    return jnp.asarray(cos_t), jnp.asarray(sin_t)


def kernel(x, c, w_ada, b_ada, w_in, b_forget, g_q_norm, w_q_up, g_kv_norm, w_kv_up, w_branch_fox,
           w_branch_mla, w_out, ln1_g, ln1_b, w_mlp_up, w_mlp_down, ln2_g, ln2_b):
    n_batch, seq, d = x.shape
    assert w_ada.shape[0] == DEPTH
    m = n_batch * seq
    x2d = x.reshape(m, d)

    mod3 = _ada(c, w_ada[0], b_ada[0]).reshape(n_batch, N_ADA, d)

    small_col = 3 * FOX_WIDTH
    gf_col = small_col + FOX_HEADS + MLA_Q_LORA + MLA_KV_LORA + MLA_ROPE_DIM
    gm_col = gf_col + d
    assert w_in.shape[2] == gm_col + d and small_col == IN_SMALL_TILE * IN_TN and gf_col <= small_col + IN_TN
    proj, narrow_hi, narrow_lo = _inproj(x2d, mod3, w_in[0].T, seq)

    cos_t, sin_t = _rope_tables(seq)
    bf_pad = _pad_cols(b_forget[0].reshape(1, FOX_HEADS), LANES)
    qn, qr, kv, kr, cumt = _prep(
        narrow_hi, narrow_lo, bf_pad, g_q_norm[0].reshape(1, -1), g_kv_norm[0].reshape(1, -1),
        _layout_q_up(w_q_up[0]), w_kv_up[0].astype(BF16), cos_t, sin_t, seq)

    y_fox, w_up_b, w_down_b = _fox_attention(
        proj, cumt.reshape(n_batch, FOX_HEADS // 2, 2, seq), seq, 0, (w_mlp_up[0], w_mlp_down[0]))
    y_mla, w_bf_b, w_bm_b, w_o_b = _mla_attention(
        qn, qr, kv, kr, seq, (w_branch_fox[0], w_branch_mla[0], w_out[0]))

    x1 = _mix(y_fox, y_mla, proj, gf_col, gm_col, x2d, mod3, w_bf_b, w_bm_b, w_o_b,
              ln1_g[0].reshape(1, d), ln1_b[0].reshape(1, d), seq)
    x2 = _mlp(x1, mod3, w_up_b, w_down_b, ln2_g[0].reshape(1, d), ln2_b[0].reshape(1, d), seq)
    return x2.reshape(n_batch, seq, d)
```

```python
import functools
import math

import jax
import jax.numpy as jnp
import numpy as np
from jax import lax
from jax.experimental import pallas as pl
from jax.experimental.pallas import tpu as pltpu

F32 = jnp.float32
BF16 = jnp.bfloat16

FOX_HEADS = 16
FOX_HEAD_DIM = 64
FOX_WIDTH = FOX_HEADS * FOX_HEAD_DIM
MLA_HEADS = 8
MLA_NOPE_DIM = 128
MLA_ROPE_DIM = 64
MLA_V_DIM = 128
MLA_Q_LORA = 512
MLA_KV_LORA = 256
MLA_QK_DIM = MLA_NOPE_DIM + MLA_ROPE_DIM
CHUNK = 64
ROPE_THETA = 10000.0
LN_EPS = 1e-5
RMS_EPS = 1e-6
N_ADA = 6
DEPTH = 1
ALPHA = (2.0 * DEPTH) ** 0.25

LANES = 128
LOG2E = math.log2(math.e)
FOX_QSCALE = FOX_HEAD_DIM ** -0.5 * LOG2E
MLA_QSCALE = MLA_QK_DIM ** -0.5 * LOG2E
NEG_BIG = -1e30
VMEM_LIMIT = 56 * 1024 * 1024

SUB = 128
HALF = 256


def _cparams(semantics):
    return pltpu.CompilerParams(dimension_semantics=semantics, vmem_limit_bytes=VMEM_LIMIT)


def _layer_norm(y, g, b):
    mu = jnp.mean(y, axis=-1, keepdims=True)
    yc = y - mu
    var = jnp.mean(yc * yc, axis=-1, keepdims=True)
    return yc * lax.rsqrt(var + LN_EPS) * g + b


def _rms_norm(c, g):
    return c * lax.rsqrt(jnp.mean(c * c, axis=-1, keepdims=True) + RMS_EPS) * g


def _ada_kernel(ct_ref, w_ref, b_ref, o_ref, *, n_batch, k_chunk):
    d = w_ref.shape[0]
    ct = ct_ref[...]
    s = ct * jax.nn.sigmoid(ct)
    accs = [b_ref[...] for _ in range(n_batch)]
    for kc in range(d // k_chunk):
        rows = slice(kc * k_chunk, (kc + 1) * k_chunk)
        w = w_ref[rows, :]
        for b in range(n_batch):
            accs[b] = accs[b] + jnp.sum(w * s[rows, b:b + 1], axis=0, keepdims=True)
    for b in range(n_batch):
        o_ref[b:b + 1, :] = accs[b]


def _ada(c, w_ada, b_ada):
    n_batch, d = c.shape
    n = w_ada.shape[1]
    tn = 1024
    return pl.pallas_call(
        functools.partial(_ada_kernel, n_batch=n_batch, k_chunk=256),
        grid=(n // tn,),
        in_specs=[
            pl.BlockSpec((d, n_batch), lambda j: (0, 0)),
            pl.BlockSpec((d, tn), lambda j: (0, j)),
            pl.BlockSpec((1, tn), lambda j: (0, j)),
        ],
        out_specs=pl.BlockSpec((n_batch, tn), lambda j: (0, j)),
        out_shape=jax.ShapeDtypeStruct((n_batch, n), F32),
        compiler_params=_cparams(("arbitrary",)),
        name="ada",
    )(c.T, w_ada, b_ada.reshape(1, n))


IN_TN = 1024
IN_SMALL_TILE = 3 * FOX_WIDTH // IN_TN
NARROW_F = (0, LANES)
NARROW_CQ = (NARROW_F[1], NARROW_F[1] + MLA_Q_LORA)
NARROW_CKV = (NARROW_CQ[1], NARROW_CQ[1] + MLA_KV_LORA)
NARROW_KR = (NARROW_CKV[1], NARROW_CKV[1] + LANES)
assert NARROW_KR[1] == IN_TN and 2 * MLA_ROPE_DIM == LANES


def _narrow_segments(acc):
    c0 = FOX_HEADS
    c1 = c0 + MLA_Q_LORA
    c2 = c1 + MLA_KV_LORA
    c3 = c2 + MLA_ROPE_DIM
    half = MLA_ROPE_DIM // 2
    return jnp.concatenate([acc[:, 0:LANES], acc[:, c0:c1], acc[:, c1:c2], acc[:, c2:c3],
                            acc[:, c2 + half:c3], acc[:, c2:c2 + half]], axis=1)


def _inproj_kernel(x_ref, mod_ref, w_ref, o_ref, hi_ref, lo_ref, u_ref, *, n_cols):
    j = pl.program_id(1)
    last = pl.num_programs(1) - 1

    def product(wt, u=None):
        u = u_ref[...] if u is None else u
        return lax.dot_general(u, wt.astype(BF16), (((1,), (1,)), ((), ())), preferred_element_type=F32)

    @pl.when(j == 0)
    def _():
        shift = mod_ref[0:1, :]
        scale = mod_ref[1:2, :]
        u = (x_ref[...] * (1.0 + scale) + shift).astype(BF16)
        u_ref[...] = u
        o_ref[...] = (product(w_ref[...], u) * FOX_QSCALE).astype(o_ref.dtype)

    @pl.when(j == IN_SMALL_TILE)
    def _():
        acc = product(w_ref[...])
        o_ref[...] = acc.astype(o_ref.dtype)
        narrow = _narrow_segments(acc)
        hi = narrow.astype(hi_ref.dtype)
        hi_ref[...] = hi
        lo_ref[...] = (narrow - hi.astype(F32)).astype(lo_ref.dtype)

    @pl.when(j == last)
    def _():
        row = lax.broadcasted_iota(jnp.int32, w_ref.shape, 0)
        wt = jnp.where(row < n_cols - last * IN_TN, w_ref[...], 0.0)
        o_ref[...] = product(wt).astype(o_ref.dtype)

    @pl.when((j != 0) & (j != IN_SMALL_TILE) & (j != last))
    def _():
        o_ref[...] = product(w_ref[...]).astype(o_ref.dtype)


def _inproj(x2d, mod3, w_in_t, seq):
    m, d = x2d.shape
    n_cols = w_in_t.shape[0]
    tm = 1024
    n_tiles = pl.cdiv(n_cols, IN_TN)
    tiles_per_batch = seq // tm
    return pl.pallas_call(
        functools.partial(_inproj_kernel, n_cols=n_cols),
        grid=(m // tm, n_tiles),
        in_specs=[
            pl.BlockSpec((tm, d), lambda i, j: (i, 0)),
            pl.BlockSpec((None, N_ADA, d), lambda i, j: (i // tiles_per_batch, 0, 0)),
            pl.BlockSpec((IN_TN, d), lambda i, j: (j, 0)),
        ],
        out_specs=[pl.BlockSpec((tm, IN_TN), lambda i, j: (i, j)),
                   pl.BlockSpec((tm, IN_TN), lambda i, j: (i, 0)),
                   pl.BlockSpec((tm, IN_TN), lambda i, j: (i, 0))],
        out_shape=[jax.ShapeDtypeStruct((m, n_tiles * IN_TN), BF16),
                   jax.ShapeDtypeStruct((m, IN_TN), BF16),
                   jax.ShapeDtypeStruct((m, IN_TN), BF16)],
        scratch_shapes=[pltpu.VMEM((tm, d), BF16)],
        compiler_params=_cparams(("parallel", "arbitrary")),
        name="inproj",
    )(x2d, mod3, w_in_t)


def _log_sigmoid(x):
    return jnp.minimum(x, 0.0) - jnp.log1p(jnp.exp(-jnp.abs(x)))


def _prep_kernel(hi_ref, lo_ref, bf_ref, gq_ref, gkv_ref, wq_ref, wkv_ref, cos_ref, sin_ref,
                 qn_ref, qr_ref, kv_ref, kr_ref, cumt_ref, carry_ref, *, tiles_per_batch):
    i = pl.program_id(0)
    tm = hi_ref.shape[0]
    ps = hi_ref[...].astype(F32) + lo_ref[...].astype(F32)
    k_rope = ps[:, NARROW_KR[0]:NARROW_KR[1]]
    k_rope_partner = pltpu.roll(k_rope, MLA_ROPE_DIM, axis=1)
    cos = cos_ref[...]
    sin = sin_ref[...]

    logf = _log_sigmoid(ps[:, NARROW_F[0]:NARROW_F[1]] + bf_ref[...]) * LOG2E
    hi = logf.astype(BF16)
    r1 = logf - hi.astype(F32)
    mid = r1.astype(BF16)
    lo = (r1 - mid.astype(F32)).astype(BF16)
    row = lax.broadcasted_iota(jnp.int32, (tm, tm), 0)
    col = lax.broadcasted_iota(jnp.int32, (tm, tm), 1)
    tri = (row >= col).astype(BF16)
    cs = (jnp.dot(tri, hi, preferred_element_type=F32)
          + jnp.dot(tri, mid, preferred_element_type=F32)
          + jnp.dot(tri, lo, preferred_element_type=F32))

    @pl.when(i % tiles_per_batch == 0)
    def _():
        carry_ref[...] = jnp.zeros_like(carry_ref)

    cum = cs + carry_ref[...]
    carry_ref[...] = cum[tm - 1:tm, :]
    cumt_ref[...] = cum.T[0:FOX_HEADS, :]

    cqn = _rms_norm(ps[:, NARROW_CQ[0]:NARROW_CQ[1]], gq_ref[...]).astype(BF16)
    qa = jnp.dot(cqn, wq_ref[...], preferred_element_type=F32)
    hw = MLA_HEADS * LANES
    qn_ref[...] = (qa[:, 0:hw] * MLA_QSCALE).astype(BF16)
    for h in range(MLA_HEADS):
        a = qa[:, hw + h * LANES: hw + (h + 1) * LANES]
        b = qa[:, 2 * hw + h * LANES: 2 * hw + (h + 1) * LANES]
        qr_ref[:, h * LANES:(h + 1) * LANES] = ((a * cos + b * sin) * MLA_QSCALE).astype(BF16)

    ckvn = _rms_norm(ps[:, NARROW_CKV[0]:NARROW_CKV[1]], gkv_ref[...]).astype(BF16)
    kv_ref[...] = jnp.dot(ckvn, wkv_ref[...], preferred_element_type=F32).astype(BF16)
    kr_ref[...] = (k_rope * cos + k_rope_partner * sin).astype(BF16)


def _prep(narrow_hi, narrow_lo, bf_pad, g_q, g_kv, w_q_all, w_kv, cos_t, sin_t, seq):
    m = narrow_hi.shape[0]
    n_batch = m // seq
    tm = 512
    tiles_per_batch = seq // tm
    hw = MLA_HEADS * LANES
    const = lambda i: (0, 0)
    return pl.pallas_call(
        functools.partial(_prep_kernel, tiles_per_batch=tiles_per_batch),
        grid=(m // tm,),
        in_specs=[
            pl.BlockSpec((tm, IN_TN), lambda i: (i, 0)),
            pl.BlockSpec((tm, IN_TN), lambda i: (i, 0)),
            pl.BlockSpec(bf_pad.shape, const),
            pl.BlockSpec(g_q.shape, const),
            pl.BlockSpec(g_kv.shape, const),
            pl.BlockSpec(w_q_all.shape, const),
            pl.BlockSpec(w_kv.shape, const),
            pl.BlockSpec((tm, LANES), lambda i: (i % tiles_per_batch, 0)),
            pl.BlockSpec((tm, LANES), lambda i: (i % tiles_per_batch, 0)),
        ],
        out_specs=[
            pl.BlockSpec((tm, hw), lambda i: (i, 0)),
            pl.BlockSpec((tm, hw), lambda i: (i, 0)),
            pl.BlockSpec((tm, w_kv.shape[1]), lambda i: (i, 0)),
            pl.BlockSpec((tm, LANES), lambda i: (i, 0)),
            pl.BlockSpec((None, FOX_HEADS, tm), lambda i: (i // tiles_per_batch, 0, i % tiles_per_batch)),
        ],
        out_shape=[
            jax.ShapeDtypeStruct((m, hw), BF16),
            jax.ShapeDtypeStruct((m, hw), BF16),
            jax.ShapeDtypeStruct((m, w_kv.shape[1]), BF16),
            jax.ShapeDtypeStruct((m, LANES), BF16),
            jax.ShapeDtypeStruct((n_batch, FOX_HEADS, seq), F32),
        ],
        scratch_shapes=[pltpu.VMEM((1, LANES), F32)],
        compiler_params=_cparams(("arbitrary",)),
        name="prep",
    )(narrow_hi, narrow_lo, bf_pad, g_q, g_kv, w_q_all, w_kv, cos_t, sin_t)


def _split3(x):
    hi = x.astype(BF16).astype(F32)
    r = x - hi
    mid = r.astype(BF16).astype(F32)
    lo = (r - mid).astype(BF16).astype(F32)
    return hi, mid, lo


def _online_softmax_step(t, m, l, acc, vt):
    d_v = acc.shape[0]
    m_new = jnp.maximum(m, jnp.max(t, axis=0, keepdims=True))
    alpha = jnp.exp2(m - m_new)
    p = jnp.exp2(t - m_new)
    pv = jnp.dot(vt, p.astype(BF16), preferred_element_type=F32)
    if vt.shape[0] > d_v:
        row_sum = pv[d_v:d_v + 1, :]
    else:
        row_sum = jnp.sum(p, axis=0, keepdims=True)
    return m_new, alpha * l + row_sum, alpha * acc + pv[0:d_v, :]


def _causal_mask(t, key0, query0, granule):
    r = lax.broadcasted_iota(jnp.int32, t.shape, 0) + key0
    c = lax.broadcasted_iota(jnp.int32, t.shape, 1) + query0
    return jnp.where(r <= (c | (granule - 1)), t, NEG_BIG)


Q_TILES = 8
FULL, DIAG, SKIP = "full", "diag", "skip"


def _flash_key_major(j, tile, keys, rhs, vt, d_v, granule, finish):
    n_rounds = tile // SUB
    chains = [(qt, half, hh) for qt in range(Q_TILES) for half in range(tile // HALF) for hh in (0, 1)]
    pairs = [(n, n + 1) for n in range(0, len(chains), 2)]

    def qk_of(k0, modes):
        loaded = {}

        def product(n):
            qt, half, hh = chains[n]
            n_keys = (half + 1) * HALF if modes[qt] == DIAG else tile
            if (hh, n_keys) not in loaded:
                loaded[hh, n_keys] = keys(hh, k0, n_keys)
            return jnp.dot(loaded[hh, n_keys], rhs(qt, hh)[:, half * HALF:(half + 1) * HALF],
                           preferred_element_type=F32)
        return product

    def visit(state, tiles):
        st = list(state)
        products = [qk_of(*t) for t in tiles]
        phases = [(ti, pair) for ti, (_, modes) in enumerate(tiles) for pair in pairs
                  if modes[chains[pair[0]][0]] != SKIP]

        def issue(phase):
            ti, pair = phase
            return {n: products[ti](n) for n in pair}

        cur = issue(phases[0])
        for idx, (ti, pair) in enumerate(phases):
            nxt = issue(phases[idx + 1]) if idx + 1 < len(phases) else None
            k0, modes = tiles[ti]
            for sb in range(n_rounds):
                ks = pl.multiple_of(k0 + sb * SUB, SUB)
                for n in pair:
                    qt, half, hh = chains[n]
                    is_diag = modes[qt] == DIAG
                    if is_diag and sb * SUB >= (half + 1) * HALF:
                        continue
                    t = cur[n][sb * SUB:(sb + 1) * SUB, :]
                    if is_diag and (sb + 1) * SUB > half * HALF:
                        t = _causal_mask(t, sb * SUB, half * HALF, granule)
                    st[n] = _online_softmax_step(t, *st[n], vt(hh, ks))
            cur = nxt
        return tuple(st)

    everyone = (FULL,) * Q_TILES

    def two_full(i, state):
        k0 = pl.multiple_of(2 * i * tile, tile)
        return visit(state, [(k0, everyone), (pl.multiple_of(k0 + tile, tile), everyone)])

    assert Q_TILES % 2 == 0
    init = tuple((jnp.full((1, HALF), NEG_BIG, F32), jnp.zeros((1, HALF), F32), jnp.zeros((d_v, HALF), F32))
                 for _ in chains)
    state = lax.fori_loop(0, j * (Q_TILES // 2), two_full, init)
    q0 = j * (Q_TILES * tile)
    last = [(pl.multiple_of(q0 + t * tile, tile),
             tuple(SKIP if qt < t else DIAG if qt == t else FULL for qt in range(Q_TILES)))
            for t in range(Q_TILES)]
    return finish([acc * (1.0 / l) for (_, l, acc) in visit(state, last)])


def _cast_specs(weights, grid):
    n_steps = math.prod(grid)
    specs, shapes = [], []
    for w in weights:
        rows, cols = w.shape
        slab = rows // n_steps
        assert slab * n_steps == rows and slab % 16 == 0, (w.shape, n_steps)
        specs.append(pl.BlockSpec((slab, cols), lambda a, b: (a * grid[1] + b, 0)))
        shapes.append(jax.ShapeDtypeStruct(w.shape, BF16))
    return specs, shapes


def _cast_slabs(refs):
    n = len(refs) // 2
    for src, dst in zip(refs[:n], refs[n:]):
        dst[...] = src[...].astype(dst.dtype)


def _stack_chain_outputs(outs, n_half):
    return jnp.concatenate(
        [jnp.concatenate([outs[half * 2 + hh] for half in range(n_half)], axis=1) for hh in (0, 1)], axis=0)


AUG_ROWS = 16


def _bias_rows(n, entries):
    r = lax.broadcasted_iota(jnp.int32, (AUG_ROWS, n), 0)
    out = jnp.zeros((AUG_ROWS, n), F32)
    for row, val in entries.items():
        out = jnp.where(r == row, val, out)
    return out.astype(BF16)


def _fox_kernel(q_ref, k_ref, v_ref, cumt_ref, *rest, tq, n_cast):
    cast_in, (o_ref, *cast_out), (vta_ref, kaug_ref) = rest[:n_cast], rest[n_cast:2 * n_cast + 1], rest[2 * n_cast + 1:]
    seq = k_ref.shape[0]
    dh = FOX_HEAD_DIM
    n_half = tq // HALF
    aug = 6

    vta_ref[...] = v_ref[...].T
    key_rows = {}
    for hh in (0, 1):
        for idx, piece in enumerate(_split3(-cumt_ref[hh:hh + 1, :])):
            key_rows[aug * hh + idx] = piece
            key_rows[aug * hh + 3 + idx] = 1.0
    kaug_t = jnp.concatenate([_bias_rows(seq, key_rows), jnp.zeros((LANES - AUG_ROWS, seq), BF16)], axis=0)
    kaug_ref[...] = kaug_t.T

    def keys(hh, k0, n):
        return jnp.concatenate([k_ref[pl.ds(k0, n), :], kaug_ref[pl.ds(k0, n), :]], axis=1)

    def make_rhs(j):
        row = lax.broadcasted_iota(jnp.int32, (LANES, tq), 0)
        pad = jnp.zeros((LANES - AUG_ROWS, tq), BF16)
        out = []
        for t in range(Q_TILES):
            q0 = pl.multiple_of((j * Q_TILES + t) * tq, tq)
            qt = q_ref[pl.ds(q0, tq), :].T
            rhs = []
            for hh in (0, 1):
                qm = jnp.where((row >= hh * dh) & (row < (hh + 1) * dh), qt, jnp.zeros_like(qt))
                query_rows = {}
                for idx, piece in enumerate(_split3(cumt_ref[hh:hh + 1, pl.ds(q0, tq)])):
                    query_rows[aug * hh + idx] = 1.0
                    query_rows[aug * hh + 3 + idx] = piece
                rhs.append(jnp.concatenate([qm, _bias_rows(tq, query_rows), pad], axis=0))
            out.append(tuple(rhs))
        return tuple(out)

    n_super = seq // (Q_TILES * tq)
    per_tile = 2 * n_half

    def super_tile(j, rhs):
        def finish(outs):
            for t in range(Q_TILES):
                q0 = pl.multiple_of((j * Q_TILES + t) * tq, tq)
                tile_outs = outs[t * per_tile:(t + 1) * per_tile]
                o_ref[pl.ds(q0, tq), :] = _stack_chain_outputs(tile_outs, n_half).T.astype(o_ref.dtype)
            return make_rhs(jnp.minimum(j + 1, n_super - 1))

        return _flash_key_major(j, tq, keys, lambda t, hh: rhs[t][hh],
                                lambda hh, ks: vta_ref[hh * dh:(hh + 1) * dh, pl.ds(ks, SUB)], dh, 1, finish)

    lax.fori_loop(0, n_super, super_tile, make_rhs(0))
    _cast_slabs(cast_in + tuple(cast_out))


def _fox_attention(proj, cumt4, seq, q_col0, cast_weights):
    m = proj.shape[0]
    n_batch = m // seq
    n_pairs = FOX_HEADS // 2
    qb = q_col0 // LANES
    grid = (n_batch, n_pairs)
    cast_specs, cast_shapes = _cast_specs(cast_weights, grid)
    return pl.pallas_call(
        functools.partial(_fox_kernel, tq=512, n_cast=len(cast_weights)),
        grid=grid,
        in_specs=[
            pl.BlockSpec((seq, LANES), lambda b, hp: (b, qb + hp)),
            pl.BlockSpec((seq, LANES), lambda b, hp: (b, qb + n_pairs + hp)),
            pl.BlockSpec((seq, LANES), lambda b, hp: (b, qb + 2 * n_pairs + hp)),
            pl.BlockSpec((None, None, 2, seq), lambda b, hp: (b, hp, 0, 0)),
        ] + cast_specs,
        out_specs=[pl.BlockSpec((seq, LANES), lambda b, hp: (b, hp))] + cast_specs,
        out_shape=[jax.ShapeDtypeStruct((m, FOX_WIDTH), BF16)] + cast_shapes,
        scratch_shapes=[
            pltpu.VMEM((2 * FOX_HEAD_DIM, seq), BF16),
            pltpu.VMEM((seq, LANES), BF16),
        ],
        compiler_params=_cparams(("parallel", "parallel")),
        name="fox_attn",
    )(proj, proj, proj, cumt4, *cast_weights)


def _mla_kernel(qn_ref, qr_ref, kv_ref, kr_ref, *rest, tq, n_cast):
    cast_in, (o_ref, *cast_out), (vta_ref,) = rest[:n_cast], rest[n_cast:2 * n_cast + 1], rest[2 * n_cast + 1:]
    seq = kv_ref.shape[0]
    dv = MLA_V_DIM
    n_half = tq // HALF

    rows = dv + 16
    tail_rows = lax.broadcasted_iota(jnp.int32, (16, seq), 0)
    tail = jnp.where(tail_rows < 8, 1.0, 0.0).astype(BF16)
    for hh in (0, 1):
        vta_ref[hh * rows:hh * rows + dv, :] = kv_ref[:, (2 * hh + 1) * LANES:(2 * hh + 2) * LANES].T
        vta_ref[hh * rows + dv:(hh + 1) * rows, :] = tail

    def keys(hh, k0, n):
        return jnp.concatenate([kv_ref[pl.ds(k0, n), 2 * hh * LANES:(2 * hh + 1) * LANES],
                                kr_ref[pl.ds(k0, n), :]], axis=1)

    def make_rhs(j):
        out = []
        for t in range(Q_TILES):
            q0 = pl.multiple_of((j * Q_TILES + t) * tq, tq)
            out.append(tuple(jnp.concatenate([qn_ref[pl.ds(q0, tq), hh * LANES:(hh + 1) * LANES],
                                              qr_ref[pl.ds(q0, tq), hh * LANES:(hh + 1) * LANES]], axis=1).T
                             for hh in (0, 1)))
        return tuple(out)

    n_super = seq // (Q_TILES * tq)
    per_tile = 2 * n_half

    def super_tile(j, rhs):
        def finish(outs):
            for t in range(Q_TILES):
                q0 = pl.multiple_of((j * Q_TILES + t) * tq, tq)
                tile_outs = outs[t * per_tile:(t + 1) * per_tile]
                o_ref[pl.ds(q0, tq), :] = _stack_chain_outputs(tile_outs, n_half).T.astype(o_ref.dtype)
            return make_rhs(jnp.minimum(j + 1, n_super - 1))

        return _flash_key_major(j, tq, keys, lambda t, hh: rhs[t][hh],
                                lambda hh, ks: vta_ref[hh * rows:(hh + 1) * rows, pl.ds(ks, SUB)], dv, CHUNK, finish)

    lax.fori_loop(0, n_super, super_tile, make_rhs(0))
    _cast_slabs(cast_in + tuple(cast_out))


def _mla_attention(qn, qr, kv, kr, seq, cast_weights):
    m = qn.shape[0]
    n_batch = m // seq
    pair = 2 * LANES
    grid = (n_batch, MLA_HEADS // 2)
    cast_specs, cast_shapes = _cast_specs(cast_weights, grid)
    return pl.pallas_call(
        functools.partial(_mla_kernel, tq=512, n_cast=len(cast_weights)),
        grid=grid,
        in_specs=[
            pl.BlockSpec((seq, pair), lambda b, hp: (b, hp)),
            pl.BlockSpec((seq, pair), lambda b, hp: (b, hp)),
            pl.BlockSpec((seq, 2 * pair), lambda b, hp: (b, hp)),
            pl.BlockSpec((seq, LANES), lambda b, hp: (b, 0)),
        ] + cast_specs,
        out_specs=[pl.BlockSpec((seq, pair), lambda b, hp: (b, hp))] + cast_specs,
        out_shape=[jax.ShapeDtypeStruct((m, MLA_HEADS * MLA_V_DIM), BF16)] + cast_shapes,
        scratch_shapes=[pltpu.VMEM((2 * (MLA_V_DIM + 16), seq), BF16)],
        compiler_params=_cparams(("parallel", "parallel")),
        name="mla_attn",
    )(qn, qr, kv, kr, *cast_weights)


def _mix_kernel(yf_ref, ym_ref, gf_ref, gm_ref, x_ref, mod_ref, wbf_ref, wbm_ref, wo_ref, g_ref, b_ref,
                o_ref, mg_ref, *, n_chunk, g_off):
    d = o_ref.shape[1]
    yf = yf_ref[...]
    ym = ym_ref[...]

    def gate(ref, c):
        window = ref[:, c * n_chunk:(c + 1) * n_chunk + LANES]
        return jax.nn.sigmoid(window[:, g_off:g_off + n_chunk].astype(F32))

    for c in range(d // n_chunk):
        cols = slice(c * n_chunk, (c + 1) * n_chunk)
        a = jnp.dot(yf, wbf_ref[:, cols], preferred_element_type=F32)
        b = jnp.dot(ym, wbm_ref[:, cols], preferred_element_type=F32)
        merged = gate(gf_ref, c) * a + gate(gm_ref, c) * b
        mg_ref[:, cols] = merged.astype(BF16)
    mix = jnp.dot(mg_ref[...], wo_ref[...], preferred_element_type=F32)
    gate1 = mod_ref[2:3, :]
    o_ref[...] = _layer_norm(ALPHA * x_ref[...] + gate1 * mix, g_ref[...], b_ref[...])


def _mix(y_fox, y_mla, proj, gf_col, gm_col, x2d, mod3, w_bf, w_bm, w_o, ln_g, ln_b, seq):
    m, d = x2d.shape
    tm = 512
    tiles_per_batch = seq // tm
    const = lambda i: (0, 0)
    resident = dict(pipeline_mode=pl.Buffered(1))
    g_off = gf_col % LANES
    assert gm_col % LANES == g_off
    g_window = (pl.Element(tm), pl.Element(d + LANES))

    def window_at(col):
        return lambda i: (pl.multiple_of(i * tm, tm), col - g_off)

    return pl.pallas_call(
        functools.partial(_mix_kernel, n_chunk=512, g_off=g_off),
        grid=(m // tm,),
        in_specs=[
            pl.BlockSpec((tm, y_fox.shape[1]), lambda i: (i, 0)),
            pl.BlockSpec((tm, y_mla.shape[1]), lambda i: (i, 0)),
            pl.BlockSpec(g_window, window_at(gf_col)),
            pl.BlockSpec(g_window, window_at(gm_col)),
            pl.BlockSpec((tm, d), lambda i: (i, 0)),
            pl.BlockSpec((None, N_ADA, d), lambda i: (i // tiles_per_batch, 0, 0)),
            pl.BlockSpec(w_bf.shape, const, **resident),
            pl.BlockSpec(w_bm.shape, const, **resident),
            pl.BlockSpec(w_o.shape, const, **resident),
            pl.BlockSpec((1, d), const),
            pl.BlockSpec((1, d), const),
        ],
        out_specs=pl.BlockSpec((tm, d), lambda i: (i, 0)),
        out_shape=jax.ShapeDtypeStruct((m, d), F32),
        scratch_shapes=[pltpu.VMEM((tm, d), BF16)],
        compiler_params=_cparams(("parallel",)),
        name="mix",
    )(y_fox, y_mla, proj, proj, x2d, mod3, w_bf, w_bm, w_o, ln_g, ln_b)


def _mlp_kernel(x_ref, mod_ref, wu_ref, wd_ref, g_ref, b_ref, o_ref, u_ref):
    f = pl.program_id(1)

    def partial_sum(u):
        z = jnp.dot(u, wu_ref[...], preferred_element_type=F32)
        h = jnp.square(jnp.maximum(z, 0.0)).astype(BF16)
        return jnp.dot(h, wd_ref[...], preferred_element_type=F32)

    @pl.when(f == 0)
    def _():
        shift = mod_ref[3:4, :]
        scale = mod_ref[4:5, :]
        u = (x_ref[...] * (1.0 + scale) + shift).astype(BF16)
        u_ref[...] = u
        o_ref[...] = partial_sum(u)

    @pl.when(f != 0)
    def _():
        o_ref[...] += partial_sum(u_ref[...])

    @pl.when(f == pl.num_programs(1) - 1)
    def _():
        gate = mod_ref[5:6, :]
        o_ref[...] = _layer_norm(ALPHA * x_ref[...] + gate * o_ref[...], g_ref[...], b_ref[...])


def _mlp(x1, mod3, w_up, w_down, ln_g, ln_b, seq):
    m, d = x1.shape
    d_ff = w_up.shape[1]
    tm, tf = 1024, 512
    tiles_per_batch = seq // tm
    return pl.pallas_call(
        _mlp_kernel,
        grid=(m // tm, d_ff // tf),
        in_specs=[
            pl.BlockSpec((tm, d), lambda i, f: (i, 0)),
            pl.BlockSpec((None, N_ADA, d), lambda i, f: (i // tiles_per_batch, 0, 0)),
            pl.BlockSpec((d, tf), lambda i, f: (0, f)),
            pl.BlockSpec((tf, d), lambda i, f: (f, 0)),
            pl.BlockSpec((1, d), lambda i, f: (0, 0)),
            pl.BlockSpec((1, d), lambda i, f: (0, 0)),
        ],
        out_specs=pl.BlockSpec((tm, d), lambda i, f: (i, 0)),
        out_shape=jax.ShapeDtypeStruct((m, d), F32),
        scratch_shapes=[pltpu.VMEM((tm, d), BF16)],
        compiler_params=_cparams(("parallel", "arbitrary")),
        name="mlp",
    )(x1, mod3, w_up, w_down, ln_g, ln_b)


def _pad_cols(w, width):
    return jnp.pad(w, ((0, 0), (0, width - w.shape[1])))


def _swap_halves(w):
    half = w.shape[-1] // 2
    return jnp.concatenate([w[..., half:], w[..., :half]], axis=-1)


def _layout_q_up(w_q_up):
    r = w_q_up.shape[0]
    w3 = w_q_up.reshape(r, MLA_HEADS, MLA_QK_DIM)
    w_nope = w3[:, :, :MLA_NOPE_DIM].reshape(r, MLA_HEADS * MLA_NOPE_DIM)
    w_rope = w3[:, :, MLA_NOPE_DIM:]
    pad = ((0, 0), (0, 0), (0, LANES - MLA_ROPE_DIM))
    w_r = jnp.pad(w_rope, pad).reshape(r, MLA_HEADS * LANES)
    w_rs = jnp.pad(_swap_halves(w_rope), pad).reshape(r, MLA_HEADS * LANES)
    return jnp.concatenate([w_nope, w_r, w_rs], axis=1).astype(BF16)


def _rope_tables(seq):
    pos = np.arange(seq, dtype=np.float64)
    inv_freq = ROPE_THETA ** (-np.arange(0, MLA_ROPE_DIM, 2, dtype=np.float64) / MLA_ROPE_DIM)
    ang = pos[:, None] * inv_freq[None, :]
    cos, sin = np.cos(ang), np.sin(ang)
    pad = np.zeros((seq, LANES - MLA_ROPE_DIM))
    cos_t = np.concatenate([cos, cos, pad], axis=1).astype(np.float32)
    sin_t = np.concatenate([-sin, sin, pad], axis=1).astype(np.float32)
    return jnp.asarray(cos_t), jnp.asarray(sin_t)


def kernel(x, c, w_ada, b_ada, w_in, b_forget, g_q_norm, w_q_up, g_kv_norm, w_kv_up, w_branch_fox,
           w_branch_mla, w_out, ln1_g, ln1_b, w_mlp_up, w_mlp_down, ln2_g, ln2_b):
    n_batch, seq, d = x.shape
    assert w_ada.shape[0] == DEPTH
    m = n_batch * seq
    x2d = x.reshape(m, d)

    mod3 = _ada(c, w_ada[0], b_ada[0]).reshape(n_batch, N_ADA, d)

    small_col = 3 * FOX_WIDTH
    gf_col = small_col + FOX_HEADS + MLA_Q_LORA + MLA_KV_LORA + MLA_ROPE_DIM
    gm_col = gf_col + d
    assert w_in.shape[2] == gm_col + d and small_col == IN_SMALL_TILE * IN_TN and gf_col <= small_col + IN_TN
    proj, narrow_hi, narrow_lo = _inproj(x2d, mod3, w_in[0].T, seq)

    cos_t, sin_t = _rope_tables(seq)
    bf_pad = _pad_cols(b_forget[0].reshape(1, FOX_HEADS), LANES)
    qn, qr, kv, kr, cumt = _prep(
        narrow_hi, narrow_lo, bf_pad, g_q_norm[0].reshape(1, -1), g_kv_norm[0].reshape(1, -1),
        _layout_q_up(w_q_up[0]), w_kv_up[0].astype(BF16), cos_t, sin_t, seq)

    y_fox, w_up_b, w_down_b = _fox_attention(
        proj, cumt.reshape(n_batch, FOX_HEADS // 2, 2, seq), seq, 0, (w_mlp_up[0], w_mlp_down[0]))
    y_mla, w_bf_b, w_bm_b, w_o_b = _mla_attention(
        qn, qr, kv, kr, seq, (w_branch_fox[0], w_branch_mla[0], w_out[0]))

    x1 = _mix(y_fox, y_mla, proj, gf_col, gm_col, x2d, mod3, w_bf_b, w_bm_b, w_o_b,
              ln1_g[0].reshape(1, d), ln1_b[0].reshape(1, d), seq)
    x2 = _mlp(x1, mod3, w_up_b, w_down_b, ln2_g[0].reshape(1, d), ln2_b[0].reshape(1, d), seq)
    return x2.reshape(n_batch, seq, d)
```

```python
import functools
import math

import jax
import jax.numpy as jnp
import numpy as np
from jax import lax
from jax.experimental import pallas as pl
from jax.experimental.pallas import tpu as pltpu

F32 = jnp.float32
BF16 = jnp.bfloat16

FOX_HEADS = 16
FOX_HEAD_DIM = 64
FOX_WIDTH = FOX_HEADS * FOX_HEAD_DIM
MLA_HEADS = 8
MLA_NOPE_DIM = 128
MLA_ROPE_DIM = 64
MLA_V_DIM = 128
MLA_Q_LORA = 512
MLA_KV_LORA = 256
MLA_QK_DIM = MLA_NOPE_DIM + MLA_ROPE_DIM
CHUNK = 64
ROPE_THETA = 10000.0
LN_EPS = 1e-5
RMS_EPS = 1e-6
N_ADA = 6
DEPTH = 1
ALPHA = (2.0 * DEPTH) ** 0.25

LANES = 128
LOG2E = math.log2(math.e)
FOX_QSCALE = FOX_HEAD_DIM ** -0.5 * LOG2E
MLA_QSCALE = MLA_QK_DIM ** -0.5 * LOG2E
NEG_BIG = -1e30
VMEM_LIMIT = 56 * 1024 * 1024

SUB = 256
HALF = 256


def _cparams(semantics):
    return pltpu.CompilerParams(dimension_semantics=semantics, vmem_limit_bytes=VMEM_LIMIT)


def _layer_norm(y, g, b):
    mu = jnp.mean(y, axis=-1, keepdims=True)
    yc = y - mu
    var = jnp.mean(yc * yc, axis=-1, keepdims=True)
    return yc * lax.rsqrt(var + LN_EPS) * g + b


def _rms_norm(c, g):
    return c * lax.rsqrt(jnp.mean(c * c, axis=-1, keepdims=True) + RMS_EPS) * g


def _ada_kernel(ct_ref, w_ref, b_ref, o_ref, *, n_batch, k_chunk):
    d = w_ref.shape[0]
    ct = ct_ref[...]
    s = ct * jax.nn.sigmoid(ct)
    accs = [b_ref[...] for _ in range(n_batch)]
    for kc in range(d // k_chunk):
        rows = slice(kc * k_chunk, (kc + 1) * k_chunk)
        w = w_ref[rows, :]
        for b in range(n_batch):
            accs[b] = accs[b] + jnp.sum(w * s[rows, b:b + 1], axis=0, keepdims=True)
    for b in range(n_batch):
        o_ref[b:b + 1, :] = accs[b]


def _ada(c, w_ada, b_ada):
    n_batch, d = c.shape
    n = w_ada.shape[1]
    tn = 1024
    return pl.pallas_call(
        functools.partial(_ada_kernel, n_batch=n_batch, k_chunk=256),
        grid=(n // tn,),
        in_specs=[
            pl.BlockSpec((d, n_batch), lambda j: (0, 0)),
            pl.BlockSpec((d, tn), lambda j: (0, j)),
            pl.BlockSpec((1, tn), lambda j: (0, j)),
        ],
        out_specs=pl.BlockSpec((n_batch, tn), lambda j: (0, j)),
        out_shape=jax.ShapeDtypeStruct((n_batch, n), F32),
        compiler_params=_cparams(("arbitrary",)),
        name="ada",
    )(c.T, w_ada, b_ada.reshape(1, n))


IN_TN = 1024
IN_SMALL_TILE = 3 * FOX_WIDTH // IN_TN
NARROW_F = (0, LANES)
NARROW_CQ = (NARROW_F[1], NARROW_F[1] + MLA_Q_LORA)
NARROW_CKV = (NARROW_CQ[1], NARROW_CQ[1] + MLA_KV_LORA)
NARROW_KR = (NARROW_CKV[1], NARROW_CKV[1] + LANES)
assert NARROW_KR[1] == IN_TN and 2 * MLA_ROPE_DIM == LANES


def _narrow_segments(acc):
    c0 = FOX_HEADS
    c1 = c0 + MLA_Q_LORA
    c2 = c1 + MLA_KV_LORA
    c3 = c2 + MLA_ROPE_DIM
    half = MLA_ROPE_DIM // 2
    return jnp.concatenate([acc[:, 0:LANES], acc[:, c0:c1], acc[:, c1:c2], acc[:, c2:c3],
                            acc[:, c2 + half:c3], acc[:, c2:c2 + half]], axis=1)


def _inproj_kernel(x_ref, mod_ref, w_ref, o_ref, hi_ref, lo_ref, u_ref, *, n_cols):
    j = pl.program_id(1)
    last = pl.num_programs(1) - 1

    def product(wt, u=None):
        u = u_ref[...] if u is None else u
        return lax.dot_general(u, wt.astype(BF16), (((1,), (1,)), ((), ())), preferred_element_type=F32)

    @pl.when(j == 0)
    def _():
        shift = mod_ref[0:1, :]
        scale = mod_ref[1:2, :]
        u = (x_ref[...] * (1.0 + scale) + shift).astype(BF16)
        u_ref[...] = u
        o_ref[...] = (product(w_ref[...], u) * FOX_QSCALE).astype(o_ref.dtype)

    @pl.when(j == IN_SMALL_TILE)
    def _():
        acc = product(w_ref[...])
        o_ref[...] = acc.astype(o_ref.dtype)
        narrow = _narrow_segments(acc)
        hi = narrow.astype(hi_ref.dtype)
        hi_ref[...] = hi
        lo_ref[...] = (narrow - hi.astype(F32)).astype(lo_ref.dtype)

    @pl.when(j == last)
    def _():
        row = lax.broadcasted_iota(jnp.int32, w_ref.shape, 0)
        wt = jnp.where(row < n_cols - last * IN_TN, w_ref[...], 0.0)
        o_ref[...] = product(wt).astype(o_ref.dtype)

    @pl.when((j != 0) & (j != IN_SMALL_TILE) & (j != last))
    def _():
        o_ref[...] = product(w_ref[...]).astype(o_ref.dtype)


def _inproj(x2d, mod3, w_in_t, seq):
    m, d = x2d.shape
    n_cols = w_in_t.shape[0]
    tm = 1024
    n_tiles = pl.cdiv(n_cols, IN_TN)
    tiles_per_batch = seq // tm
    return pl.pallas_call(
        functools.partial(_inproj_kernel, n_cols=n_cols),
        grid=(m // tm, n_tiles),
        in_specs=[
            pl.BlockSpec((tm, d), lambda i, j: (i, 0)),
            pl.BlockSpec((None, N_ADA, d), lambda i, j: (i // tiles_per_batch, 0, 0)),
            pl.BlockSpec((IN_TN, d), lambda i, j: (j, 0)),
        ],
        out_specs=[pl.BlockSpec((tm, IN_TN), lambda i, j: (i, j)),
                   pl.BlockSpec((tm, IN_TN), lambda i, j: (i, 0)),
                   pl.BlockSpec((tm, IN_TN), lambda i, j: (i, 0))],
        out_shape=[jax.ShapeDtypeStruct((m, n_tiles * IN_TN), BF16),
                   jax.ShapeDtypeStruct((m, IN_TN), BF16),
                   jax.ShapeDtypeStruct((m, IN_TN), BF16)],
        scratch_shapes=[pltpu.VMEM((tm, d), BF16)],
        compiler_params=_cparams(("parallel", "arbitrary")),
        name="inproj",
    )(x2d, mod3, w_in_t)


def _log_sigmoid(x):
    return jnp.minimum(x, 0.0) - jnp.log1p(jnp.exp(-jnp.abs(x)))


def _prep_kernel(hi_ref, lo_ref, bf_ref, gq_ref, gkv_ref, wq_ref, wkv_ref, cos_ref, sin_ref,
                 qn_ref, qr_ref, kv_ref, kr_ref, cumt_ref, carry_ref, *, tiles_per_batch):
    i = pl.program_id(0)
    tm = hi_ref.shape[0]
    ps = hi_ref[...].astype(F32) + lo_ref[...].astype(F32)
    k_rope = ps[:, NARROW_KR[0]:NARROW_KR[1]]
    k_rope_partner = pltpu.roll(k_rope, MLA_ROPE_DIM, axis=1)
    cos = cos_ref[...]
    sin = sin_ref[...]

    logf = _log_sigmoid(ps[:, NARROW_F[0]:NARROW_F[1]] + bf_ref[...]) * LOG2E
    hi = logf.astype(BF16)
    r1 = logf - hi.astype(F32)
    mid = r1.astype(BF16)
    lo = (r1 - mid.astype(F32)).astype(BF16)
    row = lax.broadcasted_iota(jnp.int32, (tm, tm), 0)
    col = lax.broadcasted_iota(jnp.int32, (tm, tm), 1)
    tri = (row >= col).astype(BF16)
    cs = (jnp.dot(tri, hi, preferred_element_type=F32)
          + jnp.dot(tri, mid, preferred_element_type=F32)
          + jnp.dot(tri, lo, preferred_element_type=F32))

    @pl.when(i % tiles_per_batch == 0)
    def _():
        carry_ref[...] = jnp.zeros_like(carry_ref)

    cum = cs + carry_ref[...]
    carry_ref[...] = cum[tm - 1:tm, :]
    cumt_ref[...] = cum.T[0:FOX_HEADS, :]

    cqn = _rms_norm(ps[:, NARROW_CQ[0]:NARROW_CQ[1]], gq_ref[...]).astype(BF16)
    qa = jnp.dot(cqn, wq_ref[...], preferred_element_type=F32)
    hw = MLA_HEADS * LANES
    qn_ref[...] = (qa[:, 0:hw] * MLA_QSCALE).astype(BF16)
    for h in range(MLA_HEADS):
        a = qa[:, hw + h * LANES: hw + (h + 1) * LANES]
        b = qa[:, 2 * hw + h * LANES: 2 * hw + (h + 1) * LANES]
        qr_ref[:, h * LANES:(h + 1) * LANES] = ((a * cos + b * sin) * MLA_QSCALE).astype(BF16)

    ckvn = _rms_norm(ps[:, NARROW_CKV[0]:NARROW_CKV[1]], gkv_ref[...]).astype(BF16)
    kv_ref[...] = jnp.dot(ckvn, wkv_ref[...], preferred_element_type=F32).astype(BF16)
    kr_ref[...] = (k_rope * cos + k_rope_partner * sin).astype(BF16)


def _prep(narrow_hi, narrow_lo, bf_pad, g_q, g_kv, w_q_all, w_kv, cos_t, sin_t, seq):
    m = narrow_hi.shape[0]
    n_batch = m // seq
    tm = 512
    tiles_per_batch = seq // tm
    hw = MLA_HEADS * LANES
    const = lambda i: (0, 0)
    return pl.pallas_call(
        functools.partial(_prep_kernel, tiles_per_batch=tiles_per_batch),
        grid=(m // tm,),
        in_specs=[
            pl.BlockSpec((tm, IN_TN), lambda i: (i, 0)),
            pl.BlockSpec((tm, IN_TN), lambda i: (i, 0)),
            pl.BlockSpec(bf_pad.shape, const),
            pl.BlockSpec(g_q.shape, const),
            pl.BlockSpec(g_kv.shape, const),
            pl.BlockSpec(w_q_all.shape, const),
            pl.BlockSpec(w_kv.shape, const),
            pl.BlockSpec((tm, LANES), lambda i: (i % tiles_per_batch, 0)),
            pl.BlockSpec((tm, LANES), lambda i: (i % tiles_per_batch, 0)),
        ],
        out_specs=[
            pl.BlockSpec((tm, hw), lambda i: (i, 0)),
            pl.BlockSpec((tm, hw), lambda i: (i, 0)),
            pl.BlockSpec((tm, w_kv.shape[1]), lambda i: (i, 0)),
            pl.BlockSpec((tm, LANES), lambda i: (i, 0)),
            pl.BlockSpec((None, FOX_HEADS, tm), lambda i: (i // tiles_per_batch, 0, i % tiles_per_batch)),
        ],
        out_shape=[
            jax.ShapeDtypeStruct((m, hw), BF16),
            jax.ShapeDtypeStruct((m, hw), BF16),
            jax.ShapeDtypeStruct((m, w_kv.shape[1]), BF16),
            jax.ShapeDtypeStruct((m, LANES), BF16),
            jax.ShapeDtypeStruct((n_batch, FOX_HEADS, seq), F32),
        ],
        scratch_shapes=[pltpu.VMEM((1, LANES), F32)],
        compiler_params=_cparams(("arbitrary",)),
        name="prep",
    )(narrow_hi, narrow_lo, bf_pad, g_q, g_kv, w_q_all, w_kv, cos_t, sin_t)


def _split3(x):
    hi = x.astype(BF16).astype(F32)
    r = x - hi
    mid = r.astype(BF16).astype(F32)
    lo = (r - mid).astype(BF16).astype(F32)
    return hi, mid, lo


def _online_softmax_step(t, m, acc, vt_aug):
    m_new = jnp.maximum(m, jnp.max(t, axis=0, keepdims=True))
    alpha = jnp.exp2(m - m_new)
    p = jnp.exp2(t - m_new).astype(BF16)
    pv = jnp.dot(vt_aug, p, preferred_element_type=F32)
    return m_new, alpha * acc + pv[0:acc.shape[0], :]


def _causal_mask(t, key0, query0, granule):
    r = lax.broadcasted_iota(jnp.int32, t.shape, 0) + key0
    c = lax.broadcasted_iota(jnp.int32, t.shape, 1) + query0
    return jnp.where(r <= (c | (granule - 1)), t, NEG_BIG)


Q_TILES = 8
FULL, DIAG, SKIP = "full", "diag", "skip"


def _flash_key_major(j, tile, keys, rhs, vt_aug, d_v, granule, finish):
    acc_rows = d_v + 8
    n_rounds = tile // SUB
    chains = [(qt, half, hh) for qt in range(Q_TILES) for half in range(tile // HALF) for hh in (0, 1)]
    pairs = [(n, n + 1) for n in range(0, len(chains), 2)]

    def qk_of(k0, modes):
        loaded = {}

        def product(n):
            qt, half, hh = chains[n]
            n_keys = (half + 1) * HALF if modes[qt] == DIAG else tile
            if (hh, n_keys) not in loaded:
                loaded[hh, n_keys] = keys(hh, k0, n_keys)
            return jnp.dot(loaded[hh, n_keys], rhs(qt, hh)[:, half * HALF:(half + 1) * HALF],
                           preferred_element_type=F32)
        return product

    def visit(state, tiles):
        st = list(state)
        products = [qk_of(*t) for t in tiles]
        phases = [(ti, pair) for ti, (_, modes) in enumerate(tiles) for pair in pairs
                  if modes[chains[pair[0]][0]] != SKIP]

        def issue(phase):
            ti, pair = phase
            return {n: products[ti](n) for n in pair}

        cur = issue(phases[0])
        for idx, (ti, pair) in enumerate(phases):
            nxt = issue(phases[idx + 1]) if idx + 1 < len(phases) else None
            k0, modes = tiles[ti]
            for sb in range(n_rounds):
                ks = pl.multiple_of(k0 + sb * SUB, SUB)
                for n in pair:
                    qt, half, hh = chains[n]
                    is_diag = modes[qt] == DIAG
                    if is_diag and sb * SUB >= (half + 1) * HALF:
                        continue
                    t = cur[n][sb * SUB:(sb + 1) * SUB, :]
                    if is_diag and (sb + 1) * SUB > half * HALF:
                        t = _causal_mask(t, sb * SUB, half * HALF, granule)
                    st[n] = _online_softmax_step(t, *st[n], vt_aug(hh, ks))
            cur = nxt
        return tuple(st)

    everyone = (FULL,) * Q_TILES

    def two_full(i, state):
        k0 = pl.multiple_of(2 * i * tile, tile)
        return visit(state, [(k0, everyone), (pl.multiple_of(k0 + tile, tile), everyone)])

    assert Q_TILES % 2 == 0
    init = tuple((jnp.full((1, HALF), NEG_BIG, F32), jnp.zeros((acc_rows, HALF), F32)) for _ in chains)
    state = lax.fori_loop(0, j * (Q_TILES // 2), two_full, init)
    q0 = j * (Q_TILES * tile)
    last = [(pl.multiple_of(q0 + t * tile, tile),
             tuple(SKIP if qt < t else DIAG if qt == t else FULL for qt in range(Q_TILES)))
            for t in range(Q_TILES)]
    return finish([acc[0:d_v, :] * (1.0 / acc[d_v:d_v + 1, :]) for (_, acc) in visit(state, last)])


def _cast_specs(weights, grid):
    n_steps = math.prod(grid)
    specs, shapes = [], []
    for w in weights:
        rows, cols = w.shape
        slab = rows // n_steps
        assert slab * n_steps == rows and slab % 16 == 0, (w.shape, n_steps)
        specs.append(pl.BlockSpec((slab, cols), lambda a, b: (a * grid[1] + b, 0)))
        shapes.append(jax.ShapeDtypeStruct(w.shape, BF16))
    return specs, shapes


def _cast_slabs(refs):
    n = len(refs) // 2
    for src, dst in zip(refs[:n], refs[n:]):
        dst[...] = src[...].astype(dst.dtype)


def _stack_chain_outputs(outs, n_half):
    return jnp.concatenate(
        [jnp.concatenate([outs[half * 2 + hh] for half in range(n_half)], axis=1) for hh in (0, 1)], axis=0)


AUG_ROWS = 16


def _bias_rows(n, entries):
    r = lax.broadcasted_iota(jnp.int32, (AUG_ROWS, n), 0)
    out = jnp.zeros((AUG_ROWS, n), F32)
    for row, val in entries.items():
        out = jnp.where(r == row, val, out)
    return out.astype(BF16)


def _fox_kernel(q_ref, k_ref, v_ref, cumt_ref, *rest, tq, n_cast):
    cast_in, (o_ref, *cast_out), (vta_ref, kaug_ref) = rest[:n_cast], rest[n_cast:2 * n_cast + 1], rest[2 * n_cast + 1:]
    seq = k_ref.shape[0]
    dh = FOX_HEAD_DIM
    n_half = tq // HALF
    aug = 6

    vt = v_ref[...].T
    tail_rows = lax.broadcasted_iota(jnp.int32, (16, seq), 0)
    tail = jnp.where(tail_rows < 8, 1.0, 0.0).astype(BF16)
    for hh in (0, 1):
        vta_ref[hh, 0:dh, :] = vt[hh * dh:(hh + 1) * dh, :]
        vta_ref[hh, dh:dh + 16, :] = tail
    key_rows = {}
    for hh in (0, 1):
        for idx, piece in enumerate(_split3(-cumt_ref[hh:hh + 1, :])):
            key_rows[aug * hh + idx] = piece
            key_rows[aug * hh + 3 + idx] = 1.0
    kaug_t = jnp.concatenate([_bias_rows(seq, key_rows), jnp.zeros((LANES - AUG_ROWS, seq), BF16)], axis=0)
    kaug_ref[...] = kaug_t.T

    def keys(hh, k0, n):
        return jnp.concatenate([k_ref[pl.ds(k0, n), :], kaug_ref[pl.ds(k0, n), :]], axis=1)

    def make_rhs(j):
        row = lax.broadcasted_iota(jnp.int32, (LANES, tq), 0)
        pad = jnp.zeros((LANES - AUG_ROWS, tq), BF16)
        out = []
        for t in range(Q_TILES):
            q0 = pl.multiple_of((j * Q_TILES + t) * tq, tq)
            qt = q_ref[pl.ds(q0, tq), :].T
            rhs = []
            for hh in (0, 1):
                qm = jnp.where((row >= hh * dh) & (row < (hh + 1) * dh), qt, jnp.zeros_like(qt))
                query_rows = {}
                for idx, piece in enumerate(_split3(cumt_ref[hh:hh + 1, pl.ds(q0, tq)])):
                    query_rows[aug * hh + idx] = 1.0
                    query_rows[aug * hh + 3 + idx] = piece
                rhs.append(jnp.concatenate([qm, _bias_rows(tq, query_rows), pad], axis=0))
            out.append(tuple(rhs))
        return tuple(out)

    n_super = seq // (Q_TILES * tq)
    per_tile = 2 * n_half

    def super_tile(j, rhs):
        def finish(outs):
            for t in range(Q_TILES):
                q0 = pl.multiple_of((j * Q_TILES + t) * tq, tq)
                tile_outs = outs[t * per_tile:(t + 1) * per_tile]
                o_ref[pl.ds(q0, tq), :] = _stack_chain_outputs(tile_outs, n_half).T.astype(o_ref.dtype)
            return make_rhs(jnp.minimum(j + 1, n_super - 1))

        return _flash_key_major(j, tq, keys, lambda t, hh: rhs[t][hh],
                                lambda hh, ks: vta_ref[hh, :, pl.ds(ks, SUB)], dh, 1, finish)

    lax.fori_loop(0, n_super, super_tile, make_rhs(0))
    _cast_slabs(cast_in + tuple(cast_out))


def _fox_attention(proj, cumt4, seq, q_col0, cast_weights):
    m = proj.shape[0]
    n_batch = m // seq
    n_pairs = FOX_HEADS // 2
    qb = q_col0 // LANES
    grid = (n_batch, n_pairs)
    cast_specs, cast_shapes = _cast_specs(cast_weights, grid)
    return pl.pallas_call(
        functools.partial(_fox_kernel, tq=512, n_cast=len(cast_weights)),
        grid=grid,
        in_specs=[
            pl.BlockSpec((seq, LANES), lambda b, hp: (b, qb + hp)),
            pl.BlockSpec((seq, LANES), lambda b, hp: (b, qb + n_pairs + hp)),
            pl.BlockSpec((seq, LANES), lambda b, hp: (b, qb + 2 * n_pairs + hp)),
            pl.BlockSpec((None, None, 2, seq), lambda b, hp: (b, hp, 0, 0)),
        ] + cast_specs,
        out_specs=[pl.BlockSpec((seq, LANES), lambda b, hp: (b, hp))] + cast_specs,
        out_shape=[jax.ShapeDtypeStruct((m, FOX_WIDTH), BF16)] + cast_shapes,
        scratch_shapes=[
            pltpu.VMEM((2, FOX_HEAD_DIM + 16, seq), BF16),
            pltpu.VMEM((seq, LANES), BF16),
        ],
        compiler_params=_cparams(("parallel", "parallel")),
        name="fox_attn",
    )(proj, proj, proj, cumt4, *cast_weights)


def _mla_kernel(qn_ref, qr_ref, kv_ref, kr_ref, *rest, tq, n_cast):
    cast_in, (o_ref, *cast_out), (vta_ref,) = rest[:n_cast], rest[n_cast:2 * n_cast + 1], rest[2 * n_cast + 1:]
    seq = kv_ref.shape[0]
    dv = MLA_V_DIM
    n_half = tq // HALF

    tail_rows = lax.broadcasted_iota(jnp.int32, (16, seq), 0)
    tail = jnp.where(tail_rows < 8, 1.0, 0.0).astype(BF16)
    for hh in (0, 1):
        vta_ref[hh, 0:dv, :] = kv_ref[:, (2 * hh + 1) * LANES:(2 * hh + 2) * LANES].T
        vta_ref[hh, dv:dv + 16, :] = tail

    def keys(hh, k0, n):
        return jnp.concatenate([kv_ref[pl.ds(k0, n), 2 * hh * LANES:(2 * hh + 1) * LANES],
                                kr_ref[pl.ds(k0, n), :]], axis=1)

    def make_rhs(j):
        out = []
        for t in range(Q_TILES):
            q0 = pl.multiple_of((j * Q_TILES + t) * tq, tq)
            out.append(tuple(jnp.concatenate([qn_ref[pl.ds(q0, tq), hh * LANES:(hh + 1) * LANES],
                                              qr_ref[pl.ds(q0, tq), hh * LANES:(hh + 1) * LANES]], axis=1).T
                             for hh in (0, 1)))
        return tuple(out)

    n_super = seq // (Q_TILES * tq)
    per_tile = 2 * n_half

    def super_tile(j, rhs):
        def finish(outs):
            for t in range(Q_TILES):
                q0 = pl.multiple_of((j * Q_TILES + t) * tq, tq)
                tile_outs = outs[t * per_tile:(t + 1) * per_tile]
                o_ref[pl.ds(q0, tq), :] = _stack_chain_outputs(tile_outs, n_half).T.astype(o_ref.dtype)
            return make_rhs(jnp.minimum(j + 1, n_super - 1))

        return _flash_key_major(j, tq, keys, lambda t, hh: rhs[t][hh],
                                lambda hh, ks: vta_ref[hh, :, pl.ds(ks, SUB)], dv, CHUNK, finish)

    lax.fori_loop(0, n_super, super_tile, make_rhs(0))
    _cast_slabs(cast_in + tuple(cast_out))


def _mla_attention(qn, qr, kv, kr, seq, cast_weights):
    m = qn.shape[0]
    n_batch = m // seq
    pair = 2 * LANES
    grid = (n_batch, MLA_HEADS // 2)
    cast_specs, cast_shapes = _cast_specs(cast_weights, grid)
    return pl.pallas_call(
        functools.partial(_mla_kernel, tq=512, n_cast=len(cast_weights)),
        grid=grid,
        in_specs=[
            pl.BlockSpec((seq, pair), lambda b, hp: (b, hp)),
            pl.BlockSpec((seq, pair), lambda b, hp: (b, hp)),
            pl.BlockSpec((seq, 2 * pair), lambda b, hp: (b, hp)),
            pl.BlockSpec((seq, LANES), lambda b, hp: (b, 0)),
        ] + cast_specs,
        out_specs=[pl.BlockSpec((seq, pair), lambda b, hp: (b, hp))] + cast_specs,
        out_shape=[jax.ShapeDtypeStruct((m, MLA_HEADS * MLA_V_DIM), BF16)] + cast_shapes,
        scratch_shapes=[pltpu.VMEM((2, MLA_V_DIM + 16, seq), BF16)],
        compiler_params=_cparams(("parallel", "parallel")),
        name="mla_attn",
    )(qn, qr, kv, kr, *cast_weights)


def _mix_kernel(yf_ref, ym_ref, gf_ref, gm_ref, x_ref, mod_ref, wbf_ref, wbm_ref, wo_ref, g_ref, b_ref,
                o_ref, mg_ref, *, n_chunk, g_off):
    d = o_ref.shape[1]
    yf = yf_ref[...]
    ym = ym_ref[...]

    def gate(ref, c):
        window = ref[:, c * n_chunk:(c + 1) * n_chunk + LANES]
        return jax.nn.sigmoid(window[:, g_off:g_off + n_chunk].astype(F32))

    for c in range(d // n_chunk):
        cols = slice(c * n_chunk, (c + 1) * n_chunk)
        a = jnp.dot(yf, wbf_ref[:, cols], preferred_element_type=F32)
        b = jnp.dot(ym, wbm_ref[:, cols], preferred_element_type=F32)
        merged = gate(gf_ref, c) * a + gate(gm_ref, c) * b
        mg_ref[:, cols] = merged.astype(BF16)
    mix = jnp.dot(mg_ref[...], wo_ref[...], preferred_element_type=F32)
    gate1 = mod_ref[2:3, :]
    o_ref[...] = _layer_norm(ALPHA * x_ref[...] + gate1 * mix, g_ref[...], b_ref[...])


def _mix(y_fox, y_mla, proj, gf_col, gm_col, x2d, mod3, w_bf, w_bm, w_o, ln_g, ln_b, seq):
    m, d = x2d.shape
    tm = 512
    tiles_per_batch = seq // tm
    const = lambda i: (0, 0)
    resident = dict(pipeline_mode=pl.Buffered(1))
    g_off = gf_col % LANES
    assert gm_col % LANES == g_off
    g_window = (pl.Element(tm), pl.Element(d + LANES))

    def window_at(col):
        return lambda i: (pl.multiple_of(i * tm, tm), col - g_off)

    return pl.pallas_call(
        functools.partial(_mix_kernel, n_chunk=512, g_off=g_off),
        grid=(m // tm,),
        in_specs=[
            pl.BlockSpec((tm, y_fox.shape[1]), lambda i: (i, 0)),
            pl.BlockSpec((tm, y_mla.shape[1]), lambda i: (i, 0)),
            pl.BlockSpec(g_window, window_at(gf_col)),
            pl.BlockSpec(g_window, window_at(gm_col)),
            pl.BlockSpec((tm, d), lambda i: (i, 0)),
            pl.BlockSpec((None, N_ADA, d), lambda i: (i // tiles_per_batch, 0, 0)),
            pl.BlockSpec(w_bf.shape, const, **resident),
            pl.BlockSpec(w_bm.shape, const, **resident),
            pl.BlockSpec(w_o.shape, const, **resident),
            pl.BlockSpec((1, d), const),
            pl.BlockSpec((1, d), const),
        ],
        out_specs=pl.BlockSpec((tm, d), lambda i: (i, 0)),
        out_shape=jax.ShapeDtypeStruct((m, d), F32),
        scratch_shapes=[pltpu.VMEM((tm, d), BF16)],
        compiler_params=_cparams(("parallel",)),
        name="mix",
    )(y_fox, y_mla, proj, proj, x2d, mod3, w_bf, w_bm, w_o, ln_g, ln_b)


def _mlp_kernel(x_ref, mod_ref, wu_ref, wd_ref, g_ref, b_ref, o_ref, u_ref):
    f = pl.program_id(1)

    def partial_sum(u):
        z = jnp.dot(u, wu_ref[...], preferred_element_type=F32)
        h = jnp.square(jnp.maximum(z, 0.0)).astype(BF16)
        return jnp.dot(h, wd_ref[...], preferred_element_type=F32)

    @pl.when(f == 0)
    def _():
        shift = mod_ref[3:4, :]
        scale = mod_ref[4:5, :]
        u = (x_ref[...] * (1.0 + scale) + shift).astype(BF16)
        u_ref[...] = u
        o_ref[...] = partial_sum(u)

    @pl.when(f != 0)
    def _():
        o_ref[...] += partial_sum(u_ref[...])

    @pl.when(f == pl.num_programs(1) - 1)
    def _():
        gate = mod_ref[5:6, :]
        o_ref[...] = _layer_norm(ALPHA * x_ref[...] + gate * o_ref[...], g_ref[...], b_ref[...])


def _mlp(x1, mod3, w_up, w_down, ln_g, ln_b, seq):
    m, d = x1.shape
    d_ff = w_up.shape[1]
    tm, tf = 1024, 512
    tiles_per_batch = seq // tm
    return pl.pallas_call(
        _mlp_kernel,
        grid=(m // tm, d_ff // tf),
        in_specs=[
            pl.BlockSpec((tm, d), lambda i, f: (i, 0)),
            pl.BlockSpec((None, N_ADA, d), lambda i, f: (i // tiles_per_batch, 0, 0)),
            pl.BlockSpec((d, tf), lambda i, f: (0, f)),
            pl.BlockSpec((tf, d), lambda i, f: (f, 0)),
            pl.BlockSpec((1, d), lambda i, f: (0, 0)),
            pl.BlockSpec((1, d), lambda i, f: (0, 0)),
        ],
        out_specs=pl.BlockSpec((tm, d), lambda i, f: (i, 0)),
        out_shape=jax.ShapeDtypeStruct((m, d), F32),
        scratch_shapes=[pltpu.VMEM((tm, d), BF16)],
        compiler_params=_cparams(("parallel", "arbitrary")),
        name="mlp",
    )(x1, mod3, w_up, w_down, ln_g, ln_b)


def _pad_cols(w, width):
    return jnp.pad(w, ((0, 0), (0, width - w.shape[1])))


def _swap_halves(w):
    half = w.shape[-1] // 2
    return jnp.concatenate([w[..., half:], w[..., :half]], axis=-1)


def _layout_q_up(w_q_up):
    r = w_q_up.shape[0]
    w3 = w_q_up.reshape(r, MLA_HEADS, MLA_QK_DIM)
    w_nope = w3[:, :, :MLA_NOPE_DIM].reshape(r, MLA_HEADS * MLA_NOPE_DIM)
    w_rope = w3[:, :, MLA_NOPE_DIM:]
    pad = ((0, 0), (0, 0), (0, LANES - MLA_ROPE_DIM))
    w_r = jnp.pad(w_rope, pad).reshape(r, MLA_HEADS * LANES)
    w_rs = jnp.pad(_swap_halves(w_rope), pad).reshape(r, MLA_HEADS * LANES)
    return jnp.concatenate([w_nope, w_r, w_rs], axis=1).astype(BF16)


def _rope_tables(seq):
    pos = np.arange(seq, dtype=np.float64)
    inv_freq = ROPE_THETA ** (-np.arange(0, MLA_ROPE_DIM, 2, dtype=np.float64) / MLA_ROPE_DIM)
    ang = pos[:, None] * inv_freq[None, :]
    cos, sin = np.cos(ang), np.sin(ang)
    pad = np.zeros((seq, LANES - MLA_ROPE_DIM))
    cos_t = np.concatenate([cos, cos, pad], axis=1).astype(np.float32)
    sin_t = np.concatenate([-sin, sin, pad], axis=1).astype(np.float32)
    return jnp.asarray(cos_t), jnp.asarray(sin_t)


def kernel(x, c, w_ada, b_ada, w_in, b_forget, g_q_norm, w_q_up, g_kv_norm, w_kv_up, w_branch_fox,
           w_branch_mla, w_out, ln1_g, ln1_b, w_mlp_up, w_mlp_down, ln2_g, ln2_b):
    n_batch, seq, d = x.shape
    assert w_ada.shape[0] == DEPTH
    m = n_batch * seq
    x2d = x.reshape(m, d)

    mod3 = _ada(c, w_ada[0], b_ada[0]).reshape(n_batch, N_ADA, d)

    small_col = 3 * FOX_WIDTH
    gf_col = small_col + FOX_HEADS + MLA_Q_LORA + MLA_KV_LORA + MLA_ROPE_DIM
    gm_col = gf_col + d
    assert w_in.shape[2] == gm_col + d and small_col == IN_SMALL_TILE * IN_TN and gf_col <= small_col + IN_TN
    proj, narrow_hi, narrow_lo = _inproj(x2d, mod3, w_in[0].T, seq)

    cos_t, sin_t = _rope_tables(seq)
    bf_pad = _pad_cols(b_forget[0].reshape(1, FOX_HEADS), LANES)
    qn, qr, kv, kr, cumt = _prep(
        narrow_hi, narrow_lo, bf_pad, g_q_norm[0].reshape(1, -1), g_kv_norm[0].reshape(1, -1),
        _layout_q_up(w_q_up[0]), w_kv_up[0].astype(BF16), cos_t, sin_t, seq)

    y_fox, w_up_b, w_down_b = _fox_attention(
        proj, cumt.reshape(n_batch, FOX_HEADS // 2, 2, seq), seq, 0, (w_mlp_up[0], w_mlp_down[0]))
    y_mla, w_bf_b, w_bm_b, w_o_b = _mla_attention(
        qn, qr, kv, kr, seq, (w_branch_fox[0], w_branch_mla[0], w_out[0]))

    x1 = _mix(y_fox, y_mla, proj, gf_col, gm_col, x2d, mod3, w_bf_b, w_bm_b, w_o_b,
              ln1_g[0].reshape(1, d), ln1_b[0].reshape(1, d), seq)
    x2 = _mlp(x1, mod3, w_up_b, w_down_b, ln2_g[0].reshape(1, d), ln2_b[0].reshape(1, d), seq)
    return x2.reshape(n_batch, seq, d)
```

```python
import functools
import math

import jax
import jax.numpy as jnp
import numpy as np
from jax import lax
from jax.experimental import pallas as pl
from jax.experimental.pallas import tpu as pltpu

F32 = jnp.float32
BF16 = jnp.bfloat16

FOX_HEADS = 16
FOX_HEAD_DIM = 64
FOX_WIDTH = FOX_HEADS * FOX_HEAD_DIM
MLA_HEADS = 8
MLA_NOPE_DIM = 128
MLA_ROPE_DIM = 64
MLA_V_DIM = 128
MLA_Q_LORA = 512
MLA_KV_LORA = 256
MLA_QK_DIM = MLA_NOPE_DIM + MLA_ROPE_DIM
CHUNK = 64
ROPE_THETA = 10000.0
LN_EPS = 1e-5
RMS_EPS = 1e-6
N_ADA = 6
DEPTH = 1
ALPHA = (2.0 * DEPTH) ** 0.25

LANES = 128
LOG2E = math.log2(math.e)
FOX_QSCALE = FOX_HEAD_DIM ** -0.5 * LOG2E
MLA_QSCALE = MLA_QK_DIM ** -0.5 * LOG2E
NEG_BIG = -1e30
VMEM_LIMIT = 56 * 1024 * 1024

FOX_SUB = 128
MLA_SUB = 256
HALF = 256


def _cparams(semantics):
    return pltpu.CompilerParams(dimension_semantics=semantics, vmem_limit_bytes=VMEM_LIMIT)


def _layer_norm(y, g, b):
    mu = jnp.mean(y, axis=-1, keepdims=True)
    yc = y - mu
    var = jnp.mean(yc * yc, axis=-1, keepdims=True)
    return yc * lax.rsqrt(var + LN_EPS) * g + b


def _rms_norm(c, g):
    return c * lax.rsqrt(jnp.mean(c * c, axis=-1, keepdims=True) + RMS_EPS) * g


def _ada_kernel(ct_ref, w_ref, b_ref, o_ref, *, n_batch, k_chunk):
    d = w_ref.shape[0]
    ct = ct_ref[...]
    s = ct * jax.nn.sigmoid(ct)
    accs = [b_ref[...] for _ in range(n_batch)]
    for kc in range(d // k_chunk):
        rows = slice(kc * k_chunk, (kc + 1) * k_chunk)
        w = w_ref[rows, :]
        for b in range(n_batch):
            accs[b] = accs[b] + jnp.sum(w * s[rows, b:b + 1], axis=0, keepdims=True)
    for b in range(n_batch):
        o_ref[b:b + 1, :] = accs[b]


def _ada(c, w_ada, b_ada):
    n_batch, d = c.shape
    n = w_ada.shape[1]
    tn = 2048
    return pl.pallas_call(
        functools.partial(_ada_kernel, n_batch=n_batch, k_chunk=256),
        grid=(n // tn,),
        in_specs=[
            pl.BlockSpec((d, n_batch), lambda j: (0, 0)),
            pl.BlockSpec((d, tn), lambda j: (0, j)),
            pl.BlockSpec((1, tn), lambda j: (0, j)),
        ],
        out_specs=pl.BlockSpec((n_batch, tn), lambda j: (0, j)),
        out_shape=jax.ShapeDtypeStruct((n_batch, n), F32),
        compiler_params=_cparams(("arbitrary",)),
        name="ada",
    )(c.T, w_ada, b_ada.reshape(1, n))


IN_TN = 1024
IN_SMALL_TILE = 3 * FOX_WIDTH // IN_TN
NARROW_F = (0, LANES)
NARROW_CQ = (NARROW_F[1], NARROW_F[1] + MLA_Q_LORA)
NARROW_CKV = (NARROW_CQ[1], NARROW_CQ[1] + MLA_KV_LORA)
NARROW_KR = (NARROW_CKV[1], NARROW_CKV[1] + LANES)
assert NARROW_KR[1] == IN_TN and 2 * MLA_ROPE_DIM == LANES


def _narrow_segments(acc):
    c0 = FOX_HEADS
    c1 = c0 + MLA_Q_LORA
    c2 = c1 + MLA_KV_LORA
    c3 = c2 + MLA_ROPE_DIM
    half = MLA_ROPE_DIM // 2
    return jnp.concatenate([acc[:, 0:LANES], acc[:, c0:c1], acc[:, c1:c2], acc[:, c2:c3],
                            acc[:, c2 + half:c3], acc[:, c2:c2 + half]], axis=1)


def _inproj_kernel(x_ref, mod_ref, w_ref, o_ref, hi_ref, lo_ref, u_ref, *, n_cols):
    j = pl.program_id(1)
    last = pl.num_programs(1) - 1

    def product(wt, u=None):
        u = u_ref[...] if u is None else u
        return lax.dot_general(u, wt.astype(BF16), (((1,), (1,)), ((), ())), preferred_element_type=F32)

    @pl.when(j == 0)
    def _():
        shift = mod_ref[0:1, :]
        scale = mod_ref[1:2, :]
        u = (x_ref[...] * (1.0 + scale) + shift).astype(BF16)
        u_ref[...] = u
        o_ref[...] = (product(w_ref[...], u) * FOX_QSCALE).astype(o_ref.dtype)

    @pl.when(j == IN_SMALL_TILE)
    def _():
        acc = product(w_ref[...])
        o_ref[...] = acc.astype(o_ref.dtype)
        narrow = _narrow_segments(acc)
        hi = narrow.astype(hi_ref.dtype)
        hi_ref[...] = hi
        lo_ref[...] = (narrow - hi.astype(F32)).astype(lo_ref.dtype)

    @pl.when(j == last)
    def _():
        row = lax.broadcasted_iota(jnp.int32, w_ref.shape, 0)
        wt = jnp.where(row < n_cols - last * IN_TN, w_ref[...], 0.0)
        o_ref[...] = product(wt).astype(o_ref.dtype)

    @pl.when((j != 0) & (j != IN_SMALL_TILE) & (j != last))
    def _():
        o_ref[...] = product(w_ref[...]).astype(o_ref.dtype)


def _inproj(x2d, mod3, w_in_t, seq):
    m, d = x2d.shape
    n_cols = w_in_t.shape[0]
    tm = 1024
    n_tiles = pl.cdiv(n_cols, IN_TN)
    tiles_per_batch = seq // tm
    return pl.pallas_call(
        functools.partial(_inproj_kernel, n_cols=n_cols),
        grid=(m // tm, n_tiles),
        in_specs=[
            pl.BlockSpec((tm, d), lambda i, j: (i, 0)),
            pl.BlockSpec((None, N_ADA, d), lambda i, j: (i // tiles_per_batch, 0, 0)),
            pl.BlockSpec((IN_TN, d), lambda i, j: (j, 0)),
        ],
        out_specs=[pl.BlockSpec((tm, IN_TN), lambda i, j: (i, j)),
                   pl.BlockSpec((tm, IN_TN), lambda i, j: (i, 0)),
                   pl.BlockSpec((tm, IN_TN), lambda i, j: (i, 0))],
        out_shape=[jax.ShapeDtypeStruct((m, n_tiles * IN_TN), BF16),
                   jax.ShapeDtypeStruct((m, IN_TN), BF16),
                   jax.ShapeDtypeStruct((m, IN_TN), BF16)],
        scratch_shapes=[pltpu.VMEM((tm, d), BF16)],
        compiler_params=_cparams(("parallel", "arbitrary")),
        name="inproj",
    )(x2d, mod3, w_in_t)


def _log_sigmoid(x):
    return jnp.minimum(x, 0.0) - jnp.log1p(jnp.exp(-jnp.abs(x)))


def _prep_kernel(hi_ref, lo_ref, bf_ref, gq_ref, gkv_ref, wq_ref, wkv_ref, cos_ref, sin_ref,
                 qn_ref, qr_ref, kv_ref, kr_ref, cumt_ref, carry_ref, *, tiles_per_batch):
    i = pl.program_id(0)
    tm = hi_ref.shape[0]
    ps = hi_ref[...].astype(F32) + lo_ref[...].astype(F32)
    k_rope = ps[:, NARROW_KR[0]:NARROW_KR[1]]
    k_rope_partner = pltpu.roll(k_rope, MLA_ROPE_DIM, axis=1)
    cos = cos_ref[...]
    sin = sin_ref[...]

    logf = _log_sigmoid(ps[:, NARROW_F[0]:NARROW_F[1]] + bf_ref[...]) * LOG2E
    hi = logf.astype(BF16)
    r1 = logf - hi.astype(F32)
    mid = r1.astype(BF16)
    lo = (r1 - mid.astype(F32)).astype(BF16)
    row = lax.broadcasted_iota(jnp.int32, (tm, tm), 0)
    col = lax.broadcasted_iota(jnp.int32, (tm, tm), 1)
    tri = (row >= col).astype(BF16)
    cs = (jnp.dot(tri, hi, preferred_element_type=F32)
          + jnp.dot(tri, mid, preferred_element_type=F32)
          + jnp.dot(tri, lo, preferred_element_type=F32))

    @pl.when(i % tiles_per_batch == 0)
    def _():
        carry_ref[...] = jnp.zeros_like(carry_ref)

    cum = cs + carry_ref[...]
    carry_ref[...] = cum[tm - 1:tm, :]
    cumt_ref[...] = cum.T[0:FOX_HEADS, :]

    cqn = _rms_norm(ps[:, NARROW_CQ[0]:NARROW_CQ[1]], gq_ref[...]).astype(BF16)
    qa = jnp.dot(cqn, wq_ref[...], preferred_element_type=F32)
    hw = MLA_HEADS * LANES
    qn_ref[...] = (qa[:, 0:hw] * MLA_QSCALE).astype(BF16)
    for h in range(MLA_HEADS):
        a = qa[:, hw + h * LANES: hw + (h + 1) * LANES]
        b = qa[:, 2 * hw + h * LANES: 2 * hw + (h + 1) * LANES]
        qr_ref[:, h * LANES:(h + 1) * LANES] = ((a * cos + b * sin) * MLA_QSCALE).astype(BF16)

    ckvn = _rms_norm(ps[:, NARROW_CKV[0]:NARROW_CKV[1]], gkv_ref[...]).astype(BF16)
    kv_ref[...] = jnp.dot(ckvn, wkv_ref[...], preferred_element_type=F32).astype(BF16)
    kr_ref[...] = (k_rope * cos + k_rope_partner * sin).astype(BF16)


def _prep(narrow_hi, narrow_lo, bf_pad, g_q, g_kv, w_q_all, w_kv, cos_t, sin_t, seq):
    m = narrow_hi.shape[0]
    n_batch = m // seq
    tm = 512
    tiles_per_batch = seq // tm
    hw = MLA_HEADS * LANES
    const = lambda i: (0, 0)
    return pl.pallas_call(
        functools.partial(_prep_kernel, tiles_per_batch=tiles_per_batch),
        grid=(m // tm,),
        in_specs=[
            pl.BlockSpec((tm, IN_TN), lambda i: (i, 0)),
            pl.BlockSpec((tm, IN_TN), lambda i: (i, 0)),
            pl.BlockSpec(bf_pad.shape, const),
            pl.BlockSpec(g_q.shape, const),
            pl.BlockSpec(g_kv.shape, const),
            pl.BlockSpec(w_q_all.shape, const),
            pl.BlockSpec(w_kv.shape, const),
            pl.BlockSpec((tm, LANES), lambda i: (i % tiles_per_batch, 0)),
            pl.BlockSpec((tm, LANES), lambda i: (i % tiles_per_batch, 0)),
        ],
        out_specs=[
            pl.BlockSpec((tm, hw), lambda i: (i, 0)),
            pl.BlockSpec((tm, hw), lambda i: (i, 0)),
            pl.BlockSpec((tm, w_kv.shape[1]), lambda i: (i, 0)),
            pl.BlockSpec((tm, LANES), lambda i: (i, 0)),
            pl.BlockSpec((None, FOX_HEADS, tm), lambda i: (i // tiles_per_batch, 0, i % tiles_per_batch)),
        ],
        out_shape=[
            jax.ShapeDtypeStruct((m, hw), BF16),
            jax.ShapeDtypeStruct((m, hw), BF16),
            jax.ShapeDtypeStruct((m, w_kv.shape[1]), BF16),
            jax.ShapeDtypeStruct((m, LANES), BF16),
            jax.ShapeDtypeStruct((n_batch, FOX_HEADS, seq), F32),
        ],
        scratch_shapes=[pltpu.VMEM((1, LANES), F32)],
        compiler_params=_cparams(("arbitrary",)),
        name="prep",
    )(narrow_hi, narrow_lo, bf_pad, g_q, g_kv, w_q_all, w_kv, cos_t, sin_t)


def _split3(x):
    hi = x.astype(BF16).astype(F32)
    r = x - hi
    mid = r.astype(BF16).astype(F32)
    lo = (r - mid).astype(BF16).astype(F32)
    return hi, mid, lo


def _online_softmax_step(t, m, acc, vt_aug):
    m_new = jnp.maximum(m, jnp.max(t, axis=0, keepdims=True))
    alpha = jnp.exp2(m - m_new)
    p = jnp.exp2(t - m_new).astype(BF16)
    pv = jnp.dot(vt_aug, p, preferred_element_type=F32)
    return m_new, alpha * acc + pv[0:acc.shape[0], :]


def _causal_mask(t, key0, query0, granule):
    r = lax.broadcasted_iota(jnp.int32, t.shape, 0) + key0
    c = lax.broadcasted_iota(jnp.int32, t.shape, 1) + query0
    return jnp.where(r <= (c | (granule - 1)), t, NEG_BIG)


Q_TILES = 8
FULL, DIAG, SKIP = "full", "diag", "skip"


def _flash_key_major(j, tile, sub, keys, rhs, vt_aug, d_v, granule, finish):
    acc_rows = d_v + 8
    n_rounds = tile // sub
    chains = [(qt, half, hh) for qt in range(Q_TILES) for half in range(tile // HALF) for hh in (0, 1)]
    pairs = [(n, n + 1) for n in range(0, len(chains), 2)]

    def qk_of(k0, modes):
        loaded = {}

        def product(n):
            qt, half, hh = chains[n]
            n_keys = (half + 1) * HALF if modes[qt] == DIAG else tile
            if (hh, n_keys) not in loaded:
                loaded[hh, n_keys] = keys(hh, k0, n_keys)
            return jnp.dot(loaded[hh, n_keys], rhs(qt, hh)[:, half * HALF:(half + 1) * HALF],
                           preferred_element_type=F32)
        return product

    def visit(state, tiles):
        st = list(state)
        products = [qk_of(*t) for t in tiles]
        phases = [(ti, pair) for ti, (_, modes) in enumerate(tiles) for pair in pairs
                  if modes[chains[pair[0]][0]] != SKIP]

        def issue(phase):
            ti, pair = phase
            return {n: products[ti](n) for n in pair}

        cur = issue(phases[0])
        for idx, (ti, pair) in enumerate(phases):
            nxt = issue(phases[idx + 1]) if idx + 1 < len(phases) else None
            k0, modes = tiles[ti]
            for sb in range(n_rounds):
                ks = pl.multiple_of(k0 + sb * sub, sub)
                for n in pair:
                    qt, half, hh = chains[n]
                    is_diag = modes[qt] == DIAG
                    if is_diag and sb * sub >= (half + 1) * HALF:
                        continue
                    t = cur[n][sb * sub:(sb + 1) * sub, :]
                    if is_diag and (sb + 1) * sub > half * HALF:
                        t = _causal_mask(t, sb * sub, half * HALF, granule)
                    st[n] = _online_softmax_step(t, *st[n], vt_aug(hh, ks))
            cur = nxt
        return tuple(st)

    everyone = (FULL,) * Q_TILES

    def two_full(i, state):
        k0 = pl.multiple_of(2 * i * tile, tile)
        return visit(state, [(k0, everyone), (pl.multiple_of(k0 + tile, tile), everyone)])

    assert Q_TILES % 2 == 0
    init = tuple((jnp.full((1, HALF), NEG_BIG, F32), jnp.zeros((acc_rows, HALF), F32)) for _ in chains)
    state = lax.fori_loop(0, j * (Q_TILES // 2), two_full, init)
    q0 = j * (Q_TILES * tile)
    last = [(pl.multiple_of(q0 + t * tile, tile),
             tuple(SKIP if qt < t else DIAG if qt == t else FULL for qt in range(Q_TILES)))
            for t in range(Q_TILES)]
    return finish([acc[0:d_v, :] * (1.0 / acc[d_v:d_v + 1, :]) for (_, acc) in visit(state, last)])


def _cast_specs(weights, grid):
    n_steps = math.prod(grid)
    specs, shapes = [], []
    for w in weights:
        rows, cols = w.shape
        slab = rows // n_steps
        assert slab * n_steps == rows and slab % 16 == 0, (w.shape, n_steps)
        specs.append(pl.BlockSpec((slab, cols), lambda a, b: (a * grid[1] + b, 0)))
        shapes.append(jax.ShapeDtypeStruct(w.shape, BF16))
    return specs, shapes


def _cast_slabs(refs):
    n = len(refs) // 2
    for src, dst in zip(refs[:n], refs[n:]):
        dst[...] = src[...].astype(dst.dtype)


def _stack_chain_outputs(outs, n_half):
    return jnp.concatenate(
        [jnp.concatenate([outs[half * 2 + hh] for half in range(n_half)], axis=1) for hh in (0, 1)], axis=0)


AUG_ROWS = 16


def _bias_rows(n, entries):
    r = lax.broadcasted_iota(jnp.int32, (AUG_ROWS, n), 0)
    out = jnp.zeros((AUG_ROWS, n), F32)
    for row, val in entries.items():
        out = jnp.where(r == row, val, out)
    return out.astype(BF16)


def _fox_kernel(q_ref, k_ref, v_ref, cumt_ref, *rest, tq, n_cast):
    cast_in, (o_ref, *cast_out), (vta_ref, kaug_ref) = rest[:n_cast], rest[n_cast:2 * n_cast + 1], rest[2 * n_cast + 1:]
    seq = k_ref.shape[0]
    dh = FOX_HEAD_DIM
    n_half = tq // HALF
    aug = 6

    vt = v_ref[...].T
    tail_rows = lax.broadcasted_iota(jnp.int32, (16, seq), 0)
    tail = jnp.where(tail_rows < 8, 1.0, 0.0).astype(BF16)
    for hh in (0, 1):
        vta_ref[hh, 0:dh, :] = vt[hh * dh:(hh + 1) * dh, :]
        vta_ref[hh, dh:dh + 16, :] = tail
    key_rows = {}
    for hh in (0, 1):
        for idx, piece in enumerate(_split3(-cumt_ref[hh:hh + 1, :])):
            key_rows[aug * hh + idx] = piece
            key_rows[aug * hh + 3 + idx] = 1.0
    kaug_t = jnp.concatenate([_bias_rows(seq, key_rows), jnp.zeros((LANES - AUG_ROWS, seq), BF16)], axis=0)
    kaug_ref[...] = kaug_t.T

    def keys(hh, k0, n):
        return jnp.concatenate([k_ref[pl.ds(k0, n), :], kaug_ref[pl.ds(k0, n), :]], axis=1)

    def make_rhs(j):
        row = lax.broadcasted_iota(jnp.int32, (LANES, tq), 0)
        pad = jnp.zeros((LANES - AUG_ROWS, tq), BF16)
        out = []
        for t in range(Q_TILES):
            q0 = pl.multiple_of((j * Q_TILES + t) * tq, tq)
            qt = q_ref[pl.ds(q0, tq), :].T
            rhs = []
            for hh in (0, 1):
                qm = jnp.where((row >= hh * dh) & (row < (hh + 1) * dh), qt, jnp.zeros_like(qt))
                query_rows = {}
                for idx, piece in enumerate(_split3(cumt_ref[hh:hh + 1, pl.ds(q0, tq)])):
                    query_rows[aug * hh + idx] = 1.0
                    query_rows[aug * hh + 3 + idx] = piece
                rhs.append(jnp.concatenate([qm, _bias_rows(tq, query_rows), pad], axis=0))
            out.append(tuple(rhs))
        return tuple(out)

    n_super = seq // (Q_TILES * tq)
    per_tile = 2 * n_half

    def super_tile(j, rhs):
        def finish(outs):
            for t in range(Q_TILES):
                q0 = pl.multiple_of((j * Q_TILES + t) * tq, tq)
                tile_outs = outs[t * per_tile:(t + 1) * per_tile]
                o_ref[pl.ds(q0, tq), :] = _stack_chain_outputs(tile_outs, n_half).T.astype(o_ref.dtype)
            return make_rhs(jnp.minimum(j + 1, n_super - 1))

        return _flash_key_major(j, tq, FOX_SUB, keys, lambda t, hh: rhs[t][hh],
                                lambda hh, ks: vta_ref[hh, :, pl.ds(ks, FOX_SUB)], dh, 1, finish)

    lax.fori_loop(0, n_super, super_tile, make_rhs(0))
    _cast_slabs(cast_in + tuple(cast_out))


def _fox_attention(proj, cumt4, seq, q_col0, cast_weights):
    m = proj.shape[0]
    n_batch = m // seq
    n_pairs = FOX_HEADS // 2
    qb = q_col0 // LANES
    grid = (n_batch, n_pairs)
    cast_specs, cast_shapes = _cast_specs(cast_weights, grid)
    return pl.pallas_call(
        functools.partial(_fox_kernel, tq=512, n_cast=len(cast_weights)),
        grid=grid,
        in_specs=[
            pl.BlockSpec((seq, LANES), lambda b, hp: (b, qb + hp)),
            pl.BlockSpec((seq, LANES), lambda b, hp: (b, qb + n_pairs + hp)),
            pl.BlockSpec((seq, LANES), lambda b, hp: (b, qb + 2 * n_pairs + hp)),
            pl.BlockSpec((None, None, 2, seq), lambda b, hp: (b, hp, 0, 0)),
        ] + cast_specs,
        out_specs=[pl.BlockSpec((seq, LANES), lambda b, hp: (b, hp))] + cast_specs,
        out_shape=[jax.ShapeDtypeStruct((m, FOX_WIDTH), BF16)] + cast_shapes,
        scratch_shapes=[
            pltpu.VMEM((2, FOX_HEAD_DIM + 16, seq), BF16),
            pltpu.VMEM((seq, LANES), BF16),
        ],
        compiler_params=_cparams(("parallel", "parallel")),
        name="fox_attn",
    )(proj, proj, proj, cumt4, *cast_weights)


def _mla_kernel(qn_ref, qr_ref, kv_ref, kr_ref, *rest, tq, n_cast):
    cast_in, (o_ref, *cast_out), (vta_ref,) = rest[:n_cast], rest[n_cast:2 * n_cast + 1], rest[2 * n_cast + 1:]
    seq = kv_ref.shape[0]
    dv = MLA_V_DIM
    n_half = tq // HALF

    tail_rows = lax.broadcasted_iota(jnp.int32, (16, seq), 0)
    tail = jnp.where(tail_rows < 8, 1.0, 0.0).astype(BF16)
    for hh in (0, 1):
        vta_ref[hh, 0:dv, :] = kv_ref[:, (2 * hh + 1) * LANES:(2 * hh + 2) * LANES].T
        vta_ref[hh, dv:dv + 16, :] = tail

    def keys(hh, k0, n):
        return jnp.concatenate([kv_ref[pl.ds(k0, n), 2 * hh * LANES:(2 * hh + 1) * LANES],
                                kr_ref[pl.ds(k0, n), :]], axis=1)

    def make_rhs(j):
        out = []
        for t in range(Q_TILES):
            q0 = pl.multiple_of((j * Q_TILES + t) * tq, tq)
            out.append(tuple(jnp.concatenate([qn_ref[pl.ds(q0, tq), hh * LANES:(hh + 1) * LANES],
                                              qr_ref[pl.ds(q0, tq), hh * LANES:(hh + 1) * LANES]], axis=1).T
                             for hh in (0, 1)))
        return tuple(out)

    n_super = seq // (Q_TILES * tq)
    per_tile = 2 * n_half

    def super_tile(j, rhs):
        def finish(outs):
            for t in range(Q_TILES):
                q0 = pl.multiple_of((j * Q_TILES + t) * tq, tq)
                tile_outs = outs[t * per_tile:(t + 1) * per_tile]
                o_ref[pl.ds(q0, tq), :] = _stack_chain_outputs(tile_outs, n_half).T.astype(o_ref.dtype)
            return make_rhs(jnp.minimum(j + 1, n_super - 1))

        return _flash_key_major(j, tq, MLA_SUB, keys, lambda t, hh: rhs[t][hh],
                                lambda hh, ks: vta_ref[hh, :, pl.ds(ks, MLA_SUB)], dv, CHUNK, finish)

    lax.fori_loop(0, n_super, super_tile, make_rhs(0))
    _cast_slabs(cast_in + tuple(cast_out))


def _mla_attention(qn, qr, kv, kr, seq, cast_weights):
    m = qn.shape[0]
    n_batch = m // seq
    pair = 2 * LANES
    grid = (n_batch, MLA_HEADS // 2)
    cast_specs, cast_shapes = _cast_specs(cast_weights, grid)
    return pl.pallas_call(
        functools.partial(_mla_kernel, tq=512, n_cast=len(cast_weights)),
        grid=grid,
        in_specs=[
            pl.BlockSpec((seq, pair), lambda b, hp: (b, hp)),
            pl.BlockSpec((seq, pair), lambda b, hp: (b, hp)),
            pl.BlockSpec((seq, 2 * pair), lambda b, hp: (b, hp)),
            pl.BlockSpec((seq, LANES), lambda b, hp: (b, 0)),
        ] + cast_specs,
        out_specs=[pl.BlockSpec((seq, pair), lambda b, hp: (b, hp))] + cast_specs,
        out_shape=[jax.ShapeDtypeStruct((m, MLA_HEADS * MLA_V_DIM), BF16)] + cast_shapes,
        scratch_shapes=[pltpu.VMEM((2, MLA_V_DIM + 16, seq), BF16)],
        compiler_params=_cparams(("parallel", "parallel")),
        name="mla_attn",
    )(qn, qr, kv, kr, *cast_weights)


def _mix_kernel(yf_ref, ym_ref, gf_ref, gm_ref, x_ref, mod_ref, wbf_ref, wbm_ref, wo_ref, g_ref, b_ref,
                o_ref, mg_ref, *, n_chunk, g_off):
    d = o_ref.shape[1]
    yf = yf_ref[...]
    ym = ym_ref[...]

    def gate(ref, c):
        window = ref[:, c * n_chunk:(c + 1) * n_chunk + LANES]
        return jax.nn.sigmoid(window[:, g_off:g_off + n_chunk].astype(F32))

    for c in range(d // n_chunk):
        cols = slice(c * n_chunk, (c + 1) * n_chunk)
        a = jnp.dot(yf, wbf_ref[:, cols], preferred_element_type=F32)
        b = jnp.dot(ym, wbm_ref[:, cols], preferred_element_type=F32)
        merged = gate(gf_ref, c) * a + gate(gm_ref, c) * b
        mg_ref[:, cols] = merged.astype(BF16)
    mix = jnp.dot(mg_ref[...], wo_ref[...], preferred_element_type=F32)
    gate1 = mod_ref[2:3, :]
    o_ref[...] = _layer_norm(ALPHA * x_ref[...] + gate1 * mix, g_ref[...], b_ref[...])


def _mix(y_fox, y_mla, proj, gf_col, gm_col, x2d, mod3, w_bf, w_bm, w_o, ln_g, ln_b, seq):
    m, d = x2d.shape
    tm = 512
    tiles_per_batch = seq // tm
    const = lambda i: (0, 0)
    resident = dict(pipeline_mode=pl.Buffered(1))
    g_off = gf_col % LANES
    assert gm_col % LANES == g_off
    g_window = (pl.Element(tm), pl.Element(d + LANES))

    def window_at(col):
        return lambda i: (pl.multiple_of(i * tm, tm), col - g_off)

    return pl.pallas_call(
        functools.partial(_mix_kernel, n_chunk=1024, g_off=g_off),
        grid=(m // tm,),
        in_specs=[
            pl.BlockSpec((tm, y_fox.shape[1]), lambda i: (i, 0)),
            pl.BlockSpec((tm, y_mla.shape[1]), lambda i: (i, 0)),
            pl.BlockSpec(g_window, window_at(gf_col)),
            pl.BlockSpec(g_window, window_at(gm_col)),
            pl.BlockSpec((tm, d), lambda i: (i, 0)),
            pl.BlockSpec((None, N_ADA, d), lambda i: (i // tiles_per_batch, 0, 0)),
            pl.BlockSpec(w_bf.shape, const, **resident),
            pl.BlockSpec(w_bm.shape, const, **resident),
            pl.BlockSpec(w_o.shape, const, **resident),
            pl.BlockSpec((1, d), const),
            pl.BlockSpec((1, d), const),
        ],
        out_specs=pl.BlockSpec((tm, d), lambda i: (i, 0)),
        out_shape=jax.ShapeDtypeStruct((m, d), F32),
        scratch_shapes=[pltpu.VMEM((tm, d), BF16)],
        compiler_params=_cparams(("parallel",)),
        name="mix",
    )(y_fox, y_mla, proj, proj, x2d, mod3, w_bf, w_bm, w_o, ln_g, ln_b)


def _mlp_kernel(x_ref, mod_ref, wu_ref, wd_ref, g_ref, b_ref, o_ref, u_ref):
    f = pl.program_id(1)

    def partial_sum(u):
        z = jnp.dot(u, wu_ref[...], preferred_element_type=F32)
        h = jnp.square(jnp.maximum(z, 0.0)).astype(BF16)
        return jnp.dot(h, wd_ref[...], preferred_element_type=F32)

    @pl.when(f == 0)
    def _():
        shift = mod_ref[3:4, :]
        scale = mod_ref[4:5, :]
        u = (x_ref[...] * (1.0 + scale) + shift).astype(BF16)
        u_ref[...] = u
        o_ref[...] = partial_sum(u)

    @pl.when(f != 0)
    def _():
        o_ref[...] += partial_sum(u_ref[...])

    @pl.when(f == pl.num_programs(1) - 1)
    def _():
        gate = mod_ref[5:6, :]
        o_ref[...] = _layer_norm(ALPHA * x_ref[...] + gate * o_ref[...], g_ref[...], b_ref[...])


def _mlp(x1, mod3, w_up, w_down, ln_g, ln_b, seq):
    m, d = x1.shape
    d_ff = w_up.shape[1]
    tm, tf = 1024, 512
    tiles_per_batch = seq // tm
    return pl.pallas_call(
        _mlp_kernel,
        grid=(m // tm, d_ff // tf),
        in_specs=[
            pl.BlockSpec((tm, d), lambda i, f: (i, 0)),
            pl.BlockSpec((None, N_ADA, d), lambda i, f: (i // tiles_per_batch, 0, 0)),
            pl.BlockSpec((d, tf), lambda i, f: (0, f)),
            pl.BlockSpec((tf, d), lambda i, f: (f, 0)),
            pl.BlockSpec((1, d), lambda i, f: (0, 0)),
            pl.BlockSpec((1, d), lambda i, f: (0, 0)),
        ],
        out_specs=pl.BlockSpec((tm, d), lambda i, f: (i, 0)),
        out_shape=jax.ShapeDtypeStruct((m, d), F32),
        scratch_shapes=[pltpu.VMEM((tm, d), BF16)],
        compiler_params=_cparams(("parallel", "arbitrary")),
        name="mlp",
    )(x1, mod3, w_up, w_down, ln_g, ln_b)


def _pad_cols(w, width):
    return jnp.pad(w, ((0, 0), (0, width - w.shape[1])))


def _swap_halves(w):
    half = w.shape[-1] // 2
    return jnp.concatenate([w[..., half:], w[..., :half]], axis=-1)


def _layout_q_up(w_q_up):
    r = w_q_up.shape[0]
    w3 = w_q_up.reshape(r, MLA_HEADS, MLA_QK_DIM)
    w_nope = w3[:, :, :MLA_NOPE_DIM].reshape(r, MLA_HEADS * MLA_NOPE_DIM)
    w_rope = w3[:, :, MLA_NOPE_DIM:]
    pad = ((0, 0), (0, 0), (0, LANES - MLA_ROPE_DIM))
    w_r = jnp.pad(w_rope, pad).reshape(r, MLA_HEADS * LANES)
    w_rs = jnp.pad(_swap_halves(w_rope), pad).reshape(r, MLA_HEADS * LANES)
    return jnp.concatenate([w_nope, w_r, w_rs], axis=1).astype(BF16)


def _rope_tables(seq):
    pos = np.arange(seq, dtype=np.float64)
    inv_freq = ROPE_THETA ** (-np.arange(0, MLA_ROPE_DIM, 2, dtype=np.float64) / MLA_ROPE_DIM)
    ang = pos[:, None] * inv_freq[None, :]
    cos, sin = np.cos(ang), np.sin(ang)
    pad = np.zeros((seq, LANES - MLA_ROPE_DIM))
    cos_t = np.concatenate([cos, cos, pad], axis=1).astype(np.float32)
    sin_t = np.concatenate([-sin, sin, pad], axis=1).astype(np.float32)
    return jnp.asarray(cos_t), jnp.asarray(sin_t)


def kernel(x, c, w_ada, b_ada, w_in, b_forget, g_q_norm, w_q_up, g_kv_norm, w_kv_up, w_branch_fox,
           w_branch_mla, w_out, ln1_g, ln1_b, w_mlp_up, w_mlp_down, ln2_g, ln2_b):
    n_batch, seq, d = x.shape
    assert w_ada.shape[0] == DEPTH
    m = n_batch * seq
    x2d = x.reshape(m, d)

    mod3 = _ada(c, w_ada[0], b_ada[0]).reshape(n_batch, N_ADA, d)

    small_col = 3 * FOX_WIDTH
    gf_col = small_col + FOX_HEADS + MLA_Q_LORA + MLA_KV_LORA + MLA_ROPE_DIM
    gm_col = gf_col + d
    assert w_in.shape[2] == gm_col + d and small_col == IN_SMALL_TILE * IN_TN and gf_col <= small_col + IN_TN
    proj, narrow_hi, narrow_lo = _inproj(x2d, mod3, w_in[0].T, seq)

    cos_t, sin_t = _rope_tables(seq)
    bf_pad = _pad_cols(b_forget[0].reshape(1, FOX_HEADS), LANES)
    qn, qr, kv, kr, cumt = _prep(
        narrow_hi, narrow_lo, bf_pad, g_q_norm[0].reshape(1, -1), g_kv_norm[0].reshape(1, -1),
        _layout_q_up(w_q_up[0]), w_kv_up[0].astype(BF16), cos_t, sin_t, seq)

    y_fox, w_up_b, w_down_b = _fox_attention(
        proj, cumt.reshape(n_batch, FOX_HEADS // 2, 2, seq), seq, 0, (w_mlp_up[0], w_mlp_down[0]))
    y_mla, w_bf_b, w_bm_b, w_o_b = _mla_attention(
        qn, qr, kv, kr, seq, (w_branch_fox[0], w_branch_mla[0], w_out[0]))

    x1 = _mix(y_fox, y_mla, proj, gf_col, gm_col, x2d, mod3, w_bf_b, w_bm_b, w_o_b,
              ln1_g[0].reshape(1, d), ln1_b[0].reshape(1, d), seq)
    x2 = _mlp(x1, mod3, w_up_b, w_down_b, ln2_g[0].reshape(1, d), ln2_b[0].reshape(1, d), seq)
    return x2.reshape(n_batch, seq, d)
```

```python
import functools
import math

import jax
import jax.numpy as jnp
import numpy as np
from jax import lax
from jax.experimental import pallas as pl
from jax.experimental.pallas import tpu as pltpu

F32 = jnp.float32
BF16 = jnp.bfloat16

FOX_HEADS = 16
FOX_HEAD_DIM = 64
FOX_WIDTH = FOX_HEADS * FOX_HEAD_DIM
MLA_HEADS = 8
MLA_NOPE_DIM = 128
MLA_ROPE_DIM = 64
MLA_V_DIM = 128
MLA_Q_LORA = 512
MLA_KV_LORA = 256
MLA_QK_DIM = MLA_NOPE_DIM + MLA_ROPE_DIM
CHUNK = 64
ROPE_THETA = 10000.0
LN_EPS = 1e-5
RMS_EPS = 1e-6
N_ADA = 6
DEPTH = 1
ALPHA = (2.0 * DEPTH) ** 0.25

LANES = 128
LOG2E = math.log2(math.e)
FOX_QSCALE = FOX_HEAD_DIM ** -0.5 * LOG2E
MLA_QSCALE = MLA_QK_DIM ** -0.5 * LOG2E
NEG_BIG = -1e30
VMEM_LIMIT = 56 * 1024 * 1024

FOX_SUB = 128
MLA_SUB = 256
HALF = 256


def _cparams(semantics):
    return pltpu.CompilerParams(dimension_semantics=semantics, vmem_limit_bytes=VMEM_LIMIT)


def _layer_norm(y, g, b):
    mu = jnp.mean(y, axis=-1, keepdims=True)
    yc = y - mu
    var = jnp.mean(yc * yc, axis=-1, keepdims=True)
    return yc * lax.rsqrt(var + LN_EPS) * g + b


def _rms_norm(c, g):
    return c * lax.rsqrt(jnp.mean(c * c, axis=-1, keepdims=True) + RMS_EPS) * g


def _ada_kernel(ct_ref, w_ref, b_ref, o_ref, *, n_batch, k_chunk):
    d = w_ref.shape[0]
    ct = ct_ref[...]
    s = ct * jax.nn.sigmoid(ct)
    accs = [b_ref[...] for _ in range(n_batch)]
    for kc in range(d // k_chunk):
        rows = slice(kc * k_chunk, (kc + 1) * k_chunk)
        w = w_ref[rows, :]
        for b in range(n_batch):
            accs[b] = accs[b] + jnp.sum(w * s[rows, b:b + 1], axis=0, keepdims=True)
    for b in range(n_batch):
        o_ref[b:b + 1, :] = accs[b]


def _ada(c, w_ada, b_ada):
    n_batch, d = c.shape
    n = w_ada.shape[1]
    tn = 1024
    return pl.pallas_call(
        functools.partial(_ada_kernel, n_batch=n_batch, k_chunk=256),
        grid=(n // tn,),
        in_specs=[
            pl.BlockSpec((d, n_batch), lambda j: (0, 0)),
            pl.BlockSpec((d, tn), lambda j: (0, j)),
            pl.BlockSpec((1, tn), lambda j: (0, j)),
        ],
        out_specs=pl.BlockSpec((n_batch, tn), lambda j: (0, j)),
        out_shape=jax.ShapeDtypeStruct((n_batch, n), F32),
        compiler_params=_cparams(("arbitrary",)),
        name="ada",
    )(c.T, w_ada, b_ada.reshape(1, n))


IN_TN = 1024
IN_SMALL_TILE = 3 * FOX_WIDTH // IN_TN
NARROW_F = (0, LANES)
NARROW_CQ = (NARROW_F[1], NARROW_F[1] + MLA_Q_LORA)
NARROW_CKV = (NARROW_CQ[1], NARROW_CQ[1] + MLA_KV_LORA)
NARROW_KR = (NARROW_CKV[1], NARROW_CKV[1] + LANES)
assert NARROW_KR[1] == IN_TN and 2 * MLA_ROPE_DIM == LANES


def _narrow_segments(acc):
    c0 = FOX_HEADS
    c1 = c0 + MLA_Q_LORA
    c2 = c1 + MLA_KV_LORA
    c3 = c2 + MLA_ROPE_DIM
    half = MLA_ROPE_DIM // 2
    return jnp.concatenate([acc[:, 0:LANES], acc[:, c0:c1], acc[:, c1:c2], acc[:, c2:c3],
                            acc[:, c2 + half:c3], acc[:, c2:c2 + half]], axis=1)


def _inproj_kernel(x_ref, mod_ref, w_ref, o_ref, hi_ref, lo_ref, u_ref, *, n_cols):
    j = pl.program_id(1)
    last = pl.num_programs(1) - 1

    def product(wt, u=None):
        u = u_ref[...] if u is None else u
        return lax.dot_general(u, wt.astype(BF16), (((1,), (1,)), ((), ())), preferred_element_type=F32)

    @pl.when(j == 0)
    def _():
        shift = mod_ref[0:1, :]
        scale = mod_ref[1:2, :]
        u = (x_ref[...] * (1.0 + scale) + shift).astype(BF16)
        u_ref[...] = u
        o_ref[...] = (product(w_ref[...], u) * FOX_QSCALE).astype(o_ref.dtype)

    @pl.when(j == IN_SMALL_TILE)
    def _():
        acc = product(w_ref[...])
        o_ref[...] = acc.astype(o_ref.dtype)
        narrow = _narrow_segments(acc)
        hi = narrow.astype(hi_ref.dtype)
        hi_ref[...] = hi
        lo_ref[...] = (narrow - hi.astype(F32)).astype(lo_ref.dtype)

    @pl.when(j == last)
    def _():
        row = lax.broadcasted_iota(jnp.int32, w_ref.shape, 0)
        wt = jnp.where(row < n_cols - last * IN_TN, w_ref[...], 0.0)
        o_ref[...] = product(wt).astype(o_ref.dtype)

    @pl.when((j != 0) & (j != IN_SMALL_TILE) & (j != last))
    def _():
        o_ref[...] = product(w_ref[...]).astype(o_ref.dtype)


def _inproj(x2d, mod3, w_in_t, seq):
    m, d = x2d.shape
    n_cols = w_in_t.shape[0]
    tm = 1024
    n_tiles = pl.cdiv(n_cols, IN_TN)
    tiles_per_batch = seq // tm
    return pl.pallas_call(
        functools.partial(_inproj_kernel, n_cols=n_cols),
        grid=(m // tm, n_tiles),
        in_specs=[
            pl.BlockSpec((tm, d), lambda i, j: (i, 0)),
            pl.BlockSpec((None, N_ADA, d), lambda i, j: (i // tiles_per_batch, 0, 0)),
            pl.BlockSpec((IN_TN, d), lambda i, j: (j, 0)),
        ],
        out_specs=[pl.BlockSpec((tm, IN_TN), lambda i, j: (i, j)),
                   pl.BlockSpec((tm, IN_TN), lambda i, j: (i, 0)),
                   pl.BlockSpec((tm, IN_TN), lambda i, j: (i, 0))],
        out_shape=[jax.ShapeDtypeStruct((m, n_tiles * IN_TN), BF16),
                   jax.ShapeDtypeStruct((m, IN_TN), BF16),
                   jax.ShapeDtypeStruct((m, IN_TN), BF16)],
        scratch_shapes=[pltpu.VMEM((tm, d), BF16)],
        compiler_params=_cparams(("parallel", "arbitrary")),
        name="inproj",
    )(x2d, mod3, w_in_t)


def _log_sigmoid(x):
    return jnp.minimum(x, 0.0) - jnp.log1p(jnp.exp(-jnp.abs(x)))


def _prep_kernel(hi_ref, lo_ref, bf_ref, gq_ref, gkv_ref, wq_ref, wkv_ref, cos_ref, sin_ref,
                 qn_ref, qr_ref, kv_ref, kr_ref, cumt_ref, carry_ref, *, tiles_per_batch):
    i = pl.program_id(0)
    tm = hi_ref.shape[0]

    @pl.when(i % tiles_per_batch == 0)
    def _():
        carry_ref[...] = jnp.zeros_like(carry_ref)

    ps = hi_ref[...].astype(F32) + lo_ref[...].astype(F32)
    k_rope = ps[:, NARROW_KR[0]:NARROW_KR[1]]
    k_rope_partner = pltpu.roll(k_rope, MLA_ROPE_DIM, axis=1)
    cos = cos_ref[...]
    sin = sin_ref[...]

    logf = _log_sigmoid(ps[:, NARROW_F[0]:NARROW_F[1]] + bf_ref[...]) * LOG2E
    hi = logf.astype(BF16)
    r1 = logf - hi.astype(F32)
    mid = r1.astype(BF16)
    lo = (r1 - mid.astype(F32)).astype(BF16)
    row = lax.broadcasted_iota(jnp.int32, (tm, tm), 0)
    col = lax.broadcasted_iota(jnp.int32, (tm, tm), 1)
    tri = (row >= col).astype(BF16)
    cs = (jnp.dot(tri, hi, preferred_element_type=F32)
          + jnp.dot(tri, mid, preferred_element_type=F32)
          + jnp.dot(tri, lo, preferred_element_type=F32))
    cum = cs + carry_ref[...]
    carry_ref[...] = cum[tm - 1:tm, :]
    cumt_ref[...] = cum.T[0:FOX_HEADS, :]

    cqn = _rms_norm(ps[:, NARROW_CQ[0]:NARROW_CQ[1]], gq_ref[...]).astype(BF16)
    qa = jnp.dot(cqn, wq_ref[...], preferred_element_type=F32)
    hw = MLA_HEADS * LANES
    qn_ref[...] = (qa[:, 0:hw] * MLA_QSCALE).astype(BF16)
    for h in range(MLA_HEADS):
        a = qa[:, hw + h * LANES: hw + (h + 1) * LANES]
        b = qa[:, 2 * hw + h * LANES: 2 * hw + (h + 1) * LANES]
        qr_ref[:, h * LANES:(h + 1) * LANES] = ((a * cos + b * sin) * MLA_QSCALE).astype(BF16)

    ckvn = _rms_norm(ps[:, NARROW_CKV[0]:NARROW_CKV[1]], gkv_ref[...]).astype(BF16)
    kv_ref[...] = jnp.dot(ckvn, wkv_ref[...], preferred_element_type=F32).astype(BF16)
    kr_ref[...] = (k_rope * cos + k_rope_partner * sin).astype(BF16)


def _prep(narrow_hi, narrow_lo, bf_pad, g_q, g_kv, w_q_all, w_kv, cos_t, sin_t, seq):
    m = narrow_hi.shape[0]
    n_batch = m // seq
    tm = 512
    tiles_per_batch = seq // tm
    hw = MLA_HEADS * LANES
    const = lambda i: (0, 0)
    return pl.pallas_call(
        functools.partial(_prep_kernel, tiles_per_batch=tiles_per_batch),
        grid=(m // tm,),
        in_specs=[
            pl.BlockSpec((tm, IN_TN), lambda i: (i, 0)),
            pl.BlockSpec((tm, IN_TN), lambda i: (i, 0)),
            pl.BlockSpec(bf_pad.shape, const),
            pl.BlockSpec(g_q.shape, const),
            pl.BlockSpec(g_kv.shape, const),
            pl.BlockSpec(w_q_all.shape, const),
            pl.BlockSpec(w_kv.shape, const),
            pl.BlockSpec((tm, LANES), lambda i: (i % tiles_per_batch, 0)),
            pl.BlockSpec((tm, LANES), lambda i: (i % tiles_per_batch, 0)),
        ],
        out_specs=[
            pl.BlockSpec((tm, hw), lambda i: (i, 0)),
            pl.BlockSpec((tm, hw), lambda i: (i, 0)),
            pl.BlockSpec((tm, w_kv.shape[1]), lambda i: (i, 0)),
            pl.BlockSpec((tm, LANES), lambda i: (i, 0)),
            pl.BlockSpec((None, FOX_HEADS, tm), lambda i: (i // tiles_per_batch, 0, i % tiles_per_batch)),
        ],
        out_shape=[
            jax.ShapeDtypeStruct((m, hw), BF16),
            jax.ShapeDtypeStruct((m, hw), BF16),
            jax.ShapeDtypeStruct((m, w_kv.shape[1]), BF16),
            jax.ShapeDtypeStruct((m, LANES), BF16),
            jax.ShapeDtypeStruct((n_batch, FOX_HEADS, seq), F32),
        ],
        scratch_shapes=[pltpu.VMEM((1, LANES), F32)],
        compiler_params=_cparams(("arbitrary",)),
        name="prep",
    )(narrow_hi, narrow_lo, bf_pad, g_q, g_kv, w_q_all, w_kv, cos_t, sin_t)


def _split3(x):
    hi = x.astype(BF16).astype(F32)
    r = x - hi
    mid = r.astype(BF16).astype(F32)
    lo = (r - mid).astype(BF16).astype(F32)
    return hi, mid, lo


def _online_softmax_step(t, m, acc, vt_aug):
    m_new = jnp.maximum(m, jnp.max(t, axis=0, keepdims=True))
    alpha = jnp.exp2(m - m_new)
    p = jnp.exp2(t - m_new).astype(BF16)
    pv = jnp.dot(vt_aug, p, preferred_element_type=F32)
    return m_new, alpha * acc + pv[0:acc.shape[0], :]


def _causal_mask(t, key0, query0, granule):
    r = lax.broadcasted_iota(jnp.int32, t.shape, 0) + key0
    c = lax.broadcasted_iota(jnp.int32, t.shape, 1) + query0
    return jnp.where(r <= (c | (granule - 1)), t, NEG_BIG)


Q_TILES = 8
FULL, DIAG, SKIP = "full", "diag", "skip"


def _flash_key_major(j, tile, sub, keys, rhs, vt_aug, d_v, granule, finish):
    acc_rows = d_v + 8
    n_rounds = tile // sub
    chains = [(qt, half, hh) for qt in range(Q_TILES) for half in range(tile // HALF) for hh in (0, 1)]
    pairs = [(n, n + 1) for n in range(0, len(chains), 2)]

    def qk_of(k0, modes):
        loaded = {}

        def product(n):
            qt, half, hh = chains[n]
            n_keys = (half + 1) * HALF if modes[qt] == DIAG else tile
            if (hh, n_keys) not in loaded:
                loaded[hh, n_keys] = keys(hh, k0, n_keys)
            return jnp.dot(loaded[hh, n_keys], rhs(qt, hh)[:, half * HALF:(half + 1) * HALF],
                           preferred_element_type=F32)
        return product

    def visit(state, tiles):
        st = list(state)
        products = [qk_of(*t) for t in tiles]
        phases = [(ti, pair) for ti, (_, modes) in enumerate(tiles) for pair in pairs
                  if modes[chains[pair[0]][0]] != SKIP]

        def issue(phase):
            ti, pair = phase
            return {n: products[ti](n) for n in pair}

        cur = issue(phases[0])
        for idx, (ti, pair) in enumerate(phases):
            nxt = issue(phases[idx + 1]) if idx + 1 < len(phases) else None
            k0, modes = tiles[ti]
            for sb in range(n_rounds):
                ks = pl.multiple_of(k0 + sb * sub, sub)
                for n in pair:
                    qt, half, hh = chains[n]
                    is_diag = modes[qt] == DIAG
                    if is_diag and sb * sub >= (half + 1) * HALF:
                        continue
                    t = cur[n][sb * sub:(sb + 1) * sub, :]
                    if is_diag and (sb + 1) * sub > half * HALF:
                        t = _causal_mask(t, sb * sub, half * HALF, granule)
                    st[n] = _online_softmax_step(t, *st[n], vt_aug(hh, ks))
            cur = nxt
        return tuple(st)

    everyone = (FULL,) * Q_TILES

    def two_full(i, state):
        k0 = pl.multiple_of(2 * i * tile, tile)
        return visit(state, [(k0, everyone), (pl.multiple_of(k0 + tile, tile), everyone)])

    assert Q_TILES % 2 == 0
    init = tuple((jnp.full((1, HALF), NEG_BIG, F32), jnp.zeros((acc_rows, HALF), F32)) for _ in chains)
    state = lax.fori_loop(0, j * (Q_TILES // 2), two_full, init)
    q0 = j * (Q_TILES * tile)
    last = [(pl.multiple_of(q0 + t * tile, tile),
             tuple(SKIP if qt < t else DIAG if qt == t else FULL for qt in range(Q_TILES)))
            for t in range(Q_TILES)]
    return finish([acc[0:d_v, :] * (1.0 / acc[d_v:d_v + 1, :]) for (_, acc) in visit(state, last)])


def _cast_specs(weights, grid):
    n_steps = math.prod(grid)
    specs, shapes = [], []
    for w in weights:
        rows, cols = w.shape
        slab = rows // n_steps
        assert slab * n_steps == rows and slab % 16 == 0, (w.shape, n_steps)
        specs.append(pl.BlockSpec((slab, cols), lambda a, b: (a * grid[1] + b, 0)))
        shapes.append(jax.ShapeDtypeStruct(w.shape, BF16))
    return specs, shapes


def _cast_slabs(refs):
    n = len(refs) // 2
    for src, dst in zip(refs[:n], refs[n:]):
        dst[...] = src[...].astype(dst.dtype)


def _stack_chain_outputs(outs, n_half):
    return jnp.concatenate(
        [jnp.concatenate([outs[half * 2 + hh] for half in range(n_half)], axis=1) for hh in (0, 1)], axis=0)


AUG_ROWS = 16


def _bias_rows(n, entries):
    r = lax.broadcasted_iota(jnp.int32, (AUG_ROWS, n), 0)
    out = jnp.zeros((AUG_ROWS, n), F32)
    for row, val in entries.items():
        out = jnp.where(r == row, val, out)
    return out.astype(BF16)


def _fox_kernel(q_ref, k_ref, v_ref, cumt_ref, *rest, tq, n_cast):
    cast_in, (o_ref, *cast_out), (vta_ref, kaug_ref) = rest[:n_cast], rest[n_cast:2 * n_cast + 1], rest[2 * n_cast + 1:]
    seq = k_ref.shape[0]
    dh = FOX_HEAD_DIM
    n_half = tq // HALF
    aug = 6

    vt = v_ref[...].T
    tail_rows = lax.broadcasted_iota(jnp.int32, (16, seq), 0)
    tail = jnp.where(tail_rows < 8, 1.0, 0.0).astype(BF16)
    for hh in (0, 1):
        vta_ref[hh, 0:dh, :] = vt[hh * dh:(hh + 1) * dh, :]
        vta_ref[hh, dh:dh + 16, :] = tail
    key_rows = {}
    for hh in (0, 1):
        for idx, piece in enumerate(_split3(-cumt_ref[hh:hh + 1, :])):
            key_rows[aug * hh + idx] = piece
            key_rows[aug * hh + 3 + idx] = 1.0
    kaug_t = jnp.concatenate([_bias_rows(seq, key_rows), jnp.zeros((LANES - AUG_ROWS, seq), BF16)], axis=0)
    kaug_ref[...] = kaug_t.T

    def keys(hh, k0, n):
        return jnp.concatenate([k_ref[pl.ds(k0, n), :], kaug_ref[pl.ds(k0, n), :]], axis=1)

    def make_rhs(j):
        row = lax.broadcasted_iota(jnp.int32, (LANES, tq), 0)
        pad = jnp.zeros((LANES - AUG_ROWS, tq), BF16)
        out = []
        for t in range(Q_TILES):
            q0 = pl.multiple_of((j * Q_TILES + t) * tq, tq)
            qt = q_ref[pl.ds(q0, tq), :].T
            rhs = []
            for hh in (0, 1):
                qm = jnp.where((row >= hh * dh) & (row < (hh + 1) * dh), qt, jnp.zeros_like(qt))
                query_rows = {}
                for idx, piece in enumerate(_split3(cumt_ref[hh:hh + 1, pl.ds(q0, tq)])):
                    query_rows[aug * hh + idx] = 1.0
                    query_rows[aug * hh + 3 + idx] = piece
                rhs.append(jnp.concatenate([qm, _bias_rows(tq, query_rows), pad], axis=0))
            out.append(tuple(rhs))
        return tuple(out)

    n_super = seq // (Q_TILES * tq)
    per_tile = 2 * n_half

    def super_tile(j, rhs):
        def finish(outs):
            for t in range(Q_TILES):
                q0 = pl.multiple_of((j * Q_TILES + t) * tq, tq)
                tile_outs = outs[t * per_tile:(t + 1) * per_tile]
                o_ref[pl.ds(q0, tq), :] = _stack_chain_outputs(tile_outs, n_half).T.astype(o_ref.dtype)
            return make_rhs(jnp.minimum(j + 1, n_super - 1))

        return _flash_key_major(j, tq, FOX_SUB, keys, lambda t, hh: rhs[t][hh],
                                lambda hh, ks: vta_ref[hh, :, pl.ds(ks, FOX_SUB)], dh, 1, finish)

    lax.fori_loop(0, n_super, super_tile, make_rhs(0))
    _cast_slabs(cast_in + tuple(cast_out))


def _fox_attention(proj, cumt4, seq, q_col0, cast_weights):
    m = proj.shape[0]
    n_batch = m // seq
    n_pairs = FOX_HEADS // 2
    qb = q_col0 // LANES
    grid = (n_batch, n_pairs)
    cast_specs, cast_shapes = _cast_specs(cast_weights, grid)
    return pl.pallas_call(
        functools.partial(_fox_kernel, tq=512, n_cast=len(cast_weights)),
        grid=grid,
        in_specs=[
            pl.BlockSpec((seq, LANES), lambda b, hp: (b, qb + hp)),
            pl.BlockSpec((seq, LANES), lambda b, hp: (b, qb + n_pairs + hp)),
            pl.BlockSpec((seq, LANES), lambda b, hp: (b, qb + 2 * n_pairs + hp)),
            pl.BlockSpec((None, None, 2, seq), lambda b, hp: (b, hp, 0, 0)),
        ] + cast_specs,
        out_specs=[pl.BlockSpec((seq, LANES), lambda b, hp: (b, hp))] + cast_specs,
        out_shape=[jax.ShapeDtypeStruct((m, FOX_WIDTH), BF16)] + cast_shapes,
        scratch_shapes=[
            pltpu.VMEM((2, FOX_HEAD_DIM + 16, seq), BF16),
            pltpu.VMEM((seq, LANES), BF16),
        ],
        compiler_params=_cparams(("parallel", "parallel")),
        name="fox_attn",
    )(proj, proj, proj, cumt4, *cast_weights)


def _mla_kernel(qn_ref, qr_ref, kv_ref, kr_ref, *rest, tq, n_cast):
    cast_in, (o_ref, *cast_out), (vta_ref,) = rest[:n_cast], rest[n_cast:2 * n_cast + 1], rest[2 * n_cast + 1:]
    seq = kv_ref.shape[0]
    dv = MLA_V_DIM
    n_half = tq // HALF

    tail_rows = lax.broadcasted_iota(jnp.int32, (16, seq), 0)
    tail = jnp.where(tail_rows < 8, 1.0, 0.0).astype(BF16)
    for hh in (0, 1):
        vta_ref[hh, 0:dv, :] = kv_ref[:, (2 * hh + 1) * LANES:(2 * hh + 2) * LANES].T
        vta_ref[hh, dv:dv + 16, :] = tail

    def keys(hh, k0, n):
        return jnp.concatenate([kv_ref[pl.ds(k0, n), 2 * hh * LANES:(2 * hh + 1) * LANES],
                                kr_ref[pl.ds(k0, n), :]], axis=1)

    def make_rhs(j):
        out = []
        for t in range(Q_TILES):
            q0 = pl.multiple_of((j * Q_TILES + t) * tq, tq)
            out.append(tuple(jnp.concatenate([qn_ref[pl.ds(q0, tq), hh * LANES:(hh + 1) * LANES],
                                              qr_ref[pl.ds(q0, tq), hh * LANES:(hh + 1) * LANES]], axis=1).T
                             for hh in (0, 1)))
        return tuple(out)

    n_super = seq // (Q_TILES * tq)
    per_tile = 2 * n_half

    def super_tile(j, rhs):
        def finish(outs):
            for t in range(Q_TILES):
                q0 = pl.multiple_of((j * Q_TILES + t) * tq, tq)
                tile_outs = outs[t * per_tile:(t + 1) * per_tile]
                o_ref[pl.ds(q0, tq), :] = _stack_chain_outputs(tile_outs, n_half).T.astype(o_ref.dtype)
            return make_rhs(jnp.minimum(j + 1, n_super - 1))

        return _flash_key_major(j, tq, MLA_SUB, keys, lambda t, hh: rhs[t][hh],
                                lambda hh, ks: vta_ref[hh, :, pl.ds(ks, MLA_SUB)], dv, CHUNK, finish)

    lax.fori_loop(0, n_super, super_tile, make_rhs(0))
    _cast_slabs(cast_in + tuple(cast_out))


def _mla_attention(qn, qr, kv, kr, seq, cast_weights):
    m = qn.shape[0]
    n_batch = m // seq
    pair = 2 * LANES
    grid = (n_batch, MLA_HEADS // 2)
    cast_specs, cast_shapes = _cast_specs(cast_weights, grid)
    return pl.pallas_call(
        functools.partial(_mla_kernel, tq=512, n_cast=len(cast_weights)),
        grid=grid,
        in_specs=[
            pl.BlockSpec((seq, pair), lambda b, hp: (b, hp)),
            pl.BlockSpec((seq, pair), lambda b, hp: (b, hp)),
            pl.BlockSpec((seq, 2 * pair), lambda b, hp: (b, hp)),
            pl.BlockSpec((seq, LANES), lambda b, hp: (b, 0)),
        ] + cast_specs,
        out_specs=[pl.BlockSpec((seq, pair), lambda b, hp: (b, hp))] + cast_specs,
        out_shape=[jax.ShapeDtypeStruct((m, MLA_HEADS * MLA_V_DIM), BF16)] + cast_shapes,
        scratch_shapes=[pltpu.VMEM((2, MLA_V_DIM + 16, seq), BF16)],
        compiler_params=_cparams(("parallel", "parallel")),
        name="mla_attn",
    )(qn, qr, kv, kr, *cast_weights)


def _mix_kernel(yf_ref, ym_ref, gf_ref, gm_ref, x_ref, mod_ref, wbf_ref, wbm_ref, wo_ref, g_ref, b_ref,
                o_ref, mg_ref, *, n_chunk, g_off):
    d = o_ref.shape[1]
    yf = yf_ref[...]
    ym = ym_ref[...]

    def gate(ref, c):
        window = ref[:, c * n_chunk:(c + 1) * n_chunk + LANES]
        return jax.nn.sigmoid(window[:, g_off:g_off + n_chunk].astype(F32))

    for c in range(d // n_chunk):
        cols = slice(c * n_chunk, (c + 1) * n_chunk)
        a = jnp.dot(yf, wbf_ref[:, cols], preferred_element_type=F32)
        b = jnp.dot(ym, wbm_ref[:, cols], preferred_element_type=F32)
        merged = gate(gf_ref, c) * a + gate(gm_ref, c) * b
        mg_ref[:, cols] = merged.astype(BF16)
    mix = jnp.dot(mg_ref[...], wo_ref[...], preferred_element_type=F32)
    gate1 = mod_ref[2:3, :]
    o_ref[...] = _layer_norm(ALPHA * x_ref[...] + gate1 * mix, g_ref[...], b_ref[...])


def _mix(y_fox, y_mla, proj, gf_col, gm_col, x2d, mod3, w_bf, w_bm, w_o, ln_g, ln_b, seq):
    m, d = x2d.shape
    tm = 512
    tiles_per_batch = seq // tm
    const = lambda i: (0, 0)
    resident = dict(pipeline_mode=pl.Buffered(1))
    g_off = gf_col % LANES
    assert gm_col % LANES == g_off
    g_window = (pl.Element(tm), pl.Element(d + LANES))

    def window_at(col):
        return lambda i: (pl.multiple_of(i * tm, tm), col - g_off)

    return pl.pallas_call(
        functools.partial(_mix_kernel, n_chunk=512, g_off=g_off),
        grid=(m // tm,),
        in_specs=[
            pl.BlockSpec((tm, y_fox.shape[1]), lambda i: (i, 0)),
            pl.BlockSpec((tm, y_mla.shape[1]), lambda i: (i, 0)),
            pl.BlockSpec(g_window, window_at(gf_col)),
            pl.BlockSpec(g_window, window_at(gm_col)),
            pl.BlockSpec((tm, d), lambda i: (i, 0)),
            pl.BlockSpec((None, N_ADA, d), lambda i: (i // tiles_per_batch, 0, 0)),
            pl.BlockSpec(w_bf.shape, const, **resident),
            pl.BlockSpec(w_bm.shape, const, **resident),
            pl.BlockSpec(w_o.shape, const, **resident),
            pl.BlockSpec((1, d), const),
            pl.BlockSpec((1, d), const),
        ],
        out_specs=pl.BlockSpec((tm, d), lambda i: (i, 0)),
        out_shape=jax.ShapeDtypeStruct((m, d), F32),
        scratch_shapes=[pltpu.VMEM((tm, d), BF16)],
        compiler_params=_cparams(("parallel",)),
        name="mix",
    )(y_fox, y_mla, proj, proj, x2d, mod3, w_bf, w_bm, w_o, ln_g, ln_b)


def _mlp_kernel(x_ref, mod_ref, wu_ref, wd_ref, g_ref, b_ref, o_ref, u_ref):
    f = pl.program_id(1)

    def partial_sum(u):
        z = jnp.dot(u, wu_ref[...], preferred_element_type=F32)
        h = jnp.square(jnp.maximum(z, 0.0)).astype(BF16)
        return jnp.dot(h, wd_ref[...], preferred_element_type=F32)

    @pl.when(f == 0)
    def _():
        shift = mod_ref[3:4, :]
        scale = mod_ref[4:5, :]
        u = (x_ref[...] * (1.0 + scale) + shift).astype(BF16)
        u_ref[...] = u
        o_ref[...] = partial_sum(u)

    @pl.when(f != 0)
    def _():
        o_ref[...] += partial_sum(u_ref[...])

    @pl.when(f == pl.num_programs(1) - 1)
    def _():
        gate = mod_ref[5:6, :]
        o_ref[...] = _layer_norm(ALPHA * x_ref[...] + gate * o_ref[...], g_ref[...], b_ref[...])


def _mlp(x1, mod3, w_up, w_down, ln_g, ln_b, seq):
    m, d = x1.shape
    d_ff = w_up.shape[1]
    tm, tf = 1024, 512
    tiles_per_batch = seq // tm
    return pl.pallas_call(
        _mlp_kernel,
        grid=(m // tm, d_ff // tf),
        in_specs=[
            pl.BlockSpec((tm, d), lambda i, f: (i, 0)),
            pl.BlockSpec((None, N_ADA, d), lambda i, f: (i // tiles_per_batch, 0, 0)),
            pl.BlockSpec((d, tf), lambda i, f: (0, f)),
            pl.BlockSpec((tf, d), lambda i, f: (f, 0)),
            pl.BlockSpec((1, d), lambda i, f: (0, 0)),
            pl.BlockSpec((1, d), lambda i, f: (0, 0)),
        ],
        out_specs=pl.BlockSpec((tm, d), lambda i, f: (i, 0)),
        out_shape=jax.ShapeDtypeStruct((m, d), F32),
        scratch_shapes=[pltpu.VMEM((tm, d), BF16)],
        compiler_params=_cparams(("parallel", "arbitrary")),
        name="mlp",
    )(x1, mod3, w_up, w_down, ln_g, ln_b)


def _pad_cols(w, width):
    return jnp.pad(w, ((0, 0), (0, width - w.shape[1])))


def _swap_halves(w):
    half = w.shape[-1] // 2
    return jnp.concatenate([w[..., half:], w[..., :half]], axis=-1)


def _layout_q_up(w_q_up):
    r = w_q_up.shape[0]
    w3 = w_q_up.reshape(r, MLA_HEADS, MLA_QK_DIM)
    w_nope = w3[:, :, :MLA_NOPE_DIM].reshape(r, MLA_HEADS * MLA_NOPE_DIM)
    w_rope = w3[:, :, MLA_NOPE_DIM:]
    pad = ((0, 0), (0, 0), (0, LANES - MLA_ROPE_DIM))
    w_r = jnp.pad(w_rope, pad).reshape(r, MLA_HEADS * LANES)
    w_rs = jnp.pad(_swap_halves(w_rope), pad).reshape(r, MLA_HEADS * LANES)
    return jnp.concatenate([w_nope, w_r, w_rs], axis=1).astype(BF16)


def _rope_tables(seq):
    pos = np.arange(seq, dtype=np.float64)
    inv_freq = ROPE_THETA ** (-np.arange(0, MLA_ROPE_DIM, 2, dtype=np.float64) / MLA_ROPE_DIM)
    ang = pos[:, None] * inv_freq[None, :]
    cos, sin = np.cos(ang), np.sin(ang)
    pad = np.zeros((seq, LANES - MLA_ROPE_DIM))
    cos_t = np.concatenate([cos, cos, pad], axis=1).astype(np.float32)
    sin_t = np.concatenate([-sin, sin, pad], axis=1).astype(np.float32)
    return jnp.asarray(cos_t), jnp.asarray(sin_t)


def kernel(x, c, w_ada, b_ada, w_in, b_forget, g_q_norm, w_q_up, g_kv_norm, w_kv_up, w_branch_fox,
           w_branch_mla, w_out, ln1_g, ln1_b, w_mlp_up, w_mlp_down, ln2_g, ln2_b):
    n_batch, seq, d = x.shape
    assert w_ada.shape[0] == DEPTH
    m = n_batch * seq
    x2d = x.reshape(m, d)

    mod3 = _ada(c, w_ada[0], b_ada[0]).reshape(n_batch, N_ADA, d)

    small_col = 3 * FOX_WIDTH
    gf_col = small_col + FOX_HEADS + MLA_Q_LORA + MLA_KV_LORA + MLA_ROPE_DIM
    gm_col = gf_col + d
    assert w_in.shape[2] == gm_col + d and small_col == IN_SMALL_TILE * IN_TN and gf_col <= small_col + IN_TN
    proj, narrow_hi, narrow_lo = _inproj(x2d, mod3, w_in[0].T, seq)

    cos_t, sin_t = _rope_tables(seq)
    bf_pad = _pad_cols(b_forget[0].reshape(1, FOX_HEADS), LANES)
    qn, qr, kv, kr, cumt = _prep(
        narrow_hi, narrow_lo, bf_pad, g_q_norm[0].reshape(1, -1), g_kv_norm[0].reshape(1, -1),
        _layout_q_up(w_q_up[0]), w_kv_up[0].astype(BF16), cos_t, sin_t, seq)

    y_fox, w_up_b, w_down_b = _fox_attention(
        proj, cumt.reshape(n_batch, FOX_HEADS // 2, 2, seq), seq, 0, (w_mlp_up[0], w_mlp_down[0]))
    y_mla, w_bf_b, w_bm_b, w_o_b = _mla_attention(
        qn, qr, kv, kr, seq, (w_branch_fox[0], w_branch_mla[0], w_out[0]))

    x1 = _mix(y_fox, y_mla, proj, gf_col, gm_col, x2d, mod3, w_bf_b, w_bm_b, w_o_b,
              ln1_g[0].reshape(1, d), ln1_b[0].reshape(1, d), seq)
    x2 = _mlp(x1, mod3, w_up_b, w_down_b, ln2_g[0].reshape(1, d), ln2_b[0].reshape(1, d), seq)
    return x2.reshape(n_batch, seq, d)
```

```python
import functools
import math

import jax
import jax.numpy as jnp
import numpy as np
from jax import lax
from jax.experimental import pallas as pl
from jax.experimental.pallas import tpu as pltpu

F32 = jnp.float32
BF16 = jnp.bfloat16

FOX_HEADS = 16
FOX_HEAD_DIM = 64
FOX_WIDTH = FOX_HEADS * FOX_HEAD_DIM
MLA_HEADS = 8
MLA_NOPE_DIM = 128
MLA_ROPE_DIM = 64
MLA_V_DIM = 128
MLA_Q_LORA = 512
MLA_KV_LORA = 256
MLA_QK_DIM = MLA_NOPE_DIM + MLA_ROPE_DIM
CHUNK = 64
ROPE_THETA = 10000.0
LN_EPS = 1e-5
RMS_EPS = 1e-6
N_ADA = 6
DEPTH = 1
ALPHA = (2.0 * DEPTH) ** 0.25

LANES = 128
LOG2E = math.log2(math.e)
FOX_QSCALE = FOX_HEAD_DIM ** -0.5 * LOG2E
MLA_QSCALE = MLA_QK_DIM ** -0.5 * LOG2E
NEG_BIG = -1e30
VMEM_LIMIT = 56 * 1024 * 1024

FOX_SUB = 128
MLA_SUB = 256
HALF = 256


def _cparams(semantics):
    return pltpu.CompilerParams(dimension_semantics=semantics, vmem_limit_bytes=VMEM_LIMIT)


def _layer_norm(y, g, b):
    mu = jnp.mean(y, axis=-1, keepdims=True)
    yc = y - mu
    var = jnp.mean(yc * yc, axis=-1, keepdims=True)
    return yc * lax.rsqrt(var + LN_EPS) * g + b


def _rms_norm(c, g):
    return c * lax.rsqrt(jnp.mean(c * c, axis=-1, keepdims=True) + RMS_EPS) * g


def _ada_kernel(ct_ref, w_ref, b_ref, o_ref, *, n_batch, k_chunk):
    d = w_ref.shape[0]
    ct = ct_ref[...]
    s = ct * jax.nn.sigmoid(ct)
    accs = [b_ref[...] for _ in range(n_batch)]
    for kc in range(d // k_chunk):
        rows = slice(kc * k_chunk, (kc + 1) * k_chunk)
        w = w_ref[rows, :]
        for b in range(n_batch):
            accs[b] = accs[b] + jnp.sum(w * s[rows, b:b + 1], axis=0, keepdims=True)
    for b in range(n_batch):
        o_ref[b:b + 1, :] = accs[b]


def _ada(c, w_ada, b_ada):
    n_batch, d = c.shape
    n = w_ada.shape[1]
    tn = 1024
    return pl.pallas_call(
        functools.partial(_ada_kernel, n_batch=n_batch, k_chunk=256),
        grid=(n // tn,),
        in_specs=[
            pl.BlockSpec((d, n_batch), lambda j: (0, 0)),
            pl.BlockSpec((d, tn), lambda j: (0, j)),
            pl.BlockSpec((1, tn), lambda j: (0, j)),
        ],
        out_specs=pl.BlockSpec((n_batch, tn), lambda j: (0, j)),
        out_shape=jax.ShapeDtypeStruct((n_batch, n), F32),
        compiler_params=_cparams(("arbitrary",)),
        name="ada",
    )(c.T, w_ada, b_ada.reshape(1, n))


IN_TN = 1024
IN_SMALL_TILE = 3 * FOX_WIDTH // IN_TN
NARROW_F = (0, LANES)
NARROW_CQ = (NARROW_F[1], NARROW_F[1] + MLA_Q_LORA)
NARROW_CKV = (NARROW_CQ[1], NARROW_CQ[1] + MLA_KV_LORA)
NARROW_KR = (NARROW_CKV[1], NARROW_CKV[1] + LANES)
assert NARROW_KR[1] == IN_TN and 2 * MLA_ROPE_DIM == LANES


def _narrow_segments(acc):
    c0 = FOX_HEADS
    c1 = c0 + MLA_Q_LORA
    c2 = c1 + MLA_KV_LORA
    c3 = c2 + MLA_ROPE_DIM
    half = MLA_ROPE_DIM // 2
    return jnp.concatenate([acc[:, 0:LANES], acc[:, c0:c1], acc[:, c1:c2], acc[:, c2:c3],
                            acc[:, c2 + half:c3], acc[:, c2:c2 + half]], axis=1)


def _inproj_kernel(x_ref, mod_ref, w_ref, o_ref, hi_ref, lo_ref, u_ref, *, n_cols):
    j = pl.program_id(1)
    last = pl.num_programs(1) - 1

    def product(wt, u=None):
        u = u_ref[...] if u is None else u
        return lax.dot_general(u, wt.astype(BF16), (((1,), (1,)), ((), ())), preferred_element_type=F32)

    @pl.when(j == 0)
    def _():
        shift = mod_ref[0:1, :]
        scale = mod_ref[1:2, :]
        u = (x_ref[...] * (1.0 + scale) + shift).astype(BF16)
        u_ref[...] = u
        o_ref[...] = (product(w_ref[...], u) * FOX_QSCALE).astype(o_ref.dtype)

    @pl.when(j == IN_SMALL_TILE)
    def _():
        acc = product(w_ref[...])
        o_ref[...] = acc.astype(o_ref.dtype)
        narrow = _narrow_segments(acc)
        hi = narrow.astype(hi_ref.dtype)
        hi_ref[...] = hi
        lo_ref[...] = (narrow - hi.astype(F32)).astype(lo_ref.dtype)

    @pl.when(j == last)
    def _():
        row = lax.broadcasted_iota(jnp.int32, w_ref.shape, 0)
        wt = jnp.where(row < n_cols - last * IN_TN, w_ref[...], 0.0)
        o_ref[...] = product(wt).astype(o_ref.dtype)

    @pl.when((j != 0) & (j != IN_SMALL_TILE) & (j != last))
    def _():
        o_ref[...] = product(w_ref[...]).astype(o_ref.dtype)


def _inproj(x2d, mod3, w_in_t, seq):
    m, d = x2d.shape
    n_cols = w_in_t.shape[0]
    tm = 1024
    n_tiles = pl.cdiv(n_cols, IN_TN)
    tiles_per_batch = seq // tm
    return pl.pallas_call(
        functools.partial(_inproj_kernel, n_cols=n_cols),
        grid=(m // tm, n_tiles),
        in_specs=[
            pl.BlockSpec((tm, d), lambda i, j: (i, 0)),
            pl.BlockSpec((None, N_ADA, d), lambda i, j: (i // tiles_per_batch, 0, 0)),
            pl.BlockSpec((IN_TN, d), lambda i, j: (j, 0)),
        ],
        out_specs=[pl.BlockSpec((tm, IN_TN), lambda i, j: (i, j)),
                   pl.BlockSpec((tm, IN_TN), lambda i, j: (i, 0)),
                   pl.BlockSpec((tm, IN_TN), lambda i, j: (i, 0))],
        out_shape=[jax.ShapeDtypeStruct((m, n_tiles * IN_TN), BF16),
                   jax.ShapeDtypeStruct((m, IN_TN), BF16),
                   jax.ShapeDtypeStruct((m, IN_TN), BF16)],
        scratch_shapes=[pltpu.VMEM((tm, d), BF16)],
        compiler_params=_cparams(("parallel", "arbitrary")),
        name="inproj",
    )(x2d, mod3, w_in_t)


def _log_sigmoid(x):
    return jnp.minimum(x, 0.0) - jnp.log1p(jnp.exp(-jnp.abs(x)))


def _prep_kernel(hi_ref, lo_ref, bf_ref, gq_ref, gkv_ref, wq_ref, wkv_ref, cos_ref, sin_ref,
                 qn_ref, qr_ref, kv_ref, kr_ref, cumt_ref, carry_ref, *, tiles_per_batch):
    i = pl.program_id(0)
    tm = hi_ref.shape[0]

    @pl.when(i % tiles_per_batch == 0)
    def _():
        carry_ref[...] = jnp.zeros_like(carry_ref)

    ps = hi_ref[...].astype(F32) + lo_ref[...].astype(F32)
    k_rope = ps[:, NARROW_KR[0]:NARROW_KR[1]]
    k_rope_partner = pltpu.roll(k_rope, MLA_ROPE_DIM, axis=1)
    cos = cos_ref[...]
    sin = sin_ref[...]

    logf = _log_sigmoid(ps[:, NARROW_F[0]:NARROW_F[1]] + bf_ref[...]) * LOG2E
    hi = logf.astype(BF16)
    r1 = logf - hi.astype(F32)
    mid = r1.astype(BF16)
    lo = (r1 - mid.astype(F32)).astype(BF16)
    row = lax.broadcasted_iota(jnp.int32, (tm, tm), 0)
    col = lax.broadcasted_iota(jnp.int32, (tm, tm), 1)
    tri = (row >= col).astype(BF16)
    cs = (jnp.dot(tri, hi, preferred_element_type=F32)
          + jnp.dot(tri, mid, preferred_element_type=F32)
          + jnp.dot(tri, lo, preferred_element_type=F32))
    cum = cs + carry_ref[...]
    carry_ref[...] = cum[tm - 1:tm, :]
    cumt_ref[...] = cum.T[0:FOX_HEADS, :]

    cqn = _rms_norm(ps[:, NARROW_CQ[0]:NARROW_CQ[1]], gq_ref[...]).astype(BF16)
    qa = jnp.dot(cqn, wq_ref[...], preferred_element_type=F32)
    hw = MLA_HEADS * LANES
    qn_ref[...] = (qa[:, 0:hw] * MLA_QSCALE).astype(BF16)
    for h in range(MLA_HEADS):
        a = qa[:, hw + h * LANES: hw + (h + 1) * LANES]
        b = qa[:, 2 * hw + h * LANES: 2 * hw + (h + 1) * LANES]
        qr_ref[:, h * LANES:(h + 1) * LANES] = ((a * cos + b * sin) * MLA_QSCALE).astype(BF16)

    ckvn = _rms_norm(ps[:, NARROW_CKV[0]:NARROW_CKV[1]], gkv_ref[...]).astype(BF16)
    kv_ref[...] = jnp.dot(ckvn, wkv_ref[...], preferred_element_type=F32).astype(BF16)
    kr_ref[...] = (k_rope * cos + k_rope_partner * sin).astype(BF16)


def _prep(narrow_hi, narrow_lo, bf_pad, g_q, g_kv, w_q_all, w_kv, cos_t, sin_t, seq):
    m = narrow_hi.shape[0]
    n_batch = m // seq
    tm = 512
    tiles_per_batch = seq // tm
    hw = MLA_HEADS * LANES
    const = lambda i: (0, 0)
    return pl.pallas_call(
        functools.partial(_prep_kernel, tiles_per_batch=tiles_per_batch),
        grid=(m // tm,),
        in_specs=[
            pl.BlockSpec((tm, IN_TN), lambda i: (i, 0)),
            pl.BlockSpec((tm, IN_TN), lambda i: (i, 0)),
            pl.BlockSpec(bf_pad.shape, const),
            pl.BlockSpec(g_q.shape, const),
            pl.BlockSpec(g_kv.shape, const),
            pl.BlockSpec(w_q_all.shape, const),
            pl.BlockSpec(w_kv.shape, const),
            pl.BlockSpec((tm, LANES), lambda i: (i % tiles_per_batch, 0)),
            pl.BlockSpec((tm, LANES), lambda i: (i % tiles_per_batch, 0)),
        ],
        out_specs=[
            pl.BlockSpec((tm, hw), lambda i: (i, 0)),
            pl.BlockSpec((tm, hw), lambda i: (i, 0)),
            pl.BlockSpec((tm, w_kv.shape[1]), lambda i: (i, 0)),
            pl.BlockSpec((tm, LANES), lambda i: (i, 0)),
            pl.BlockSpec((None, FOX_HEADS, tm), lambda i: (i // tiles_per_batch, 0, i % tiles_per_batch)),
        ],
        out_shape=[
            jax.ShapeDtypeStruct((m, hw), BF16),
            jax.ShapeDtypeStruct((m, hw), BF16),
            jax.ShapeDtypeStruct((m, w_kv.shape[1]), BF16),
            jax.ShapeDtypeStruct((m, LANES), BF16),
            jax.ShapeDtypeStruct((n_batch, FOX_HEADS, seq), F32),
        ],
        scratch_shapes=[pltpu.VMEM((1, LANES), F32)],
        compiler_params=_cparams(("arbitrary",)),
        name="prep",
    )(narrow_hi, narrow_lo, bf_pad, g_q, g_kv, w_q_all, w_kv, cos_t, sin_t)


def _split3(x):
    hi = x.astype(BF16).astype(F32)
    r = x - hi
    mid = r.astype(BF16).astype(F32)
    lo = (r - mid).astype(BF16).astype(F32)
    return hi, mid, lo


def _online_softmax_step(t, m, acc, vt_aug):
    m_new = jnp.maximum(m, jnp.max(t, axis=0, keepdims=True))
    alpha = jnp.exp2(m - m_new)
    p = jnp.exp2(t - m_new).astype(BF16)
    pv = jnp.dot(vt_aug, p, preferred_element_type=F32)
    return m_new, alpha * acc + pv[0:acc.shape[0], :]


def _causal_mask(t, key0, query0, granule):
    r = lax.broadcasted_iota(jnp.int32, t.shape, 0) + key0
    c = lax.broadcasted_iota(jnp.int32, t.shape, 1) + query0
    return jnp.where(r <= (c | (granule - 1)), t, NEG_BIG)


Q_TILES = 8
FULL, DIAG, SKIP = "full", "diag", "skip"


def _flash_key_major(j, tile, sub, keys, rhs, vt_aug, d_v, granule, finish):
    acc_rows = d_v + 8
    n_rounds = tile // sub
    chains = [(qt, half, hh) for qt in range(Q_TILES) for half in range(tile // HALF) for hh in (0, 1)]
    pairs = [(n, n + 1) for n in range(0, len(chains), 2)]

    def qk_of(k0, modes):
        loaded = {}

        def product(n):
            qt, half, hh = chains[n]
            n_keys = (half + 1) * HALF if modes[qt] == DIAG else tile
            if (hh, n_keys) not in loaded:
                loaded[hh, n_keys] = keys(hh, k0, n_keys)
            return jnp.dot(loaded[hh, n_keys], rhs(qt, hh)[:, half * HALF:(half + 1) * HALF],
                           preferred_element_type=F32)
        return product

    def visit(state, tiles):
        st = list(state)
        products = [qk_of(*t) for t in tiles]
        phases = [(ti, pair) for ti, (_, modes) in enumerate(tiles) for pair in pairs
                  if modes[chains[pair[0]][0]] != SKIP]

        def issue(phase):
            ti, pair = phase
            return {n: products[ti](n) for n in pair}

        cur = issue(phases[0])
        for idx, (ti, pair) in enumerate(phases):
            nxt = issue(phases[idx + 1]) if idx + 1 < len(phases) else None
            k0, modes = tiles[ti]
            for sb in range(n_rounds):
                ks = pl.multiple_of(k0 + sb * sub, sub)
                for n in pair:
                    qt, half, hh = chains[n]
                    is_diag = modes[qt] == DIAG
                    if is_diag and sb * sub >= (half + 1) * HALF:
                        continue
                    t = cur[n][sb * sub:(sb + 1) * sub, :]
                    if is_diag and (sb + 1) * sub > half * HALF:
                        t = _causal_mask(t, sb * sub, half * HALF, granule)
                    st[n] = _online_softmax_step(t, *st[n], vt_aug(hh, ks))
            cur = nxt
        return tuple(st)

    everyone = (FULL,) * Q_TILES

    def two_full(i, state):
        k0 = pl.multiple_of(2 * i * tile, tile)
        return visit(state, [(k0, everyone), (pl.multiple_of(k0 + tile, tile), everyone)])

    assert Q_TILES % 2 == 0
    init = tuple((jnp.full((1, HALF), NEG_BIG, F32), jnp.zeros((acc_rows, HALF), F32)) for _ in chains)
    state = lax.fori_loop(0, j * (Q_TILES // 2), two_full, init)
    q0 = j * (Q_TILES * tile)
    last = [(pl.multiple_of(q0 + t * tile, tile),
             tuple(SKIP if qt < t else DIAG if qt == t else FULL for qt in range(Q_TILES)))
            for t in range(Q_TILES)]
    return finish([acc[0:d_v, :] * (1.0 / acc[d_v:d_v + 1, :]) for (_, acc) in visit(state, last)])


def _cast_specs(weights, grid):
    n_steps = math.prod(grid)
    specs, shapes = [], []
    for w in weights:
        rows, cols = w.shape
        slab = rows // n_steps
        assert slab * n_steps == rows and slab % 16 == 0, (w.shape, n_steps)
        specs.append(pl.BlockSpec((slab, cols), lambda a, b: (a * grid[1] + b, 0)))
        shapes.append(jax.ShapeDtypeStruct(w.shape, BF16))
    return specs, shapes


def _cast_slabs(refs):
    n = len(refs) // 2
    for src, dst in zip(refs[:n], refs[n:]):
        dst[...] = src[...].astype(dst.dtype)


def _stack_chain_outputs(outs, n_half):
    return jnp.concatenate(
        [jnp.concatenate([outs[half * 2 + hh] for half in range(n_half)], axis=1) for hh in (0, 1)], axis=0)


AUG_ROWS = 16


def _bias_rows(n, entries):
    r = lax.broadcasted_iota(jnp.int32, (AUG_ROWS, n), 0)
    out = jnp.zeros((AUG_ROWS, n), F32)
    for row, val in entries.items():
        out = jnp.where(r == row, val, out)
    return out.astype(BF16)


def _fox_kernel(q_ref, k_ref, v_ref, cumt_ref, *rest, tq, n_cast):
    cast_in, (o_ref, *cast_out), (vta_ref, kaug_ref) = rest[:n_cast], rest[n_cast:2 * n_cast + 1], rest[2 * n_cast + 1:]
    seq = k_ref.shape[0]
    dh = FOX_HEAD_DIM
    n_half = tq // HALF
    aug = 6

    vt = v_ref[...].T
    tail_rows = lax.broadcasted_iota(jnp.int32, (16, seq), 0)
    tail = jnp.where(tail_rows < 8, 1.0, 0.0).astype(BF16)
    for hh in (0, 1):
        vta_ref[hh, 0:dh, :] = vt[hh * dh:(hh + 1) * dh, :]
        vta_ref[hh, dh:dh + 16, :] = tail
    key_rows = {}
    for hh in (0, 1):
        for idx, piece in enumerate(_split3(-cumt_ref[hh:hh + 1, :])):
            key_rows[aug * hh + idx] = piece
            key_rows[aug * hh + 3 + idx] = 1.0
    kaug_t = jnp.concatenate([_bias_rows(seq, key_rows), jnp.zeros((LANES - AUG_ROWS, seq), BF16)], axis=0)
    kaug_ref[...] = kaug_t.T

    def keys(hh, k0, n):
        return jnp.concatenate([k_ref[pl.ds(k0, n), :], kaug_ref[pl.ds(k0, n), :]], axis=1)

    def make_rhs(j):
        row = lax.broadcasted_iota(jnp.int32, (LANES, tq), 0)
        pad = jnp.zeros((LANES - AUG_ROWS, tq), BF16)
        out = []
        for t in range(Q_TILES):
            q0 = pl.multiple_of((j * Q_TILES + t) * tq, tq)
            qt = q_ref[pl.ds(q0, tq), :].T
            rhs = []
            for hh in (0, 1):
                qm = jnp.where((row >= hh * dh) & (row < (hh + 1) * dh), qt, jnp.zeros_like(qt))
                query_rows = {}
                for idx, piece in enumerate(_split3(cumt_ref[hh:hh + 1, pl.ds(q0, tq)])):
                    query_rows[aug * hh + idx] = 1.0
                    query_rows[aug * hh + 3 + idx] = piece
                rhs.append(jnp.concatenate([qm, _bias_rows(tq, query_rows), pad], axis=0))
            out.append(tuple(rhs))
        return tuple(out)

    n_super = seq // (Q_TILES * tq)
    per_tile = 2 * n_half

    def super_tile(j, rhs):
        def finish(outs):
            for t in range(Q_TILES):
                q0 = pl.multiple_of((j * Q_TILES + t) * tq, tq)
                tile_outs = outs[t * per_tile:(t + 1) * per_tile]
                o_ref[pl.ds(q0, tq), :] = _stack_chain_outputs(tile_outs, n_half).T.astype(o_ref.dtype)
            return make_rhs(jnp.minimum(j + 1, n_super - 1))

        return _flash_key_major(j, tq, FOX_SUB, keys, lambda t, hh: rhs[t][hh],
                                lambda hh, ks: vta_ref[hh, :, pl.ds(ks, FOX_SUB)], dh, 1, finish)

    lax.fori_loop(0, n_super, super_tile, make_rhs(0))
    _cast_slabs(cast_in + tuple(cast_out))


def _fox_attention(proj, cumt4, seq, q_col0, cast_weights):
    m = proj.shape[0]
    n_batch = m // seq
    n_pairs = FOX_HEADS // 2
    qb = q_col0 // LANES
    grid = (n_batch, n_pairs)
    cast_specs, cast_shapes = _cast_specs(cast_weights, grid)
    return pl.pallas_call(
        functools.partial(_fox_kernel, tq=512, n_cast=len(cast_weights)),
        grid=grid,
        in_specs=[
            pl.BlockSpec((seq, LANES), lambda b, hp: (b, qb + hp)),
            pl.BlockSpec((seq, LANES), lambda b, hp: (b, qb + n_pairs + hp)),
            pl.BlockSpec((seq, LANES), lambda b, hp: (b, qb + 2 * n_pairs + hp)),
            pl.BlockSpec((None, None, 2, seq), lambda b, hp: (b, hp, 0, 0)),
        ] + cast_specs,
        out_specs=[pl.BlockSpec((seq, LANES), lambda b, hp: (b, hp))] + cast_specs,
        out_shape=[jax.ShapeDtypeStruct((m, FOX_WIDTH), BF16)] + cast_shapes,
        scratch_shapes=[
            pltpu.VMEM((2, FOX_HEAD_DIM + 16, seq), BF16),
            pltpu.VMEM((seq, LANES), BF16),
        ],
        compiler_params=_cparams(("parallel", "parallel")),
        name="fox_attn",
    )(proj, proj, proj, cumt4, *cast_weights)


def _mla_kernel(qn_ref, qr_ref, kv_ref, kr_ref, *rest, tq, n_cast):
    cast_in, (o_ref, *cast_out), (vta_ref,) = rest[:n_cast], rest[n_cast:2 * n_cast + 1], rest[2 * n_cast + 1:]
    seq = kv_ref.shape[0]
    dv = MLA_V_DIM
    n_half = tq // HALF

    tail_rows = lax.broadcasted_iota(jnp.int32, (16, seq), 0)
    tail = jnp.where(tail_rows < 8, 1.0, 0.0).astype(BF16)
    for hh in (0, 1):
        vta_ref[hh, 0:dv, :] = kv_ref[:, (2 * hh + 1) * LANES:(2 * hh + 2) * LANES].T
        vta_ref[hh, dv:dv + 16, :] = tail

    def keys(hh, k0, n):
        return jnp.concatenate([kv_ref[pl.ds(k0, n), 2 * hh * LANES:(2 * hh + 1) * LANES],
                                kr_ref[pl.ds(k0, n), :]], axis=1)

    def make_rhs(j):
        out = []
        for t in range(Q_TILES):
            q0 = pl.multiple_of((j * Q_TILES + t) * tq, tq)
            out.append(tuple(jnp.concatenate([qn_ref[pl.ds(q0, tq), hh * LANES:(hh + 1) * LANES],
                                              qr_ref[pl.ds(q0, tq), hh * LANES:(hh + 1) * LANES]], axis=1).T
                             for hh in (0, 1)))
        return tuple(out)

    n_super = seq // (Q_TILES * tq)
    per_tile = 2 * n_half

    def super_tile(j, rhs):
        def finish(outs):
            for t in range(Q_TILES):
                q0 = pl.multiple_of((j * Q_TILES + t) * tq, tq)
                tile_outs = outs[t * per_tile:(t + 1) * per_tile]
                o_ref[pl.ds(q0, tq), :] = _stack_chain_outputs(tile_outs, n_half).T.astype(o_ref.dtype)
            return make_rhs(jnp.minimum(j + 1, n_super - 1))

        return _flash_key_major(j, tq, MLA_SUB, keys, lambda t, hh: rhs[t][hh],
                                lambda hh, ks: vta_ref[hh, :, pl.ds(ks, MLA_SUB)], dv, CHUNK, finish)

    lax.fori_loop(0, n_super, super_tile, make_rhs(0))
    _cast_slabs(cast_in + tuple(cast_out))


def _mla_attention(qn, qr, kv, kr, seq, cast_weights):
    m = qn.shape[0]
    n_batch = m // seq
    pair = 2 * LANES
    grid = (n_batch, MLA_HEADS // 2)
    cast_specs, cast_shapes = _cast_specs(cast_weights, grid)
    return pl.pallas_call(
        functools.partial(_mla_kernel, tq=512, n_cast=len(cast_weights)),
        grid=grid,
        in_specs=[
            pl.BlockSpec((seq, pair), lambda b, hp: (b, hp)),
            pl.BlockSpec((seq, pair), lambda b, hp: (b, hp)),
            pl.BlockSpec((seq, 2 * pair), lambda b, hp: (b, hp)),
            pl.BlockSpec((seq, LANES), lambda b, hp: (b, 0)),
        ] + cast_specs,
        out_specs=[pl.BlockSpec((seq, pair), lambda b, hp: (b, hp))] + cast_specs,
        out_shape=[jax.ShapeDtypeStruct((m, MLA_HEADS * MLA_V_DIM), BF16)] + cast_shapes,
        scratch_shapes=[pltpu.VMEM((2, MLA_V_DIM + 16, seq), BF16)],
        compiler_params=_cparams(("parallel", "parallel")),
        name="mla_attn",
    )(qn, qr, kv, kr, *cast_weights)


def _mix_kernel(yf_ref, ym_ref, gf_ref, gm_ref, x_ref, mod_ref, wbf_ref, wbm_ref, wo_ref, g_ref, b_ref,
                o_ref, mg_ref, *, n_chunk, g_off):
    d = o_ref.shape[1]
    yf = yf_ref[...]
    ym = ym_ref[...]

    def gate(ref, c):
        window = ref[:, c * n_chunk:(c + 1) * n_chunk + LANES]
        return jax.nn.sigmoid(window[:, g_off:g_off + n_chunk].astype(F32))

    for c in range(d // n_chunk):
        cols = slice(c * n_chunk, (c + 1) * n_chunk)
        a = jnp.dot(yf, wbf_ref[:, cols], preferred_element_type=F32)
        b = jnp.dot(ym, wbm_ref[:, cols], preferred_element_type=F32)
        merged = gate(gf_ref, c) * a + gate(gm_ref, c) * b
        mg_ref[:, cols] = merged.astype(BF16)
    mix = jnp.dot(mg_ref[...], wo_ref[...], preferred_element_type=F32)
    gate1 = mod_ref[2:3, :]
    o_ref[...] = _layer_norm(ALPHA * x_ref[...] + gate1 * mix, g_ref[...], b_ref[...])


def _mix(y_fox, y_mla, proj, gf_col, gm_col, x2d, mod3, w_bf, w_bm, w_o, ln_g, ln_b, seq):
    m, d = x2d.shape
    tm = 512
    tiles_per_batch = seq // tm
    const = lambda i: (0, 0)
    resident = dict(pipeline_mode=pl.Buffered(1))
    g_off = gf_col % LANES
    assert gm_col % LANES == g_off
    g_window = (pl.Element(tm), pl.Element(d + LANES))

    def window_at(col):
        return lambda i: (pl.multiple_of(i * tm, tm), col - g_off)

    return pl.pallas_call(
        functools.partial(_mix_kernel, n_chunk=512, g_off=g_off),
        grid=(m // tm,),
        in_specs=[
            pl.BlockSpec((tm, y_fox.shape[1]), lambda i: (i, 0)),
            pl.BlockSpec((tm, y_mla.shape[1]), lambda i: (i, 0)),
            pl.BlockSpec(g_window, window_at(gf_col)),
            pl.BlockSpec(g_window, window_at(gm_col)),
            pl.BlockSpec((tm, d), lambda i: (i, 0)),
            pl.BlockSpec((None, N_ADA, d), lambda i: (i // tiles_per_batch, 0, 0)),
            pl.BlockSpec(w_bf.shape, const, **resident),
            pl.BlockSpec(w_bm.shape, const, **resident),
            pl.BlockSpec(w_o.shape, const, **resident),
            pl.BlockSpec((1, d), const),
            pl.BlockSpec((1, d), const),
        ],
        out_specs=pl.BlockSpec((tm, d), lambda i: (i, 0)),
        out_shape=jax.ShapeDtypeStruct((m, d), F32),
        scratch_shapes=[pltpu.VMEM((tm, d), BF16)],
        compiler_params=_cparams(("parallel",)),
        name="mix",
    )(y_fox, y_mla, proj, proj, x2d, mod3, w_bf, w_bm, w_o, ln_g, ln_b)


def _mlp_kernel(x_ref, mod_ref, wu_ref, wd_ref, g_ref, b_ref, o_ref, u_ref):
    f = pl.program_id(1)

    def partial_sum(u):
        z = jnp.dot(u, wu_ref[...], preferred_element_type=F32)
        h = jnp.square(jnp.maximum(z, 0.0)).astype(BF16)
        return jnp.dot(h, wd_ref[...], preferred_element_type=F32)

    @pl.when(f == 0)
    def _():
        shift = mod_ref[3:4, :]
        scale = mod_ref[4:5, :]
        u = (x_ref[...] * (1.0 + scale) + shift).astype(BF16)
        u_ref[...] = u
        o_ref[...] = partial_sum(u)

    last = pl.num_programs(1) - 1

    @pl.when((f != 0) & (f != last))
    def _():
        o_ref[...] += partial_sum(u_ref[...])

    @pl.when(f == last)
    def _():
        gate = mod_ref[5:6, :]
        h = o_ref[...] + partial_sum(u_ref[...])
        o_ref[...] = _layer_norm(ALPHA * x_ref[...] + gate * h, g_ref[...], b_ref[...])


def _mlp(x1, mod3, w_up, w_down, ln_g, ln_b, seq):
    m, d = x1.shape
    d_ff = w_up.shape[1]
    tm, tf = 1024, 512
    tiles_per_batch = seq // tm
    return pl.pallas_call(
        _mlp_kernel,
        grid=(m // tm, d_ff // tf),
        in_specs=[
            pl.BlockSpec((tm, d), lambda i, f: (i, 0)),
            pl.BlockSpec((None, N_ADA, d), lambda i, f: (i // tiles_per_batch, 0, 0)),
            pl.BlockSpec((d, tf), lambda i, f: (0, f)),
            pl.BlockSpec((tf, d), lambda i, f: (f, 0)),
            pl.BlockSpec((1, d), lambda i, f: (0, 0)),
            pl.BlockSpec((1, d), lambda i, f: (0, 0)),
        ],
        out_specs=pl.BlockSpec((tm, d), lambda i, f: (i, 0)),
        out_shape=jax.ShapeDtypeStruct((m, d), F32),
        scratch_shapes=[pltpu.VMEM((tm, d), BF16)],
        compiler_params=_cparams(("parallel", "arbitrary")),
        name="mlp",
    )(x1, mod3, w_up, w_down, ln_g, ln_b)


def _pad_cols(w, width):
    return jnp.pad(w, ((0, 0), (0, width - w.shape[1])))


def _swap_halves(w):
    half = w.shape[-1] // 2
    return jnp.concatenate([w[..., half:], w[..., :half]], axis=-1)


def _layout_q_up(w_q_up):
    r = w_q_up.shape[0]
    w3 = w_q_up.reshape(r, MLA_HEADS, MLA_QK_DIM)
    w_nope = w3[:, :, :MLA_NOPE_DIM].reshape(r, MLA_HEADS * MLA_NOPE_DIM)
    w_rope = w3[:, :, MLA_NOPE_DIM:]
    pad = ((0, 0), (0, 0), (0, LANES - MLA_ROPE_DIM))
    w_r = jnp.pad(w_rope, pad).reshape(r, MLA_HEADS * LANES)
    w_rs = jnp.pad(_swap_halves(w_rope), pad).reshape(r, MLA_HEADS * LANES)
    return jnp.concatenate([w_nope, w_r, w_rs], axis=1).astype(BF16)


def _rope_tables(seq):
    pos = np.arange(seq, dtype=np.float64)
    inv_freq = ROPE_THETA ** (-np.arange(0, MLA_ROPE_DIM, 2, dtype=np.float64) / MLA_ROPE_DIM)
    ang = pos[:, None] * inv_freq[None, :]
    cos, sin = np.cos(ang), np.sin(ang)
    pad = np.zeros((seq, LANES - MLA_ROPE_DIM))
    cos_t = np.concatenate([cos, cos, pad], axis=1).astype(np.float32)
    sin_t = np.concatenate([-sin, sin, pad], axis=1).astype(np.float32)
    return jnp.asarray(cos_t), jnp.asarray(sin_t)


def kernel(x, c, w_ada, b_ada, w_in, b_forget, g_q_norm, w_q_up, g_kv_norm, w_kv_up, w_branch_fox,
           w_branch_mla, w_out, ln1_g, ln1_b, w_mlp_up, w_mlp_down, ln2_g, ln2_b):
    n_batch, seq, d = x.shape
    assert w_ada.shape[0] == DEPTH
    m = n_batch * seq
    x2d = x.reshape(m, d)

    mod3 = _ada(c, w_ada[0], b_ada[0]).reshape(n_batch, N_ADA, d)

    small_col = 3 * FOX_WIDTH
    gf_col = small_col + FOX_HEADS + MLA_Q_LORA + MLA_KV_LORA + MLA_ROPE_DIM
    gm_col = gf_col + d
    assert w_in.shape[2] == gm_col + d and small_col == IN_SMALL_TILE * IN_TN and gf_col <= small_col + IN_TN
    proj, narrow_hi, narrow_lo = _inproj(x2d, mod3, w_in[0].T, seq)

    cos_t, sin_t = _rope_tables(seq)
    bf_pad = _pad_cols(b_forget[0].reshape(1, FOX_HEADS), LANES)
    qn, qr, kv, kr, cumt = _prep(
        narrow_hi, narrow_lo, bf_pad, g_q_norm[0].reshape(1, -1), g_kv_norm[0].reshape(1, -1),
        _layout_q_up(w_q_up[0]), w_kv_up[0].astype(BF16), cos_t, sin_t, seq)

    y_fox, w_up_b, w_down_b = _fox_attention(
        proj, cumt.reshape(n_batch, FOX_HEADS // 2, 2, seq), seq, 0, (w_mlp_up[0], w_mlp_down[0]))
    y_mla, w_bf_b, w_bm_b, w_o_b = _mla_attention(
        qn, qr, kv, kr, seq, (w_branch_fox[0], w_branch_mla[0], w_out[0]))

    x1 = _mix(y_fox, y_mla, proj, gf_col, gm_col, x2d, mod3, w_bf_b, w_bm_b, w_o_b,
              ln1_g[0].reshape(1, d), ln1_b[0].reshape(1, d), seq)
    x2 = _mlp(x1, mod3, w_up_b, w_down_b, ln2_g[0].reshape(1, d), ln2_b[0].reshape(1, d), seq)
    return x2.reshape(n_batch, seq, d)
```
